```python
import math
import jax
import jax.numpy as jnp
from jax import lax
import numpy as np

D_MODEL = 1024
BATCH = 4
SEQ = 4096
DEPTH = 4
DEC_BATCH = 128
DEC_SEQ = 4
PAST_LEN = 2048
PAGE_SIZE = 128

N_MIXERS = 4
N_LAYERS_A = len(range(0, DEPTH, N_MIXERS))
N_LAYERS_B = len(range(1, DEPTH, N_MIXERS))
N_LAYERS_C = len(range(2, DEPTH, N_MIXERS))
N_LAYERS_D = len(range(3, DEPTH, N_MIXERS))

HEAD_DIM = 64
ROPE_DIM = HEAD_DIM // 4
ROPE_THETA = 500000.0
EPS = 1e-6
D_FF = 2816
OUT_SCALE = 0.5

A_WINDOWS = (128, 512, 2048)
A_DILATIONS = (1, 4, 16)
A_GROUPS = 3
A_HEADS = D_MODEL // 128
A_WIDTH = A_HEADS * HEAD_DIM
A_SPAN = 128
A_BLOCK = A_SPAN

B_HEADS = D_MODEL // HEAD_DIM
B_KV_HEADS = 4
B_REP = B_HEADS // B_KV_HEADS
B_BLOCK = 256
B_TOPK = 3
B_QCHUNK = 64

C_HEADS = 4
C_KEY = D_MODEL // 2
C_VAL = D_MODEL
C_DK = C_KEY // C_HEADS
C_DV = C_VAL // C_HEADS
C_GATE_RANK = 16
C_TAU = 16.0
C_CHUNK = 32

D_INNER = 2 * D_MODEL
D_HEADDIM = 64
D_HEADS = D_INNER // D_HEADDIM
D_GROUPS = 4
D_HPG = D_HEADS // D_GROUPS
D_STATE = 128
D_CONV = 4
D_CHUNK = 64
D_XBC = D_INNER + 2 * D_GROUPS * D_STATE

kernel_name = 'hybrid_dilated_moba_gla_ssd_decoder_step'


def rms_norm(x, g):
    xf = x.astype(jnp.float32)
    y = xf * lax.rsqrt(jnp.mean(xf * xf, axis=-1, keepdims=True) + EPS)
    return (y * g.astype(jnp.float32)).astype(x.dtype)


def rope_partial(x, pos):
    half = ROPE_DIM // 2
    inv = ROPE_THETA ** (-jnp.arange(half, dtype=jnp.float32) / half)
    ang = pos.astype(jnp.float32)[:, None] * inv[None, :]
    cos = jnp.cos(ang)[:, None, :]
    sin = jnp.sin(ang)[:, None, :]
    xr = x[..., :ROPE_DIM].astype(jnp.float32)
    x1, x2 = xr[..., :half], xr[..., half:]
    rot = jnp.concatenate([x1 * cos - x2 * sin, x2 * cos + x1 * sin], axis=-1)
    return jnp.concatenate([rot.astype(x.dtype), x[..., ROPE_DIM:]], axis=-1)


def swiglu(x, w_up, w_down):
    g, u = jnp.split(x @ w_up, 2, axis=-1)
    return (jax.nn.silu(g) * u) @ w_down


def dilated_band_prompt(q, k, v, d):
    Bn, S, H, Dh = q.shape
    n = S // d
    nb = -(-n // A_BLOCK)
    npad = nb * A_BLOCK

    def split(t):
        return t.astype(jnp.float32).reshape(Bn, n, d, H, Dh).transpose(0, 2, 1, 3, 4)

    qs, ks, vs = split(q), split(k), split(v)
    qb = jnp.pad(qs, ((0, 0), (0, 0), (0, npad - n), (0, 0), (0, 0))).reshape(Bn, d, nb, A_BLOCK, H, Dh)
    kv_pad = ((0, 0), (0, 0), (A_BLOCK, npad - n), (0, 0), (0, 0))
    kb = jnp.pad(ks, kv_pad).reshape(Bn, d, nb + 1, A_BLOCK, H, Dh)
    vb = jnp.pad(vs, kv_pad).reshape(Bn, d, nb + 1, A_BLOCK, H, Dh)
    kband = jnp.concatenate([kb[:, :, :-1], kb[:, :, 1:]], axis=3)
    vband = jnp.concatenate([vb[:, :, :-1], vb[:, :, 1:]], axis=3)
    s = jnp.einsum('bdnqhe,bdnkhe->bdnhqk', qb, kband) * HEAD_DIM ** -0.5
    qq = jnp.arange(A_BLOCK)[:, None]
    kk = jnp.arange(2 * A_BLOCK)[None, :]
    rel = qq + A_BLOCK - kk
    upos = jnp.arange(nb)[:, None, None] * A_BLOCK - A_BLOCK + kk[None]
    valid = ((rel >= 0) & (rel <= A_SPAN))[None] & (upos >= 0)
    s = jnp.where(valid[None, None, :, None], s, -jnp.inf)
    m = jnp.max(s, axis=-1, keepdims=True)
    p = jnp.exp(s - m)
    den = jnp.sum(p, axis=-1, keepdims=True)
    o = jnp.einsum('bdnhqk,bdnkhe->bdnqhe', p / den, vband)
    lse = (m + jnp.log(den))[..., 0].transpose(0, 1, 2, 4, 3)
    o = o.reshape(Bn, d, npad, H, Dh)[:, :, :n].transpose(0, 2, 1, 3, 4).reshape(Bn, S, H, Dh)
    lse = lse.reshape(Bn, d, npad, H)[:, :, :n].transpose(0, 2, 1, 3).reshape(Bn, S, H)
    return o, lse


def dilated_gather_sample(q, k_new, v_new, k_buf, v_buf, d):
    L = k_buf.shape[1]
    T = q.shape[1]
    k_all = jnp.concatenate([k_buf.astype(jnp.float32), k_new.astype(jnp.float32)], axis=1)
    v_all = jnp.concatenate([v_buf.astype(jnp.float32), v_new.astype(jnp.float32)], axis=1)
    idx = L + jnp.arange(T)[:, None] - d * jnp.arange(A_SPAN + 1)[None, :]
    valid = idx >= 0
    idx = jnp.maximum(idx, 0)
    kg = k_all[:, idx]
    vg = v_all[:, idx]
    s = jnp.einsum('bthe,btjhe->bthj', q.astype(jnp.float32), kg) * HEAD_DIM ** -0.5
    s = jnp.where(valid[None, :, None, :], s, -jnp.inf)
    m = jnp.max(s, axis=-1, keepdims=True)
    p = jnp.exp(s - m)
    den = jnp.sum(p, axis=-1, keepdims=True)
    o = jnp.einsum('bthj,btjhe->bthe', p / den, vg)
    return o, (m + jnp.log(den))[..., 0]


def mixer_a(hp, hs, pos_p, pos_s, bufs, w_in, qk_gain, w_out):
    def project(h, pos):
        Bn, L, _ = h.shape
        qkv = (h @ w_in).reshape(Bn, L, A_GROUPS, 3, A_HEADS, HEAD_DIM).transpose(2, 3, 0, 1, 4, 5)
        q = rope_partial(rms_norm(qkv[:, 0], qk_gain[0]), pos)
        k = rope_partial(rms_norm(qkv[:, 1], qk_gain[1]), pos)
        return q, k, qkv[:, 2]

    qp, kp, vp = project(hp, pos_p)
    qs, ks, vs = project(hs, pos_s)
    S = hp.shape[1]
    op, lp, osm, lsm, new_p, new_s = [], [], [], [], [], []
    for g in range(A_GROUPS):
        d = A_DILATIONS[g]
        o, l = dilated_band_prompt(qp[g], kp[g], vp[g], d)
        op.append(o)
        lp.append(l)
        o, l = dilated_gather_sample(qs[g], ks[g], vs[g], bufs[g][:, :, 0], bufs[g][:, :, 1], d)
        osm.append(o)
        lsm.append(l)
        new_p.append(jnp.stack([kp[g], vp[g]], axis=2)[:, S - min(A_WINDOWS[g], S):])
        new_s.append(jnp.stack([ks[g], vs[g]], axis=2))

    def merge(o_list, l_list, dtype):
        w = jax.nn.softmax(jnp.stack(l_list), axis=0)
        o = jnp.einsum('gblh,gblhe->blhe', w, jnp.stack(o_list))
        return o.reshape(o.shape[0], o.shape[1], A_WIDTH).astype(dtype) @ w_out

    return merge(op, lp, hp.dtype), merge(osm, lsm, hs.dtype), new_p, new_s


def moba_blocks(k, v):
    Bn, L = k.shape[:2]
    nblk = -(-L // B_BLOCK)
    pad = ((0, 0), (0, nblk * B_BLOCK - L), (0, 0), (0, 0))
    kb = jnp.pad(k, pad).reshape(Bn, nblk, B_BLOCK, B_KV_HEADS, HEAD_DIM)
    vb = jnp.pad(v, pad).reshape(Bn, nblk, B_BLOCK, B_KV_HEADS, HEAD_DIM)
    kmean = jnp.mean(kb.astype(jnp.float32), axis=2)
    return kb, vb, kmean, min(B_TOPK, nblk)


def moba_attend(q, tq, kb, vb, kmean, n_sel):
    Bn, nblk = kb.shape[:2]
    scale = HEAD_DIM ** -0.5
    qf = q.astype(jnp.float32)
    own = tq // B_BLOCK
    gate = jnp.einsum('bqhrd,bnhd->bqhrn', qf, kmean)
    past = jnp.arange(nblk)[None, :] < own[:, None]
    gate = jnp.where(past[None, :, None, None, :], gate, -jnp.inf)
    sel = lax.top_k(gate, n_sel)[1]
    sel_ok = jnp.arange(n_sel)[None, :] < own[:, None]
    k_own = kb[:, own].astype(jnp.float32)
    v_own = vb[:, own].astype(jnp.float32)
    s_own = jnp.einsum('bqhrd,bqkhd->bqhrk', qf, k_own) * scale
    kpos = own[:, None] * B_BLOCK + jnp.arange(B_BLOCK)[None, :]
    s_own = jnp.where((kpos <= tq[:, None])[None, :, None, None, :], s_own, -jnp.inf)
    bi = jnp.arange(Bn)[:, None, None, None]
    hi = jnp.arange(B_KV_HEADS)[None, None, :, None]
    scores = [s_own]
    for slot in range(n_sel):
        kg = kb[bi, sel[..., slot], :, hi].astype(jnp.float32)
        s_r = jnp.einsum('bqhrd,bqhrkd->bqhrk', qf, kg) * scale
        scores.append(jnp.where(sel_ok[None, :, None, None, slot:slot + 1], s_r, -jnp.inf))
    p = jax.nn.softmax(jnp.concatenate(scores, axis=-1), axis=-1)
    o = jnp.einsum('bqhrk,bqkhd->bqhrd', p[..., :B_BLOCK], v_own)
    for slot in range(n_sel):
        vg = vb[bi, sel[..., slot], :, hi].astype(jnp.float32)
        o = o + jnp.einsum('bqhrk,bqhrkd->bqhrd', p[..., (slot + 1) * B_BLOCK:(slot + 2) * B_BLOCK], vg)
    return o


def mixer_b(hp, hs, pos_p, pos_s, pool, page_table, w_in, qk_gain, w_out):
    nq = B_HEADS * HEAD_DIM
    nk = B_KV_HEADS * HEAD_DIM

    def project(h, pos):
        Bn, L, _ = h.shape
        y = h @ w_in
        q = y[..., :nq].reshape(Bn, L, B_HEADS, HEAD_DIM)
        k = y[..., nq:nq + nk].reshape(Bn, L, B_KV_HEADS, HEAD_DIM)
        v = y[..., nq + nk:].reshape(Bn, L, B_KV_HEADS, HEAD_DIM)
        q = rope_partial(rms_norm(q, qk_gain[0]), pos)
        k = rope_partial(rms_norm(k, qk_gain[1]), pos)
        return q.reshape(Bn, L, B_KV_HEADS, B_REP, HEAD_DIM), k, v

    qp, kp, vp = project(hp, pos_p)
    qs, ks, vs = project(hs, pos_s)
    Bp, S = hp.shape[:2]
    kb, vb, kmean, n_sel = moba_blocks(kp, vp)
    nqc = S // B_QCHUNK
    qc = qp.reshape(Bp, nqc, B_QCHUNK, B_KV_HEADS, B_REP, HEAD_DIM).swapaxes(0, 1)
    tq = pos_p.reshape(nqc, B_QCHUNK)
    o_p = lax.map(lambda a: moba_attend(a[0], a[1], kb, vb, kmean, n_sel), (qc, tq))
    o_p = o_p.swapaxes(0, 1).reshape(Bp, S, nq)
    Bd, T = hs.shape[:2]
    past = pool[page_table].reshape(Bd, -1, 2, B_KV_HEADS, HEAD_DIM)
    k_all = jnp.concatenate([past[:, :, 0], ks], axis=1)
    v_all = jnp.concatenate([past[:, :, 1], vs], axis=1)
    kb_s, vb_s, kmean_s, n_sel_s = moba_blocks(k_all, v_all)
    o_s = moba_attend(qs, pos_s, kb_s, vb_s, kmean_s, n_sel_s).reshape(Bd, T, nq)
    out_p = o_p.astype(hp.dtype) @ w_out
    out_s = o_s.astype(hs.dtype) @ w_out
    return out_p, out_s, jnp.stack([kp, vp], axis=2), jnp.stack([ks, vs], axis=2)


def gla_chunked(q, k, v, la, s0):
    Bn, L, H, dk = q.shape
    dv = v.shape[-1]
    nc = -(-L // C_CHUNK)
    pad = nc * C_CHUNK - L

    def blk(t):
        t = jnp.pad(t.astype(jnp.float32), ((0, 0), (0, pad), (0, 0), (0, 0)))
        return t.reshape(Bn, nc, C_CHUNK, H, t.shape[-1])

    q, k, v, la = blk(q), blk(k), blk(v), blk(la)
    b = jnp.cumsum(la, axis=2)
    b_end = b[:, :, -1:]
    qe = q * jnp.exp(b)
    ke = k * jnp.exp(-b)
    kd = k * jnp.exp(b_end - b)
    causal = jnp.tril(jnp.ones((C_CHUNK, C_CHUNK), dtype=bool))
    att = jnp.where(causal, jnp.einsum('bnthk,bnshk->bnhts', qe, ke), 0.0)
    o_intra = jnp.einsum('bnhts,bnshv->bnthv', att, v)
    cs = jnp.einsum('bnshk,bnshv->bnhkv', kd, v)
    dec = jnp.exp(b_end[:, :, 0])

    def step(st, inp):
        d_c, cs_c = inp
        return d_c[..., None] * st + cs_c, st

    sT, s_in = lax.scan(step, s0.astype(jnp.float32), (dec.swapaxes(0, 1), cs.swapaxes(0, 1)))
    o_inter = jnp.einsum('bnthk,bnhkv->bnthv', qe, s_in.swapaxes(0, 1))
    o = (o_intra + o_inter).reshape(Bn, nc * C_CHUNK, H, dv)[:, :L]
    return o, sT


def mixer_c(hp, hs, state, w_in, w_gate2, b_gate, norm_g, w_out):
    def run(h, s0):
        Bn, L, _ = h.shape
        y = h @ w_in
        q = y[..., :C_KEY].reshape(Bn, L, C_HEADS, C_DK) * C_DK ** -0.5
        k = y[..., C_KEY:2 * C_KEY].reshape(Bn, L, C_HEADS, C_DK)
        v = y[..., 2 * C_KEY:2 * C_KEY + C_VAL].reshape(Bn, L, C_HEADS, C_DV)
        r = y[..., 2 * C_KEY + C_VAL:2 * C_KEY + 2 * C_VAL]
        glr = y[..., 2 * C_KEY + 2 * C_VAL:]
        la = jax.nn.log_sigmoid((glr @ w_gate2 + b_gate).astype(jnp.float32)) / C_TAU
        o, sT = gla_chunked(q, k, v, la.reshape(Bn, L, C_HEADS, C_DK), s0)
        o = rms_norm(o, norm_g).reshape(Bn, L, C_VAL).astype(h.dtype) * jax.nn.silu(r)
        return o @ w_out, sT

    out_p, s_p = run(hp, jnp.zeros((hp.shape[0], C_HEADS, C_DK, C_DV), jnp.float32))
    out_s, s_s = run(hs, state)
    return out_p, out_s, s_p.astype(hp.dtype), s_s.astype(hs.dtype)


def ssd_chunked(x, dt, A, Bm, Cm, h0):
    Bn, L = x.shape[:2]
    nc = -(-L // D_CHUNK)
    pad = nc * D_CHUNK - L

    def blk(t):
        t = jnp.pad(t.astype(jnp.float32), [(0, 0), (0, pad)] + [(0, 0)] * (t.ndim - 2))
        return t.reshape((Bn, nc, D_CHUNK) + t.shape[2:])

    x, dt, Bm, Cm = blk(x), blk(dt), blk(Bm), blk(Cm)
    cum = jnp.cumsum(dt * A, axis=2)
    seg = cum[:, :, :, None] - cum[:, :, None, :]
    causal = jnp.tril(jnp.ones((D_CHUNK, D_CHUNK), dtype=bool))
    lmat = jnp.exp(jnp.where(causal[:, :, None, None], seg, -jnp.inf))
    cb = jnp.einsum('bctgn,bcsgn->bctsg', Cm, Bm)
    y_diag = jnp.einsum('bctsgj,bcsgjp->bctgjp', cb[..., None] * lmat * dt[:, :, None], x)
    decay_end = jnp.exp(cum[:, :, -1:] - cum)
    states = jnp.einsum('bcsgn,bcsgj,bcsgjp->bcgjpn', Bm, decay_end * dt, x)
    chunk_dec = jnp.exp(cum[:, :, -1])

    def step(h, inp):
        d_c, st = inp
        return d_c[..., None, None] * h + st, h

    hT, h_in = lax.scan(step, h0.astype(jnp.float32), (chunk_dec.swapaxes(0, 1), states.swapaxes(0, 1)))
    y_off = jnp.einsum('bctgn,bcgjpn->bctgjp', Cm, h_in.swapaxes(0, 1)) * jnp.exp(cum)[..., None]
    y = (y_diag + y_off).reshape((Bn, nc * D_CHUNK) + x.shape[3:])[:, :L]
    return y, hT


def mixer_d(hp, hs, ssm_state, conv_state, w_in, conv_w, conv_b, dt_bias, a_log, d_skip, norm_g, w_out):
    A = -jnp.exp(a_log.astype(jnp.float32)).reshape(D_GROUPS, D_HPG)

    def run(h, h0, cbuf):
        Bn, L, _ = h.shape
        y = h @ w_in
        z = y[..., :D_INNER]
        xbc = y[..., D_INNER:D_INNER + D_XBC]
        dt_raw = y[..., D_INNER + D_XBC:]
        xpad = jnp.concatenate([cbuf.astype(xbc.dtype), xbc], axis=1)
        conv = conv_b
        for w in range(D_CONV):
            conv = conv + xpad[:, w:w + L] * conv_w[w]
        xbc = jax.nn.silu(conv)
        xs_ = xbc[..., :D_INNER].reshape(Bn, L, D_GROUPS, D_HPG, D_HEADDIM)
        Bm = xbc[..., D_INNER:D_INNER + D_GROUPS * D_STATE].reshape(Bn, L, D_GROUPS, D_STATE)
        Cm = xbc[..., D_INNER + D_GROUPS * D_STATE:].reshape(Bn, L, D_GROUPS, D_STATE)
        dt = jax.nn.softplus(dt_raw.astype(jnp.float32) + dt_bias).reshape(Bn, L, D_GROUPS, D_HPG)
        yssm, hT = ssd_chunked(xs_, dt, A, Bm, Cm, h0)
        yssm = yssm + d_skip.reshape(D_GROUPS, D_HPG)[..., None] * xs_.astype(jnp.float32)
        gated = yssm.reshape(Bn, L, D_INNER).astype(h.dtype) * jax.nn.silu(z)
        gated = rms_norm(gated.reshape(Bn, L, D_GROUPS, D_INNER // D_GROUPS),
                         norm_g.reshape(D_GROUPS, D_INNER // D_GROUPS)).reshape(Bn, L, D_INNER)
        return gated @ w_out, hT.reshape(Bn, D_HEADS, D_HEADDIM, D_STATE), xpad[:, L:]

    Bp = hp.shape[0]
    h0p = jnp.zeros((Bp, D_GROUPS, D_HPG, D_HEADDIM, D_STATE), jnp.float32)
    c0p = jnp.zeros((Bp, D_CONV - 1, D_XBC), hp.dtype)
    out_p, hp_T, cp_T = run(hp, h0p, c0p)
    h0s = ssm_state.reshape(hs.shape[0], D_GROUPS, D_HPG, D_HEADDIM, D_STATE)
    out_s, hs_T, cs_T = run(hs, h0s, conv_state)
    return out_p, out_s, hp_T.astype(hp.dtype), hs_T.astype(hs.dtype), cp_T, cs_T


def setup_inputs(seed: int = 0) -> dict:
    key = jax.random.key(seed)
    keys = iter(jax.random.split(key, 48))

    def nrm(shape, scale=1.0):
        return jax.random.normal(next(keys), shape, jnp.float32) * scale

    def gain(shape):
        return 1.0 + nrm(shape, 0.02)

    n_pages = PAST_LEN // PAGE_SIZE
    n_used = DEC_BATCH * n_pages
    n_pool = n_used + max(1, n_used // 4)
    perm = jax.random.permutation(next(keys), n_pool)
    page_table = perm[:n_used].reshape(DEC_BATCH, n_pages).astype(jnp.int32)
    dt0 = jnp.exp(jax.random.uniform(next(keys), (N_LAYERS_D, D_HEADS), jnp.float32,
                                     math.log(1e-3), math.log(1e-1)))
    dt_bias = dt0 + jnp.log(-jnp.expm1(-dt0))
    a_log = jnp.log(jax.random.uniform(next(keys), (N_LAYERS_D, D_HEADS), jnp.float32, 1.0, 16.0))
    return {
        'x_prompt': nrm((BATCH, SEQ, D_MODEL)),
        'x_sample': nrm((DEC_BATCH, DEC_SEQ, D_MODEL)),
        'cache_a_w1': nrm((N_LAYERS_A, DEC_BATCH, min(A_WINDOWS[0], PAST_LEN), 2, A_HEADS, HEAD_DIM)),
        'cache_a_w2': nrm((N_LAYERS_A, DEC_BATCH, min(A_WINDOWS[1], PAST_LEN), 2, A_HEADS, HEAD_DIM)),
        'cache_a_w3': nrm((N_LAYERS_A, DEC_BATCH, min(A_WINDOWS[2], PAST_LEN), 2, A_HEADS, HEAD_DIM)),
        'cache_b_kv': nrm((N_LAYERS_B, n_pool, PAGE_SIZE, 2, B_KV_HEADS, HEAD_DIM)),
        'page_table': page_table,
        'state_c': nrm((N_LAYERS_C, DEC_BATCH, C_HEADS, C_DK, C_DV)),
        'state_d_ssm': nrm((N_LAYERS_D, DEC_BATCH, D_HEADS, D_HEADDIM, D_STATE), 0.5),
        'state_d_conv': nrm((N_LAYERS_D, DEC_BATCH, D_CONV - 1, D_XBC)),
        'norm_gain': gain((DEPTH, 3, D_MODEL)),
        'w_ffn_up': nrm((DEPTH, 2, D_MODEL, 2 * D_FF), D_MODEL ** -0.5),
        'w_ffn_down': nrm((DEPTH, 2, D_FF, D_MODEL), OUT_SCALE * D_FF ** -0.5),
        'w_a_in': nrm((N_LAYERS_A, D_MODEL, A_GROUPS * 3 * A_WIDTH), D_MODEL ** -0.5),
        'a_qk_gain': gain((N_LAYERS_A, 2, HEAD_DIM)),
        'w_a_out': nrm((N_LAYERS_A, A_WIDTH, D_MODEL), OUT_SCALE * A_WIDTH ** -0.5),
        'w_b_in': nrm((N_LAYERS_B, D_MODEL, (B_HEADS + 2 * B_KV_HEADS) * HEAD_DIM), D_MODEL ** -0.5),
        'b_qk_gain': gain((N_LAYERS_B, 2, HEAD_DIM)),
        'w_b_out': nrm((N_LAYERS_B, B_HEADS * HEAD_DIM, D_MODEL), OUT_SCALE * (B_HEADS * HEAD_DIM) ** -0.5),
        'w_c_in': nrm((N_LAYERS_C, D_MODEL, 2 * C_KEY + 2 * C_VAL + C_GATE_RANK), D_MODEL ** -0.5),
        'w_c_gate2': nrm((N_LAYERS_C, C_GATE_RANK, C_KEY), C_GATE_RANK ** -0.5),
        'b_c_gate': nrm((N_LAYERS_C, C_KEY), 0.1),
        'c_norm_gain': gain((N_LAYERS_C, C_DV)),
        'w_c_out': nrm((N_LAYERS_C, C_VAL, D_MODEL), OUT_SCALE * C_VAL ** -0.5),
        'w_d_in': nrm((N_LAYERS_D, D_MODEL, D_INNER + D_XBC + D_HEADS), D_MODEL ** -0.5),
        'd_conv_w': nrm((N_LAYERS_D, D_CONV, D_XBC), D_CONV ** -0.5),
        'd_conv_b': nrm((N_LAYERS_D, D_XBC), 0.02),
        'd_dt_bias': dt_bias,
        'd_a_log': a_log,
        'd_skip': 1.0 + nrm((N_LAYERS_D, D_HEADS), 0.1),
        'd_norm_gain': gain((N_LAYERS_D, D_INNER)),
        'w_d_out': nrm((N_LAYERS_D, D_INNER, D_MODEL), OUT_SCALE * D_INNER ** -0.5),
    }


def reference(x_prompt, x_sample, cache_a_w1, cache_a_w2, cache_a_w3, cache_b_kv, page_table,
              state_c, state_d_ssm, state_d_conv, norm_gain, w_ffn_up, w_ffn_down,
              w_a_in, a_qk_gain, w_a_out, w_b_in, b_qk_gain, w_b_out,
              w_c_in, w_c_gate2, b_c_gate, c_norm_gain, w_c_out,
              w_d_in, d_conv_w, d_conv_b, d_dt_bias, d_a_log, d_skip, d_norm_gain, w_d_out):
    S = x_prompt.shape[1]
    T = x_sample.shape[1]
    past_len = page_table.shape[1] * cache_b_kv.shape[2]
    pos_p = jnp.arange(S, dtype=jnp.int32)
    pos_s = past_len + jnp.arange(T, dtype=jnp.int32)
    a_p = ([], [], [])
    a_s = ([], [], [])
    b_p, b_s, c_p, c_s, dh_p, dh_s, dc_p, dc_s = [], [], [], [], [], [], [], []
    xp, xs = x_prompt, x_sample
    for i in range(DEPTH):
        m, j = i % N_MIXERS, i // N_MIXERS
        g = norm_gain[i]
        xp = xp + 0.5 * swiglu(rms_norm(xp, g[0]), w_ffn_up[i, 0], w_ffn_down[i, 0])
        xs = xs + 0.5 * swiglu(rms_norm(xs, g[0]), w_ffn_up[i, 0], w_ffn_down[i, 0])
        hp, hs = rms_norm(xp, g[1]), rms_norm(xs, g[1])
        if m == 0:
            op, osm, np_, ns_ = mixer_a(hp, hs, pos_p, pos_s,
                                        (cache_a_w1[j], cache_a_w2[j], cache_a_w3[j]),
                                        w_a_in[j], a_qk_gain[j], w_a_out[j])
            for gi in range(A_GROUPS):
                a_p[gi].append(np_[gi])
                a_s[gi].append(ns_[gi])
        elif m == 1:
            op, osm, kvp, kvs = mixer_b(hp, hs, pos_p, pos_s, cache_b_kv[j], page_table,
                                        w_b_in[j], b_qk_gain[j], w_b_out[j])
            b_p.append(kvp)
            b_s.append(kvs)
        elif m == 2:
            op, osm, sp_, ss_ = mixer_c(hp, hs, state_c[j], w_c_in[j], w_c_gate2[j], b_c_gate[j],
                                        c_norm_gain[j], w_c_out[j])
            c_p.append(sp_)
            c_s.append(ss_)
        else:
            op, osm, hpT, hsT, cpT, csT = mixer_d(hp, hs, state_d_ssm[j], state_d_conv[j], w_d_in[j],
                                                  d_conv_w[j], d_conv_b[j], d_dt_bias[j], d_a_log[j],
                                                  d_skip[j], d_norm_gain[j], w_d_out[j])
            dh_p.append(hpT)
            dh_s.append(hsT)
            dc_p.append(cpT)
            dc_s.append(csT)
        xp = xp + op
        xs = xs + osm
        xp = xp + 0.5 * swiglu(rms_norm(xp, g[2]), w_ffn_up[i, 1], w_ffn_down[i, 1])
        xs = xs + 0.5 * swiglu(rms_norm(xs, g[2]), w_ffn_up[i, 1], w_ffn_down[i, 1])
    return (xp, xs,
            jnp.stack(a_p[0]), jnp.stack(a_s[0]), jnp.stack(a_p[1]), jnp.stack(a_s[1]),
            jnp.stack(a_p[2]), jnp.stack(a_s[2]),
            jnp.stack(b_p), jnp.stack(b_s), jnp.stack(c_p), jnp.stack(c_s),
            jnp.stack(dh_p), jnp.stack(dh_s), jnp.stack(dc_p), jnp.stack(dc_s))
```

```python
import functools
import math

import numpy as np
import jax
import jax.numpy as jnp
from jax import lax
from jax.experimental import pallas as pl
from jax.experimental.pallas import tpu as pltpu

F32 = jnp.float32
BF16 = jnp.bfloat16

D_MODEL = 1024
HEAD_DIM = 64
ROPE_DIM = HEAD_DIM // 4
ROPE_THETA = 500000.0
EPS = 1e-6
D_FF = 2816
NEG = -1e30

A_GROUPS = 3
A_DILATIONS = (1, 4, 16)
A_HEADS = 8
A_WIDTH = A_HEADS * HEAD_DIM
A_SPAN = 128

B_HEADS = 16
B_KV_HEADS = 4
B_REP = B_HEADS // B_KV_HEADS
B_BLOCK = 256
B_TOPK = 3

C_HEADS = 4
C_KEY = 512
C_VAL = 1024
C_DK = 128
C_DV = 256
C_RANK = 16
C_TAU = 16.0
C_CHUNK = 32
C_NPAD = 3200

D_INNER = 2048
D_HEADDIM = 64
D_HEADS = 32
D_GROUPS = 4
D_HPG = 8
D_STATE = 128
D_CONV = 4
D_XBC = D_INNER + 2 * D_GROUPS * D_STATE
D_NPAD = 5376
D_DT_COL = (D_INNER + D_XBC) // 128

LANES = 128
FF_TILE = 256
SEQ_TILE = 128


def _params(*sem):
    return pltpu.CompilerParams(dimension_semantics=sem)


def _split2(a):
    hi = a.astype(BF16)
    lo = (a - hi.astype(F32)).astype(BF16)
    return hi, lo


def _split3(a):
    hi = a.astype(BF16)
    r = a - hi.astype(F32)
    mid = r.astype(BF16)
    lo = (r - mid.astype(F32)).astype(BF16)
    return hi, mid, lo


def _dot(a, b):
    return jnp.dot(a, b, preferred_element_type=F32)


def _dot_nt(a, b):
    return lax.dot_general(a, b, (((1,), (1,)), ((), ())), preferred_element_type=F32)


def _sel_right(a, sel01, terms=2):
    parts = _split2(a) if terms == 2 else _split3(a)
    out = None
    for p in reversed(parts):
        d = _dot(p, sel01)
        out = d if out is None else out + d
    return out


def _sel_left(sel01, a, terms=2):
    parts = _split2(a) if terms == 2 else _split3(a)
    out = None
    for p in reversed(parts):
        d = _dot(sel01, p)
        out = d if out is None else out + d
    return out


def _silu(x):
    return x * jax.nn.sigmoid(x)


def _softplus(x):
    return jnp.maximum(x, 0.0) + jnp.log(1.0 + jnp.exp(-jnp.abs(x)))


def _rms_rows(x, g):
    ms = jnp.mean(x * x, axis=-1, keepdims=True)
    return x * lax.rsqrt(ms + EPS) * g


def _ffn_kernel(x_ref, g_ref, wg_ref, wu_ref, wd_ref, o_ref, h_ref, acc_ref):
    j = pl.program_id(1)

    @pl.when(j == 0)
    def _():
        h_ref[...] = _rms_rows(x_ref[...], g_ref[...]).astype(BF16)
        acc_ref[...] = jnp.zeros_like(acc_ref)

    h = h_ref[...]
    a = _dot(h, wg_ref[...])
    u = _dot(h, wu_ref[...])
    act = (_silu(a) * u).astype(BF16)
    acc_ref[...] += _dot(act, wd_ref[...])

    @pl.when(j == pl.num_programs(1) - 1)
    def _():
        o_ref[...] = x_ref[...] + 0.5 * acc_ref[...]


def _ffn(x, gain, w_up, w_down, tm):
    T = x.shape[0]
    nj = D_FF // FF_TILE
    return pl.pallas_call(
        _ffn_kernel,
        grid=(T // tm, nj),
        in_specs=[
            pl.BlockSpec((tm, D_MODEL), lambda i, j: (i, 0)),
            pl.BlockSpec((1, D_MODEL), lambda i, j: (0, 0)),
            pl.BlockSpec((D_MODEL, FF_TILE), lambda i, j: (0, j)),
            pl.BlockSpec((D_MODEL, FF_TILE), lambda i, j: (0, j + nj)),
            pl.BlockSpec((FF_TILE, D_MODEL), lambda i, j: (j, 0)),
        ],
        out_specs=pl.BlockSpec((tm, D_MODEL), lambda i, j: (i, 0)),
        out_shape=jax.ShapeDtypeStruct((T, D_MODEL), F32),
        scratch_shapes=[pltpu.VMEM((tm, D_MODEL), BF16), pltpu.VMEM((tm, D_MODEL), F32)],
        compiler_params=_params("parallel", "arbitrary"),
        name="ffn",
    )(x, gain.reshape(1, D_MODEL), w_up, w_up, w_down)


def _proj_kernel(x_ref, g_ref, w_ref, o_ref, h_ref):
    @pl.when(pl.program_id(1) == 0)
    def _():
        h_ref[...] = _rms_rows(x_ref[...], g_ref[...]).astype(BF16)

    o_ref[...] = _dot(h_ref[...], w_ref[...])


def _proj(x, gain, w, tm, tn):
    T = x.shape[0]
    N = w.shape[1]
    return pl.pallas_call(
        _proj_kernel,
        grid=(T // tm, N // tn),
        in_specs=[
            pl.BlockSpec((tm, D_MODEL), lambda i, j: (i, 0)),
            pl.BlockSpec((1, D_MODEL), lambda i, j: (0, 0)),
            pl.BlockSpec((D_MODEL, tn), lambda i, j: (0, j)),
        ],
        out_specs=pl.BlockSpec((tm, tn), lambda i, j: (i, j)),
        out_shape=jax.ShapeDtypeStruct((T, N), F32),
        scratch_shapes=[pltpu.VMEM((tm, D_MODEL), BF16)],
        compiler_params=_params("parallel", "arbitrary"),
        name="proj",
    )(x, gain.reshape(1, D_MODEL), w)


def _proj_qk_kernel(flag_ref, x_ref, g_ref, w_ref, bd_ref, gain_ref, mask_ref,
                    cos_ref, sa_ref, sb_ref, o_ref, h_ref):
    j = pl.program_id(1)

    @pl.when(j == 0)
    def _():
        h_ref[...] = _rms_rows(x_ref[...], g_ref[...]).astype(BF16)

    y = _dot(h_ref[...], w_ref[...])
    tn = y.shape[1]

    @pl.when(flag_ref[j] == 0)
    def _():
        o_ref[...] = y

    @pl.when(flag_ref[j] != 0)
    def _():
        ss = _dot((y * y).astype(BF16), bd_ref[...])
        yn = y * lax.rsqrt(ss * (1.0 / HEAD_DIM) + EPS) * gain_ref[0]
        cos, sa, sb = cos_ref[...], sa_ref[...], sb_ref[...]
        mask = mask_ref[0]
        for c in range(tn // LANES):
            sl = slice(c * LANES, (c + 1) * LANES)
            v = yn[:, sl]
            up = pltpu.roll(v, LANES - ROPE_DIM // 2, 1)
            dn = pltpu.roll(v, ROPE_DIM // 2, 1)
            rot = v * cos + up * sa + dn * sb
            o_ref[:, sl] = jnp.where(mask[:, sl] > 0.0, rot, y[:, sl])


def _proj_qk(x, gain, w, flags, gain_rows, mask_rows, tables, tm, tn):
    T = x.shape[0]
    N = w.shape[1]
    nj = N // tn
    cos, sa, sb = tables
    n_pos_tiles = cos.shape[0] // tm
    bd = jnp.asarray(np.kron(np.eye(tn // HEAD_DIM), np.ones((HEAD_DIM, HEAD_DIM))), BF16)
    tab_spec = pl.BlockSpec((tm, LANES), lambda i, j, f: (i % n_pos_tiles, 0))
    return pl.pallas_call(
        _proj_qk_kernel,
        grid_spec=pltpu.PrefetchScalarGridSpec(
            num_scalar_prefetch=1,
            grid=(T // tm, nj),
            in_specs=[
                pl.BlockSpec((tm, D_MODEL), lambda i, j, f: (i, 0)),
                pl.BlockSpec((1, D_MODEL), lambda i, j, f: (0, 0)),
                pl.BlockSpec((D_MODEL, tn), lambda i, j, f: (0, j)),
                pl.BlockSpec((tn, tn), lambda i, j, f: (0, 0)),
                pl.BlockSpec((1, 1, tn), lambda i, j, f: (j, 0, 0)),
                pl.BlockSpec((1, 1, tn), lambda i, j, f: (j, 0, 0)),
                tab_spec, tab_spec, tab_spec,
            ],
            out_specs=pl.BlockSpec((tm, tn), lambda i, j, f: (i, j)),
            scratch_shapes=[pltpu.VMEM((tm, D_MODEL), BF16)],
        ),
        out_shape=jax.ShapeDtypeStruct((T, N), F32),
        compiler_params=_params("parallel", "arbitrary"),
        name="proj_qk",
    )(flags, x, gain.reshape(1, D_MODEL), w, bd, gain_rows, mask_rows, cos, sa, sb)


def _rope_tables(pos):
    half = ROPE_DIM // 2
    inv = ROPE_THETA ** (-jnp.arange(half, dtype=F32) / half)
    ang = pos.astype(F32)[:, None] * inv[None, :]
    cos, sin = jnp.cos(ang), jnp.sin(ang)
    n = pos.shape[0]
    one = jnp.ones((n, HEAD_DIM - ROPE_DIM), F32)
    zero = jnp.zeros((n, HEAD_DIM - ROPE_DIM), F32)
    zh = jnp.zeros((n, half), F32)
    c64 = jnp.concatenate([cos, cos, one], axis=1)
    sa64 = jnp.concatenate([-sin, zh, zero], axis=1)
    sb64 = jnp.concatenate([zh, sin, zero], axis=1)
    rep = LANES // HEAD_DIM
    return tuple(jnp.tile(t, (1, rep)) for t in (c64, sa64, sb64))


def _out_plain_kernel(x_ref, o_ref, w_ref, y_ref):
    y_ref[...] = x_ref[...] + _dot(o_ref[...].astype(BF16), w_ref[...])


def _out_a_kernel(x_ref, o0, o1, o2, l0, l1, l2, w_ref, y_ref):
    la, lb, lc = l0[...], l1[...], l2[...]
    m = jnp.maximum(jnp.maximum(la, lb), lc)
    ea, eb, ec = jnp.exp(la - m), jnp.exp(lb - m), jnp.exp(lc - m)
    o = (ea * o0[...] + eb * o1[...] + ec * o2[...]) / (ea + eb + ec)
    y_ref[...] = x_ref[...] + _dot(o.astype(BF16), w_ref[...])


def _out_c_kernel(x_ref, o_ref, r_ref, g_ref, w_ref, y_ref):
    o = o_ref[...]
    g = g_ref[...]
    parts = [_rms_rows(o[:, h * C_DV:(h + 1) * C_DV], g) for h in range(C_HEADS)]
    on = jnp.concatenate(parts, axis=1) * _silu(r_ref[...])
    y_ref[...] = x_ref[...] + _dot(on.astype(BF16), w_ref[...])


def _out_d_kernel(x_ref, o_ref, z_ref, g_ref, w_ref, y_ref):
    gated = o_ref[...] * _silu(z_ref[...])
    gw = D_INNER // D_GROUPS
    g = g_ref[...]
    parts = [_rms_rows(gated[:, k * gw:(k + 1) * gw], g[:, k * gw:(k + 1) * gw]) for k in range(D_GROUPS)]
    y_ref[...] = x_ref[...] + _dot(jnp.concatenate(parts, axis=1).astype(BF16), w_ref[...])


def _row_call(kernel, name, x, row_inputs, const_inputs, tm):
    T = x.shape[0]
    in_specs = [pl.BlockSpec((tm, D_MODEL), lambda i: (i, 0))]
    args = [x]
    for arr, width, cb in row_inputs:
        in_specs.append(pl.BlockSpec((tm, width), lambda i, cb=cb: (i, cb)))
        args.append(arr)
    for arr in const_inputs:
        in_specs.append(pl.BlockSpec(arr.shape, lambda i, nd=arr.ndim: (0,) * nd))
        args.append(arr)
    return pl.pallas_call(
        kernel,
        grid=(T // tm,),
        in_specs=in_specs,
        out_specs=pl.BlockSpec((tm, D_MODEL), lambda i: (i, 0)),
        out_shape=jax.ShapeDtypeStruct((T, D_MODEL), F32),
        compiler_params=_params("parallel"),
        name=name,
    )(*args)


def _a_prompt_kernel(q_ref, kp_ref, kc_ref, vp_ref, vc_ref, o_ref, l_ref):
    n = pl.program_id(2)
    q = q_ref[0] * (HEAD_DIM ** -0.5)
    k = jnp.concatenate([kp_ref[0], kc_ref[0]], axis=0)
    v = jnp.concatenate([vp_ref[0], vc_ref[0]], axis=0)
    qq = lax.broadcasted_iota(jnp.int32, (A_SPAN, 2 * A_SPAN), 0)
    kk = lax.broadcasted_iota(jnp.int32, (A_SPAN, 2 * A_SPAN), 1)
    rel = qq + A_SPAN - kk
    valid = (rel >= 0) & (rel <= A_SPAN) & ((kk >= A_SPAN) | (n > 0))
    for h in range(A_HEADS):
        sl = slice(h * HEAD_DIM, (h + 1) * HEAD_DIM)
        s = _dot_nt(q[:, sl].astype(BF16), k[:, sl].astype(BF16))
        s = jnp.where(valid, s, -jnp.inf)
        m = jnp.max(s, axis=-1, keepdims=True)
        p = jnp.exp(s - m)
        den = jnp.sum(p, axis=-1, keepdims=True)
        o = _dot(p.astype(BF16), v[:, sl].astype(BF16)) / den
        o_ref[0, :, sl] = o
        l_ref[0, :, sl] = jnp.broadcast_to(m + jnp.log(den), (A_SPAN, HEAD_DIM))


def _a_prompt(y, g, Bn, S):
    d = A_DILATIONS[g]
    n = S // d
    nb = n // A_SPAN
    width = A_GROUPS * 3 * A_WIDTH
    yv = y.reshape(Bn, n, d * width)
    cpr = width // A_WIDTH
    base = 3 * g

    def spec(off, prev):
        if prev:
            return pl.BlockSpec((1, A_SPAN, A_WIDTH),
                                lambda b, r, i: (b, jnp.maximum(i - 1, 0), r * cpr + base + off))
        return pl.BlockSpec((1, A_SPAN, A_WIDTH), lambda b, r, i: (b, i, r * cpr + base + off))

    out_spec = pl.BlockSpec((1, A_SPAN, A_WIDTH), lambda b, r, i: (b, i, r))
    o, l = pl.pallas_call(
        _a_prompt_kernel,
        grid=(Bn, d, nb),
        in_specs=[spec(0, False), spec(1, True), spec(1, False), spec(2, True), spec(2, False)],
        out_specs=[out_spec, out_spec],
        out_shape=[jax.ShapeDtypeStruct((Bn, n, d * A_WIDTH), F32)] * 2,
        compiler_params=_params("parallel", "parallel", "arbitrary"),
        name="a_prompt_d%d" % d,
    )(yv, yv, yv, yv, yv)
    return o.reshape(Bn * S, A_WIDTH), l.reshape(Bn * S, A_WIDTH)


def _a_sample_kernel(T, y_ref, c1_ref, c2_ref, c3_ref, ind_ref, indt_ref, *out_refs):
    y = y_ref[0]
    ind = ind_ref[...]
    indt = indt_ref[...]
    caches = (c1_ref, c2_ref, c3_ref)
    row = lax.broadcasted_iota(jnp.int32, (A_SPAN, LANES), 0)
    rown = lax.broadcasted_iota(jnp.int32, (T, LANES), 0)
    width = 3 * A_WIDTH
    for g in range(A_GROUPS):
        d = A_DILATIONS[g]
        qg = y[:, g * width:g * width + A_WIDTH] * (HEAD_DIM ** -0.5)
        kn = y[:, g * width + A_WIDTH:g * width + 2 * A_WIDTH]
        vn = y[:, g * width + 2 * A_WIDTH:(g + 1) * width]
        for i in range(T):
            off = 0 if d == 1 else i * 2 * A_WIDTH
            kc = caches[g][0, :, off:off + A_WIDTH]
            vc = caches[g][0, :, off + A_WIDTH:off + 2 * A_WIDTH]
            qi = qg[i:i + 1, :]
            sc = _sel_right(kc * qi, ind)
            sn = _sel_right(kn * qi, ind)
            if d == 1:
                sc = jnp.where(row >= i, sc, -jnp.inf)
                sn = jnp.where(rown <= i, sn, -jnp.inf)
            else:
                sn = jnp.where(rown == i, sn, -jnp.inf)
            m = jnp.maximum(jnp.max(sc, axis=0, keepdims=True), jnp.max(sn, axis=0, keepdims=True))
            pc = jnp.exp(sc - m)
            pn = jnp.exp(sn - m)
            den = jnp.sum(pc, axis=0, keepdims=True) + jnp.sum(pn, axis=0, keepdims=True)
            acc = (jnp.sum(_sel_right(pc, indt) * vc, axis=0, keepdims=True)
                   + jnp.sum(_sel_right(pn, indt) * vn, axis=0, keepdims=True))
            stats = jnp.concatenate([den, m + jnp.log(den)], axis=0)
            stats = _sel_right(stats, indt, terms=3)
            out_refs[g][0, i:i + 1, :] = acc / stats[0:1, :]
            out_refs[A_GROUPS + g][0, i:i + 1, :] = stats[1:2, :]


def _a_sample(y, caches, Bd, T):
    width = A_GROUPS * 3 * A_WIDTH
    row = 2 * A_WIDTH
    views = []
    specs = [pl.BlockSpec((1, T, width), lambda b: (b, 0, 0))]
    for g, c in enumerate(caches):
        d = A_DILATIONS[g]
        assert c.shape[1] == A_SPAN * d and (d == 1 or T <= d)
        views.append(c.reshape(Bd, A_SPAN, d * row))
        cols = row if d == 1 else T * row
        specs.append(pl.BlockSpec((1, A_SPAN, cols), lambda b: (b, 0, 0)))
    ind_np = np.zeros((A_WIDTH, LANES), np.float32)
    ind_np[np.arange(A_WIDTH), np.arange(A_WIDTH) // HEAD_DIM] = 1.0
    ind = jnp.asarray(ind_np, BF16)
    indt = jnp.asarray(ind_np.T, BF16)
    specs += [pl.BlockSpec(ind.shape, lambda b: (0, 0)), pl.BlockSpec(indt.shape, lambda b: (0, 0))]
    out_spec = pl.BlockSpec((1, T, A_WIDTH), lambda b: (b, 0, 0))
    outs = pl.pallas_call(
        functools.partial(_a_sample_kernel, T),
        grid=(Bd,),
        in_specs=specs,
        out_specs=[out_spec] * (2 * A_GROUPS),
        out_shape=[jax.ShapeDtypeStruct((Bd, T, A_WIDTH), F32)] * (2 * A_GROUPS),
        compiler_params=_params("parallel"),
        name="a_sample",
    )(y.reshape(Bd, T, width), *views, ind, indt)
    return [o.reshape(Bd * T, A_WIDTH) for o in outs]


def _mixer_a(xp, xs, gain, w_in, qk_gain, w_out, caches, tabs_p, tabs_s, Bn, S, Bd, T):
    flags = jnp.asarray([1, 1, 0] * A_GROUPS, jnp.int32)
    ones = jnp.ones((A_WIDTH,), F32)
    rows = [jnp.tile(qk_gain[0], A_HEADS), jnp.tile(qk_gain[1], A_HEADS), ones] * A_GROUPS
    gain_rows = jnp.stack(rows).reshape(3 * A_GROUPS, 1, A_WIDTH)
    mask_rows = jnp.stack([ones, ones, 0.0 * ones] * A_GROUPS).reshape(3 * A_GROUPS, 1, A_WIDTH)
    w = w_in.astype(BF16)
    yp = _proj_qk(xp, gain, w, flags, gain_rows, mask_rows, tabs_p, 1024, A_WIDTH)
    ys = _proj_qk(xs, gain, w, flags, gain_rows, mask_rows, tabs_s, xs.shape[0], A_WIDTH)
    wo = w_out.astype(BF16)
    pr = [_a_prompt(yp, g, Bn, S) for g in range(A_GROUPS)]
    xp = _row_call(_out_a_kernel, "out_a", xp,
                   [(o, A_WIDTH, 0) for o, _ in pr] + [(l, A_WIDTH, 0) for _, l in pr], [wo], 512)
    sr = _a_sample(ys, caches, Bd, T)
    xs = _row_call(_out_a_kernel, "out_a", xs, [(o, A_WIDTH, 0) for o in sr], [wo], xs.shape[0])
    yp6 = yp.reshape(Bn, S, A_GROUPS, 3, A_HEADS, HEAD_DIM)
    ys6 = ys.reshape(Bd, T, A_GROUPS, 3, A_HEADS, HEAD_DIM)
    new_p, new_s = [], []
    for g in range(A_GROUPS):
        win = min(A_SPAN * A_DILATIONS[g], S)
        new_p.append(yp6[:, S - win:, g, 1:3])
        new_s.append(ys6[:, :, g, 1:3])
    return xp, xs, new_p, new_s


def _b_kmean_kernel(k_ref, o_ref):
    nblk = k_ref.shape[1] // B_BLOCK
    rows = [jnp.mean(k_ref[0, n * B_BLOCK:(n + 1) * B_BLOCK, :], axis=0, keepdims=True) for n in range(nblk)]
    o_ref[0] = jnp.concatenate(rows, axis=0)


def _seg_reduce(x, op, lane, width):
    s = 1
    while s < width:
        up = pltpu.roll(x, s, 1)
        dn = pltpu.roll(x, LANES - s, 1)
        x = op(x, jnp.where((lane & s) != 0, up, dn))
        s *= 2
    return x


def _b_gate_kernel(q_ref, kmh_ref, kml_ref, o_ref):
    i = pl.program_id(1)
    q = q_ref[0]
    qh, ql = _split2(q)
    kmh, kml = kmh_ref[0], kml_ref[0]
    gate = _dot(qh, kmh) + (_dot(qh, kml) + _dot(ql, kmh))
    tq = q.shape[0]
    nblk = gate.shape[1] // B_HEADS
    lane = lax.broadcasted_iota(jnp.int32, (tq, LANES), 1)
    blk = lane & (nblk - 1)
    blkf = blk.astype(F32)
    for c in range(gate.shape[1] // LANES):
        gch = jnp.where(blk < i, gate[:, c * LANES:(c + 1) * LANES], -jnp.inf)
        sel = jnp.zeros((tq, LANES), jnp.bool_)
        for _ in range(B_TOPK):
            mx = _seg_reduce(gch, jnp.maximum, lane, nblk)
            first = _seg_reduce(jnp.where(gch == mx, blkf, float(nblk)), jnp.minimum, lane, nblk)
            hit = blkf == first
            sel = sel | (hit & (mx > -jnp.inf))
            gch = jnp.where(hit, -jnp.inf, gch)
        o_ref[0, :, c * LANES:(c + 1) * LANES] = jnp.where(sel | (blk >= i), 0.0, NEG)


def _b_attn_kernel(qi_ref, kn_ref, q_ref, k_ref, v_ref, o_ref, m_ref, l_ref, acc_ref):
    t = pl.program_id(1)
    i, n = qi_ref[t], kn_ref[t]
    tq = B_BLOCK

    @pl.when(n == 0)
    def _():
        m_ref[...] = jnp.full_like(m_ref, -jnp.inf)
        l_ref[...] = jnp.zeros_like(l_ref)
        acc_ref[...] = jnp.zeros_like(acc_ref)

    qq = lax.broadcasted_iota(jnp.int32, (tq, B_BLOCK), 0)
    kk = lax.broadcasted_iota(jnp.int32, (tq, B_BLOCK), 1)
    keep = (kk <= qq) | (n < i)
    for kvh in range(B_KV_HEADS):
        k = k_ref[0, :, kvh * LANES:(kvh + 1) * LANES]
        v = v_ref[0, :, kvh * LANES:(kvh + 1) * LANES]
        for r in range(B_REP):
            rows = slice(r * tq, (r + 1) * tq)
            s = _dot_nt(q_ref[0, 0, kvh, rows, :], k)
            s = jnp.where(keep, s, NEG)
            m_old = m_ref[kvh, rows, :]
            m_new = jnp.maximum(m_old, jnp.max(s, axis=-1, keepdims=True))
            alpha = jnp.exp(m_old - m_new)
            p = jnp.exp(s - m_new)
            l_ref[kvh, rows, :] = alpha * l_ref[kvh, rows, :] + jnp.sum(p, axis=-1, keepdims=True)
            acc_ref[kvh, rows, :] = alpha * acc_ref[kvh, rows, :] + _dot(p.astype(BF16), v)
            m_ref[kvh, rows, :] = m_new

    @pl.when(n == i)
    def _():
        for kvh in range(B_KV_HEADS):
            for r in range(B_REP):
                rows = slice(r * tq, (r + 1) * tq)
                h = kvh * B_REP + r
                o_ref[0, :, h * HEAD_DIM:(h + 1) * HEAD_DIM] = (
                    acc_ref[kvh, rows, :HEAD_DIM] / l_ref[kvh, rows, :])


def _b_prompt(y, Bn, S):
    nblk = S // B_BLOCK
    assert nblk & (nblk - 1) == 0 and LANES % nblk == 0
    nq = B_HEADS * HEAD_DIM
    nk = B_KV_HEADS * HEAD_DIM
    y3 = y.reshape(Bn, S, nq + 2 * nk)
    kmean = pl.pallas_call(
        _b_kmean_kernel,
        grid=(Bn,),
        in_specs=[pl.BlockSpec((1, S, nk), lambda b: (b, 0, nq // nk))],
        out_specs=pl.BlockSpec((1, nblk, nk), lambda b: (b, 0, 0)),
        out_shape=jax.ShapeDtypeStruct((Bn, nblk, nk), F32),
        compiler_params=_params("parallel"),
        name="b_kmean",
    )(y3)
    km = kmean.reshape(Bn, nblk, B_KV_HEADS, HEAD_DIM)
    km = jnp.repeat(km, B_REP, axis=2)
    eye = jnp.eye(B_HEADS, dtype=F32)
    kmbd = jnp.einsum('bnhe,hg->bhegn', km, eye).reshape(Bn, nq, B_HEADS * nblk)
    kmh = kmbd.astype(BF16)
    kml = (kmbd - kmh.astype(F32)).astype(BF16)
    bias = pl.pallas_call(
        _b_gate_kernel,
        grid=(Bn, nblk),
        in_specs=[
            pl.BlockSpec((1, B_BLOCK, nq), lambda b, i: (b, i, 0)),
            pl.BlockSpec((1, nq, B_HEADS * nblk), lambda b, i: (b, 0, 0)),
            pl.BlockSpec((1, nq, B_HEADS * nblk), lambda b, i: (b, 0, 0)),
        ],
        out_specs=pl.BlockSpec((1, B_BLOCK, B_HEADS * nblk), lambda b, i: (b, i, 0)),
        out_shape=jax.ShapeDtypeStruct((Bn, S, B_HEADS * nblk), F32),
        compiler_params=_params("parallel", "arbitrary"),
        name="b_gate",
    )(y3, kmh, kml)
    q = (y3[:, :, :nq] * (HEAD_DIM ** -0.5)).astype(BF16).reshape(Bn, S, B_HEADS, HEAD_DIM)
    pad = LANES - HEAD_DIM - nblk
    q_aug = jnp.concatenate([q, bias.astype(BF16).reshape(Bn, S, B_HEADS, nblk),
                             jnp.zeros((Bn, S, B_HEADS, pad), BF16)], axis=-1)
    q_aug = q_aug.reshape(Bn, nblk, B_BLOCK, B_KV_HEADS, B_REP, LANES).transpose(0, 1, 3, 4, 2, 5)
    q_aug = q_aug.reshape(Bn, nblk, B_KV_HEADS, B_REP * B_BLOCK, LANES)
    k = y3[:, :, nq:nq + nk].astype(BF16).reshape(Bn, S, B_KV_HEADS, HEAD_DIM)
    onehot = jax.nn.one_hot(jnp.arange(S) // B_BLOCK, nblk, dtype=BF16)
    onehot = jnp.broadcast_to(onehot[None, :, None, :], (Bn, S, B_KV_HEADS, nblk))
    k_aug = jnp.concatenate([k, onehot, jnp.zeros((Bn, S, B_KV_HEADS, pad), BF16)], axis=-1)
    k_aug = k_aug.reshape(Bn, S, B_KV_HEADS * LANES)
    v = y3[:, :, nq + nk:].astype(BF16).reshape(Bn, S, B_KV_HEADS, HEAD_DIM)
    v_pad = jnp.concatenate([v, jnp.zeros_like(v)], axis=-1).reshape(Bn, S, B_KV_HEADS * LANES)
    pairs = [(i, n) for i in range(nblk) for n in range(i + 1)]
    qi = jnp.asarray([p[0] for p in pairs], jnp.int32)
    kn = jnp.asarray([p[1] for p in pairs], jnp.int32)
    o = pl.pallas_call(
        _b_attn_kernel,
        grid_spec=pltpu.PrefetchScalarGridSpec(
            num_scalar_prefetch=2,
            grid=(Bn, len(pairs)),
            in_specs=[
                pl.BlockSpec((1, 1, B_KV_HEADS, B_REP * B_BLOCK, LANES),
                             lambda b, t, qi, kn: (b, qi[t], 0, 0, 0)),
                pl.BlockSpec((1, B_BLOCK, B_KV_HEADS * LANES), lambda b, t, qi, kn: (b, kn[t], 0)),
                pl.BlockSpec((1, B_BLOCK, B_KV_HEADS * LANES), lambda b, t, qi, kn: (b, kn[t], 0)),
            ],
            out_specs=pl.BlockSpec((1, B_BLOCK, nq), lambda b, t, qi, kn: (b, qi[t], 0)),
            scratch_shapes=[
                pltpu.VMEM((B_KV_HEADS, B_REP * B_BLOCK, 1), F32),
                pltpu.VMEM((B_KV_HEADS, B_REP * B_BLOCK, 1), F32),
                pltpu.VMEM((B_KV_HEADS, B_REP * B_BLOCK, LANES), F32),
            ],
        ),
        out_shape=jax.ShapeDtypeStruct((Bn, S, nq), F32),
        compiler_params=_params("parallel", "arbitrary"),
        name="b_attn",
    )(qi, kn, q_aug, k_aug, v_pad)
    return o.reshape(Bn * S, nq)


def _b_sample_kernel(n_pages, T, pt_ref, q_ref, kn_ref, vn_ref, *refs):
    del pt_ref
    page_refs = refs[:n_pages]
    o_ref = refs[n_pages]
    nk = B_KV_HEADS * HEAD_DIM
    q = q_ref[0]
    qb = q.astype(BF16)
    rows = q.shape[0]
    page_rows = page_refs[0].shape[1]
    pages_per_block = B_BLOCK // page_rows
    n_blocks = n_pages // pages_per_block
    scale = HEAD_DIM ** -0.5
    scores, ksum = [], []
    for p in range(n_pages):
        k = page_refs[p][0, :, :nk]
        scores.append(_dot_nt(qb, k.astype(BF16)) * scale)
        ksum.append(jnp.sum(k, axis=0, keepdims=True))
    gates = []
    for b in range(n_blocks):
        km = ksum[b * pages_per_block]
        for e in range(1, pages_per_block):
            km = km + ksum[b * pages_per_block + e]
        gates.append(jnp.sum(q * (km * (1.0 / B_BLOCK)), axis=1, keepdims=True))
    sel = []
    for b in range(n_blocks):
        rank = jnp.zeros((rows, 1), F32)
        for c in range(n_blocks):
            if c == b:
                continue
            ahead = (gates[c] > gates[b]) | ((gates[c] == gates[b]) & (c < b))
            rank = rank + ahead.astype(F32)
        sel.append(rank < float(min(B_TOPK, n_blocks)))
    qidx = lax.broadcasted_iota(jnp.int32, (rows, 1), 0) % T
    kn, vn = kn_ref[0], vn_ref[0]
    own = []
    for j in range(T):
        sj = jnp.sum(q * kn[j:j + 1, :], axis=1, keepdims=True) * scale
        own.append(jnp.where(qidx >= j, sj, NEG))
    m = own[0]
    for j in range(1, T):
        m = jnp.maximum(m, own[j])
    for p in range(n_pages):
        scores[p] = jnp.where(sel[p // pages_per_block], scores[p], NEG)
        m = jnp.maximum(m, jnp.max(scores[p], axis=-1, keepdims=True))
    den = jnp.zeros((rows, 1), F32)
    acc = jnp.zeros((rows, nk), F32)
    for j in range(T):
        pj = jnp.exp(own[j] - m)
        den = den + pj
        acc = acc + pj * vn[j:j + 1, :]
    for p in range(n_pages):
        pp = jnp.exp(scores[p] - m)
        den = den + jnp.sum(pp, axis=-1, keepdims=True)
        acc = acc + _dot(pp.astype(BF16), page_refs[p][0, :, nk:].astype(BF16))
    acc = acc / den
    kvh = lax.broadcasted_iota(jnp.int32, (rows, 1), 0) // (T * B_REP)
    out = jnp.zeros((rows, HEAD_DIM), F32)
    for h in range(B_KV_HEADS):
        out = out + jnp.where(kvh == h, acc[:, h * HEAD_DIM:(h + 1) * HEAD_DIM], 0.0)
    o_ref[0] = out


def _b_sample(y, pool, page_table, Bd, T):
    nq = B_HEADS * HEAD_DIM
    nk = B_KV_HEADS * HEAD_DIM
    n_pages = page_table.shape[1]
    page_rows = pool.shape[1]
    assert B_BLOCK % page_rows == 0 and (n_pages * page_rows) % B_BLOCK == 0 and T <= B_BLOCK
    y3 = y.reshape(Bd, T, nq + 2 * nk)
    q = y3[:, :, :nq].reshape(Bd, T, B_HEADS, HEAD_DIM).transpose(0, 2, 1, 3)
    kvsel = jnp.asarray(np.kron(np.eye(B_KV_HEADS), np.ones((B_REP, 1))), F32)
    qbd = (q[:, :, :, None, :] * kvsel[None, :, None, :, None]).reshape(Bd, B_HEADS * T, nk)
    kn = y3[:, :, nq:nq + nk]
    vn = y3[:, :, nq + nk:]
    pool2 = pool.reshape(pool.shape[0], page_rows, 2 * nk)
    page_specs = [pl.BlockSpec((1, page_rows, 2 * nk), lambda b, pt, p=p: (pt[b, p], 0, 0))
                  for p in range(n_pages)]
    o = pl.pallas_call(
        functools.partial(_b_sample_kernel, n_pages, T),
        grid_spec=pltpu.PrefetchScalarGridSpec(
            num_scalar_prefetch=1,
            grid=(Bd,),
            in_specs=[
                pl.BlockSpec((1, B_HEADS * T, nk), lambda b, pt: (b, 0, 0)),
                pl.BlockSpec((1, T, nk), lambda b, pt: (b, 0, 0)),
                pl.BlockSpec((1, T, nk), lambda b, pt: (b, 0, 0)),
            ] + page_specs,
            out_specs=pl.BlockSpec((1, B_HEADS * T, HEAD_DIM), lambda b, pt: (b, 0, 0)),
        ),
        out_shape=jax.ShapeDtypeStruct((Bd, B_HEADS * T, HEAD_DIM), F32),
        compiler_params=_params("parallel"),
        name="b_sample",
    )(page_table, qbd, kn, vn, *([pool2] * n_pages))
    return o.reshape(Bd, B_HEADS, T, HEAD_DIM).transpose(0, 2, 1, 3).reshape(Bd * T, nq)


def _mixer_b(xp, xs, gain, w_in, qk_gain, w_out, pool, page_table, tabs_p, tabs_s, Bn, S, Bd, T):
    tn = 512
    nq = B_HEADS * HEAD_DIM
    nk = B_KV_HEADS * HEAD_DIM
    flags = jnp.asarray([1, 1, 1], jnp.int32)
    ones = jnp.ones((nk,), F32)
    qg = jnp.tile(qk_gain[0], tn // HEAD_DIM)
    kg = jnp.concatenate([jnp.tile(qk_gain[1], B_KV_HEADS), ones])
    gain_rows = jnp.stack([qg, qg, kg]).reshape(3, 1, tn)
    mask_rows = jnp.stack([jnp.ones((tn,), F32), jnp.ones((tn,), F32),
                           jnp.concatenate([ones, 0.0 * ones])]).reshape(3, 1, tn)
    w = w_in.astype(BF16)
    yp = _proj_qk(xp, gain, w, flags, gain_rows, mask_rows, tabs_p, 1024, tn)
    ys = _proj_qk(xs, gain, w, flags, gain_rows, mask_rows, tabs_s, xs.shape[0], tn)
    wo = w_out.astype(BF16)
    op = _b_prompt(yp, Bn, S)
    xp = _row_call(_out_plain_kernel, "out_b", xp, [(op, nq, 0)], [wo], 512)
    osm = _b_sample(ys, pool, page_table, Bd, T)
    xs = _row_call(_out_plain_kernel, "out_b", xs, [(osm, nq, 0)], [wo], xs.shape[0])
    kv_p = yp[:, nq:].reshape(Bn, S, 2, B_KV_HEADS, HEAD_DIM)
    kv_s = ys[:, nq:].reshape(Bd, T, 2, B_KV_HEADS, HEAD_DIM)
    return xp, xs, kv_p, kv_s


def _gla_kernel(nv, q_ref, k_ref, v_ref, glr_ref, wg_ref, bg_ref, tri_ref, blk_ref, s0_ref,
                o_ref, sT_ref, s_ref, qp_ref, kp_ref, vp_ref, gp_ref):
    t = pl.program_id(1)
    Tt = SEQ_TILE

    @pl.when(t == 0)
    def _():
        s_ref[...] = s0_ref[0]

    if nv < Tt:
        qp_ref[...] = jnp.zeros_like(qp_ref)
        kp_ref[...] = jnp.zeros_like(kp_ref)
        vp_ref[...] = jnp.zeros_like(vp_ref)
        gp_ref[...] = jnp.zeros_like(gp_ref)
    qp_ref[0:nv, :] = q_ref[0]
    kp_ref[0:nv, :] = k_ref[0]
    vp_ref[0:nv, :] = v_ref[0]
    gp_ref[0:nv, :] = glr_ref[0]
    q, k, v = qp_ref[...], kp_ref[...], vp_ref[...]

    x = _dot(gp_ref[...].astype(BF16), wg_ref[...]) + bg_ref[...]
    la = (jnp.minimum(x, 0.0) - jnp.log(1.0 + jnp.exp(-jnp.abs(x)))) * (1.0 / C_TAU)
    row = lax.broadcasted_iota(jnp.int32, (Tt, C_KEY), 0)
    la = jnp.where(row < nv, la, 0.0)
    b = _sel_left(tri_ref[...], la, terms=3)
    bend = _sel_left(blk_ref[...], la, terms=3)
    qe = q * (C_DK ** -0.5) * jnp.exp(b)
    ke = k * jnp.exp(-b)
    kd = k * jnp.exp(bend - b)
    kdt = kd.T
    bendt = bend.T
    qeb, keb, vb = qe.astype(BF16), ke.astype(BF16), v.astype(BF16)
    causal = tri_ref[...] > 0
    lane_t = lax.broadcasted_iota(jnp.int32, (C_DK, Tt), 1)
    n_chunks = -(-nv // C_CHUNK)
    for h in range(C_HEADS):
        ks = slice(h * C_DK, (h + 1) * C_DK)
        vs = slice(h * C_DV, (h + 1) * C_DV)
        att = jnp.where(causal, _dot_nt(qeb[:, ks], keb[:, ks]), 0.0)
        o_intra = _dot(att.astype(BF16), vb[:, vs])
        st = s_ref[h]
        parts = []
        for c in range(n_chunks):
            rows = slice(c * C_CHUNK, (c + 1) * C_CHUNK)
            parts.append(o_intra[rows] + _dot(qeb[rows, ks], st.astype(BF16)))
            in_chunk = (lane_t >= c * C_CHUNK) & (lane_t < (c + 1) * C_CHUNK)
            last = lane_t == (c + 1) * C_CHUNK - 1
            dec = jnp.exp(jnp.sum(jnp.where(last, bendt[ks, :], 0.0), axis=1, keepdims=True))
            kdc = jnp.where(in_chunk, kdt[ks, :], 0.0).astype(BF16)
            st = dec * st + _dot(kdc, vb[:, vs])
        s_ref[h] = st
        oh = parts[0] if n_chunks == 1 else jnp.concatenate(parts, axis=0)
        o_ref[0, :, vs] = oh[0:nv]

    @pl.when(t == pl.num_programs(1) - 1)
    def _():
        sT_ref[0] = s_ref[...]


def _gla(y, w_gate2, b_gate, s0, nb, nt, nv):
    Tt = SEQ_TILE
    y3 = y.reshape(nb * nt, nv, C_NPAD)
    wg = jnp.zeros((LANES, C_KEY), F32).at[:C_RANK].set(w_gate2).astype(BF16)
    idx = np.arange(Tt)
    same = (idx[:, None] // C_CHUNK) == (idx[None, :] // C_CHUNK)
    tri = jnp.asarray(same & (idx[None, :] <= idx[:, None]), BF16)
    blk = jnp.asarray(same, BF16)

    def yspec(width, cb):
        return pl.BlockSpec((1, nv, width), lambda b, t: (b * nt + t, 0, cb))

    def cspec(a):
        return pl.BlockSpec(a.shape, lambda b, t, nd=a.ndim: (0,) * nd)

    bg = b_gate.reshape(1, C_KEY)
    o, sT = pl.pallas_call(
        functools.partial(_gla_kernel, nv),
        grid=(nb, nt),
        in_specs=[
            yspec(C_KEY, 0), yspec(C_KEY, 1), yspec(C_VAL, 1), yspec(LANES, (2 * C_KEY + 2 * C_VAL) // LANES),
            cspec(wg), cspec(bg), cspec(tri), cspec(blk),
            pl.BlockSpec((1, C_HEADS, C_DK, C_DV), lambda b, t: (b, 0, 0, 0)),
        ],
        out_specs=[
            pl.BlockSpec((1, nv, C_VAL), lambda b, t: (b * nt + t, 0, 0)),
            pl.BlockSpec((1, C_HEADS, C_DK, C_DV), lambda b, t: (b, 0, 0, 0)),
        ],
        out_shape=[
            jax.ShapeDtypeStruct((nb * nt, nv, C_VAL), F32),
            jax.ShapeDtypeStruct((nb, C_HEADS, C_DK, C_DV), F32),
        ],
        scratch_shapes=[
            pltpu.VMEM((C_HEADS, C_DK, C_DV), F32),
            pltpu.VMEM((Tt, C_KEY), F32), pltpu.VMEM((Tt, C_KEY), F32),
            pltpu.VMEM((Tt, C_VAL), F32), pltpu.VMEM((Tt, LANES), F32),
        ],
        compiler_params=_params("parallel", "arbitrary"),
        name="gla",
    )(y3, y3, y3, y3, wg, bg, tri, blk, s0)
    return o.reshape(nb * nt * nv, C_VAL), sT


def _mixer_c(xp, xs, gain, w_in, w_gate2, b_gate, norm_g, w_out, state, Bn, S, Bd, T):
    n_in = w_in.shape[1]
    w = jnp.zeros((D_MODEL, C_NPAD), BF16).at[:, :n_in].set(w_in.astype(BF16))
    yp = _proj(xp, gain, w, 1024, 640)
    ys = _proj(xs, gain, w, xs.shape[0], 640)
    zero = jnp.zeros((Bn, C_HEADS, C_DK, C_DV), F32)
    op, sp = _gla(yp, w_gate2, b_gate, zero, Bn, S // SEQ_TILE, SEQ_TILE)
    osm, ss = _gla(ys, w_gate2, b_gate, state, Bd, 1, T)
    wo = w_out.astype(BF16)
    ng = norm_g.reshape(1, C_DV)
    rcb = (2 * C_KEY + C_VAL) // C_VAL
    xp = _row_call(_out_c_kernel, "out_c", xp, [(op, C_VAL, 0), (yp, C_VAL, rcb)], [ng, wo], 512)
    xs = _row_call(_out_c_kernel, "out_c", xs, [(osm, C_VAL, 0), (ys, C_VAL, rcb)], [ng, wo], xs.shape[0])
    return xp, xs, sp, ss


def _ssd_kernel(nv, xa_ref, xb_ref, bc_ref, dt_ref, cs_ref, cw_ref, cb_ref, dtb_ref, alog_ref, dsk_ref,
                tri_ref, exp_ref, expt_ref, h0_ref, y_ref, hT_ref, h_ref, xp_ref, dtp_ref, yd_ref):
    t = pl.program_id(1)
    Tt = SEQ_TILE
    pre = 8

    @pl.when(t == 0)
    def _():
        h_ref[...] = h0_ref[0]
        xp_ref[0:pre, :] = cs_ref[0]

    if nv < Tt:
        xp_ref[pre:, :] = jnp.zeros((Tt, D_XBC), F32)
        dtp_ref[...] = jnp.zeros_like(dtp_ref)
    xp_ref[pre:pre + nv, 0:1024] = xa_ref[0]
    xp_ref[pre:pre + nv, 1024:2048] = xb_ref[0]
    xp_ref[pre:pre + nv, 2048:3072] = bc_ref[0]
    dtp_ref[0:nv, :] = dt_ref[0]

    conv = cb_ref[...]
    for w in range(D_CONV):
        conv = conv + xp_ref[pl.ds(pre - (D_CONV - 1) + w, Tt), :] * cw_ref[w:w + 1, :]
    xp_ref[0:pre, :] = xp_ref[Tt:Tt + pre, :]
    xbc = _silu(conv)
    x = xbc[:, :D_INNER]
    nbc = D_GROUPS * D_STATE
    bm = xbc[:, D_INNER:D_INNER + nbc].astype(BF16)
    cm = xbc[:, D_INNER + nbc:].astype(BF16)

    row = lax.broadcasted_iota(jnp.int32, (Tt, LANES), 0)
    lane = lax.broadcasted_iota(jnp.int32, (Tt, LANES), 1)
    live = (row < nv) & (lane < D_HEADS)
    dt = jnp.where(live, _softplus(dtp_ref[...] + dtb_ref[...]), 0.0)
    cum = _sel_left(tri_ref[...], dt * (-jnp.exp(alog_ref[...])), terms=3)
    cumt = cum.T
    cend = cum[Tt - 1:Tt, :]
    ex = exp_ref[...]
    stack = jnp.concatenate([jnp.exp(cend - cum) * dt, dt, jnp.exp(cum)], axis=0)
    wide = _sel_right(stack, ex)
    x_state = (x * wide[0:Tt]).astype(BF16)
    x_dt = (x * wide[Tt:2 * Tt]).astype(BF16)
    off_scale = wide[2 * Tt:]

    causal = tri_ref[...] > 0
    gw = D_INNER // D_GROUPS
    for g in range(D_GROUPS):
        bg = bm[:, g * D_STATE:(g + 1) * D_STATE]
        cg = cm[:, g * D_STATE:(g + 1) * D_STATE]
        cb = _dot_nt(cg, bg)
        for jj in range(0, D_HPG, 2):
            pair = []
            for j in (g * D_HPG + jj, g * D_HPG + jj + 1):
                seg = cum[:, j:j + 1] - cumt[j:j + 1, :]
                lmat = jnp.exp(jnp.where(causal, seg, -jnp.inf))
                pair.append(_dot((cb * lmat).astype(BF16), x_dt[:, j * D_HEADDIM:(j + 1) * D_HEADDIM]))
            c0 = (g * D_HPG + jj) * D_HEADDIM
            yd_ref[:, c0:c0 + 2 * D_HEADDIM] = jnp.concatenate(pair, axis=1)
        hg = h_ref[g * gw:(g + 1) * gw, :]
        y_off = _dot_nt(cg, hg.astype(BF16)) * off_scale[:, g * gw:(g + 1) * gw]
        yd_ref[:, g * gw:(g + 1) * gw] += y_off
    y = yd_ref[...] + dsk_ref[...] * x
    y_ref[0] = y[0:nv]

    dcol = jnp.broadcast_to(jnp.exp(cumt[:, Tt - 1:Tt]), (LANES, D_STATE))
    dfull = _sel_left(expt_ref[...], dcol)
    xst = x_state.astype(F32).T.astype(BF16)
    for g in range(D_GROUPS):
        rows = slice(g * gw, (g + 1) * gw)
        h_ref[rows, :] = dfull[rows] * h_ref[rows, :] + _dot(xst[rows], bm[:, g * D_STATE:(g + 1) * D_STATE])

    @pl.when(t == pl.num_programs(1) - 1)
    def _():
        hT_ref[0] = h_ref[...]


def _ssd(y, conv_w, conv_b, dt_bias, a_log, d_skip, h0, c0, nb, nt, nv):
    Tt = SEQ_TILE
    y3 = y.reshape(nb * nt, nv, D_NPAD)
    idx = np.arange(Tt)
    tri = jnp.asarray(idx[None, :] <= idx[:, None], BF16)
    ex_np = np.zeros((LANES, D_INNER), np.float32)
    ex_np[np.arange(D_INNER) // D_HEADDIM, np.arange(D_INNER)] = 1.0
    ex = jnp.asarray(ex_np, BF16)
    ext = jnp.asarray(ex_np.T, BF16)

    def pad_row(v):
        return jnp.zeros((1, LANES), F32).at[0, :D_HEADS].set(v)

    cs = jnp.zeros((nb, 8, D_XBC), F32).at[:, 8 - (D_CONV - 1):].set(c0)
    dsk = jnp.repeat(d_skip, D_HEADDIM).reshape(1, D_INNER)

    def yspec(width, cb):
        return pl.BlockSpec((1, nv, width), lambda b, t: (b * nt + t, 0, cb))

    def cspec(a):
        return pl.BlockSpec(a.shape, lambda b, t, nd=a.ndim: (0,) * nd)

    consts = [conv_w, conv_b.reshape(1, D_XBC), pad_row(dt_bias), pad_row(a_log), dsk, tri, ex, ext]
    yo, hT = pl.pallas_call(
        functools.partial(_ssd_kernel, nv),
        grid=(nb, nt),
        in_specs=[yspec(1024, 2), yspec(1024, 3), yspec(1024, 4), yspec(LANES, D_DT_COL),
                  pl.BlockSpec((1, 8, D_XBC), lambda b, t: (b, 0, 0))]
        + [cspec(a) for a in consts]
        + [pl.BlockSpec((1, D_INNER, D_STATE), lambda b, t: (b, 0, 0))],
        out_specs=[
            pl.BlockSpec((1, nv, D_INNER), lambda b, t: (b * nt + t, 0, 0)),
            pl.BlockSpec((1, D_INNER, D_STATE), lambda b, t: (b, 0, 0)),
        ],
        out_shape=[
            jax.ShapeDtypeStruct((nb * nt, nv, D_INNER), F32),
            jax.ShapeDtypeStruct((nb, D_INNER, D_STATE), F32),
        ],
        scratch_shapes=[
            pltpu.VMEM((D_INNER, D_STATE), F32),
            pltpu.VMEM((Tt + 8, D_XBC), F32),
            pltpu.VMEM((Tt, LANES), F32),
            pltpu.VMEM((Tt, D_INNER), F32),
        ],
        compiler_params=_params("parallel", "arbitrary"),
        name="ssd",
    )(y3, y3, y3, y3, cs, *consts, h0)
    return yo.reshape(nb * nt * nv, D_INNER), hT


def _mixer_d(xp, xs, gain, w_in, conv_w, conv_b, dt_bias, a_log, d_skip, norm_g, w_out,
             ssm_state, conv_state, Bn, S, Bd, T):
    n_in = w_in.shape[1]
    w = jnp.zeros((D_MODEL, D_NPAD), BF16).at[:, :n_in].set(w_in.astype(BF16))
    yp = _proj(xp, gain, w, 1024, 768)
    ys = _proj(xs, gain, w, xs.shape[0], 768)
    h0p = jnp.zeros((Bn, D_INNER, D_STATE), F32)
    c0p = jnp.zeros((Bn, D_CONV - 1, D_XBC), F32)
    op, hp = _ssd(yp, conv_w, conv_b, dt_bias, a_log, d_skip, h0p, c0p, Bn, S // SEQ_TILE, SEQ_TILE)
    h0s = ssm_state.reshape(Bd, D_INNER, D_STATE)
    osm, hs = _ssd(ys, conv_w, conv_b, dt_bias, a_log, d_skip, h0s, conv_state, Bd, 1, T)
    wo = w_out.astype(BF16)
    ng = norm_g.reshape(1, D_INNER)
    xp = _row_call(_out_d_kernel, "out_d", xp, [(op, D_INNER, 0), (yp, D_INNER, 0)], [ng, wo], 512)
    xs = _row_call(_out_d_kernel, "out_d", xs, [(osm, D_INNER, 0), (ys, D_INNER, 0)], [ng, wo], xs.shape[0])
    keep = D_CONV - 1
    xbc_p = yp.reshape(Bn, S, D_NPAD)[:, :, D_INNER:D_INNER + D_XBC]
    xbc_s = ys.reshape(Bd, T, D_NPAD)[:, :, D_INNER:D_INNER + D_XBC]
    cp = jnp.concatenate([c0p, xbc_p], axis=1)[:, -keep:] if S < keep else xbc_p[:, S - keep:]
    cs = jnp.concatenate([conv_state, xbc_s], axis=1)[:, -keep:]
    return (xp, xs, hp.reshape(Bn, D_HEADS, D_HEADDIM, D_STATE), hs.reshape(Bd, D_HEADS, D_HEADDIM, D_STATE),
            cp, cs)


def kernel(x_prompt, x_sample, cache_a_w1, cache_a_w2, cache_a_w3, cache_b_kv, page_table, state_c, state_d_ssm, state_d_conv, norm_gain, w_ffn_up, w_ffn_down, w_a_in, a_qk_gain, w_a_out, w_b_in, b_qk_gain, w_b_out, w_c_in, w_c_gate2, b_c_gate, c_norm_gain, w_c_out, w_d_in, d_conv_w, d_conv_b, d_dt_bias, d_a_log, d_skip, d_norm_gain, w_d_out):
    Bn, S, _ = x_prompt.shape
    Bd, T, _ = x_sample.shape
    depth = norm_gain.shape[0]
    past_len = page_table.shape[1] * cache_b_kv.shape[2]
    tabs_p = _rope_tables(jnp.arange(S, dtype=jnp.int32))
    tabs_s = _rope_tables(jnp.tile(past_len + jnp.arange(T, dtype=jnp.int32), Bd))
    xp = x_prompt.reshape(Bn * S, D_MODEL)
    xs = x_sample.reshape(Bd * T, D_MODEL)
    ts = xs.shape[0]
    w_up = w_ffn_up.astype(BF16)
    w_down = w_ffn_down.astype(BF16)
    outs = {k: [] for k in ("a0p", "a0s", "a1p", "a1s", "a2p", "a2s", "bp", "bs", "cp", "cs",
                            "hp", "hs", "dp", "ds")}
    for i in range(depth):
        m, j = i % 4, i // 4
        g = norm_gain[i]
        xp = _ffn(xp, g[0], w_up[i, 0], w_down[i, 0], 1024)
        xs = _ffn(xs, g[0], w_up[i, 0], w_down[i, 0], ts)
        if m == 0:
            xp, xs, new_p, new_s = _mixer_a(xp, xs, g[1], w_a_in[j], a_qk_gain[j], w_a_out[j],
                                            (cache_a_w1[j], cache_a_w2[j], cache_a_w3[j]),
                                            tabs_p, tabs_s, Bn, S, Bd, T)
            for gi in range(A_GROUPS):
                outs["a%dp" % gi].append(new_p[gi])
                outs["a%ds" % gi].append(new_s[gi])
        elif m == 1:
            xp, xs, kvp, kvs = _mixer_b(xp, xs, g[1], w_b_in[j], b_qk_gain[j], w_b_out[j],
                                        cache_b_kv[j], page_table, tabs_p, tabs_s, Bn, S, Bd, T)
            outs["bp"].append(kvp)
            outs["bs"].append(kvs)
        elif m == 2:
            xp, xs, sp, ss = _mixer_c(xp, xs, g[1], w_c_in[j], w_c_gate2[j], b_c_gate[j], c_norm_gain[j],
                                      w_c_out[j], state_c[j], Bn, S, Bd, T)
            outs["cp"].append(sp)
            outs["cs"].append(ss)
        else:
            xp, xs, hp, hs, cp, cs = _mixer_d(xp, xs, g[1], w_d_in[j], d_conv_w[j], d_conv_b[j], d_dt_bias[j],
                                              d_a_log[j], d_skip[j], d_norm_gain[j], w_d_out[j],
                                              state_d_ssm[j], state_d_conv[j], Bn, S, Bd, T)
            outs["hp"].append(hp)
            outs["hs"].append(hs)
            outs["dp"].append(cp)
            outs["ds"].append(cs)
        xp = _ffn(xp, g[2], w_up[i, 1], w_down[i, 1], 1024)
        xs = _ffn(xs, g[2], w_up[i, 1], w_down[i, 1], ts)
    st = {k: jnp.stack(v) for k, v in outs.items()}
    return (xp.reshape(Bn, S, D_MODEL), xs.reshape(Bd, T, D_MODEL),
            st["a0p"], st["a0s"], st["a1p"], st["a1s"], st["a2p"], st["a2s"],
            st["bp"], st["bs"], st["cp"], st["cs"], st["hp"], st["hs"], st["dp"], st["ds"])
```

```python
import functools
import math

import numpy as np
import jax
import jax.numpy as jnp
from jax import lax
from jax.experimental import pallas as pl
from jax.experimental.pallas import tpu as pltpu

F32 = jnp.float32
BF16 = jnp.bfloat16

D_MODEL = 1024
HEAD_DIM = 64
ROPE_DIM = HEAD_DIM // 4
ROPE_THETA = 500000.0
EPS = 1e-6
D_FF = 2816
NEG = -1e30

A_GROUPS = 3
A_DILATIONS = (1, 4, 16)
A_HEADS = 8
A_WIDTH = A_HEADS * HEAD_DIM
A_SPAN = 128

B_HEADS = 16
B_KV_HEADS = 4
B_REP = B_HEADS // B_KV_HEADS
B_BLOCK = 256
B_TOPK = 3

C_HEADS = 4
C_KEY = 512
C_VAL = 1024
C_DK = 128
C_DV = 256
C_RANK = 16
C_TAU = 16.0
C_CHUNK = 32
C_NPAD = 3200

D_INNER = 2048
D_HEADDIM = 64
D_HEADS = 32
D_GROUPS = 4
D_HPG = 8
D_STATE = 128
D_CONV = 4
D_XBC = D_INNER + 2 * D_GROUPS * D_STATE
D_NPAD = 5376
D_DT_COL = (D_INNER + D_XBC) // 128

LANES = 128
FF_TILE = 256
SEQ_TILE = 128


def _params(*sem):
    return pltpu.CompilerParams(dimension_semantics=sem)


def _split2(a):
    hi = a.astype(BF16)
    lo = (a - hi.astype(F32)).astype(BF16)
    return hi, lo


def _split3(a):
    hi = a.astype(BF16)
    r = a - hi.astype(F32)
    mid = r.astype(BF16)
    lo = (r - mid.astype(F32)).astype(BF16)
    return hi, mid, lo


def _dot(a, b):
    return jnp.dot(a, b, preferred_element_type=F32)


def _dot_nt(a, b):
    return lax.dot_general(a, b, (((1,), (1,)), ((), ())), preferred_element_type=F32)


def _sel_right(a, sel01, terms=2):
    parts = _split2(a) if terms == 2 else _split3(a)
    out = None
    for p in reversed(parts):
        d = _dot(p, sel01)
        out = d if out is None else out + d
    return out


def _sel_left(sel01, a, terms=2):
    parts = _split2(a) if terms == 2 else _split3(a)
    out = None
    for p in reversed(parts):
        d = _dot(sel01, p)
        out = d if out is None else out + d
    return out


def _silu(x):
    return x * jax.nn.sigmoid(x)


def _softplus(x):
    return jnp.maximum(x, 0.0) + jnp.log(1.0 + jnp.exp(-jnp.abs(x)))


def _rms_rows(x, g):
    ms = jnp.mean(x * x, axis=-1, keepdims=True)
    return x * lax.rsqrt(ms + EPS) * g


def _ffn_kernel(x_ref, g_ref, wg_ref, wu_ref, wd_ref, o_ref, h_ref, acc_ref):
    j = pl.program_id(1)

    @pl.when(j == 0)
    def _():
        h_ref[...] = _rms_rows(x_ref[...], g_ref[...]).astype(BF16)
        acc_ref[...] = jnp.zeros_like(acc_ref)

    h = h_ref[...]
    a = _dot(h, wg_ref[...])
    u = _dot(h, wu_ref[...])
    act = (_silu(a) * u).astype(BF16)
    acc_ref[...] += _dot(act, wd_ref[...])

    @pl.when(j == pl.num_programs(1) - 1)
    def _():
        o_ref[...] = x_ref[...] + 0.5 * acc_ref[...]


def _ffn(x, gain, w_up, w_down, tm):
    T = x.shape[0]
    nj = D_FF // FF_TILE
    return pl.pallas_call(
        _ffn_kernel,
        grid=(T // tm, nj),
        in_specs=[
            pl.BlockSpec((tm, D_MODEL), lambda i, j: (i, 0)),
            pl.BlockSpec((1, D_MODEL), lambda i, j: (0, 0)),
            pl.BlockSpec((D_MODEL, FF_TILE), lambda i, j: (0, j)),
            pl.BlockSpec((D_MODEL, FF_TILE), lambda i, j: (0, j + nj)),
            pl.BlockSpec((FF_TILE, D_MODEL), lambda i, j: (j, 0)),
        ],
        out_specs=pl.BlockSpec((tm, D_MODEL), lambda i, j: (i, 0)),
        out_shape=jax.ShapeDtypeStruct((T, D_MODEL), F32),
        scratch_shapes=[pltpu.VMEM((tm, D_MODEL), BF16), pltpu.VMEM((tm, D_MODEL), F32)],
        compiler_params=_params("parallel", "arbitrary"),
        name="ffn",
    )(x, gain.reshape(1, D_MODEL), w_up, w_up, w_down)


def _proj_kernel(x_ref, g_ref, w_ref, o_ref, h_ref):
    @pl.when(pl.program_id(1) == 0)
    def _():
        h_ref[...] = _rms_rows(x_ref[...], g_ref[...]).astype(BF16)

    o_ref[...] = _dot(h_ref[...], w_ref[...])


def _proj(x, gain, w, tm, tn):
    T = x.shape[0]
    N = w.shape[1]
    return pl.pallas_call(
        _proj_kernel,
        grid=(T // tm, N // tn),
        in_specs=[
            pl.BlockSpec((tm, D_MODEL), lambda i, j: (i, 0)),
            pl.BlockSpec((1, D_MODEL), lambda i, j: (0, 0)),
            pl.BlockSpec((D_MODEL, tn), lambda i, j: (0, j)),
        ],
        out_specs=pl.BlockSpec((tm, tn), lambda i, j: (i, j)),
        out_shape=jax.ShapeDtypeStruct((T, N), F32),
        scratch_shapes=[pltpu.VMEM((tm, D_MODEL), BF16)],
        compiler_params=_params("parallel", "arbitrary"),
        name="proj",
    )(x, gain.reshape(1, D_MODEL), w)


def _proj_qk_kernel(flag_ref, x_ref, g_ref, w_ref, bd_ref, gain_ref, mask_ref,
                    cos_ref, sa_ref, sb_ref, o_ref, h_ref):
    j = pl.program_id(1)

    @pl.when(j == 0)
    def _():
        h_ref[...] = _rms_rows(x_ref[...], g_ref[...]).astype(BF16)

    y = _dot(h_ref[...], w_ref[...])
    tn = y.shape[1]

    @pl.when(flag_ref[j] == 0)
    def _():
        o_ref[...] = y

    @pl.when(flag_ref[j] != 0)
    def _():
        ss = _dot((y * y).astype(BF16), bd_ref[...])
        yn = y * lax.rsqrt(ss * (1.0 / HEAD_DIM) + EPS) * gain_ref[0]
        cos, sa, sb = cos_ref[...], sa_ref[...], sb_ref[...]
        mask = mask_ref[0]
        for c in range(tn // LANES):
            sl = slice(c * LANES, (c + 1) * LANES)
            v = yn[:, sl]
            up = pltpu.roll(v, LANES - ROPE_DIM // 2, 1)
            dn = pltpu.roll(v, ROPE_DIM // 2, 1)
            rot = v * cos + up * sa + dn * sb
            o_ref[:, sl] = jnp.where(mask[:, sl] > 0.0, rot, y[:, sl])


def _proj_qk(x, gain, w, flags, gain_rows, mask_rows, tables, tm, tn):
    T = x.shape[0]
    N = w.shape[1]
    nj = N // tn
    cos, sa, sb = tables
    n_pos_tiles = cos.shape[0] // tm
    bd = jnp.asarray(np.kron(np.eye(tn // HEAD_DIM), np.ones((HEAD_DIM, HEAD_DIM))), BF16)
    tab_spec = pl.BlockSpec((tm, LANES), lambda i, j, f: (i % n_pos_tiles, 0))
    return pl.pallas_call(
        _proj_qk_kernel,
        grid_spec=pltpu.PrefetchScalarGridSpec(
            num_scalar_prefetch=1,
            grid=(T // tm, nj),
            in_specs=[
                pl.BlockSpec((tm, D_MODEL), lambda i, j, f: (i, 0)),
                pl.BlockSpec((1, D_MODEL), lambda i, j, f: (0, 0)),
                pl.BlockSpec((D_MODEL, tn), lambda i, j, f: (0, j)),
                pl.BlockSpec((tn, tn), lambda i, j, f: (0, 0)),
                pl.BlockSpec((1, 1, tn), lambda i, j, f: (j, 0, 0)),
                pl.BlockSpec((1, 1, tn), lambda i, j, f: (j, 0, 0)),
                tab_spec, tab_spec, tab_spec,
            ],
            out_specs=pl.BlockSpec((tm, tn), lambda i, j, f: (i, j)),
            scratch_shapes=[pltpu.VMEM((tm, D_MODEL), BF16)],
        ),
        out_shape=jax.ShapeDtypeStruct((T, N), F32),
        compiler_params=_params("parallel", "arbitrary"),
        name="proj_qk",
    )(flags, x, gain.reshape(1, D_MODEL), w, bd, gain_rows, mask_rows, cos, sa, sb)


def _rope_tables(pos):
    half = ROPE_DIM // 2
    inv = ROPE_THETA ** (-jnp.arange(half, dtype=F32) / half)
    ang = pos.astype(F32)[:, None] * inv[None, :]
    cos, sin = jnp.cos(ang), jnp.sin(ang)
    n = pos.shape[0]
    one = jnp.ones((n, HEAD_DIM - ROPE_DIM), F32)
    zero = jnp.zeros((n, HEAD_DIM - ROPE_DIM), F32)
    zh = jnp.zeros((n, half), F32)
    c64 = jnp.concatenate([cos, cos, one], axis=1)
    sa64 = jnp.concatenate([-sin, zh, zero], axis=1)
    sb64 = jnp.concatenate([zh, sin, zero], axis=1)
    rep = LANES // HEAD_DIM
    return tuple(jnp.tile(t, (1, rep)) for t in (c64, sa64, sb64))


def _out_plain_kernel(x_ref, o_ref, w_ref, y_ref):
    y_ref[...] = x_ref[...] + _dot(o_ref[...].astype(BF16), w_ref[...])


def _out_a_kernel(x_ref, o0, o1, o2, l0, l1, l2, w_ref, y_ref):
    la, lb, lc = l0[...], l1[...], l2[...]
    m = jnp.maximum(jnp.maximum(la, lb), lc)
    ea, eb, ec = jnp.exp(la - m), jnp.exp(lb - m), jnp.exp(lc - m)
    o = (ea * o0[...] + eb * o1[...] + ec * o2[...]) / (ea + eb + ec)
    y_ref[...] = x_ref[...] + _dot(o.astype(BF16), w_ref[...])


def _out_c_kernel(x_ref, o_ref, r_ref, g_ref, w_ref, y_ref):
    o = o_ref[...]
    g = g_ref[...]
    parts = [_rms_rows(o[:, h * C_DV:(h + 1) * C_DV], g) for h in range(C_HEADS)]
    on = jnp.concatenate(parts, axis=1) * _silu(r_ref[...])
    y_ref[...] = x_ref[...] + _dot(on.astype(BF16), w_ref[...])


def _out_d_kernel(x_ref, o_ref, z_ref, g_ref, w_ref, y_ref):
    gated = o_ref[...] * _silu(z_ref[...])
    gw = D_INNER // D_GROUPS
    g = g_ref[...]
    parts = [_rms_rows(gated[:, k * gw:(k + 1) * gw], g[:, k * gw:(k + 1) * gw]) for k in range(D_GROUPS)]
    y_ref[...] = x_ref[...] + _dot(jnp.concatenate(parts, axis=1).astype(BF16), w_ref[...])


def _row_call(kernel, name, x, row_inputs, const_inputs, tm):
    T = x.shape[0]
    in_specs = [pl.BlockSpec((tm, D_MODEL), lambda i: (i, 0))]
    args = [x]
    for arr, width, cb in row_inputs:
        in_specs.append(pl.BlockSpec((tm, width), lambda i, cb=cb: (i, cb)))
        args.append(arr)
    for arr in const_inputs:
        in_specs.append(pl.BlockSpec(arr.shape, lambda i, nd=arr.ndim: (0,) * nd))
        args.append(arr)
    return pl.pallas_call(
        kernel,
        grid=(T // tm,),
        in_specs=in_specs,
        out_specs=pl.BlockSpec((tm, D_MODEL), lambda i: (i, 0)),
        out_shape=jax.ShapeDtypeStruct((T, D_MODEL), F32),
        compiler_params=_params("parallel"),
        name=name,
    )(*args)


def _a_prompt_kernel(q_ref, kp_ref, kc_ref, vp_ref, vc_ref, o_ref, l_ref):
    n = pl.program_id(2)
    q = q_ref[0] * (HEAD_DIM ** -0.5)
    k = jnp.concatenate([kp_ref[0], kc_ref[0]], axis=0)
    v = jnp.concatenate([vp_ref[0], vc_ref[0]], axis=0)
    qq = lax.broadcasted_iota(jnp.int32, (A_SPAN, 2 * A_SPAN), 0)
    kk = lax.broadcasted_iota(jnp.int32, (A_SPAN, 2 * A_SPAN), 1)
    rel = qq + A_SPAN - kk
    valid = (rel >= 0) & (rel <= A_SPAN) & ((kk >= A_SPAN) | (n > 0))
    for h in range(A_HEADS):
        sl = slice(h * HEAD_DIM, (h + 1) * HEAD_DIM)
        s = _dot_nt(q[:, sl].astype(BF16), k[:, sl].astype(BF16))
        s = jnp.where(valid, s, -jnp.inf)
        m = jnp.max(s, axis=-1, keepdims=True)
        p = jnp.exp(s - m)
        den = jnp.sum(p, axis=-1, keepdims=True)
        o = _dot(p.astype(BF16), v[:, sl].astype(BF16)) / den
        o_ref[0, :, sl] = o
        l_ref[0, :, sl] = jnp.broadcast_to(m + jnp.log(den), (A_SPAN, HEAD_DIM))


def _a_prompt(y, g, Bn, S):
    d = A_DILATIONS[g]
    n = S // d
    nb = n // A_SPAN
    width = A_GROUPS * 3 * A_WIDTH
    yv = y.reshape(Bn, n, d * width)
    cpr = width // A_WIDTH
    base = 3 * g

    def spec(off, prev):
        if prev:
            return pl.BlockSpec((1, A_SPAN, A_WIDTH),
                                lambda b, r, i: (b, jnp.maximum(i - 1, 0), r * cpr + base + off))
        return pl.BlockSpec((1, A_SPAN, A_WIDTH), lambda b, r, i: (b, i, r * cpr + base + off))

    out_spec = pl.BlockSpec((1, A_SPAN, A_WIDTH), lambda b, r, i: (b, i, r))
    o, l = pl.pallas_call(
        _a_prompt_kernel,
        grid=(Bn, d, nb),
        in_specs=[spec(0, False), spec(1, True), spec(1, False), spec(2, True), spec(2, False)],
        out_specs=[out_spec, out_spec],
        out_shape=[jax.ShapeDtypeStruct((Bn, n, d * A_WIDTH), F32)] * 2,
        compiler_params=_params("parallel", "parallel", "arbitrary"),
        name="a_prompt_d%d" % d,
    )(yv, yv, yv, yv, yv)
    return o.reshape(Bn * S, A_WIDTH), l.reshape(Bn * S, A_WIDTH)


def _nt_sel(mask, a):
    out = None
    for p in reversed(_split3(a)):
        d = _dot_nt(mask, p)
        out = d if out is None else out + d
    return out


def _a_sample_kernel(T, y_ref, yt_ref, c1_ref, c2_ref, c3_ref, hind_ref, hindt_ref, *out_refs):
    caches = (c1_ref, c2_ref, c3_ref)
    width = 3 * A_WIDTH
    scale = HEAD_DIM ** -0.5
    hind = hind_ref[...]
    hindt = hindt_ref[...]
    lane = lax.broadcasted_iota(jnp.int32, (8, LANES), 1)
    sub = lax.broadcasted_iota(jnp.int32, (8, LANES), 0)
    zrows = jnp.zeros((LANES - T, A_WIDTH), F32)
    zsq = jnp.zeros((LANES - 8, LANES), F32)

    def head_rows(prod):
        return _nt_sel(hind, jnp.concatenate([prod, zrows], axis=0))

    def sublane_total(x):
        x = x + pltpu.roll(x, 4, 0)
        x = x + pltpu.roll(x, 2, 0)
        return x + pltpu.roll(x, 1, 0)

    def head_scores(k_of, pat_of):
        out = jnp.zeros((8, LANES), F32)
        for h in range(A_HEADS):
            prod = k_of(h) * pat_of(h)
            part = prod[0:8]
            for j in range(1, HEAD_DIM // 8):
                part = part + prod[8 * j:8 * j + 8]
            out = jnp.where(sub == h, sublane_total(part), out)
        return out

    def rows_of(packed):
        return jnp.concatenate([packed, zsq], axis=0).T

    def widen(rows8):
        return _sel_right(rows8, hindt, terms=3)

    def place(col, at):
        return jnp.where(lane == at, col, 0.0)

    for g in range(A_GROUPS):
        d = A_DILATIONS[g]
        c_ref = caches[g]
        n_tiles = c_ref.shape[-1] // LANES
        q_nat = y_ref[0, :, g * width:g * width + A_WIDTH] * scale
        kn_nat = y_ref[0, :, g * width + A_WIDTH:g * width + 2 * A_WIDTH]
        vn_nat = y_ref[0, :, g * width + 2 * A_WIDTH:(g + 1) * width]
        if d == 1:
            packed = jnp.zeros((8, LANES), F32)
            res = []
            for i in range(T):
                s = head_scores(lambda h: c_ref[0, 0, h], lambda h: yt_ref[0, g, 0, h][:, i:i + 1] * scale)
                s = jnp.where(lane >= i, s, -jnp.inf)
                sn = jnp.where(lane <= i, head_rows(q_nat[i:i + 1, :] * kn_nat), -jnp.inf)
                m = jnp.maximum(jnp.max(s, axis=1, keepdims=True), jnp.max(sn, axis=1, keepdims=True))
                p = jnp.exp(s - m)
                pn = jnp.exp(sn - m)
                den = jnp.sum(p, axis=1, keepdims=True) + jnp.sum(pn, axis=1, keepdims=True)
                acc = jnp.concatenate([c_ref[0, 1, h] * p[h:h + 1, :] for h in range(A_HEADS)], axis=0)
                res.append(_nt_sel((sub == 0).astype(BF16), acc)[0:1, :])
                packed = packed + place(den, i) + place(m + jnp.log(den), 8 + i)
                packed = packed + pltpu.roll(pn, 16 + 8 * i, 1)
            sq = rows_of(packed)
            denx = widen(sq[0:8])
            lsex = widen(sq[8:16])
            for i in range(T):
                pnx = widen(sq[16 + 8 * i:24 + 8 * i])
                o = res[i] + jnp.sum(pnx[0:T] * vn_nat, axis=0, keepdims=True)
                out_refs[g][0, i:i + 1, :] = o / denx[i:i + 1, :]
            out_refs[A_GROUPS + g][0] = lsex[0:T]
        else:
            cls = lane & (d - 1)
            self_s = head_rows(q_nat * kn_nat)
            pats = []
            for h in range(A_HEADS):
                qt = yt_ref[0, g, 0, h] * scale
                pat = jnp.zeros((HEAD_DIM, LANES), F32)
                for i in range(T):
                    pat = pat + qt[:, i:i + 1] * (cls[0:1, :] == i).astype(F32)
                pats.append(pat)
            s_tiles = [head_scores(lambda h: c_ref[0, 0, h, :, t * LANES:(t + 1) * LANES], lambda h: pats[h])
                       for t in range(n_tiles)]
            smax = s_tiles[0]
            for t in range(1, n_tiles):
                smax = jnp.maximum(smax, s_tiles[t])
            mrow = jnp.full((8, LANES), jnp.inf, F32)
            m_cls = []
            for i in range(T):
                mi = jnp.maximum(jnp.max(jnp.where(cls == i, smax, -jnp.inf), axis=1, keepdims=True),
                                 self_s[:, i:i + 1])
                m_cls.append(mi)
                mrow = jnp.where(cls == i, mi, mrow)
            p_tiles = [jnp.exp(s_tiles[t] - mrow) for t in range(n_tiles)]
            psum = p_tiles[0]
            for t in range(1, n_tiles):
                psum = psum + p_tiles[t]
            accs = []
            for h in range(A_HEADS):
                acc = jnp.zeros((HEAD_DIM, LANES), F32)
                for t in range(n_tiles):
                    acc = acc + c_ref[0, 1, h, :, t * LANES:(t + 1) * LANES] * p_tiles[t][h:h + 1, :]
                accs.append(acc)
            res = _nt_sel((cls == sub).astype(BF16), jnp.concatenate(accs, axis=0))
            packed = jnp.zeros((8, LANES), F32)
            for i in range(T):
                ps = jnp.exp(self_s[:, i:i + 1] - m_cls[i])
                den = jnp.sum(jnp.where(cls == i, psum, 0.0), axis=1, keepdims=True) + ps
                packed = packed + place(ps, i) + place(den, 8 + i) + place(m_cls[i] + jnp.log(den), 16 + i)
            sq = rows_of(packed)
            psx, denx, lsex = widen(sq[0:8]), widen(sq[8:16]), widen(sq[16:24])
            out_refs[g][0] = (res[0:T] + psx[0:T] * vn_nat) / denx[0:T]
            out_refs[A_GROUPS + g][0] = lsex[0:T]


def _a_sample(y, caches, Bd, T):
    width = A_GROUPS * 3 * A_WIDTH
    y3 = y.reshape(Bd, T, width)
    yt = y3.reshape(Bd, T, A_GROUPS, 3, A_HEADS, HEAD_DIM).transpose(0, 2, 3, 4, 5, 1)
    views = []
    specs = [pl.BlockSpec((1, T, width), lambda b: (b, 0, 0)),
             pl.BlockSpec((1, A_GROUPS, 3, A_HEADS, HEAD_DIM, T), lambda b: (b, 0, 0, 0, 0, 0))]
    for g, c in enumerate(caches):
        d = A_DILATIONS[g]
        assert c.shape[1] == A_SPAN * d and (d == 1 or T <= d) and T <= 8
        views.append(c.transpose(0, 2, 3, 4, 1))
        specs.append(pl.BlockSpec((1, 2, A_HEADS, HEAD_DIM, A_SPAN * d), lambda b: (b, 0, 0, 0, 0)))
    hind_np = np.zeros((LANES, A_WIDTH), np.float32)
    hind_np[np.arange(A_WIDTH) // HEAD_DIM, np.arange(A_WIDTH)] = 1.0
    hind = jnp.asarray(hind_np[:A_HEADS], BF16)
    hindt = jnp.asarray(hind_np, BF16)
    specs += [pl.BlockSpec(hind.shape, lambda b: (0, 0)), pl.BlockSpec(hindt.shape, lambda b: (0, 0))]
    out_spec = pl.BlockSpec((1, T, A_WIDTH), lambda b: (b, 0, 0))
    outs = pl.pallas_call(
        functools.partial(_a_sample_kernel, T),
        grid=(Bd,),
        in_specs=specs,
        out_specs=[out_spec] * (2 * A_GROUPS),
        out_shape=[jax.ShapeDtypeStruct((Bd, T, A_WIDTH), F32)] * (2 * A_GROUPS),
        compiler_params=_params("parallel"),
        name="a_sample",
    )(y3, yt, *views, hind, hindt)
    return [o.reshape(Bd * T, A_WIDTH) for o in outs]


def _mixer_a(xp, xs, gain, w_in, qk_gain, w_out, caches, tabs_p, tabs_s, Bn, S, Bd, T):
    flags = jnp.asarray([1, 1, 0] * A_GROUPS, jnp.int32)
    ones = jnp.ones((A_WIDTH,), F32)
    rows = [jnp.tile(qk_gain[0], A_HEADS), jnp.tile(qk_gain[1], A_HEADS), ones] * A_GROUPS
    gain_rows = jnp.stack(rows).reshape(3 * A_GROUPS, 1, A_WIDTH)
    mask_rows = jnp.stack([ones, ones, 0.0 * ones] * A_GROUPS).reshape(3 * A_GROUPS, 1, A_WIDTH)
    w = w_in.astype(BF16)
    yp = _proj_qk(xp, gain, w, flags, gain_rows, mask_rows, tabs_p, 1024, A_WIDTH)
    ys = _proj_qk(xs, gain, w, flags, gain_rows, mask_rows, tabs_s, xs.shape[0], A_WIDTH)
    wo = w_out.astype(BF16)
    pr = [_a_prompt(yp, g, Bn, S) for g in range(A_GROUPS)]
    xp = _row_call(_out_a_kernel, "out_a", xp,
                   [(o, A_WIDTH, 0) for o, _ in pr] + [(l, A_WIDTH, 0) for _, l in pr], [wo], 512)
    sr = _a_sample(ys, caches, Bd, T)
    xs = _row_call(_out_a_kernel, "out_a", xs, [(o, A_WIDTH, 0) for o in sr], [wo], xs.shape[0])
    yp3 = yp.reshape(Bn, S, A_GROUPS * 3 * A_WIDTH)
    ys3 = ys.reshape(Bd, T, A_GROUPS * 3 * A_WIDTH)
    new_p, new_s = [], []
    for g in range(A_GROUPS):
        win = min(A_SPAN * A_DILATIONS[g], S)
        c0 = (3 * g + 1) * A_WIDTH
        new_p.append(yp3[:, S - win:, c0:c0 + 2 * A_WIDTH].reshape(Bn, win, 2, A_HEADS, HEAD_DIM))
        new_s.append(ys3[:, :, c0:c0 + 2 * A_WIDTH].reshape(Bd, T, 2, A_HEADS, HEAD_DIM))
    return xp, xs, new_p, new_s


def _b_kmean_kernel(k_ref, o_ref):
    nblk = k_ref.shape[1] // B_BLOCK
    rows = [jnp.mean(k_ref[0, n * B_BLOCK:(n + 1) * B_BLOCK, :], axis=0, keepdims=True) for n in range(nblk)]
    o_ref[0] = jnp.concatenate(rows, axis=0)


def _seg_reduce(x, op, lane, width):
    s = 1
    while s < width:
        up = pltpu.roll(x, s, 1)
        dn = pltpu.roll(x, LANES - s, 1)
        x = op(x, jnp.where((lane & s) != 0, up, dn))
        s *= 2
    return x


def _b_gate_kernel(q_ref, kmh_ref, kml_ref, o_ref):
    i = pl.program_id(1)
    q = q_ref[0]
    qh, ql = _split2(q)
    kmh, kml = kmh_ref[0], kml_ref[0]
    gate = _dot(qh, kmh) + (_dot(qh, kml) + _dot(ql, kmh))
    tq = q.shape[0]
    nblk = gate.shape[1] // B_HEADS
    lane = lax.broadcasted_iota(jnp.int32, (tq, LANES), 1)
    blk = lane & (nblk - 1)
    blkf = blk.astype(F32)
    for c in range(gate.shape[1] // LANES):
        gch = jnp.where(blk < i, gate[:, c * LANES:(c + 1) * LANES], -jnp.inf)
        sel = jnp.zeros((tq, LANES), jnp.bool_)
        for _ in range(B_TOPK):
            mx = _seg_reduce(gch, jnp.maximum, lane, nblk)
            first = _seg_reduce(jnp.where(gch == mx, blkf, float(nblk)), jnp.minimum, lane, nblk)
            hit = blkf == first
            sel = sel | (hit & (mx > -jnp.inf))
            gch = jnp.where(hit, -jnp.inf, gch)
        o_ref[0, :, c * LANES:(c + 1) * LANES] = jnp.where(sel | (blk >= i), 0.0, NEG)


def _b_attn_kernel(qi_ref, kn_ref, q_ref, k_ref, vt_ref, o_ref, m_ref, l_ref, acc_ref):
    t = pl.program_id(1)
    i, n = qi_ref[t], kn_ref[t]
    tq = B_BLOCK
    wide = B_REP * tq

    @pl.when(n == 0)
    def _():
        m_ref[...] = jnp.full_like(m_ref, -jnp.inf)
        l_ref[...] = jnp.zeros_like(l_ref)
        acc_ref[...] = jnp.zeros_like(acc_ref)

    def sweep(causal):
        if causal:
            kk = lax.broadcasted_iota(jnp.int32, (B_BLOCK, wide), 0)
            qq = lax.broadcasted_iota(jnp.int32, (B_BLOCK, wide), 1) & (tq - 1)
            keep = kk <= qq
        for kvh in range(B_KV_HEADS):
            k = k_ref[0, :, kvh * LANES:(kvh + 1) * LANES]
            s = _dot_nt(k, q_ref[0, 0, kvh])
            if causal:
                s = jnp.where(keep, s, NEG)
            m_old = m_ref[kvh]
            m_new = jnp.maximum(m_old, jnp.max(s, axis=0, keepdims=True))
            alpha = jnp.exp(m_old - m_new)
            p = jnp.exp(s - m_new)
            l_ref[kvh] = alpha * l_ref[kvh] + jnp.sum(p, axis=0, keepdims=True)
            acc_ref[kvh] = alpha * acc_ref[kvh] + _dot(vt_ref[0, kvh], p.astype(BF16))
            m_ref[kvh] = m_new

    @pl.when(n < i)
    def _():
        sweep(False)

    @pl.when(n == i)
    def _():
        sweep(True)
        for kvh in range(B_KV_HEADS):
            ot = acc_ref[kvh] / l_ref[kvh]
            for r in range(0, B_REP, 2):
                pair = [ot[:, (r + e) * tq:(r + e + 1) * tq].T for e in range(2)]
                c0 = (kvh * B_REP + r) * HEAD_DIM
                o_ref[0, :, c0:c0 + 2 * HEAD_DIM] = jnp.concatenate(pair, axis=1)


def _b_prompt(y, Bn, S):
    nblk = S // B_BLOCK
    assert nblk & (nblk - 1) == 0 and LANES % nblk == 0
    nq = B_HEADS * HEAD_DIM
    nk = B_KV_HEADS * HEAD_DIM
    y3 = y.reshape(Bn, S, nq + 2 * nk)
    kmean = pl.pallas_call(
        _b_kmean_kernel,
        grid=(Bn,),
        in_specs=[pl.BlockSpec((1, S, nk), lambda b: (b, 0, nq // nk))],
        out_specs=pl.BlockSpec((1, nblk, nk), lambda b: (b, 0, 0)),
        out_shape=jax.ShapeDtypeStruct((Bn, nblk, nk), F32),
        compiler_params=_params("parallel"),
        name="b_kmean",
    )(y3)
    km = kmean.reshape(Bn, nblk, B_KV_HEADS, HEAD_DIM)
    km = jnp.repeat(km, B_REP, axis=2)
    eye = jnp.eye(B_HEADS, dtype=F32)
    kmbd = jnp.einsum('bnhe,hg->bhegn', km, eye).reshape(Bn, nq, B_HEADS * nblk)
    kmh = kmbd.astype(BF16)
    kml = (kmbd - kmh.astype(F32)).astype(BF16)
    bias = pl.pallas_call(
        _b_gate_kernel,
        grid=(Bn, nblk),
        in_specs=[
            pl.BlockSpec((1, B_BLOCK, nq), lambda b, i: (b, i, 0)),
            pl.BlockSpec((1, nq, B_HEADS * nblk), lambda b, i: (b, 0, 0)),
            pl.BlockSpec((1, nq, B_HEADS * nblk), lambda b, i: (b, 0, 0)),
        ],
        out_specs=pl.BlockSpec((1, B_BLOCK, B_HEADS * nblk), lambda b, i: (b, i, 0)),
        out_shape=jax.ShapeDtypeStruct((Bn, S, B_HEADS * nblk), F32),
        compiler_params=_params("parallel", "arbitrary"),
        name="b_gate",
    )(y3, kmh, kml)
    q = (y3[:, :, :nq] * (HEAD_DIM ** -0.5)).astype(BF16).reshape(Bn, S, B_HEADS, HEAD_DIM)
    pad = LANES - HEAD_DIM - nblk
    q_aug = jnp.concatenate([q, bias.astype(BF16).reshape(Bn, S, B_HEADS, nblk),
                             jnp.zeros((Bn, S, B_HEADS, pad), BF16)], axis=-1)
    q_aug = q_aug.reshape(Bn, nblk, B_BLOCK, B_KV_HEADS, B_REP, LANES).transpose(0, 1, 3, 4, 2, 5)
    q_aug = q_aug.reshape(Bn, nblk, B_KV_HEADS, B_REP * B_BLOCK, LANES)
    k = y3[:, :, nq:nq + nk].astype(BF16).reshape(Bn, S, B_KV_HEADS, HEAD_DIM)
    onehot = jax.nn.one_hot(jnp.arange(S) // B_BLOCK, nblk, dtype=BF16)
    onehot = jnp.broadcast_to(onehot[None, :, None, :], (Bn, S, B_KV_HEADS, nblk))
    k_aug = jnp.concatenate([k, onehot, jnp.zeros((Bn, S, B_KV_HEADS, pad), BF16)], axis=-1)
    k_aug = k_aug.reshape(Bn, S, B_KV_HEADS * LANES)
    vt = y3[:, :, nq + nk:].astype(BF16).reshape(Bn, S, B_KV_HEADS, HEAD_DIM).transpose(0, 2, 3, 1)
    pairs = [(i, n) for i in range(nblk) for n in range(i + 1)]
    qi = jnp.asarray([p[0] for p in pairs], jnp.int32)
    kn = jnp.asarray([p[1] for p in pairs], jnp.int32)
    o = pl.pallas_call(
        _b_attn_kernel,
        grid_spec=pltpu.PrefetchScalarGridSpec(
            num_scalar_prefetch=2,
            grid=(Bn, len(pairs)),
            in_specs=[
                pl.BlockSpec((1, 1, B_KV_HEADS, B_REP * B_BLOCK, LANES),
                             lambda b, t, qi, kn: (b, qi[t], 0, 0, 0)),
                pl.BlockSpec((1, B_BLOCK, B_KV_HEADS * LANES), lambda b, t, qi, kn: (b, kn[t], 0)),
                pl.BlockSpec((1, B_KV_HEADS, HEAD_DIM, B_BLOCK), lambda b, t, qi, kn: (b, 0, 0, kn[t])),
            ],
            out_specs=pl.BlockSpec((1, B_BLOCK, nq), lambda b, t, qi, kn: (b, qi[t], 0)),
            scratch_shapes=[
                pltpu.VMEM((B_KV_HEADS, 1, B_REP * B_BLOCK), F32),
                pltpu.VMEM((B_KV_HEADS, 1, B_REP * B_BLOCK), F32),
                pltpu.VMEM((B_KV_HEADS, HEAD_DIM, B_REP * B_BLOCK), F32),
            ],
        ),
        out_shape=jax.ShapeDtypeStruct((Bn, S, nq), F32),
        compiler_params=_params("parallel", "arbitrary"),
        name="b_attn",
    )(qi, kn, q_aug, k_aug, vt)
    return o.reshape(Bn * S, nq)


def _b_sample_kernel(n_pages, T, pt_ref, q_ref, kn_ref, vn_ref, *refs):
    del pt_ref
    page_refs = refs[:n_pages]
    o_ref = refs[n_pages]
    nk = B_KV_HEADS * HEAD_DIM
    q = q_ref[0]
    qh, ql = _split2(q)
    rows = q.shape[0]
    page_rows = page_refs[0].shape[2]
    pages_per_block = B_BLOCK // page_rows
    n_blocks = n_pages // pages_per_block
    scale = HEAD_DIM ** -0.5
    lane = lax.broadcasted_iota(jnp.int32, (1, LANES), 1)
    lanef = lax.broadcasted_iota(jnp.int32, (rows, LANES), 1).astype(F32)
    scores = []
    km = jnp.zeros((nk, LANES), F32)
    for p in range(n_pages):
        kt = page_refs[p][0, :nk, :]
        scores.append(_dot(qh, kt.astype(BF16)) * scale)
        ksum = jnp.sum(kt, axis=1, keepdims=True) * (1.0 / B_BLOCK)
        km = km + ksum * (lane == p // pages_per_block).astype(F32)
    kmh, kml = _split2(km)
    gate = _dot(qh, kmh) + (_dot(qh, kml) + _dot(ql, kmh))
    gate = jnp.where(lanef < float(n_blocks), gate, -jnp.inf)
    sel = jnp.zeros((rows, LANES), jnp.bool_)
    for _ in range(min(B_TOPK, n_blocks)):
        mx = jnp.max(gate, axis=1, keepdims=True)
        first = jnp.min(jnp.where(gate == mx, lanef, float(LANES)), axis=1, keepdims=True)
        hit = lanef == first
        sel = sel | (hit & (mx > -jnp.inf))
        gate = jnp.where(hit, -jnp.inf, gate)
    self = sel.astype(F32)
    qidx = lax.broadcasted_iota(jnp.int32, (rows, 1), 0) % T
    kn, vn = kn_ref[0], vn_ref[0]
    own = []
    for j in range(T):
        sj = jnp.sum(q * kn[j:j + 1, :], axis=1, keepdims=True) * scale
        own.append(jnp.where(qidx >= j, sj, NEG))
    m = own[0]
    for j in range(1, T):
        m = jnp.maximum(m, own[j])
    for p in range(n_pages):
        b = p // pages_per_block
        scores[p] = jnp.where(self[:, b:b + 1] > 0.0, scores[p], NEG)
        m = jnp.maximum(m, jnp.max(scores[p], axis=-1, keepdims=True))
    den = jnp.zeros((rows, 1), F32)
    acc = jnp.zeros((rows, nk), F32)
    for j in range(T):
        pj = jnp.exp(own[j] - m)
        den = den + pj
        acc = acc + pj * vn[j:j + 1, :]
    for p in range(n_pages):
        pp = jnp.exp(scores[p] - m)
        den = den + jnp.sum(pp, axis=-1, keepdims=True)
        acc = acc + _dot_nt(pp.astype(BF16), page_refs[p][0, nk:, :].astype(BF16))
    acc = acc / den
    kvh = lax.broadcasted_iota(jnp.int32, (rows, 1), 0) // (T * B_REP)
    out = jnp.zeros((rows, HEAD_DIM), F32)
    for h in range(B_KV_HEADS):
        out = out + jnp.where(kvh == h, acc[:, h * HEAD_DIM:(h + 1) * HEAD_DIM], 0.0)
    o_ref[0] = out


def _b_sample(y, pool, page_table, Bd, T):
    nq = B_HEADS * HEAD_DIM
    nk = B_KV_HEADS * HEAD_DIM
    n_pages = page_table.shape[1]
    page_rows = pool.shape[1]
    assert B_BLOCK % page_rows == 0 and (n_pages * page_rows) % B_BLOCK == 0 and T <= B_BLOCK
    assert n_pages * page_rows // B_BLOCK <= LANES
    y3 = y.reshape(Bd, T, nq + 2 * nk)
    q = y3[:, :, :nq].reshape(Bd, T, B_HEADS, HEAD_DIM).transpose(0, 2, 1, 3)
    kvsel = jnp.asarray(np.kron(np.eye(B_KV_HEADS), np.ones((B_REP, 1))), F32)
    qbd = (q[:, :, :, None, :] * kvsel[None, :, None, :, None]).reshape(Bd, B_HEADS * T, nk)
    kn = y3[:, :, nq:nq + nk]
    vn = y3[:, :, nq + nk:]
    pool_t = pool.transpose(0, 2, 3, 4, 1).reshape(pool.shape[0], 2 * nk, page_rows)
    page_specs = [pl.BlockSpec((1, 2 * nk, page_rows), lambda b, pt, p=p: (pt[b, p], 0, 0))
                  for p in range(n_pages)]
    o = pl.pallas_call(
        functools.partial(_b_sample_kernel, n_pages, T),
        grid_spec=pltpu.PrefetchScalarGridSpec(
            num_scalar_prefetch=1,
            grid=(Bd,),
            in_specs=[
                pl.BlockSpec((1, B_HEADS * T, nk), lambda b, pt: (b, 0, 0)),
                pl.BlockSpec((1, T, nk), lambda b, pt: (b, 0, 0)),
                pl.BlockSpec((1, T, nk), lambda b, pt: (b, 0, 0)),
            ] + page_specs,
            out_specs=pl.BlockSpec((1, B_HEADS * T, HEAD_DIM), lambda b, pt: (b, 0, 0)),
        ),
        out_shape=jax.ShapeDtypeStruct((Bd, B_HEADS * T, HEAD_DIM), F32),
        compiler_params=_params("parallel"),
        name="b_sample",
    )(page_table, qbd, kn, vn, *([pool_t] * n_pages))
    return o.reshape(Bd, B_HEADS, T, HEAD_DIM).transpose(0, 2, 1, 3).reshape(Bd * T, nq)


def _mixer_b(xp, xs, gain, w_in, qk_gain, w_out, pool, page_table, tabs_p, tabs_s, Bn, S, Bd, T):
    tn = 512
    nq = B_HEADS * HEAD_DIM
    nk = B_KV_HEADS * HEAD_DIM
    flags = jnp.asarray([1, 1, 1], jnp.int32)
    ones = jnp.ones((nk,), F32)
    qg = jnp.tile(qk_gain[0], tn // HEAD_DIM)
    kg = jnp.concatenate([jnp.tile(qk_gain[1], B_KV_HEADS), ones])
    gain_rows = jnp.stack([qg, qg, kg]).reshape(3, 1, tn)
    mask_rows = jnp.stack([jnp.ones((tn,), F32), jnp.ones((tn,), F32),
                           jnp.concatenate([ones, 0.0 * ones])]).reshape(3, 1, tn)
    w = w_in.astype(BF16)
    yp = _proj_qk(xp, gain, w, flags, gain_rows, mask_rows, tabs_p, 1024, tn)
    ys = _proj_qk(xs, gain, w, flags, gain_rows, mask_rows, tabs_s, xs.shape[0], tn)
    wo = w_out.astype(BF16)
    op = _b_prompt(yp, Bn, S)
    xp = _row_call(_out_plain_kernel, "out_b", xp, [(op, nq, 0)], [wo], 512)
    osm = _b_sample(ys, pool, page_table, Bd, T)
    xs = _row_call(_out_plain_kernel, "out_b", xs, [(osm, nq, 0)], [wo], xs.shape[0])
    kv_p = yp[:, nq:].reshape(Bn, S, 2, B_KV_HEADS, HEAD_DIM)
    kv_s = ys[:, nq:].reshape(Bd, T, 2, B_KV_HEADS, HEAD_DIM)
    return xp, xs, kv_p, kv_s


def _gla_kernel(nv, q_ref, k_ref, v_ref, glr_ref, wg_ref, bg_ref, tri_ref, blk_ref, s0_ref,
                o_ref, sT_ref, s_ref, qp_ref, kp_ref, vp_ref, gp_ref):
    t = pl.program_id(1)
    Tt = SEQ_TILE

    @pl.when(t == 0)
    def _():
        s_ref[...] = s0_ref[0]

    if nv < Tt:
        qp_ref[...] = jnp.zeros_like(qp_ref)
        kp_ref[...] = jnp.zeros_like(kp_ref)
        vp_ref[...] = jnp.zeros_like(vp_ref)
        gp_ref[...] = jnp.zeros_like(gp_ref)
    qp_ref[0:nv, :] = q_ref[0]
    kp_ref[0:nv, :] = k_ref[0]
    vp_ref[0:nv, :] = v_ref[0]
    gp_ref[0:nv, :] = glr_ref[0]
    q, k, v = qp_ref[...], kp_ref[...], vp_ref[...]

    x = _dot(gp_ref[...].astype(BF16), wg_ref[...]) + bg_ref[...]
    la = (jnp.minimum(x, 0.0) - jnp.log(1.0 + jnp.exp(-jnp.abs(x)))) * (1.0 / C_TAU)
    row = lax.broadcasted_iota(jnp.int32, (Tt, C_KEY), 0)
    la = jnp.where(row < nv, la, 0.0)
    b = _sel_left(tri_ref[...], la, terms=3)
    bend = _sel_left(blk_ref[...], la, terms=3)
    qe = q * (C_DK ** -0.5) * jnp.exp(b)
    ke = k * jnp.exp(-b)
    kd = k * jnp.exp(bend - b)
    kdt = kd.T
    bendt = bend.T
    qeb, keb, vb = qe.astype(BF16), ke.astype(BF16), v.astype(BF16)
    causal = tri_ref[...] > 0
    lane_t = lax.broadcasted_iota(jnp.int32, (C_DK, Tt), 1)
    n_chunks = -(-nv // C_CHUNK)
    for h in range(C_HEADS):
        ks = slice(h * C_DK, (h + 1) * C_DK)
        vs = slice(h * C_DV, (h + 1) * C_DV)
        att = jnp.where(causal, _dot_nt(qeb[:, ks], keb[:, ks]), 0.0)
        o_intra = _dot(att.astype(BF16), vb[:, vs])
        st = s_ref[h]
        parts = []
        for c in range(n_chunks):
            rows = slice(c * C_CHUNK, (c + 1) * C_CHUNK)
            parts.append(o_intra[rows] + _dot(qeb[rows, ks], st.astype(BF16)))
            in_chunk = (lane_t >= c * C_CHUNK) & (lane_t < (c + 1) * C_CHUNK)
            last = lane_t == (c + 1) * C_CHUNK - 1
            dec = jnp.exp(jnp.sum(jnp.where(last, bendt[ks, :], 0.0), axis=1, keepdims=True))
            kdc = jnp.where(in_chunk, kdt[ks, :], 0.0).astype(BF16)
            st = dec * st + _dot(kdc, vb[:, vs])
        s_ref[h] = st
        oh = parts[0] if n_chunks == 1 else jnp.concatenate(parts, axis=0)
        o_ref[0, :, vs] = oh[0:nv]

    @pl.when(t == pl.num_programs(1) - 1)
    def _():
        sT_ref[0] = s_ref[...]


def _gla(y, w_gate2, b_gate, s0, nb, nt, nv):
    Tt = SEQ_TILE
    y3 = y.reshape(nb * nt, nv, C_NPAD)
    wg = jnp.zeros((LANES, C_KEY), F32).at[:C_RANK].set(w_gate2).astype(BF16)
    idx = np.arange(Tt)
    same = (idx[:, None] // C_CHUNK) == (idx[None, :] // C_CHUNK)
    tri = jnp.asarray(same & (idx[None, :] <= idx[:, None]), BF16)
    blk = jnp.asarray(same, BF16)

    def yspec(width, cb):
        return pl.BlockSpec((1, nv, width), lambda b, t: (b * nt + t, 0, cb))

    def cspec(a):
        return pl.BlockSpec(a.shape, lambda b, t, nd=a.ndim: (0,) * nd)

    bg = b_gate.reshape(1, C_KEY)
    o, sT = pl.pallas_call(
        functools.partial(_gla_kernel, nv),
        grid=(nb, nt),
        in_specs=[
            yspec(C_KEY, 0), yspec(C_KEY, 1), yspec(C_VAL, 1), yspec(LANES, (2 * C_KEY + 2 * C_VAL) // LANES),
            cspec(wg), cspec(bg), cspec(tri), cspec(blk),
            pl.BlockSpec((1, C_HEADS, C_DK, C_DV), lambda b, t: (b, 0, 0, 0)),
        ],
        out_specs=[
            pl.BlockSpec((1, nv, C_VAL), lambda b, t: (b * nt + t, 0, 0)),
            pl.BlockSpec((1, C_HEADS, C_DK, C_DV), lambda b, t: (b, 0, 0, 0)),
        ],
        out_shape=[
            jax.ShapeDtypeStruct((nb * nt, nv, C_VAL), F32),
            jax.ShapeDtypeStruct((nb, C_HEADS, C_DK, C_DV), F32),
        ],
        scratch_shapes=[
            pltpu.VMEM((C_HEADS, C_DK, C_DV), F32),
            pltpu.VMEM((Tt, C_KEY), F32), pltpu.VMEM((Tt, C_KEY), F32),
            pltpu.VMEM((Tt, C_VAL), F32), pltpu.VMEM((Tt, LANES), F32),
        ],
        compiler_params=_params("parallel", "arbitrary"),
        name="gla",
    )(y3, y3, y3, y3, wg, bg, tri, blk, s0)
    return o.reshape(nb * nt * nv, C_VAL), sT


def _mixer_c(xp, xs, gain, w_in, w_gate2, b_gate, norm_g, w_out, state, Bn, S, Bd, T):
    n_in = w_in.shape[1]
    w = jnp.zeros((D_MODEL, C_NPAD), BF16).at[:, :n_in].set(w_in.astype(BF16))
    yp = _proj(xp, gain, w, 1024, 640)
    ys = _proj(xs, gain, w, xs.shape[0], 640)
    zero = jnp.zeros((Bn, C_HEADS, C_DK, C_DV), F32)
    op, sp = _gla(yp, w_gate2, b_gate, zero, Bn, S // SEQ_TILE, SEQ_TILE)
    osm, ss = _gla(ys, w_gate2, b_gate, state, Bd, 1, T)
    wo = w_out.astype(BF16)
    ng = norm_g.reshape(1, C_DV)
    rcb = (2 * C_KEY + C_VAL) // C_VAL
    xp = _row_call(_out_c_kernel, "out_c", xp, [(op, C_VAL, 0), (yp, C_VAL, rcb)], [ng, wo], 512)
    xs = _row_call(_out_c_kernel, "out_c", xs, [(osm, C_VAL, 0), (ys, C_VAL, rcb)], [ng, wo], xs.shape[0])
    return xp, xs, sp, ss


def _ssd_kernel(nv, xa_ref, xb_ref, bc_ref, dt_ref, cs_ref, cw_ref, cb_ref, dtb_ref, alog_ref, dsk_ref,
                tri_ref, exp_ref, expt_ref, h0_ref, y_ref, hT_ref, h_ref, xp_ref, dtp_ref, yd_ref):
    t = pl.program_id(1)
    Tt = SEQ_TILE
    pre = 8

    @pl.when(t == 0)
    def _():
        h_ref[...] = h0_ref[0]
        xp_ref[0:pre, :] = cs_ref[0]

    if nv < Tt:
        xp_ref[pre:, :] = jnp.zeros((Tt, D_XBC), F32)
        dtp_ref[...] = jnp.zeros_like(dtp_ref)
    xp_ref[pre:pre + nv, 0:1024] = xa_ref[0]
    xp_ref[pre:pre + nv, 1024:2048] = xb_ref[0]
    xp_ref[pre:pre + nv, 2048:3072] = bc_ref[0]
    dtp_ref[0:nv, :] = dt_ref[0]

    conv = cb_ref[...]
    for w in range(D_CONV):
        conv = conv + xp_ref[pl.ds(pre - (D_CONV - 1) + w, Tt), :] * cw_ref[w:w + 1, :]
    xp_ref[0:pre, :] = xp_ref[Tt:Tt + pre, :]
    xbc = _silu(conv)
    x = xbc[:, :D_INNER]
    nbc = D_GROUPS * D_STATE
    bm = xbc[:, D_INNER:D_INNER + nbc].astype(BF16)
    cm = xbc[:, D_INNER + nbc:].astype(BF16)

    row = lax.broadcasted_iota(jnp.int32, (Tt, LANES), 0)
    lane = lax.broadcasted_iota(jnp.int32, (Tt, LANES), 1)
    live = (row < nv) & (lane < D_HEADS)
    dt = jnp.where(live, _softplus(dtp_ref[...] + dtb_ref[...]), 0.0)
    cum = _sel_left(tri_ref[...], dt * (-jnp.exp(alog_ref[...])), terms=3)
    cumt = cum.T
    cend = cum[Tt - 1:Tt, :]
    ex = exp_ref[...]
    stack = jnp.concatenate([jnp.exp(cend - cum) * dt, dt, jnp.exp(cum)], axis=0)
    wide = _sel_right(stack, ex)
    x_state = (x * wide[0:Tt]).astype(BF16)
    x_dt = (x * wide[Tt:2 * Tt]).astype(BF16)
    off_scale = wide[2 * Tt:]

    causal = tri_ref[...] > 0
    gw = D_INNER // D_GROUPS
    for g in range(D_GROUPS):
        bg = bm[:, g * D_STATE:(g + 1) * D_STATE]
        cg = cm[:, g * D_STATE:(g + 1) * D_STATE]
        cb = _dot_nt(cg, bg)
        for jj in range(0, D_HPG, 2):
            pair = []
            for j in (g * D_HPG + jj, g * D_HPG + jj + 1):
                seg = cum[:, j:j + 1] - cumt[j:j + 1, :]
                lmat = jnp.exp(jnp.where(causal, seg, -jnp.inf))
                pair.append(_dot((cb * lmat).astype(BF16), x_dt[:, j * D_HEADDIM:(j + 1) * D_HEADDIM]))
            c0 = (g * D_HPG + jj) * D_HEADDIM
            yd_ref[:, c0:c0 + 2 * D_HEADDIM] = jnp.concatenate(pair, axis=1)
        hg = h_ref[g * gw:(g + 1) * gw, :]
        y_off = _dot_nt(cg, hg.astype(BF16)) * off_scale[:, g * gw:(g + 1) * gw]
        yd_ref[:, g * gw:(g + 1) * gw] += y_off
    y = yd_ref[...] + dsk_ref[...] * x
    y_ref[0] = y[0:nv]

    dcol = jnp.broadcast_to(jnp.exp(cumt[:, Tt - 1:Tt]), (LANES, D_STATE))
    dfull = _sel_left(expt_ref[...], dcol)
    xst = x_state.astype(F32).T.astype(BF16)
    for g in range(D_GROUPS):
        rows = slice(g * gw, (g + 1) * gw)
        h_ref[rows, :] = dfull[rows] * h_ref[rows, :] + _dot(xst[rows], bm[:, g * D_STATE:(g + 1) * D_STATE])

    @pl.when(t == pl.num_programs(1) - 1)
    def _():
        hT_ref[0] = h_ref[...]


def _ssd(y, conv_w, conv_b, dt_bias, a_log, d_skip, h0, c0, nb, nt, nv):
    Tt = SEQ_TILE
    y3 = y.reshape(nb * nt, nv, D_NPAD)
    idx = np.arange(Tt)
    tri = jnp.asarray(idx[None, :] <= idx[:, None], BF16)
    ex_np = np.zeros((LANES, D_INNER), np.float32)
    ex_np[np.arange(D_INNER) // D_HEADDIM, np.arange(D_INNER)] = 1.0
    ex = jnp.asarray(ex_np, BF16)
    ext = jnp.asarray(ex_np.T, BF16)

    def pad_row(v):
        return jnp.zeros((1, LANES), F32).at[0, :D_HEADS].set(v)

    cs = jnp.zeros((nb, 8, D_XBC), F32).at[:, 8 - (D_CONV - 1):].set(c0)
    dsk = jnp.repeat(d_skip, D_HEADDIM).reshape(1, D_INNER)

    def yspec(width, cb):
        return pl.BlockSpec((1, nv, width), lambda b, t: (b * nt + t, 0, cb))

    def cspec(a):
        return pl.BlockSpec(a.shape, lambda b, t, nd=a.ndim: (0,) * nd)

    consts = [conv_w, conv_b.reshape(1, D_XBC), pad_row(dt_bias), pad_row(a_log), dsk, tri, ex, ext]
    yo, hT = pl.pallas_call(
        functools.partial(_ssd_kernel, nv),
        grid=(nb, nt),
        in_specs=[yspec(1024, 2), yspec(1024, 3), yspec(1024, 4), yspec(LANES, D_DT_COL),
                  pl.BlockSpec((1, 8, D_XBC), lambda b, t: (b, 0, 0))]
        + [cspec(a) for a in consts]
        + [pl.BlockSpec((1, D_INNER, D_STATE), lambda b, t: (b, 0, 0))],
        out_specs=[
            pl.BlockSpec((1, nv, D_INNER), lambda b, t: (b * nt + t, 0, 0)),
            pl.BlockSpec((1, D_INNER, D_STATE), lambda b, t: (b, 0, 0)),
        ],
        out_shape=[
            jax.ShapeDtypeStruct((nb * nt, nv, D_INNER), F32),
            jax.ShapeDtypeStruct((nb, D_INNER, D_STATE), F32),
        ],
        scratch_shapes=[
            pltpu.VMEM((D_INNER, D_STATE), F32),
            pltpu.VMEM((Tt + 8, D_XBC), F32),
            pltpu.VMEM((Tt, LANES), F32),
            pltpu.VMEM((Tt, D_INNER), F32),
        ],
        compiler_params=_params("parallel", "arbitrary"),
        name="ssd",
    )(y3, y3, y3, y3, cs, *consts, h0)
    return yo.reshape(nb * nt * nv, D_INNER), hT


def _mixer_d(xp, xs, gain, w_in, conv_w, conv_b, dt_bias, a_log, d_skip, norm_g, w_out,
             ssm_state, conv_state, Bn, S, Bd, T):
    n_in = w_in.shape[1]
    w = jnp.zeros((D_MODEL, D_NPAD), BF16).at[:, :n_in].set(w_in.astype(BF16))
    yp = _proj(xp, gain, w, 1024, 768)
    ys = _proj(xs, gain, w, xs.shape[0], 768)
    h0p = jnp.zeros((Bn, D_INNER, D_STATE), F32)
    c0p = jnp.zeros((Bn, D_CONV - 1, D_XBC), F32)
    op, hp = _ssd(yp, conv_w, conv_b, dt_bias, a_log, d_skip, h0p, c0p, Bn, S // SEQ_TILE, SEQ_TILE)
    h0s = ssm_state.reshape(Bd, D_INNER, D_STATE)
    osm, hs = _ssd(ys, conv_w, conv_b, dt_bias, a_log, d_skip, h0s, conv_state, Bd, 1, T)
    wo = w_out.astype(BF16)
    ng = norm_g.reshape(1, D_INNER)
    xp = _row_call(_out_d_kernel, "out_d", xp, [(op, D_INNER, 0), (yp, D_INNER, 0)], [ng, wo], 512)
    xs = _row_call(_out_d_kernel, "out_d", xs, [(osm, D_INNER, 0), (ys, D_INNER, 0)], [ng, wo], xs.shape[0])
    keep = D_CONV - 1
    xbc_p = yp.reshape(Bn, S, D_NPAD)[:, :, D_INNER:D_INNER + D_XBC]
    xbc_s = ys.reshape(Bd, T, D_NPAD)[:, :, D_INNER:D_INNER + D_XBC]
    cp = jnp.concatenate([c0p, xbc_p], axis=1)[:, -keep:] if S < keep else xbc_p[:, S - keep:]
    cs = jnp.concatenate([conv_state, xbc_s], axis=1)[:, -keep:]
    return (xp, xs, hp.reshape(Bn, D_HEADS, D_HEADDIM, D_STATE), hs.reshape(Bd, D_HEADS, D_HEADDIM, D_STATE),
            cp, cs)


def kernel(x_prompt, x_sample, cache_a_w1, cache_a_w2, cache_a_w3, cache_b_kv, page_table, state_c, state_d_ssm, state_d_conv, norm_gain, w_ffn_up, w_ffn_down, w_a_in, a_qk_gain, w_a_out, w_b_in, b_qk_gain, w_b_out, w_c_in, w_c_gate2, b_c_gate, c_norm_gain, w_c_out, w_d_in, d_conv_w, d_conv_b, d_dt_bias, d_a_log, d_skip, d_norm_gain, w_d_out):
    Bn, S, _ = x_prompt.shape
    Bd, T, _ = x_sample.shape
    depth = norm_gain.shape[0]
    past_len = page_table.shape[1] * cache_b_kv.shape[2]
    tabs_p = _rope_tables(jnp.arange(S, dtype=jnp.int32))
    tabs_s = _rope_tables(jnp.tile(past_len + jnp.arange(T, dtype=jnp.int32), Bd))
    xp = x_prompt.reshape(Bn * S, D_MODEL)
    xs = x_sample.reshape(Bd * T, D_MODEL)
    ts = xs.shape[0]
    w_up = w_ffn_up.astype(BF16)
    w_down = w_ffn_down.astype(BF16)
    outs = {k: [] for k in ("a0p", "a0s", "a1p", "a1s", "a2p", "a2s", "bp", "bs", "cp", "cs",
                            "hp", "hs", "dp", "ds")}
    for i in range(depth):
        m, j = i % 4, i // 4
        g = norm_gain[i]
        xp = _ffn(xp, g[0], w_up[i, 0], w_down[i, 0], 1024)
        xs = _ffn(xs, g[0], w_up[i, 0], w_down[i, 0], ts)
        if m == 0:
            xp, xs, new_p, new_s = _mixer_a(xp, xs, g[1], w_a_in[j], a_qk_gain[j], w_a_out[j],
                                            (cache_a_w1[j], cache_a_w2[j], cache_a_w3[j]),
                                            tabs_p, tabs_s, Bn, S, Bd, T)
            for gi in range(A_GROUPS):
                outs["a%dp" % gi].append(new_p[gi])
                outs["a%ds" % gi].append(new_s[gi])
        elif m == 1:
            xp, xs, kvp, kvs = _mixer_b(xp, xs, g[1], w_b_in[j], b_qk_gain[j], w_b_out[j],
                                        cache_b_kv[j], page_table, tabs_p, tabs_s, Bn, S, Bd, T)
            outs["bp"].append(kvp)
            outs["bs"].append(kvs)
        elif m == 2:
            xp, xs, sp, ss = _mixer_c(xp, xs, g[1], w_c_in[j], w_c_gate2[j], b_c_gate[j], c_norm_gain[j],
                                      w_c_out[j], state_c[j], Bn, S, Bd, T)
            outs["cp"].append(sp)
            outs["cs"].append(ss)
        else:
            xp, xs, hp, hs, cp, cs = _mixer_d(xp, xs, g[1], w_d_in[j], d_conv_w[j], d_conv_b[j], d_dt_bias[j],
                                              d_a_log[j], d_skip[j], d_norm_gain[j], w_d_out[j],
                                              state_d_ssm[j], state_d_conv[j], Bn, S, Bd, T)
            outs["hp"].append(hp)
            outs["hs"].append(hs)
            outs["dp"].append(cp)
            outs["ds"].append(cs)
        xp = _ffn(xp, g[2], w_up[i, 1], w_down[i, 1], 1024)
        xs = _ffn(xs, g[2], w_up[i, 1], w_down[i, 1], ts)
    st = {k: jnp.stack(v) for k, v in outs.items()}
    return (xp.reshape(Bn, S, D_MODEL), xs.reshape(Bd, T, D_MODEL),
            st["a0p"], st["a0s"], st["a1p"], st["a1s"], st["a2p"], st["a2s"],
            st["bp"], st["bs"], st["cp"], st["cs"], st["hp"], st["hs"], st["dp"], st["ds"])
```

```python
import functools
import math

import numpy as np
import jax
import jax.numpy as jnp
from jax import lax
from jax.experimental import pallas as pl
from jax.experimental.pallas import tpu as pltpu

F32 = jnp.float32
BF16 = jnp.bfloat16

D_MODEL = 1024
HEAD_DIM = 64
ROPE_DIM = HEAD_DIM // 4
ROPE_THETA = 500000.0
EPS = 1e-6
D_FF = 2816
NEG = -1e30

A_GROUPS = 3
A_DILATIONS = (1, 4, 16)
A_HEADS = 8
A_WIDTH = A_HEADS * HEAD_DIM
A_SPAN = 128

B_HEADS = 16
B_KV_HEADS = 4
B_REP = B_HEADS // B_KV_HEADS
B_BLOCK = 256
B_TOPK = 3

C_HEADS = 4
C_KEY = 512
C_VAL = 1024
C_DK = 128
C_DV = 256
C_RANK = 16
C_TAU = 16.0
C_CHUNK = 32
C_NPAD = 3200

D_INNER = 2048
D_HEADDIM = 64
D_HEADS = 32
D_GROUPS = 4
D_HPG = 8
D_STATE = 128
D_CONV = 4
D_XBC = D_INNER + 2 * D_GROUPS * D_STATE
D_NPAD = 5376
D_DT_COL = (D_INNER + D_XBC) // 128

LANES = 128
FFN_ROWS = 256
SEQ_TILE = 128


def _params(*sem):
    return pltpu.CompilerParams(dimension_semantics=sem)


def _split2(a):
    hi = a.astype(BF16)
    lo = (a - hi.astype(F32)).astype(BF16)
    return hi, lo


def _split3(a):
    hi = a.astype(BF16)
    r = a - hi.astype(F32)
    mid = r.astype(BF16)
    lo = (r - mid.astype(F32)).astype(BF16)
    return hi, mid, lo


def _dot(a, b):
    return jnp.dot(a, b, preferred_element_type=F32)


def _dot_nt(a, b):
    return lax.dot_general(a, b, (((1,), (1,)), ((), ())), preferred_element_type=F32)


def _sel_right(a, sel01, terms=2):
    parts = _split2(a) if terms == 2 else _split3(a)
    out = None
    for p in reversed(parts):
        d = _dot(p, sel01)
        out = d if out is None else out + d
    return out


def _sel_left(sel01, a, terms=2):
    parts = _split2(a) if terms == 2 else _split3(a)
    out = None
    for p in reversed(parts):
        d = _dot(sel01, p)
        out = d if out is None else out + d
    return out


def _silu(x):
    return x * jax.nn.sigmoid(x)


def _softplus(x):
    return jnp.maximum(x, 0.0) + jnp.log(1.0 + jnp.exp(-jnp.abs(x)))


def _rms_rows(x, g):
    ms = jnp.mean(x * x, axis=-1, keepdims=True)
    return x * lax.rsqrt(ms + EPS) * g


def _ffn_kernel(x_ref, g_ref, wg_ref, wu_ref, wd_ref, o_ref):
    x = x_ref[...]
    h = _rms_rows(x, g_ref[...]).astype(BF16)
    a = _dot(h, wg_ref[...])
    u = _dot(h, wu_ref[...])
    act = (_silu(a) * u).astype(BF16)
    o_ref[...] = x + 0.5 * _dot(act, wd_ref[...])


def _ffn(x, gain, w_up, w_down, tm):
    T = x.shape[0]
    tm = min(tm, FFN_ROWS)
    resident = dict(pipeline_mode=pl.Buffered(1))
    return pl.pallas_call(
        _ffn_kernel,
        grid=(T // tm,),
        in_specs=[
            pl.BlockSpec((tm, D_MODEL), lambda i: (i, 0)),
            pl.BlockSpec((1, D_MODEL), lambda i: (0, 0), **resident),
            pl.BlockSpec((D_MODEL, D_FF), lambda i: (0, 0), **resident),
            pl.BlockSpec((D_MODEL, D_FF), lambda i: (0, 1), **resident),
            pl.BlockSpec((D_FF, D_MODEL), lambda i: (0, 0), **resident),
        ],
        out_specs=pl.BlockSpec((tm, D_MODEL), lambda i: (i, 0)),
        out_shape=jax.ShapeDtypeStruct((T, D_MODEL), F32),
        compiler_params=_params("parallel"),
        name="ffn",
    )(x, gain.reshape(1, D_MODEL), w_up, w_up, w_down)


def _proj_kernel(x_ref, g_ref, w_ref, o_ref, h_ref):
    @pl.when(pl.program_id(1) == 0)
    def _():
        h_ref[...] = _rms_rows(x_ref[...], g_ref[...]).astype(BF16)

    o_ref[...] = _dot(h_ref[...], w_ref[...])


def _proj(x, gain, w, tm, tn):
    T = x.shape[0]
    N = w.shape[1]
    return pl.pallas_call(
        _proj_kernel,
        grid=(T // tm, N // tn),
        in_specs=[
            pl.BlockSpec((tm, D_MODEL), lambda i, j: (i, 0)),
            pl.BlockSpec((1, D_MODEL), lambda i, j: (0, 0)),
            pl.BlockSpec((D_MODEL, tn), lambda i, j: (0, j)),
        ],
        out_specs=pl.BlockSpec((tm, tn), lambda i, j: (i, j)),
        out_shape=jax.ShapeDtypeStruct((T, N), F32),
        scratch_shapes=[pltpu.VMEM((tm, D_MODEL), BF16)],
        compiler_params=_params("parallel", "arbitrary"),
        name="proj",
    )(x, gain.reshape(1, D_MODEL), w)


def _proj_qk_kernel(flag_ref, x_ref, g_ref, w_ref, bd_ref, gain_ref, mask_ref,
                    cos_ref, sa_ref, sb_ref, o_ref, h_ref):
    j = pl.program_id(1)

    @pl.when(j == 0)
    def _():
        h_ref[...] = _rms_rows(x_ref[...], g_ref[...]).astype(BF16)

    y = _dot(h_ref[...], w_ref[...])
    tn = y.shape[1]

    @pl.when(flag_ref[j] == 0)
    def _():
        o_ref[...] = y

    @pl.when(flag_ref[j] != 0)
    def _():
        ss = _dot((y * y).astype(BF16), bd_ref[...])
        yn = y * lax.rsqrt(ss * (1.0 / HEAD_DIM) + EPS) * gain_ref[0]
        cos, sa, sb = cos_ref[...], sa_ref[...], sb_ref[...]
        mask = mask_ref[0]
        for c in range(tn // LANES):
            sl = slice(c * LANES, (c + 1) * LANES)
            v = yn[:, sl]
            up = pltpu.roll(v, LANES - ROPE_DIM // 2, 1)
            dn = pltpu.roll(v, ROPE_DIM // 2, 1)
            rot = v * cos + up * sa + dn * sb
            o_ref[:, sl] = jnp.where(mask[:, sl] > 0.0, rot, y[:, sl])


def _proj_qk(x, gain, w, flags, gain_rows, mask_rows, tables, tm, tn):
    T = x.shape[0]
    N = w.shape[1]
    nj = N // tn
    cos, sa, sb = tables
    n_pos_tiles = cos.shape[0] // tm
    bd = jnp.asarray(np.kron(np.eye(tn // HEAD_DIM), np.ones((HEAD_DIM, HEAD_DIM))), BF16)
    tab_spec = pl.BlockSpec((tm, LANES), lambda i, j, f: (i % n_pos_tiles, 0))
    return pl.pallas_call(
        _proj_qk_kernel,
        grid_spec=pltpu.PrefetchScalarGridSpec(
            num_scalar_prefetch=1,
            grid=(T // tm, nj),
            in_specs=[
                pl.BlockSpec((tm, D_MODEL), lambda i, j, f: (i, 0)),
                pl.BlockSpec((1, D_MODEL), lambda i, j, f: (0, 0)),
                pl.BlockSpec((D_MODEL, tn), lambda i, j, f: (0, j)),
                pl.BlockSpec((tn, tn), lambda i, j, f: (0, 0)),
                pl.BlockSpec((1, 1, tn), lambda i, j, f: (j, 0, 0)),
                pl.BlockSpec((1, 1, tn), lambda i, j, f: (j, 0, 0)),
                tab_spec, tab_spec, tab_spec,
            ],
            out_specs=pl.BlockSpec((tm, tn), lambda i, j, f: (i, j)),
            scratch_shapes=[pltpu.VMEM((tm, D_MODEL), BF16)],
        ),
        out_shape=jax.ShapeDtypeStruct((T, N), F32),
        compiler_params=_params("parallel", "arbitrary"),
        name="proj_qk",
    )(flags, x, gain.reshape(1, D_MODEL), w, bd, gain_rows, mask_rows, cos, sa, sb)


def _rope_tables(pos):
    half = ROPE_DIM // 2
    inv = ROPE_THETA ** (-jnp.arange(half, dtype=F32) / half)
    ang = pos.astype(F32)[:, None] * inv[None, :]
    cos, sin = jnp.cos(ang), jnp.sin(ang)
    n = pos.shape[0]
    one = jnp.ones((n, HEAD_DIM - ROPE_DIM), F32)
    zero = jnp.zeros((n, HEAD_DIM - ROPE_DIM), F32)
    zh = jnp.zeros((n, half), F32)
    c64 = jnp.concatenate([cos, cos, one], axis=1)
    sa64 = jnp.concatenate([-sin, zh, zero], axis=1)
    sb64 = jnp.concatenate([zh, sin, zero], axis=1)
    rep = LANES // HEAD_DIM
    return tuple(jnp.tile(t, (1, rep)) for t in (c64, sa64, sb64))


def _out_plain_kernel(x_ref, o_ref, w_ref, y_ref):
    y_ref[...] = x_ref[...] + _dot(o_ref[...].astype(BF16), w_ref[...])


def _out_a_kernel(x_ref, o0, o1, o2, l0, l1, l2, w_ref, y_ref):
    la, lb, lc = l0[...], l1[...], l2[...]
    m = jnp.maximum(jnp.maximum(la, lb), lc)
    ea, eb, ec = jnp.exp(la - m), jnp.exp(lb - m), jnp.exp(lc - m)
    o = (ea * o0[...] + eb * o1[...] + ec * o2[...]) / (ea + eb + ec)
    y_ref[...] = x_ref[...] + _dot(o.astype(BF16), w_ref[...])


def _out_c_kernel(x_ref, o_ref, r_ref, g_ref, w_ref, y_ref):
    o = o_ref[...]
    g = g_ref[...]
    parts = [_rms_rows(o[:, h * C_DV:(h + 1) * C_DV], g) for h in range(C_HEADS)]
    on = jnp.concatenate(parts, axis=1) * _silu(r_ref[...])
    y_ref[...] = x_ref[...] + _dot(on.astype(BF16), w_ref[...])


def _out_d_kernel(x_ref, o_ref, z_ref, g_ref, w_ref, y_ref):
    gated = o_ref[...] * _silu(z_ref[...])
    gw = D_INNER // D_GROUPS
    g = g_ref[...]
    parts = [_rms_rows(gated[:, k * gw:(k + 1) * gw], g[:, k * gw:(k + 1) * gw]) for k in range(D_GROUPS)]
    y_ref[...] = x_ref[...] + _dot(jnp.concatenate(parts, axis=1).astype(BF16), w_ref[...])


def _row_call(kernel, name, x, row_inputs, const_inputs, tm):
    T = x.shape[0]
    in_specs = [pl.BlockSpec((tm, D_MODEL), lambda i: (i, 0))]
    args = [x]
    for arr, width, cb in row_inputs:
        in_specs.append(pl.BlockSpec((tm, width), lambda i, cb=cb: (i, cb)))
        args.append(arr)
    for arr in const_inputs:
        in_specs.append(pl.BlockSpec(arr.shape, lambda i, nd=arr.ndim: (0,) * nd))
        args.append(arr)
    return pl.pallas_call(
        kernel,
        grid=(T // tm,),
        in_specs=in_specs,
        out_specs=pl.BlockSpec((tm, D_MODEL), lambda i: (i, 0)),
        out_shape=jax.ShapeDtypeStruct((T, D_MODEL), F32),
        compiler_params=_params("parallel"),
        name=name,
    )(*args)


def _a_prompt_kernel(d, q_ref, kp_ref, kc_ref, vp_ref, vc_ref, o_ref, l_ref):
    j = pl.program_id(1)
    heads = q_ref.shape[2] // HEAD_DIM
    per_pass = max(1, A_HEADS // heads)
    n_prob = per_pass * heads
    scale = HEAD_DIM ** -0.5
    qq = lax.broadcasted_iota(jnp.int32, (n_prob * A_SPAN, 2 * A_SPAN), 0) & (A_SPAN - 1)
    kk = lax.broadcasted_iota(jnp.int32, (n_prob * A_SPAN, 2 * A_SPAN), 1)
    rel = qq + A_SPAN - kk
    valid = (rel >= 0) & (rel <= A_SPAN) & ((kk >= A_SPAN) | (j > 0))
    hs = [slice(h * HEAD_DIM, (h + 1) * HEAD_DIM) for h in range(heads)]
    for r0 in range(0, d, per_pass):
        rows = [pl.ds(r0 + e, A_SPAN, stride=d) if d > 1 else pl.ds(0, A_SPAN) for e in range(per_pass)]
        s, vs = [], []
        for rw in rows:
            q = (q_ref[0, rw, :] * scale).astype(BF16)
            k = jnp.concatenate([kp_ref[0, rw, :], kc_ref[0, rw, :]], axis=0).astype(BF16)
            vs.append(jnp.concatenate([vp_ref[0, rw, :], vc_ref[0, rw, :]], axis=0).astype(BF16))
            s += [_dot_nt(q[:, sl], k[:, sl]) for sl in hs]
        s = jnp.where(valid, jnp.concatenate(s, axis=0), -jnp.inf)
        m = jnp.max(s, axis=-1, keepdims=True)
        p = jnp.exp(s - m)
        den = jnp.sum(p, axis=-1, keepdims=True)
        pb = p.astype(BF16)
        lse = m + jnp.log(den)
        inv = 1.0 / den
        for e, rw in enumerate(rows):
            o, l = [], []
            for h, sl in enumerate(hs):
                blk = slice((e * heads + h) * A_SPAN, (e * heads + h + 1) * A_SPAN)
                o.append(_dot(pb[blk], vs[e][:, sl]) * inv[blk])
                l.append(jnp.broadcast_to(lse[blk], (A_SPAN, HEAD_DIM)))
            o_ref[0, rw, :] = jnp.concatenate(o, axis=1)
            l_ref[0, rw, :] = jnp.concatenate(l, axis=1)


def _a_prompt(y, g, Bn, S):
    d = A_DILATIONS[g]
    slab = A_SPAN * d
    heads = A_HEADS if d == 1 else LANES // HEAD_DIM
    cols = heads * HEAD_DIM
    ncb = A_WIDTH // cols
    y3 = y.reshape(Bn, S, A_GROUPS * 3 * A_WIDTH)

    def spec(which, prev):
        c0 = (3 * g + which) * ncb
        if prev:
            return pl.BlockSpec((1, slab, cols), lambda b, j, c: (b, jnp.maximum(j - 1, 0), c0 + c))
        return pl.BlockSpec((1, slab, cols), lambda b, j, c: (b, j, c0 + c))

    out_spec = pl.BlockSpec((1, slab, cols), lambda b, j, c: (b, j, c))
    o, l = pl.pallas_call(
        functools.partial(_a_prompt_kernel, d),
        grid=(Bn, S // slab, ncb),
        in_specs=[spec(0, False), spec(1, True), spec(1, False), spec(2, True), spec(2, False)],
        out_specs=[out_spec, out_spec],
        out_shape=[jax.ShapeDtypeStruct((Bn, S, A_WIDTH), F32)] * 2,
        compiler_params=_params("parallel", "arbitrary", "arbitrary"),
        name="a_prompt_d%d" % d,
    )(y3, y3, y3, y3, y3)
    return o.reshape(Bn * S, A_WIDTH), l.reshape(Bn * S, A_WIDTH)


def _nt_sel(mask, a):
    out = None
    for p in reversed(_split3(a)):
        d = _dot_nt(mask, p)
        out = d if out is None else out + d
    return out


def _a_sample_kernel(T, y_ref, c1_ref, c2_ref, c3_ref, hind_ref, hindt_ref, *out_refs):
    caches = (c1_ref, c2_ref, c3_ref)
    width = 3 * A_WIDTH
    scale = HEAD_DIM ** -0.5
    hind = hind_ref[...]
    hindt = hindt_ref[...]
    lane = lax.broadcasted_iota(jnp.int32, (8, LANES), 1)
    sub = lax.broadcasted_iota(jnp.int32, (8, LANES), 0)
    zrows = jnp.zeros((LANES - T, A_WIDTH), F32)
    zsq = jnp.zeros((LANES - 8, LANES), F32)

    def head_rows(prod):
        return _nt_sel(hind, jnp.concatenate([prod, zrows], axis=0))

    def sublane_total(x):
        x = x + pltpu.roll(x, 4, 0)
        x = x + pltpu.roll(x, 2, 0)
        return x + pltpu.roll(x, 1, 0)

    def head_scores(k_of, pat_of):
        out = jnp.zeros((8, LANES), F32)
        for h in range(A_HEADS):
            prod = k_of(h) * pat_of(h)
            part = prod[0:8]
            for j in range(1, HEAD_DIM // 8):
                part = part + prod[8 * j:8 * j + 8]
            out = jnp.where(sub == h, sublane_total(part), out)
        return out

    def rows_of(packed):
        return jnp.concatenate([packed, zsq], axis=0).T

    def widen(rows8):
        return _sel_right(rows8, hindt, terms=3)

    def place(col, at):
        return jnp.where(lane == at, col, 0.0)

    def patterns(q_t, pick):
        return _sel_right(q_t, pick.astype(BF16), terms=3)

    row_sq = lax.broadcasted_iota(jnp.int32, (LANES, LANES), 0)
    lane_sq = lax.broadcasted_iota(jnp.int32, (LANES, LANES), 1)

    for g in range(A_GROUPS):
        d = A_DILATIONS[g]
        c_ref = caches[g]
        n_tiles = c_ref.shape[-1] // LANES
        q_nat = y_ref[0, :, g * width:g * width + A_WIDTH] * scale
        kn_nat = y_ref[0, :, g * width + A_WIDTH:g * width + 2 * A_WIDTH]
        vn_nat = y_ref[0, :, g * width + 2 * A_WIDTH:(g + 1) * width]
        q_t = jnp.concatenate([q_nat, zrows], axis=0).T
        if d == 1:
            packed = jnp.zeros((8, LANES), F32)
            res = []
            for i in range(T):
                pat = patterns(q_t, row_sq == i)
                s = head_scores(lambda h: c_ref[0, 0, h], lambda h: pat[h * HEAD_DIM:(h + 1) * HEAD_DIM])
                s = jnp.where(lane >= i, s, -jnp.inf)
                sn = jnp.where(lane <= i, head_rows(q_nat[i:i + 1, :] * kn_nat), -jnp.inf)
                m = jnp.maximum(jnp.max(s, axis=1, keepdims=True), jnp.max(sn, axis=1, keepdims=True))
                p = jnp.exp(s - m)
                pn = jnp.exp(sn - m)
                den = jnp.sum(p, axis=1, keepdims=True) + jnp.sum(pn, axis=1, keepdims=True)
                acc = jnp.concatenate([c_ref[0, 1, h] * p[h:h + 1, :] for h in range(A_HEADS)], axis=0)
                res.append(_nt_sel((sub == 0).astype(BF16), acc)[0:1, :])
                packed = packed + place(den, i) + place(m + jnp.log(den), 8 + i)
                packed = packed + pltpu.roll(pn, 16 + 8 * i, 1)
            sq = rows_of(packed)
            denx = widen(sq[0:8])
            lsex = widen(sq[8:16])
            for i in range(T):
                pnx = widen(sq[16 + 8 * i:24 + 8 * i])
                o = res[i] + jnp.sum(pnx[0:T] * vn_nat, axis=0, keepdims=True)
                out_refs[g][0, i:i + 1, :] = o / denx[i:i + 1, :]
            out_refs[A_GROUPS + g][0] = lsex[0:T]
        else:
            cls = lane & (d - 1)
            self_s = head_rows(q_nat * kn_nat)
            pat = patterns(q_t, (row_sq == (lane_sq & (d - 1))) & (row_sq < T))
            s_tiles = [head_scores(lambda h: c_ref[0, 0, h, :, t * LANES:(t + 1) * LANES],
                                   lambda h: pat[h * HEAD_DIM:(h + 1) * HEAD_DIM])
                       for t in range(n_tiles)]
            smax = s_tiles[0]
            for t in range(1, n_tiles):
                smax = jnp.maximum(smax, s_tiles[t])
            mrow = jnp.full((8, LANES), jnp.inf, F32)
            m_cls = []
            for i in range(T):
                mi = jnp.maximum(jnp.max(jnp.where(cls == i, smax, -jnp.inf), axis=1, keepdims=True),
                                 self_s[:, i:i + 1])
                m_cls.append(mi)
                mrow = jnp.where(cls == i, mi, mrow)
            p_tiles = [jnp.exp(s_tiles[t] - mrow) for t in range(n_tiles)]
            psum = p_tiles[0]
            for t in range(1, n_tiles):
                psum = psum + p_tiles[t]
            accs = []
            for h in range(A_HEADS):
                acc = jnp.zeros((HEAD_DIM, LANES), F32)
                for t in range(n_tiles):
                    acc = acc + c_ref[0, 1, h, :, t * LANES:(t + 1) * LANES] * p_tiles[t][h:h + 1, :]
                accs.append(acc)
            res = _nt_sel((cls == sub).astype(BF16), jnp.concatenate(accs, axis=0))
            packed = jnp.zeros((8, LANES), F32)
            for i in range(T):
                ps = jnp.exp(self_s[:, i:i + 1] - m_cls[i])
                den = jnp.sum(jnp.where(cls == i, psum, 0.0), axis=1, keepdims=True) + ps
                packed = packed + place(ps, i) + place(den, 8 + i) + place(m_cls[i] + jnp.log(den), 16 + i)
            sq = rows_of(packed)
            psx, denx, lsex = widen(sq[0:8]), widen(sq[8:16]), widen(sq[16:24])
            out_refs[g][0] = (res[0:T] + psx[0:T] * vn_nat) / denx[0:T]
            out_refs[A_GROUPS + g][0] = lsex[0:T]


def _a_sample(y, caches, Bd, T):
    width = A_GROUPS * 3 * A_WIDTH
    y3 = y.reshape(Bd, T, width)
    views = []
    specs = [pl.BlockSpec((1, T, width), lambda b: (b, 0, 0))]
    for g, c in enumerate(caches):
        d = A_DILATIONS[g]
        assert c.shape[1] == A_SPAN * d and (d == 1 or T <= d) and T <= 8
        views.append(c.transpose(0, 2, 3, 4, 1))
        specs.append(pl.BlockSpec((1, 2, A_HEADS, HEAD_DIM, A_SPAN * d), lambda b: (b, 0, 0, 0, 0)))
    hind_np = np.zeros((LANES, A_WIDTH), np.float32)
    hind_np[np.arange(A_WIDTH) // HEAD_DIM, np.arange(A_WIDTH)] = 1.0
    hind = jnp.asarray(hind_np[:A_HEADS], BF16)
    hindt = jnp.asarray(hind_np, BF16)
    specs += [pl.BlockSpec(hind.shape, lambda b: (0, 0)), pl.BlockSpec(hindt.shape, lambda b: (0, 0))]
    out_spec = pl.BlockSpec((1, T, A_WIDTH), lambda b: (b, 0, 0))
    outs = pl.pallas_call(
        functools.partial(_a_sample_kernel, T),
        grid=(Bd,),
        in_specs=specs,
        out_specs=[out_spec] * (2 * A_GROUPS),
        out_shape=[jax.ShapeDtypeStruct((Bd, T, A_WIDTH), F32)] * (2 * A_GROUPS),
        compiler_params=_params("parallel"),
        name="a_sample",
    )(y3, *views, hind, hindt)
    return [o.reshape(Bd * T, A_WIDTH) for o in outs]


def _mixer_a(xp, xs, gain, w_in, qk_gain, w_out, caches, tabs_p, tabs_s, Bn, S, Bd, T):
    flags = jnp.asarray([1, 1, 0] * A_GROUPS, jnp.int32)
    ones = jnp.ones((A_WIDTH,), F32)
    rows = [jnp.tile(qk_gain[0], A_HEADS), jnp.tile(qk_gain[1], A_HEADS), ones] * A_GROUPS
    gain_rows = jnp.stack(rows).reshape(3 * A_GROUPS, 1, A_WIDTH)
    mask_rows = jnp.stack([ones, ones, 0.0 * ones] * A_GROUPS).reshape(3 * A_GROUPS, 1, A_WIDTH)
    w = w_in.astype(BF16)
    yp = _proj_qk(xp, gain, w, flags, gain_rows, mask_rows, tabs_p, 1024, A_WIDTH)
    ys = _proj_qk(xs, gain, w, flags, gain_rows, mask_rows, tabs_s, xs.shape[0], A_WIDTH)
    wo = w_out.astype(BF16)
    pr = [_a_prompt(yp, g, Bn, S) for g in range(A_GROUPS)]
    xp = _row_call(_out_a_kernel, "out_a", xp,
                   [(o, A_WIDTH, 0) for o, _ in pr] + [(l, A_WIDTH, 0) for _, l in pr], [wo], 512)
    sr = _a_sample(ys, caches, Bd, T)
    xs = _row_call(_out_a_kernel, "out_a", xs, [(o, A_WIDTH, 0) for o in sr], [wo], xs.shape[0])
    yp3 = yp.reshape(Bn, S, A_GROUPS * 3 * A_WIDTH)
    ys3 = ys.reshape(Bd, T, A_GROUPS * 3 * A_WIDTH)
    new_p, new_s = [], []
    for g in range(A_GROUPS):
        win = min(A_SPAN * A_DILATIONS[g], S)
        c0 = (3 * g + 1) * A_WIDTH
        new_p.append(yp3[:, S - win:, c0:c0 + 2 * A_WIDTH].reshape(Bn, win, 2, A_HEADS, HEAD_DIM))
        new_s.append(ys3[:, :, c0:c0 + 2 * A_WIDTH].reshape(Bd, T, 2, A_HEADS, HEAD_DIM))
    return xp, xs, new_p, new_s


def _b_kmean_kernel(k_ref, o_ref):
    nblk = k_ref.shape[1] // B_BLOCK
    rows = [jnp.mean(k_ref[0, n * B_BLOCK:(n + 1) * B_BLOCK, :], axis=0, keepdims=True) for n in range(nblk)]
    o_ref[0] = jnp.concatenate(rows, axis=0)


def _seg_reduce(x, op, lane, width):
    s = 1
    while s < width:
        up = pltpu.roll(x, s, 1)
        dn = pltpu.roll(x, LANES - s, 1)
        x = op(x, jnp.where((lane & s) != 0, up, dn))
        s *= 2
    return x


def _b_gate_kernel(q_ref, kmh_ref, kml_ref, o_ref):
    i = pl.program_id(1)
    q = q_ref[0]
    qh, ql = _split2(q)
    kmh, kml = kmh_ref[0], kml_ref[0]
    gate = _dot(qh, kmh) + (_dot(qh, kml) + _dot(ql, kmh))
    tq = q.shape[0]
    nblk = gate.shape[1] // B_HEADS
    lane = lax.broadcasted_iota(jnp.int32, (tq, LANES), 1)
    blk = lane & (nblk - 1)
    blkf = blk.astype(F32)
    for c in range(gate.shape[1] // LANES):
        gch = jnp.where(blk < i, gate[:, c * LANES:(c + 1) * LANES], -jnp.inf)
        sel = jnp.zeros((tq, LANES), jnp.bool_)
        for _ in range(B_TOPK):
            mx = _seg_reduce(gch, jnp.maximum, lane, nblk)
            first = _seg_reduce(jnp.where(gch == mx, blkf, float(nblk)), jnp.minimum, lane, nblk)
            hit = blkf == first
            sel = sel | (hit & (mx > -jnp.inf))
            gch = jnp.where(hit, -jnp.inf, gch)
        o_ref[0, :, c * LANES:(c + 1) * LANES] = jnp.where(sel | (blk >= i), 0.0, NEG)


def _b_attn_kernel(qi_ref, kn_ref, q_ref, k_ref, vt_ref, o_ref, m_ref, l_ref, acc_ref):
    t = pl.program_id(1)
    i, n = qi_ref[t], kn_ref[t]
    tq = B_BLOCK
    wide = B_REP * tq

    @pl.when(n == 0)
    def _():
        m_ref[...] = jnp.full_like(m_ref, -jnp.inf)
        l_ref[...] = jnp.zeros_like(l_ref)
        acc_ref[...] = jnp.zeros_like(acc_ref)

    def sweep(causal):
        if causal:
            kk = lax.broadcasted_iota(jnp.int32, (B_BLOCK, wide), 0)
            qq = lax.broadcasted_iota(jnp.int32, (B_BLOCK, wide), 1) & (tq - 1)
            keep = kk <= qq
        for kvh in range(B_KV_HEADS):
            k = k_ref[0, :, kvh * LANES:(kvh + 1) * LANES]
            s = _dot_nt(k, q_ref[0, 0, kvh])
            if causal:
                s = jnp.where(keep, s, NEG)
            m_old = m_ref[kvh]
            m_new = jnp.maximum(m_old, jnp.max(s, axis=0, keepdims=True))
            alpha = jnp.exp(m_old - m_new)
            p = jnp.exp(s - m_new)
            l_ref[kvh] = alpha * l_ref[kvh] + jnp.sum(p, axis=0, keepdims=True)
            acc_ref[kvh] = alpha * acc_ref[kvh] + _dot(vt_ref[0, kvh], p.astype(BF16))
            m_ref[kvh] = m_new

    @pl.when(n < i)
    def _():
        sweep(False)

    @pl.when(n == i)
    def _():
        sweep(True)
        for kvh in range(B_KV_HEADS):
            ot = acc_ref[kvh] / l_ref[kvh]
            for r in range(0, B_REP, 2):
                pair = [ot[:, (r + e) * tq:(r + e + 1) * tq].T for e in range(2)]
                c0 = (kvh * B_REP + r) * HEAD_DIM
                o_ref[0, :, c0:c0 + 2 * HEAD_DIM] = jnp.concatenate(pair, axis=1)


def _b_prompt(y, Bn, S):
    nblk = S // B_BLOCK
    assert nblk & (nblk - 1) == 0 and LANES % nblk == 0
    nq = B_HEADS * HEAD_DIM
    nk = B_KV_HEADS * HEAD_DIM
    y3 = y.reshape(Bn, S, nq + 2 * nk)
    kmean = pl.pallas_call(
        _b_kmean_kernel,
        grid=(Bn,),
        in_specs=[pl.BlockSpec((1, S, nk), lambda b: (b, 0, nq // nk))],
        out_specs=pl.BlockSpec((1, nblk, nk), lambda b: (b, 0, 0)),
        out_shape=jax.ShapeDtypeStruct((Bn, nblk, nk), F32),
        compiler_params=_params("parallel"),
        name="b_kmean",
    )(y3)
    km = kmean.reshape(Bn, nblk, B_KV_HEADS, HEAD_DIM)
    km = jnp.repeat(km, B_REP, axis=2)
    eye = jnp.eye(B_HEADS, dtype=F32)
    kmbd = jnp.einsum('bnhe,hg->bhegn', km, eye).reshape(Bn, nq, B_HEADS * nblk)
    kmh = kmbd.astype(BF16)
    kml = (kmbd - kmh.astype(F32)).astype(BF16)
    bias = pl.pallas_call(
        _b_gate_kernel,
        grid=(Bn, nblk),
        in_specs=[
            pl.BlockSpec((1, B_BLOCK, nq), lambda b, i: (b, i, 0)),
            pl.BlockSpec((1, nq, B_HEADS * nblk), lambda b, i: (b, 0, 0)),
            pl.BlockSpec((1, nq, B_HEADS * nblk), lambda b, i: (b, 0, 0)),
        ],
        out_specs=pl.BlockSpec((1, B_BLOCK, B_HEADS * nblk), lambda b, i: (b, i, 0)),
        out_shape=jax.ShapeDtypeStruct((Bn, S, B_HEADS * nblk), F32),
        compiler_params=_params("parallel", "arbitrary"),
        name="b_gate",
    )(y3, kmh, kml)
    q = (y3[:, :, :nq] * (HEAD_DIM ** -0.5)).astype(BF16).reshape(Bn, S, B_HEADS, HEAD_DIM)
    pad = LANES - HEAD_DIM - nblk
    q_aug = jnp.concatenate([q, bias.astype(BF16).reshape(Bn, S, B_HEADS, nblk),
                             jnp.zeros((Bn, S, B_HEADS, pad), BF16)], axis=-1)
    q_aug = q_aug.reshape(Bn, nblk, B_BLOCK, B_KV_HEADS, B_REP, LANES).transpose(0, 1, 3, 4, 2, 5)
    q_aug = q_aug.reshape(Bn, nblk, B_KV_HEADS, B_REP * B_BLOCK, LANES)
    k = y3[:, :, nq:nq + nk].astype(BF16).reshape(Bn, S, B_KV_HEADS, HEAD_DIM)
    onehot = jax.nn.one_hot(jnp.arange(S) // B_BLOCK, nblk, dtype=BF16)
    onehot = jnp.broadcast_to(onehot[None, :, None, :], (Bn, S, B_KV_HEADS, nblk))
    k_aug = jnp.concatenate([k, onehot, jnp.zeros((Bn, S, B_KV_HEADS, pad), BF16)], axis=-1)
    k_aug = k_aug.reshape(Bn, S, B_KV_HEADS * LANES)
    vt = y3[:, :, nq + nk:].astype(BF16).reshape(Bn, S, B_KV_HEADS, HEAD_DIM).transpose(0, 2, 3, 1)
    pairs = [(i, n) for i in range(nblk) for n in range(i + 1)]
    qi = jnp.asarray([p[0] for p in pairs], jnp.int32)
    kn = jnp.asarray([p[1] for p in pairs], jnp.int32)
    o = pl.pallas_call(
        _b_attn_kernel,
        grid_spec=pltpu.PrefetchScalarGridSpec(
            num_scalar_prefetch=2,
            grid=(Bn, len(pairs)),
            in_specs=[
                pl.BlockSpec((1, 1, B_KV_HEADS, B_REP * B_BLOCK, LANES),
                             lambda b, t, qi, kn: (b, qi[t], 0, 0, 0)),
                pl.BlockSpec((1, B_BLOCK, B_KV_HEADS * LANES), lambda b, t, qi, kn: (b, kn[t], 0)),
                pl.BlockSpec((1, B_KV_HEADS, HEAD_DIM, B_BLOCK), lambda b, t, qi, kn: (b, 0, 0, kn[t])),
            ],
            out_specs=pl.BlockSpec((1, B_BLOCK, nq), lambda b, t, qi, kn: (b, qi[t], 0)),
            scratch_shapes=[
                pltpu.VMEM((B_KV_HEADS, 1, B_REP * B_BLOCK), F32),
                pltpu.VMEM((B_KV_HEADS, 1, B_REP * B_BLOCK), F32),
                pltpu.VMEM((B_KV_HEADS, HEAD_DIM, B_REP * B_BLOCK), F32),
            ],
        ),
        out_shape=jax.ShapeDtypeStruct((Bn, S, nq), F32),
        compiler_params=_params("parallel", "arbitrary"),
        name="b_attn",
    )(qi, kn, q_aug, k_aug, vt)
    return o.reshape(Bn * S, nq)


def _b_sample_kernel(n_pages, T, pt_ref, q_ref, kn_ref, vn_ref, *refs):
    del pt_ref
    page_refs = refs[:n_pages]
    o_ref = refs[n_pages]
    nk = B_KV_HEADS * HEAD_DIM
    q = q_ref[0]
    qh, ql = _split2(q)
    rows = q.shape[0]
    page_rows = page_refs[0].shape[2]
    pages_per_block = B_BLOCK // page_rows
    n_blocks = n_pages // pages_per_block
    scale = HEAD_DIM ** -0.5
    lane = lax.broadcasted_iota(jnp.int32, (1, LANES), 1)
    lanef = lax.broadcasted_iota(jnp.int32, (rows, LANES), 1).astype(F32)
    scores = []
    km = jnp.zeros((nk, LANES), F32)
    for p in range(n_pages):
        kt = page_refs[p][0, :nk, :]
        scores.append(_dot(qh, kt.astype(BF16)) * scale)
        ksum = jnp.sum(kt, axis=1, keepdims=True) * (1.0 / B_BLOCK)
        km = km + ksum * (lane == p // pages_per_block).astype(F32)
    kmh, kml = _split2(km)
    gate = _dot(qh, kmh) + (_dot(qh, kml) + _dot(ql, kmh))
    gate = jnp.where(lanef < float(n_blocks), gate, -jnp.inf)
    sel = jnp.zeros((rows, LANES), jnp.bool_)
    for _ in range(min(B_TOPK, n_blocks)):
        mx = jnp.max(gate, axis=1, keepdims=True)
        first = jnp.min(jnp.where(gate == mx, lanef, float(LANES)), axis=1, keepdims=True)
        hit = lanef == first
        sel = sel | (hit & (mx > -jnp.inf))
        gate = jnp.where(hit, -jnp.inf, gate)
    self = sel.astype(F32)
    qidx = lax.broadcasted_iota(jnp.int32, (rows, 1), 0) % T
    kn, vn = kn_ref[0], vn_ref[0]
    own = []
    for j in range(T):
        sj = jnp.sum(q * kn[j:j + 1, :], axis=1, keepdims=True) * scale
        own.append(jnp.where(qidx >= j, sj, NEG))
    m = own[0]
    for j in range(1, T):
        m = jnp.maximum(m, own[j])
    for p in range(n_pages):
        b = p // pages_per_block
        scores[p] = jnp.where(self[:, b:b + 1] > 0.0, scores[p], NEG)
        m = jnp.maximum(m, jnp.max(scores[p], axis=-1, keepdims=True))
    den = jnp.zeros((rows, 1), F32)
    acc = jnp.zeros((rows, nk), F32)
    for j in range(T):
        pj = jnp.exp(own[j] - m)
        den = den + pj
        acc = acc + pj * vn[j:j + 1, :]
    for p in range(n_pages):
        pp = jnp.exp(scores[p] - m)
        den = den + jnp.sum(pp, axis=-1, keepdims=True)
        acc = acc + _dot_nt(pp.astype(BF16), page_refs[p][0, nk:, :].astype(BF16))
    acc = acc / den
    kvh = lax.broadcasted_iota(jnp.int32, (rows, 1), 0) // (T * B_REP)
    out = jnp.zeros((rows, HEAD_DIM), F32)
    for h in range(B_KV_HEADS):
        out = out + jnp.where(kvh == h, acc[:, h * HEAD_DIM:(h + 1) * HEAD_DIM], 0.0)
    o_ref[0] = out


def _b_sample(y, pool, page_table, Bd, T):
    nq = B_HEADS * HEAD_DIM
    nk = B_KV_HEADS * HEAD_DIM
    n_pages = page_table.shape[1]
    page_rows = pool.shape[1]
    assert B_BLOCK % page_rows == 0 and (n_pages * page_rows) % B_BLOCK == 0 and T <= B_BLOCK
    assert n_pages * page_rows // B_BLOCK <= LANES
    y3 = y.reshape(Bd, T, nq + 2 * nk)
    q = y3[:, :, :nq].reshape(Bd, T, B_HEADS, HEAD_DIM).transpose(0, 2, 1, 3)
    kvsel = jnp.asarray(np.kron(np.eye(B_KV_HEADS), np.ones((B_REP, 1))), F32)
    qbd = (q[:, :, :, None, :] * kvsel[None, :, None, :, None]).reshape(Bd, B_HEADS * T, nk)
    kn = y3[:, :, nq:nq + nk]
    vn = y3[:, :, nq + nk:]
    pool_t = pool.transpose(0, 2, 3, 4, 1).reshape(pool.shape[0], 2 * nk, page_rows)
    page_specs = [pl.BlockSpec((1, 2 * nk, page_rows), lambda b, pt, p=p: (pt[b, p], 0, 0))
                  for p in range(n_pages)]
    o = pl.pallas_call(
        functools.partial(_b_sample_kernel, n_pages, T),
        grid_spec=pltpu.PrefetchScalarGridSpec(
            num_scalar_prefetch=1,
            grid=(Bd,),
            in_specs=[
                pl.BlockSpec((1, B_HEADS * T, nk), lambda b, pt: (b, 0, 0)),
                pl.BlockSpec((1, T, nk), lambda b, pt: (b, 0, 0)),
                pl.BlockSpec((1, T, nk), lambda b, pt: (b, 0, 0)),
            ] + page_specs,
            out_specs=pl.BlockSpec((1, B_HEADS * T, HEAD_DIM), lambda b, pt: (b, 0, 0)),
        ),
        out_shape=jax.ShapeDtypeStruct((Bd, B_HEADS * T, HEAD_DIM), F32),
        compiler_params=_params("parallel"),
        name="b_sample",
    )(page_table, qbd, kn, vn, *([pool_t] * n_pages))
    return o.reshape(Bd, B_HEADS, T, HEAD_DIM).transpose(0, 2, 1, 3).reshape(Bd * T, nq)


def _mixer_b(xp, xs, gain, w_in, qk_gain, w_out, pool, page_table, tabs_p, tabs_s, Bn, S, Bd, T):
    tn = 512
    nq = B_HEADS * HEAD_DIM
    nk = B_KV_HEADS * HEAD_DIM
    flags = jnp.asarray([1, 1, 1], jnp.int32)
    ones = jnp.ones((nk,), F32)
    qg = jnp.tile(qk_gain[0], tn // HEAD_DIM)
    kg = jnp.concatenate([jnp.tile(qk_gain[1], B_KV_HEADS), ones])
    gain_rows = jnp.stack([qg, qg, kg]).reshape(3, 1, tn)
    mask_rows = jnp.stack([jnp.ones((tn,), F32), jnp.ones((tn,), F32),
                           jnp.concatenate([ones, 0.0 * ones])]).reshape(3, 1, tn)
    w = w_in.astype(BF16)
    yp = _proj_qk(xp, gain, w, flags, gain_rows, mask_rows, tabs_p, 1024, tn)
    ys = _proj_qk(xs, gain, w, flags, gain_rows, mask_rows, tabs_s, xs.shape[0], tn)
    wo = w_out.astype(BF16)
    op = _b_prompt(yp, Bn, S)
    xp = _row_call(_out_plain_kernel, "out_b", xp, [(op, nq, 0)], [wo], 512)
    osm = _b_sample(ys, pool, page_table, Bd, T)
    xs = _row_call(_out_plain_kernel, "out_b", xs, [(osm, nq, 0)], [wo], xs.shape[0])
    kv_p = yp[:, nq:].reshape(Bn, S, 2, B_KV_HEADS, HEAD_DIM)
    kv_s = ys[:, nq:].reshape(Bd, T, 2, B_KV_HEADS, HEAD_DIM)
    return xp, xs, kv_p, kv_s


def _gla_kernel(nv, q_ref, k_ref, v_ref, glr_ref, wg_ref, bg_ref, tri_ref, blk_ref, s0_ref,
                o_ref, sT_ref, s_ref, qp_ref, kp_ref, vp_ref, gp_ref):
    t = pl.program_id(1)
    Tt = SEQ_TILE

    @pl.when(t == 0)
    def _():
        s_ref[...] = s0_ref[0]

    if nv < Tt:
        qp_ref[...] = jnp.zeros_like(qp_ref)
        kp_ref[...] = jnp.zeros_like(kp_ref)
        vp_ref[...] = jnp.zeros_like(vp_ref)
        gp_ref[...] = jnp.zeros_like(gp_ref)
    qp_ref[0:nv, :] = q_ref[0]
    kp_ref[0:nv, :] = k_ref[0]
    vp_ref[0:nv, :] = v_ref[0]
    gp_ref[0:nv, :] = glr_ref[0]
    q, k, v = qp_ref[...], kp_ref[...], vp_ref[...]

    x = _dot(gp_ref[...].astype(BF16), wg_ref[...]) + bg_ref[...]
    la = (jnp.minimum(x, 0.0) - jnp.log(1.0 + jnp.exp(-jnp.abs(x)))) * (1.0 / C_TAU)
    row = lax.broadcasted_iota(jnp.int32, (Tt, C_KEY), 0)
    la = jnp.where(row < nv, la, 0.0)
    b = _sel_left(tri_ref[...], la, terms=3)
    bend = _sel_left(blk_ref[...], la, terms=3)
    qe = q * (C_DK ** -0.5) * jnp.exp(b)
    ke = k * jnp.exp(-b)
    kd = k * jnp.exp(bend - b)
    kdt = kd.T
    bendt = bend.T
    qeb, keb, vb = qe.astype(BF16), ke.astype(BF16), v.astype(BF16)
    causal = tri_ref[...] > 0
    lane_t = lax.broadcasted_iota(jnp.int32, (C_DK, Tt), 1)
    n_chunks = -(-nv // C_CHUNK)
    for h in range(C_HEADS):
        ks = slice(h * C_DK, (h + 1) * C_DK)
        vs = slice(h * C_DV, (h + 1) * C_DV)
        att = jnp.where(causal, _dot_nt(qeb[:, ks], keb[:, ks]), 0.0)
        o_intra = _dot(att.astype(BF16), vb[:, vs])
        st = s_ref[h]
        parts = []
        for c in range(n_chunks):
            rows = slice(c * C_CHUNK, (c + 1) * C_CHUNK)
            parts.append(o_intra[rows] + _dot(qeb[rows, ks], st.astype(BF16)))
            in_chunk = (lane_t >= c * C_CHUNK) & (lane_t < (c + 1) * C_CHUNK)
            last = lane_t == (c + 1) * C_CHUNK - 1
            dec = jnp.exp(jnp.sum(jnp.where(last, bendt[ks, :], 0.0), axis=1, keepdims=True))
            kdc = jnp.where(in_chunk, kdt[ks, :], 0.0).astype(BF16)
            st = dec * st + _dot(kdc, vb[:, vs])
        s_ref[h] = st
        oh = parts[0] if n_chunks == 1 else jnp.concatenate(parts, axis=0)
        o_ref[0, :, vs] = oh[0:nv]

    @pl.when(t == pl.num_programs(1) - 1)
    def _():
        sT_ref[0] = s_ref[...]


def _gla(y, w_gate2, b_gate, s0, nb, nt, nv):
    Tt = SEQ_TILE
    y3 = y.reshape(nb * nt, nv, C_NPAD)
    wg = jnp.zeros((LANES, C_KEY), F32).at[:C_RANK].set(w_gate2).astype(BF16)
    idx = np.arange(Tt)
    same = (idx[:, None] // C_CHUNK) == (idx[None, :] // C_CHUNK)
    tri = jnp.asarray(same & (idx[None, :] <= idx[:, None]), BF16)
    blk = jnp.asarray(same, BF16)

    def yspec(width, cb):
        return pl.BlockSpec((1, nv, width), lambda b, t: (b * nt + t, 0, cb))

    def cspec(a):
        return pl.BlockSpec(a.shape, lambda b, t, nd=a.ndim: (0,) * nd)

    bg = b_gate.reshape(1, C_KEY)
    o, sT = pl.pallas_call(
        functools.partial(_gla_kernel, nv),
        grid=(nb, nt),
        in_specs=[
            yspec(C_KEY, 0), yspec(C_KEY, 1), yspec(C_VAL, 1), yspec(LANES, (2 * C_KEY + 2 * C_VAL) // LANES),
            cspec(wg), cspec(bg), cspec(tri), cspec(blk),
            pl.BlockSpec((1, C_HEADS, C_DK, C_DV), lambda b, t: (b, 0, 0, 0)),
        ],
        out_specs=[
            pl.BlockSpec((1, nv, C_VAL), lambda b, t: (b * nt + t, 0, 0)),
            pl.BlockSpec((1, C_HEADS, C_DK, C_DV), lambda b, t: (b, 0, 0, 0)),
        ],
        out_shape=[
            jax.ShapeDtypeStruct((nb * nt, nv, C_VAL), F32),
            jax.ShapeDtypeStruct((nb, C_HEADS, C_DK, C_DV), F32),
        ],
        scratch_shapes=[
            pltpu.VMEM((C_HEADS, C_DK, C_DV), F32),
            pltpu.VMEM((Tt, C_KEY), F32), pltpu.VMEM((Tt, C_KEY), F32),
            pltpu.VMEM((Tt, C_VAL), F32), pltpu.VMEM((Tt, LANES), F32),
        ],
        compiler_params=_params("parallel", "arbitrary"),
        name="gla",
    )(y3, y3, y3, y3, wg, bg, tri, blk, s0)
    return o.reshape(nb * nt * nv, C_VAL), sT


def _mixer_c(xp, xs, gain, w_in, w_gate2, b_gate, norm_g, w_out, state, Bn, S, Bd, T):
    n_in = w_in.shape[1]
    w = jnp.zeros((D_MODEL, C_NPAD), BF16).at[:, :n_in].set(w_in.astype(BF16))
    yp = _proj(xp, gain, w, 1024, 640)
    ys = _proj(xs, gain, w, xs.shape[0], 640)
    zero = jnp.zeros((Bn, C_HEADS, C_DK, C_DV), F32)
    op, sp = _gla(yp, w_gate2, b_gate, zero, Bn, S // SEQ_TILE, SEQ_TILE)
    osm, ss = _gla(ys, w_gate2, b_gate, state, Bd, 1, T)
    wo = w_out.astype(BF16)
    ng = norm_g.reshape(1, C_DV)
    rcb = (2 * C_KEY + C_VAL) // C_VAL
    xp = _row_call(_out_c_kernel, "out_c", xp, [(op, C_VAL, 0), (yp, C_VAL, rcb)], [ng, wo], 512)
    xs = _row_call(_out_c_kernel, "out_c", xs, [(osm, C_VAL, 0), (ys, C_VAL, rcb)], [ng, wo], xs.shape[0])
    return xp, xs, sp, ss


def _ssd_kernel(nv, xa_ref, xb_ref, bc_ref, dt_ref, cs_ref, cw_ref, cb_ref, dtb_ref, alog_ref, dsk_ref,
                tri_ref, exp_ref, expt_ref, h0_ref, y_ref, hT_ref, h_ref, xp_ref, dtp_ref, yd_ref):
    t = pl.program_id(1)
    Tt = SEQ_TILE
    pre = 8

    @pl.when(t == 0)
    def _():
        h_ref[...] = h0_ref[0]
        xp_ref[0:pre, :] = cs_ref[0]

    if nv < Tt:
        xp_ref[pre:, :] = jnp.zeros((Tt, D_XBC), F32)
        dtp_ref[...] = jnp.zeros_like(dtp_ref)
    xp_ref[pre:pre + nv, 0:1024] = xa_ref[0]
    xp_ref[pre:pre + nv, 1024:2048] = xb_ref[0]
    xp_ref[pre:pre + nv, 2048:3072] = bc_ref[0]
    dtp_ref[0:nv, :] = dt_ref[0]

    conv = cb_ref[...]
    for w in range(D_CONV):
        conv = conv + xp_ref[pl.ds(pre - (D_CONV - 1) + w, Tt), :] * cw_ref[w:w + 1, :]
    xp_ref[0:pre, :] = xp_ref[Tt:Tt + pre, :]
    xbc = _silu(conv)
    x = xbc[:, :D_INNER]
    nbc = D_GROUPS * D_STATE
    bm = xbc[:, D_INNER:D_INNER + nbc].astype(BF16)
    cm = xbc[:, D_INNER + nbc:].astype(BF16)

    row = lax.broadcasted_iota(jnp.int32, (Tt, LANES), 0)
    lane = lax.broadcasted_iota(jnp.int32, (Tt, LANES), 1)
    live = (row < nv) & (lane < D_HEADS)
    dt = jnp.where(live, _softplus(dtp_ref[...] + dtb_ref[...]), 0.0)
    cum = _sel_left(tri_ref[...], dt * (-jnp.exp(alog_ref[...])), terms=3)
    cumt = cum.T
    cend = cum[Tt - 1:Tt, :]
    ex = exp_ref[...]
    stack = jnp.concatenate([jnp.exp(cend - cum) * dt, dt, jnp.exp(cum)], axis=0)
    wide = _sel_right(stack, ex)
    x_state = (x * wide[0:Tt]).astype(BF16)
    x_dt = (x * wide[Tt:2 * Tt]).astype(BF16)
    off_scale = wide[2 * Tt:]

    causal = tri_ref[...] > 0
    gw = D_INNER // D_GROUPS
    for g in range(D_GROUPS):
        bg = bm[:, g * D_STATE:(g + 1) * D_STATE]
        cg = cm[:, g * D_STATE:(g + 1) * D_STATE]
        cb = _dot_nt(cg, bg)
        for jj in range(0, D_HPG, 2):
            pair = []
            for j in (g * D_HPG + jj, g * D_HPG + jj + 1):
                seg = cum[:, j:j + 1] - cumt[j:j + 1, :]
                lmat = jnp.exp(jnp.where(causal, seg, -jnp.inf))
                pair.append(_dot((cb * lmat).astype(BF16), x_dt[:, j * D_HEADDIM:(j + 1) * D_HEADDIM]))
            c0 = (g * D_HPG + jj) * D_HEADDIM
            yd_ref[:, c0:c0 + 2 * D_HEADDIM] = jnp.concatenate(pair, axis=1)
        hg = h_ref[g * gw:(g + 1) * gw, :]
        y_off = _dot_nt(cg, hg.astype(BF16)) * off_scale[:, g * gw:(g + 1) * gw]
        yd_ref[:, g * gw:(g + 1) * gw] += y_off
    y = yd_ref[...] + dsk_ref[...] * x
    y_ref[0] = y[0:nv]

    dcol = jnp.broadcast_to(jnp.exp(cumt[:, Tt - 1:Tt]), (LANES, D_STATE))
    dfull = _sel_left(expt_ref[...], dcol)
    xst = x_state.astype(F32).T.astype(BF16)
    for g in range(D_GROUPS):
        rows = slice(g * gw, (g + 1) * gw)
        h_ref[rows, :] = dfull[rows] * h_ref[rows, :] + _dot(xst[rows], bm[:, g * D_STATE:(g + 1) * D_STATE])

    @pl.when(t == pl.num_programs(1) - 1)
    def _():
        hT_ref[0] = h_ref[...]


def _ssd(y, conv_w, conv_b, dt_bias, a_log, d_skip, h0, c0, nb, nt, nv):
    Tt = SEQ_TILE
    y3 = y.reshape(nb * nt, nv, D_NPAD)
    idx = np.arange(Tt)
    tri = jnp.asarray(idx[None, :] <= idx[:, None], BF16)
    ex_np = np.zeros((LANES, D_INNER), np.float32)
    ex_np[np.arange(D_INNER) // D_HEADDIM, np.arange(D_INNER)] = 1.0
    ex = jnp.asarray(ex_np, BF16)
    ext = jnp.asarray(ex_np.T, BF16)

    def pad_row(v):
        return jnp.zeros((1, LANES), F32).at[0, :D_HEADS].set(v)

    cs = jnp.zeros((nb, 8, D_XBC), F32).at[:, 8 - (D_CONV - 1):].set(c0)
    dsk = jnp.repeat(d_skip, D_HEADDIM).reshape(1, D_INNER)

    def yspec(width, cb):
        return pl.BlockSpec((1, nv, width), lambda b, t: (b * nt + t, 0, cb))

    def cspec(a):
        return pl.BlockSpec(a.shape, lambda b, t, nd=a.ndim: (0,) * nd)

    consts = [conv_w, conv_b.reshape(1, D_XBC), pad_row(dt_bias), pad_row(a_log), dsk, tri, ex, ext]
    yo, hT = pl.pallas_call(
        functools.partial(_ssd_kernel, nv),
        grid=(nb, nt),
        in_specs=[yspec(1024, 2), yspec(1024, 3), yspec(1024, 4), yspec(LANES, D_DT_COL),
                  pl.BlockSpec((1, 8, D_XBC), lambda b, t: (b, 0, 0))]
        + [cspec(a) for a in consts]
        + [pl.BlockSpec((1, D_INNER, D_STATE), lambda b, t: (b, 0, 0))],
        out_specs=[
            pl.BlockSpec((1, nv, D_INNER), lambda b, t: (b * nt + t, 0, 0)),
            pl.BlockSpec((1, D_INNER, D_STATE), lambda b, t: (b, 0, 0)),
        ],
        out_shape=[
            jax.ShapeDtypeStruct((nb * nt, nv, D_INNER), F32),
            jax.ShapeDtypeStruct((nb, D_INNER, D_STATE), F32),
        ],
        scratch_shapes=[
            pltpu.VMEM((D_INNER, D_STATE), F32),
            pltpu.VMEM((Tt + 8, D_XBC), F32),
            pltpu.VMEM((Tt, LANES), F32),
            pltpu.VMEM((Tt, D_INNER), F32),
        ],
        compiler_params=_params("parallel", "arbitrary"),
        name="ssd",
    )(y3, y3, y3, y3, cs, *consts, h0)
    return yo.reshape(nb * nt * nv, D_INNER), hT


def _mixer_d(xp, xs, gain, w_in, conv_w, conv_b, dt_bias, a_log, d_skip, norm_g, w_out,
             ssm_state, conv_state, Bn, S, Bd, T):
    n_in = w_in.shape[1]
    w = jnp.zeros((D_MODEL, D_NPAD), BF16).at[:, :n_in].set(w_in.astype(BF16))
    yp = _proj(xp, gain, w, 1024, 768)
    ys = _proj(xs, gain, w, xs.shape[0], 768)
    h0p = jnp.zeros((Bn, D_INNER, D_STATE), F32)
    c0p = jnp.zeros((Bn, D_CONV - 1, D_XBC), F32)
    op, hp = _ssd(yp, conv_w, conv_b, dt_bias, a_log, d_skip, h0p, c0p, Bn, S // SEQ_TILE, SEQ_TILE)
    h0s = ssm_state.reshape(Bd, D_INNER, D_STATE)
    osm, hs = _ssd(ys, conv_w, conv_b, dt_bias, a_log, d_skip, h0s, conv_state, Bd, 1, T)
    wo = w_out.astype(BF16)
    ng = norm_g.reshape(1, D_INNER)
    xp = _row_call(_out_d_kernel, "out_d", xp, [(op, D_INNER, 0), (yp, D_INNER, 0)], [ng, wo], 512)
    xs = _row_call(_out_d_kernel, "out_d", xs, [(osm, D_INNER, 0), (ys, D_INNER, 0)], [ng, wo], xs.shape[0])
    keep = D_CONV - 1
    xbc_p = yp.reshape(Bn, S, D_NPAD)[:, :, D_INNER:D_INNER + D_XBC]
    xbc_s = ys.reshape(Bd, T, D_NPAD)[:, :, D_INNER:D_INNER + D_XBC]
    cp = jnp.concatenate([c0p, xbc_p], axis=1)[:, -keep:] if S < keep else xbc_p[:, S - keep:]
    cs = jnp.concatenate([conv_state, xbc_s], axis=1)[:, -keep:]
    return (xp, xs, hp.reshape(Bn, D_HEADS, D_HEADDIM, D_STATE), hs.reshape(Bd, D_HEADS, D_HEADDIM, D_STATE),
            cp, cs)


def kernel(x_prompt, x_sample, cache_a_w1, cache_a_w2, cache_a_w3, cache_b_kv, page_table, state_c, state_d_ssm, state_d_conv, norm_gain, w_ffn_up, w_ffn_down, w_a_in, a_qk_gain, w_a_out, w_b_in, b_qk_gain, w_b_out, w_c_in, w_c_gate2, b_c_gate, c_norm_gain, w_c_out, w_d_in, d_conv_w, d_conv_b, d_dt_bias, d_a_log, d_skip, d_norm_gain, w_d_out):
    Bn, S, _ = x_prompt.shape
    Bd, T, _ = x_sample.shape
    depth = norm_gain.shape[0]
    past_len = page_table.shape[1] * cache_b_kv.shape[2]
    tabs_p = _rope_tables(jnp.arange(S, dtype=jnp.int32))
    tabs_s = _rope_tables(jnp.tile(past_len + jnp.arange(T, dtype=jnp.int32), Bd))
    xp = x_prompt.reshape(Bn * S, D_MODEL)
    xs = x_sample.reshape(Bd * T, D_MODEL)
    ts = xs.shape[0]
    w_up = w_ffn_up.astype(BF16)
    w_down = w_ffn_down.astype(BF16)
    outs = {k: [] for k in ("a0p", "a0s", "a1p", "a1s", "a2p", "a2s", "bp", "bs", "cp", "cs",
                            "hp", "hs", "dp", "ds")}
    for i in range(depth):
        m, j = i % 4, i // 4
        g = norm_gain[i]
        xp = _ffn(xp, g[0], w_up[i, 0], w_down[i, 0], 1024)
        xs = _ffn(xs, g[0], w_up[i, 0], w_down[i, 0], ts)
        if m == 0:
            xp, xs, new_p, new_s = _mixer_a(xp, xs, g[1], w_a_in[j], a_qk_gain[j], w_a_out[j],
                                            (cache_a_w1[j], cache_a_w2[j], cache_a_w3[j]),
                                            tabs_p, tabs_s, Bn, S, Bd, T)
            for gi in range(A_GROUPS):
                outs["a%dp" % gi].append(new_p[gi])
                outs["a%ds" % gi].append(new_s[gi])
        elif m == 1:
            xp, xs, kvp, kvs = _mixer_b(xp, xs, g[1], w_b_in[j], b_qk_gain[j], w_b_out[j],
                                        cache_b_kv[j], page_table, tabs_p, tabs_s, Bn, S, Bd, T)
            outs["bp"].append(kvp)
            outs["bs"].append(kvs)
        elif m == 2:
            xp, xs, sp, ss = _mixer_c(xp, xs, g[1], w_c_in[j], w_c_gate2[j], b_c_gate[j], c_norm_gain[j],
                                      w_c_out[j], state_c[j], Bn, S, Bd, T)
            outs["cp"].append(sp)
            outs["cs"].append(ss)
        else:
            xp, xs, hp, hs, cp, cs = _mixer_d(xp, xs, g[1], w_d_in[j], d_conv_w[j], d_conv_b[j], d_dt_bias[j],
                                              d_a_log[j], d_skip[j], d_norm_gain[j], w_d_out[j],
                                              state_d_ssm[j], state_d_conv[j], Bn, S, Bd, T)
            outs["hp"].append(hp)
            outs["hs"].append(hs)
            outs["dp"].append(cp)
            outs["ds"].append(cs)
        xp = _ffn(xp, g[2], w_up[i, 1], w_down[i, 1], 1024)
        xs = _ffn(xs, g[2], w_up[i, 1], w_down[i, 1], ts)
    st = {k: jnp.stack(v) for k, v in outs.items()}
    return (xp.reshape(Bn, S, D_MODEL), xs.reshape(Bd, T, D_MODEL),
            st["a0p"], st["a0s"], st["a1p"], st["a1s"], st["a2p"], st["a2s"],
            st["bp"], st["bs"], st["cp"], st["cs"], st["hp"], st["hs"], st["dp"], st["ds"])
```

```python
import functools
import math

import numpy as np
import jax
import jax.numpy as jnp
from jax import lax
from jax.experimental import pallas as pl
from jax.experimental.pallas import tpu as pltpu

F32 = jnp.float32
BF16 = jnp.bfloat16

D_MODEL = 1024
HEAD_DIM = 64
ROPE_DIM = HEAD_DIM // 4
ROPE_THETA = 500000.0
EPS = 1e-6
D_FF = 2816
NEG = -1e30

A_GROUPS = 3
A_DILATIONS = (1, 4, 16)
A_HEADS = 8
A_WIDTH = A_HEADS * HEAD_DIM
A_SPAN = 128

B_HEADS = 16
B_KV_HEADS = 4
B_REP = B_HEADS // B_KV_HEADS
B_BLOCK = 256
B_TOPK = 3
B_VT_ROWS = HEAD_DIM + 8
LOG2E = 1.4426950408889634

C_HEADS = 4
C_KEY = 512
C_VAL = 1024
C_DK = 128
C_DV = 256
C_RANK = 16
C_TAU = 16.0
C_CHUNK = 32
C_NPAD = 3200

D_INNER = 2048
D_HEADDIM = 64
D_HEADS = 32
D_GROUPS = 4
D_HPG = 8
D_STATE = 128
D_CONV = 4
D_XBC = D_INNER + 2 * D_GROUPS * D_STATE
D_NPAD = 5376
D_DT_COL = (D_INNER + D_XBC) // 128

LANES = 128
FFN_ROWS = 256
PROJ_ROWS = 256
SEQ_TILE = 128


def _params(*sem):
    return pltpu.CompilerParams(dimension_semantics=sem)


def _split2(a):
    hi = a.astype(BF16)
    lo = (a - hi.astype(F32)).astype(BF16)
    return hi, lo


def _split3(a):
    hi = a.astype(BF16)
    r = a - hi.astype(F32)
    mid = r.astype(BF16)
    lo = (r - mid.astype(F32)).astype(BF16)
    return hi, mid, lo


def _dot(a, b):
    return jnp.dot(a, b, preferred_element_type=F32)


def _dot_nt(a, b):
    return lax.dot_general(a, b, (((1,), (1,)), ((), ())), preferred_element_type=F32)


def _sel_right(a, sel01, terms=2):
    parts = _split2(a) if terms == 2 else _split3(a)
    out = None
    for p in reversed(parts):
        d = _dot(p, sel01)
        out = d if out is None else out + d
    return out


def _sel_left(sel01, a, terms=2):
    parts = _split2(a) if terms == 2 else _split3(a)
    out = None
    for p in reversed(parts):
        d = _dot(sel01, p)
        out = d if out is None else out + d
    return out


def _silu(x):
    return x * jax.nn.sigmoid(x)


def _softplus(x):
    return jnp.maximum(x, 0.0) + jnp.log(1.0 + jnp.exp(-jnp.abs(x)))


def _rms_rows(x, g):
    ms = jnp.mean(x * x, axis=-1, keepdims=True)
    return x * lax.rsqrt(ms + EPS) * g


def _ffn_kernel(x_ref, g_ref, wg_ref, wu_ref, wd_ref, o_ref):
    x = x_ref[...]
    h = _rms_rows(x, g_ref[...]).astype(BF16)
    a = _dot(h, wg_ref[...])
    u = _dot(h, wu_ref[...])
    act = (_silu(a) * u).astype(BF16)
    o_ref[...] = x + 0.5 * _dot(act, wd_ref[...])


def _ffn(x, gain, w_up, w_down, tm):
    T = x.shape[0]
    tm = min(tm, FFN_ROWS)
    resident = dict(pipeline_mode=pl.Buffered(1))
    return pl.pallas_call(
        _ffn_kernel,
        grid=(T // tm,),
        in_specs=[
            pl.BlockSpec((tm, D_MODEL), lambda i: (i, 0)),
            pl.BlockSpec((1, D_MODEL), lambda i: (0, 0), **resident),
            pl.BlockSpec((D_MODEL, D_FF), lambda i: (0, 0), **resident),
            pl.BlockSpec((D_MODEL, D_FF), lambda i: (0, 1), **resident),
            pl.BlockSpec((D_FF, D_MODEL), lambda i: (0, 0), **resident),
        ],
        out_specs=pl.BlockSpec((tm, D_MODEL), lambda i: (i, 0)),
        out_shape=jax.ShapeDtypeStruct((T, D_MODEL), F32),
        compiler_params=_params("parallel"),
        name="ffn",
    )(x, gain.reshape(1, D_MODEL), w_up, w_up, w_down)


def _proj_kernel(x_ref, g_ref, w_ref, o_ref):
    o_ref[...] = _dot(_rms_rows(x_ref[...], g_ref[...]).astype(BF16), w_ref[...])


def _proj(x, gain, w, tm):
    T = x.shape[0]
    N = w.shape[1]
    tm = min(tm, PROJ_ROWS)
    resident = dict(pipeline_mode=pl.Buffered(1))
    return pl.pallas_call(
        _proj_kernel,
        grid=(T // tm,),
        in_specs=[
            pl.BlockSpec((tm, D_MODEL), lambda i: (i, 0)),
            pl.BlockSpec((1, D_MODEL), lambda i: (0, 0), **resident),
            pl.BlockSpec((D_MODEL, N), lambda i: (0, 0), **resident),
        ],
        out_specs=pl.BlockSpec((tm, N), lambda i: (i, 0)),
        out_shape=jax.ShapeDtypeStruct((T, N), F32),
        compiler_params=_params("parallel"),
        name="proj",
    )(x, gain.reshape(1, D_MODEL), w)


def _proj_qk_kernel(flags, x_ref, g_ref, w_ref, bd_ref, gain_ref, mask_ref, cos_ref, sa_ref, sb_ref, o_ref):
    h = _rms_rows(x_ref[...], g_ref[...]).astype(BF16)
    tn = bd_ref.shape[0]
    cos, sa, sb = cos_ref[...], sa_ref[...], sb_ref[...]
    for j, flag in enumerate(flags):
        y = _dot(h, w_ref[:, j * tn:(j + 1) * tn])
        if not flag:
            o_ref[:, j * tn:(j + 1) * tn] = y
            continue
        ss = _dot((y * y).astype(BF16), bd_ref[...])
        yn = y * lax.rsqrt(ss * (1.0 / HEAD_DIM) + EPS) * gain_ref[j]
        mask = mask_ref[j]
        for c in range(tn // LANES):
            sl = slice(c * LANES, (c + 1) * LANES)
            v = yn[:, sl]
            up = pltpu.roll(v, LANES - ROPE_DIM // 2, 1)
            dn = pltpu.roll(v, ROPE_DIM // 2, 1)
            rot = v * cos + up * sa + dn * sb
            o_ref[:, j * tn + c * LANES:j * tn + (c + 1) * LANES] = jnp.where(mask[:, sl] > 0.0, rot, y[:, sl])


def _proj_qk(x, gain, w, flags, gain_rows, mask_rows, tables, tm, tn):
    T = x.shape[0]
    N = w.shape[1]
    nj = N // tn
    tm = min(tm, PROJ_ROWS)
    cos, sa, sb = tables
    n_pos_tiles = cos.shape[0] // tm
    bd = jnp.asarray(np.kron(np.eye(tn // HEAD_DIM), np.ones((HEAD_DIM, HEAD_DIM))), BF16)
    resident = dict(pipeline_mode=pl.Buffered(1))
    tab_spec = pl.BlockSpec((tm, LANES), lambda i: (i % n_pos_tiles, 0))
    return pl.pallas_call(
        functools.partial(_proj_qk_kernel, flags),
        grid=(T // tm,),
        in_specs=[
            pl.BlockSpec((tm, D_MODEL), lambda i: (i, 0)),
            pl.BlockSpec((1, D_MODEL), lambda i: (0, 0), **resident),
            pl.BlockSpec((D_MODEL, N), lambda i: (0, 0), **resident),
            pl.BlockSpec((tn, tn), lambda i: (0, 0), **resident),
            pl.BlockSpec((nj, 1, tn), lambda i: (0, 0, 0), **resident),
            pl.BlockSpec((nj, 1, tn), lambda i: (0, 0, 0), **resident),
            tab_spec, tab_spec, tab_spec,
        ],
        out_specs=pl.BlockSpec((tm, N), lambda i: (i, 0)),
        out_shape=jax.ShapeDtypeStruct((T, N), F32),
        compiler_params=_params("parallel"),
        name="proj_qk",
    )(x, gain.reshape(1, D_MODEL), w, bd, gain_rows, mask_rows, cos, sa, sb)


def _rope_tables(pos):
    half = ROPE_DIM // 2
    inv = ROPE_THETA ** (-jnp.arange(half, dtype=F32) / half)
    ang = pos.astype(F32)[:, None] * inv[None, :]
    cos, sin = jnp.cos(ang), jnp.sin(ang)
    n = pos.shape[0]
    one = jnp.ones((n, HEAD_DIM - ROPE_DIM), F32)
    zero = jnp.zeros((n, HEAD_DIM - ROPE_DIM), F32)
    zh = jnp.zeros((n, half), F32)
    c64 = jnp.concatenate([cos, cos, one], axis=1)
    sa64 = jnp.concatenate([-sin, zh, zero], axis=1)
    sb64 = jnp.concatenate([zh, sin, zero], axis=1)
    rep = LANES // HEAD_DIM
    return tuple(jnp.tile(t, (1, rep)) for t in (c64, sa64, sb64))


def _out_plain_kernel(x_ref, o_ref, w_ref, y_ref):
    y_ref[...] = x_ref[...] + _dot(o_ref[...].astype(BF16), w_ref[...])


def _out_a_kernel(x_ref, o0, o1, o2, l0, l1, l2, w_ref, y_ref):
    la, lb, lc = l0[...], l1[...], l2[...]
    m = jnp.maximum(jnp.maximum(la, lb), lc)
    ea, eb, ec = jnp.exp(la - m), jnp.exp(lb - m), jnp.exp(lc - m)
    o = (ea * o0[...] + eb * o1[...] + ec * o2[...]) / (ea + eb + ec)
    y_ref[...] = x_ref[...] + _dot(o.astype(BF16), w_ref[...])


def _out_c_kernel(x_ref, o_ref, r_ref, g_ref, w_ref, y_ref):
    o = o_ref[...]
    g = g_ref[...]
    parts = [_rms_rows(o[:, h * C_DV:(h + 1) * C_DV], g) for h in range(C_HEADS)]
    on = jnp.concatenate(parts, axis=1) * _silu(r_ref[...])
    y_ref[...] = x_ref[...] + _dot(on.astype(BF16), w_ref[...])


def _out_d_kernel(x_ref, o_ref, z_ref, g_ref, w_ref, y_ref):
    gated = o_ref[...] * _silu(z_ref[...])
    gw = D_INNER // D_GROUPS
    g = g_ref[...]
    parts = [_rms_rows(gated[:, k * gw:(k + 1) * gw], g[:, k * gw:(k + 1) * gw]) for k in range(D_GROUPS)]
    y_ref[...] = x_ref[...] + _dot(jnp.concatenate(parts, axis=1).astype(BF16), w_ref[...])


def _row_call(kernel, name, x, row_inputs, const_inputs, tm):
    T = x.shape[0]
    in_specs = [pl.BlockSpec((tm, D_MODEL), lambda i: (i, 0))]
    args = [x]
    for arr, width, cb in row_inputs:
        in_specs.append(pl.BlockSpec((tm, width), lambda i, cb=cb: (i, cb)))
        args.append(arr)
    for arr in const_inputs:
        in_specs.append(pl.BlockSpec(arr.shape, lambda i, nd=arr.ndim: (0,) * nd))
        args.append(arr)
    return pl.pallas_call(
        kernel,
        grid=(T // tm,),
        in_specs=in_specs,
        out_specs=pl.BlockSpec((tm, D_MODEL), lambda i: (i, 0)),
        out_shape=jax.ShapeDtypeStruct((T, D_MODEL), F32),
        compiler_params=_params("parallel"),
        name=name,
    )(*args)


def _a_prompt_kernel(d, q_ref, kp_ref, kc_ref, vp_ref, vc_ref, o_ref, l_ref):
    j = pl.program_id(1)
    heads = q_ref.shape[2] // HEAD_DIM
    per_pass = max(1, A_HEADS // heads)
    n_prob = per_pass * heads
    scale = HEAD_DIM ** -0.5
    qq = lax.broadcasted_iota(jnp.int32, (n_prob * A_SPAN, 2 * A_SPAN), 0) & (A_SPAN - 1)
    kk = lax.broadcasted_iota(jnp.int32, (n_prob * A_SPAN, 2 * A_SPAN), 1)
    rel = qq + A_SPAN - kk
    valid = (rel >= 0) & (rel <= A_SPAN) & ((kk >= A_SPAN) | (j > 0))
    hs = [slice(h * HEAD_DIM, (h + 1) * HEAD_DIM) for h in range(heads)]
    for r0 in range(0, d, per_pass):
        rows = [pl.ds(r0 + e, A_SPAN, stride=d) if d > 1 else pl.ds(0, A_SPAN) for e in range(per_pass)]
        s, vs = [], []
        for rw in rows:
            q = (q_ref[0, rw, :] * scale).astype(BF16)
            k = jnp.concatenate([kp_ref[0, rw, :], kc_ref[0, rw, :]], axis=0).astype(BF16)
            vs.append(jnp.concatenate([vp_ref[0, rw, :], vc_ref[0, rw, :]], axis=0).astype(BF16))
            s += [_dot_nt(q[:, sl], k[:, sl]) for sl in hs]
        s = jnp.where(valid, jnp.concatenate(s, axis=0), -jnp.inf)
        m = jnp.max(s, axis=-1, keepdims=True)
        p = jnp.exp(s - m)
        den = jnp.sum(p, axis=-1, keepdims=True)
        pb = p.astype(BF16)
        lse = m + jnp.log(den)
        inv = 1.0 / den
        for e, rw in enumerate(rows):
            o, l = [], []
            for h, sl in enumerate(hs):
                blk = slice((e * heads + h) * A_SPAN, (e * heads + h + 1) * A_SPAN)
                o.append(_dot(pb[blk], vs[e][:, sl]) * inv[blk])
                l.append(jnp.broadcast_to(lse[blk], (A_SPAN, HEAD_DIM)))
            o_ref[0, rw, :] = jnp.concatenate(o, axis=1)
            l_ref[0, rw, :] = jnp.concatenate(l, axis=1)


def _a_prompt(y, g, Bn, S):
    d = A_DILATIONS[g]
    slab = A_SPAN * d
    heads = A_HEADS if d == 1 else LANES // HEAD_DIM
    cols = heads * HEAD_DIM
    ncb = A_WIDTH // cols
    y3 = y.reshape(Bn, S, A_GROUPS * 3 * A_WIDTH)

    def spec(which, prev):
        c0 = (3 * g + which) * ncb
        if prev:
            return pl.BlockSpec((1, slab, cols), lambda b, j, c: (b, jnp.maximum(j - 1, 0), c0 + c))
        return pl.BlockSpec((1, slab, cols), lambda b, j, c: (b, j, c0 + c))

    out_spec = pl.BlockSpec((1, slab, cols), lambda b, j, c: (b, j, c))
    o, l = pl.pallas_call(
        functools.partial(_a_prompt_kernel, d),
        grid=(Bn, S // slab, ncb),
        in_specs=[spec(0, False), spec(1, True), spec(1, False), spec(2, True), spec(2, False)],
        out_specs=[out_spec, out_spec],
        out_shape=[jax.ShapeDtypeStruct((Bn, S, A_WIDTH), F32)] * 2,
        compiler_params=_params("parallel", "arbitrary", "arbitrary"),
        name="a_prompt_d%d" % d,
    )(y3, y3, y3, y3, y3)
    return o.reshape(Bn * S, A_WIDTH), l.reshape(Bn * S, A_WIDTH)


def _nt_sel(mask, a):
    out = None
    for p in reversed(_split3(a)):
        d = _dot_nt(mask, p)
        out = d if out is None else out + d
    return out


def _a_sample_kernel(T, y_ref, c1_ref, c2_ref, c3_ref, hind_ref, hindt_ref, *out_refs):
    caches = (c1_ref, c2_ref, c3_ref)
    width = 3 * A_WIDTH
    scale = HEAD_DIM ** -0.5
    hind = hind_ref[...]
    hindt = hindt_ref[...]
    lane = lax.broadcasted_iota(jnp.int32, (8, LANES), 1)
    sub = lax.broadcasted_iota(jnp.int32, (8, LANES), 0)
    zrows = jnp.zeros((LANES - T, A_WIDTH), F32)
    zsq = jnp.zeros((LANES - 8, LANES), F32)

    def head_rows(prod):
        return _nt_sel(hind, jnp.concatenate([prod, zrows], axis=0))

    def sublane_total(x):
        x = x + pltpu.roll(x, 4, 0)
        x = x + pltpu.roll(x, 2, 0)
        return x + pltpu.roll(x, 1, 0)

    def head_scores(k_of, pat_of):
        out = jnp.zeros((8, LANES), F32)
        for h in range(A_HEADS):
            prod = k_of(h) * pat_of(h)
            part = prod[0:8]
            for j in range(1, HEAD_DIM // 8):
                part = part + prod[8 * j:8 * j + 8]
            out = jnp.where(sub == h, sublane_total(part), out)
        return out

    def rows_of(packed):
        return jnp.concatenate([packed, zsq], axis=0).T

    def widen(rows8):
        return _sel_right(rows8, hindt, terms=3)

    def place(col, at):
        return jnp.where(lane == at, col, 0.0)

    def patterns(q_t, pick):
        return _sel_right(q_t, pick.astype(BF16), terms=3)

    row_sq = lax.broadcasted_iota(jnp.int32, (LANES, LANES), 0)
    lane_sq = lax.broadcasted_iota(jnp.int32, (LANES, LANES), 1)

    for g in range(A_GROUPS):
        d = A_DILATIONS[g]
        c_ref = caches[g]
        n_tiles = c_ref.shape[-1] // LANES
        q_nat = y_ref[0, :, g * width:g * width + A_WIDTH] * scale
        kn_nat = y_ref[0, :, g * width + A_WIDTH:g * width + 2 * A_WIDTH]
        vn_nat = y_ref[0, :, g * width + 2 * A_WIDTH:(g + 1) * width]
        q_t = jnp.concatenate([q_nat, zrows], axis=0).T
        if d == 1:
            packed = jnp.zeros((8, LANES), F32)
            res = []
            for i in range(T):
                pat = patterns(q_t, row_sq == i)
                s = head_scores(lambda h: c_ref[0, 0, h], lambda h: pat[h * HEAD_DIM:(h + 1) * HEAD_DIM])
                s = jnp.where(lane >= i, s, -jnp.inf)
                sn = jnp.where(lane <= i, head_rows(q_nat[i:i + 1, :] * kn_nat), -jnp.inf)
                m = jnp.maximum(jnp.max(s, axis=1, keepdims=True), jnp.max(sn, axis=1, keepdims=True))
                p = jnp.exp(s - m)
                pn = jnp.exp(sn - m)
                den = jnp.sum(p, axis=1, keepdims=True) + jnp.sum(pn, axis=1, keepdims=True)
                acc = jnp.concatenate([c_ref[0, 1, h] * p[h:h + 1, :] for h in range(A_HEADS)], axis=0)
                res.append(_nt_sel((sub == 0).astype(BF16), acc)[0:1, :])
                packed = packed + place(den, i) + place(m + jnp.log(den), 8 + i)
                packed = packed + pltpu.roll(pn, 16 + 8 * i, 1)
            sq = rows_of(packed)
            denx = widen(sq[0:8])
            lsex = widen(sq[8:16])
            for i in range(T):
                pnx = widen(sq[16 + 8 * i:24 + 8 * i])
                o = res[i] + jnp.sum(pnx[0:T] * vn_nat, axis=0, keepdims=True)
                out_refs[g][0, i:i + 1, :] = o / denx[i:i + 1, :]
            out_refs[A_GROUPS + g][0] = lsex[0:T]
        else:
            cls = lane & (d - 1)
            self_s = head_rows(q_nat * kn_nat)
            pat = patterns(q_t, (row_sq == (lane_sq & (d - 1))) & (row_sq < T))
            s_tiles = [head_scores(lambda h: c_ref[0, 0, h, :, t * LANES:(t + 1) * LANES],
                                   lambda h: pat[h * HEAD_DIM:(h + 1) * HEAD_DIM])
                       for t in range(n_tiles)]
            smax = s_tiles[0]
            for t in range(1, n_tiles):
                smax = jnp.maximum(smax, s_tiles[t])
            mrow = jnp.full((8, LANES), jnp.inf, F32)
            m_cls = []
            for i in range(T):
                mi = jnp.maximum(jnp.max(jnp.where(cls == i, smax, -jnp.inf), axis=1, keepdims=True),
                                 self_s[:, i:i + 1])
                m_cls.append(mi)
                mrow = jnp.where(cls == i, mi, mrow)
            p_tiles = [jnp.exp(s_tiles[t] - mrow) for t in range(n_tiles)]
            psum = p_tiles[0]
            for t in range(1, n_tiles):
                psum = psum + p_tiles[t]
            accs = []
            for h in range(A_HEADS):
                acc = jnp.zeros((HEAD_DIM, LANES), F32)
                for t in range(n_tiles):
                    acc = acc + c_ref[0, 1, h, :, t * LANES:(t + 1) * LANES] * p_tiles[t][h:h + 1, :]
                accs.append(acc)
            res = _nt_sel((cls == sub).astype(BF16), jnp.concatenate(accs, axis=0))
            packed = jnp.zeros((8, LANES), F32)
            for i in range(T):
                ps = jnp.exp(self_s[:, i:i + 1] - m_cls[i])
                den = jnp.sum(jnp.where(cls == i, psum, 0.0), axis=1, keepdims=True) + ps
                packed = packed + place(ps, i) + place(den, 8 + i) + place(m_cls[i] + jnp.log(den), 16 + i)
            sq = rows_of(packed)
            psx, denx, lsex = widen(sq[0:8]), widen(sq[8:16]), widen(sq[16:24])
            out_refs[g][0] = (res[0:T] + psx[0:T] * vn_nat) / denx[0:T]
            out_refs[A_GROUPS + g][0] = lsex[0:T]


def _a_sample(y, caches, Bd, T):
    width = A_GROUPS * 3 * A_WIDTH
    y3 = y.reshape(Bd, T, width)
    views = []
    specs = [pl.BlockSpec((1, T, width), lambda b: (b, 0, 0))]
    for g, c in enumerate(caches):
        d = A_DILATIONS[g]
        assert c.shape[1] == A_SPAN * d and (d == 1 or T <= d) and T <= 8
        views.append(c.transpose(0, 2, 3, 4, 1))
        specs.append(pl.BlockSpec((1, 2, A_HEADS, HEAD_DIM, A_SPAN * d), lambda b: (b, 0, 0, 0, 0)))
    hind_np = np.zeros((LANES, A_WIDTH), np.float32)
    hind_np[np.arange(A_WIDTH) // HEAD_DIM, np.arange(A_WIDTH)] = 1.0
    hind = jnp.asarray(hind_np[:A_HEADS], BF16)
    hindt = jnp.asarray(hind_np, BF16)
    specs += [pl.BlockSpec(hind.shape, lambda b: (0, 0)), pl.BlockSpec(hindt.shape, lambda b: (0, 0))]
    out_spec = pl.BlockSpec((1, T, A_WIDTH), lambda b: (b, 0, 0))
    outs = pl.pallas_call(
        functools.partial(_a_sample_kernel, T),
        grid=(Bd,),
        in_specs=specs,
        out_specs=[out_spec] * (2 * A_GROUPS),
        out_shape=[jax.ShapeDtypeStruct((Bd, T, A_WIDTH), F32)] * (2 * A_GROUPS),
        compiler_params=_params("parallel"),
        name="a_sample",
    )(y3, *views, hind, hindt)
    return [o.reshape(Bd * T, A_WIDTH) for o in outs]


def _mixer_a(xp, xs, gain, w_in, qk_gain, w_out, caches, tabs_p, tabs_s, Bn, S, Bd, T):
    flags = (True, True, False) * A_GROUPS
    ones = jnp.ones((A_WIDTH,), F32)
    rows = [jnp.tile(qk_gain[0], A_HEADS), jnp.tile(qk_gain[1], A_HEADS), ones] * A_GROUPS
    gain_rows = jnp.stack(rows).reshape(3 * A_GROUPS, 1, A_WIDTH)
    mask_rows = jnp.stack([ones, ones, 0.0 * ones] * A_GROUPS).reshape(3 * A_GROUPS, 1, A_WIDTH)
    w = w_in.astype(BF16)
    yp = _proj_qk(xp, gain, w, flags, gain_rows, mask_rows, tabs_p, 1024, A_WIDTH)
    ys = _proj_qk(xs, gain, w, flags, gain_rows, mask_rows, tabs_s, xs.shape[0], A_WIDTH)
    wo = w_out.astype(BF16)
    pr = [_a_prompt(yp, g, Bn, S) for g in range(A_GROUPS)]
    xp = _row_call(_out_a_kernel, "out_a", xp,
                   [(o, A_WIDTH, 0) for o, _ in pr] + [(l, A_WIDTH, 0) for _, l in pr], [wo], 512)
    sr = _a_sample(ys, caches, Bd, T)
    xs = _row_call(_out_a_kernel, "out_a", xs, [(o, A_WIDTH, 0) for o in sr], [wo], xs.shape[0])
    yp3 = yp.reshape(Bn, S, A_GROUPS * 3 * A_WIDTH)
    ys3 = ys.reshape(Bd, T, A_GROUPS * 3 * A_WIDTH)
    new_p, new_s = [], []
    for g in range(A_GROUPS):
        win = min(A_SPAN * A_DILATIONS[g], S)
        c0 = (3 * g + 1) * A_WIDTH
        new_p.append(yp3[:, S - win:, c0:c0 + 2 * A_WIDTH].reshape(Bn, win, 2, A_HEADS, HEAD_DIM))
        new_s.append(ys3[:, :, c0:c0 + 2 * A_WIDTH].reshape(Bd, T, 2, A_HEADS, HEAD_DIM))
    return xp, xs, new_p, new_s


def _b_kmean_kernel(k_ref, o_ref):
    nblk = k_ref.shape[1] // B_BLOCK
    rows = [jnp.mean(k_ref[0, n * B_BLOCK:(n + 1) * B_BLOCK, :], axis=0, keepdims=True) for n in range(nblk)]
    o_ref[0] = jnp.concatenate(rows, axis=0)


def _b_gate_kernel(q_ref, kmh_ref, kml_ref, o_ref):
    i = pl.program_id(1)
    qh, ql = _split2(q_ref[0])
    kmh, kml = kmh_ref[0], kml_ref[0]
    gate = _dot_nt(kmh, qh) + (_dot_nt(kmh, ql) + _dot_nt(kml, qh))
    tq = gate.shape[1]
    nblk = gate.shape[0] // B_HEADS
    gate = gate.reshape(B_HEADS, nblk, tq)
    blk = lax.broadcasted_iota(jnp.int32, (B_HEADS, nblk, tq), 1)
    blkf = blk.astype(F32)
    gate = jnp.where(blk < i, gate, -jnp.inf)
    sel = jnp.zeros((B_HEADS, nblk, tq), jnp.bool_)
    for _ in range(B_TOPK):
        mx = jnp.max(gate, axis=1, keepdims=True)
        first = jnp.min(jnp.where(gate == mx, blkf, float(nblk)), axis=1, keepdims=True)
        hit = blkf == first
        sel = sel | (hit & (mx > -jnp.inf))
        gate = jnp.where(hit, -jnp.inf, gate)
    bias = jnp.where(sel | (blk >= i), 0.0, NEG).reshape(B_HEADS * nblk, tq)
    o_ref[0] = bias.T


def _b_attn_kernel(qi_ref, kn_ref, q_ref, k_ref, vt_ref, o_ref, *state):
    t = pl.program_id(1)
    i, n = qi_ref[t], kn_ref[t]
    tq = B_BLOCK
    wide = B_REP * tq
    m_refs, acc_refs = state[:B_KV_HEADS], state[B_KV_HEADS:]

    @pl.when(n == 0)
    def _():
        for kvh in range(B_KV_HEADS):
            m_refs[kvh][...] = jnp.full_like(m_refs[kvh], -jnp.inf)
            acc_refs[kvh][...] = jnp.zeros_like(acc_refs[kvh])

    def sweep(causal):
        if causal:
            kk = lax.broadcasted_iota(jnp.int32, (B_BLOCK, wide), 0)
            qq = lax.broadcasted_iota(jnp.int32, (B_BLOCK, wide), 1) & (tq - 1)
            keep = kk <= qq
        for kvh in range(B_KV_HEADS):
            k = k_ref[0, :, kvh * LANES:(kvh + 1) * LANES]
            s = _dot_nt(k, q_ref[0, 0, kvh])
            if causal:
                s = jnp.where(keep, s, NEG)
            m_old = m_refs[kvh][...]
            m_new = jnp.maximum(m_old, jnp.max(s, axis=0, keepdims=True))
            p = jnp.exp2(s - m_new).astype(BF16)
            acc_refs[kvh][...] = jnp.exp2(m_old - m_new) * acc_refs[kvh][...] + _dot(vt_ref[0, kvh], p)
            m_refs[kvh][...] = m_new

    @pl.when(n < i)
    def _():
        sweep(False)

    @pl.when(n == i)
    def _():
        sweep(True)
        for kvh in range(B_KV_HEADS):
            acc = acc_refs[kvh][...]
            ot = acc[0:HEAD_DIM] / acc[HEAD_DIM:HEAD_DIM + 1]
            for r in range(0, B_REP, 2):
                pair = [ot[:, (r + e) * tq:(r + e + 1) * tq].T for e in range(2)]
                c0 = (kvh * B_REP + r) * HEAD_DIM
                o_ref[0, :, c0:c0 + 2 * HEAD_DIM] = jnp.concatenate(pair, axis=1)


def _b_prompt(y, Bn, S):
    nblk = S // B_BLOCK
    assert nblk % 8 == 0 and LANES % nblk == 0
    nq = B_HEADS * HEAD_DIM
    nk = B_KV_HEADS * HEAD_DIM
    y3 = y.reshape(Bn, S, nq + 2 * nk)
    kmean = pl.pallas_call(
        _b_kmean_kernel,
        grid=(Bn,),
        in_specs=[pl.BlockSpec((1, S, nk), lambda b: (b, 0, nq // nk))],
        out_specs=pl.BlockSpec((1, nblk, nk), lambda b: (b, 0, 0)),
        out_shape=jax.ShapeDtypeStruct((Bn, nblk, nk), F32),
        compiler_params=_params("parallel"),
        name="b_kmean",
    )(y3)
    km = kmean.reshape(Bn, nblk, B_KV_HEADS, HEAD_DIM)
    km = jnp.repeat(km, B_REP, axis=2)
    eye = jnp.eye(B_HEADS, dtype=F32)
    kmbd = jnp.einsum('bnhe,hg->bgnhe', km, eye).reshape(Bn, B_HEADS * nblk, nq)
    kmh = kmbd.astype(BF16)
    kml = (kmbd - kmh.astype(F32)).astype(BF16)
    bias = pl.pallas_call(
        _b_gate_kernel,
        grid=(Bn, nblk),
        in_specs=[
            pl.BlockSpec((1, B_BLOCK, nq), lambda b, i: (b, i, 0)),
            pl.BlockSpec((1, B_HEADS * nblk, nq), lambda b, i: (b, 0, 0)),
            pl.BlockSpec((1, B_HEADS * nblk, nq), lambda b, i: (b, 0, 0)),
        ],
        out_specs=pl.BlockSpec((1, B_BLOCK, B_HEADS * nblk), lambda b, i: (b, i, 0)),
        out_shape=jax.ShapeDtypeStruct((Bn, S, B_HEADS * nblk), F32),
        compiler_params=_params("parallel", "arbitrary"),
        name="b_gate",
    )(y3, kmh, kml)
    q = (y3[:, :, :nq] * (HEAD_DIM ** -0.5 * LOG2E)).astype(BF16).reshape(Bn, S, B_HEADS, HEAD_DIM)
    pad = LANES - HEAD_DIM - nblk
    q_aug = jnp.concatenate([q, bias.astype(BF16).reshape(Bn, S, B_HEADS, nblk),
                             jnp.zeros((Bn, S, B_HEADS, pad), BF16)], axis=-1)
    q_aug = q_aug.reshape(Bn, nblk, B_BLOCK, B_KV_HEADS, B_REP, LANES).transpose(0, 1, 3, 4, 2, 5)
    q_aug = q_aug.reshape(Bn, nblk, B_KV_HEADS, B_REP * B_BLOCK, LANES)
    k = y3[:, :, nq:nq + nk].astype(BF16).reshape(Bn, S, B_KV_HEADS, HEAD_DIM)
    onehot = jax.nn.one_hot(jnp.arange(S) // B_BLOCK, nblk, dtype=BF16)
    onehot = jnp.broadcast_to(onehot[None, :, None, :], (Bn, S, B_KV_HEADS, nblk))
    k_aug = jnp.concatenate([k, onehot, jnp.zeros((Bn, S, B_KV_HEADS, pad), BF16)], axis=-1)
    k_aug = k_aug.reshape(Bn, S, B_KV_HEADS * LANES)
    vt = y3[:, :, nq + nk:].astype(BF16).reshape(Bn, S, B_KV_HEADS, HEAD_DIM).transpose(0, 2, 3, 1)
    vt = jnp.concatenate([vt, jnp.ones((Bn, B_KV_HEADS, 1, S), BF16),
                          jnp.zeros((Bn, B_KV_HEADS, B_VT_ROWS - HEAD_DIM - 1, S), BF16)], axis=2)
    pairs = [(i, n) for i in range(nblk) for n in range(i + 1)]
    qi = jnp.asarray([p[0] for p in pairs], jnp.int32)
    kn = jnp.asarray([p[1] for p in pairs], jnp.int32)
    o = pl.pallas_call(
        _b_attn_kernel,
        grid_spec=pltpu.PrefetchScalarGridSpec(
            num_scalar_prefetch=2,
            grid=(Bn, len(pairs)),
            in_specs=[
                pl.BlockSpec((1, 1, B_KV_HEADS, B_REP * B_BLOCK, LANES),
                             lambda b, t, qi, kn: (b, qi[t], 0, 0, 0)),
                pl.BlockSpec((1, B_BLOCK, B_KV_HEADS * LANES), lambda b, t, qi, kn: (b, kn[t], 0)),
                pl.BlockSpec((1, B_KV_HEADS, B_VT_ROWS, B_BLOCK), lambda b, t, qi, kn: (b, 0, 0, kn[t])),
            ],
            out_specs=pl.BlockSpec((1, B_BLOCK, nq), lambda b, t, qi, kn: (b, qi[t], 0)),
            scratch_shapes=([pltpu.VMEM((1, B_REP * B_BLOCK), F32)] * B_KV_HEADS
                            + [pltpu.VMEM((B_VT_ROWS, B_REP * B_BLOCK), F32)] * B_KV_HEADS),
        ),
        out_shape=jax.ShapeDtypeStruct((Bn, S, nq), F32),
        compiler_params=_params("parallel", "arbitrary"),
        name="b_attn",
    )(qi, kn, q_aug, k_aug, vt)
    return o.reshape(Bn * S, nq)


def _b_sample_kernel(n_pages, T, pt_ref, q_ref, kn_ref, vn_ref, *refs):
    del pt_ref
    page_refs = refs[:n_pages]
    o_ref = refs[n_pages]
    nk = B_KV_HEADS * HEAD_DIM
    q = q_ref[0]
    qh, ql = _split2(q)
    rows = q.shape[0]
    page_rows = page_refs[0].shape[2]
    pages_per_block = B_BLOCK // page_rows
    n_blocks = n_pages // pages_per_block
    scale = HEAD_DIM ** -0.5
    lane = lax.broadcasted_iota(jnp.int32, (1, LANES), 1)
    lanef = lax.broadcasted_iota(jnp.int32, (rows, LANES), 1).astype(F32)
    scores = []
    km = jnp.zeros((nk, LANES), F32)
    for p in range(n_pages):
        kt = page_refs[p][0, :nk, :]
        scores.append(_dot(qh, kt.astype(BF16)) * scale)
        ksum = jnp.sum(kt, axis=1, keepdims=True) * (1.0 / B_BLOCK)
        km = km + ksum * (lane == p // pages_per_block).astype(F32)
    kmh, kml = _split2(km)
    gate = _dot(qh, kmh) + (_dot(qh, kml) + _dot(ql, kmh))
    gate = jnp.where(lanef < float(n_blocks), gate, -jnp.inf)
    sel = jnp.zeros((rows, LANES), jnp.bool_)
    for _ in range(min(B_TOPK, n_blocks)):
        mx = jnp.max(gate, axis=1, keepdims=True)
        first = jnp.min(jnp.where(gate == mx, lanef, float(LANES)), axis=1, keepdims=True)
        hit = lanef == first
        sel = sel | (hit & (mx > -jnp.inf))
        gate = jnp.where(hit, -jnp.inf, gate)
    self = sel.astype(F32)
    qidx = lax.broadcasted_iota(jnp.int32, (rows, 1), 0) % T
    kn, vn = kn_ref[0], vn_ref[0]
    own = []
    for j in range(T):
        sj = jnp.sum(q * kn[j:j + 1, :], axis=1, keepdims=True) * scale
        own.append(jnp.where(qidx >= j, sj, NEG))
    m = own[0]
    for j in range(1, T):
        m = jnp.maximum(m, own[j])
    for p in range(n_pages):
        b = p // pages_per_block
        scores[p] = jnp.where(self[:, b:b + 1] > 0.0, scores[p], NEG)
        m = jnp.maximum(m, jnp.max(scores[p], axis=-1, keepdims=True))
    den = jnp.zeros((rows, 1), F32)
    acc = jnp.zeros((rows, nk), F32)
    for j in range(T):
        pj = jnp.exp(own[j] - m)
        den = den + pj
        acc = acc + pj * vn[j:j + 1, :]
    for p in range(n_pages):
        pp = jnp.exp(scores[p] - m)
        den = den + jnp.sum(pp, axis=-1, keepdims=True)
        acc = acc + _dot_nt(pp.astype(BF16), page_refs[p][0, nk:, :].astype(BF16))
    acc = acc / den
    kvh = lax.broadcasted_iota(jnp.int32, (rows, 1), 0) // (T * B_REP)
    out = jnp.zeros((rows, HEAD_DIM), F32)
    for h in range(B_KV_HEADS):
        out = out + jnp.where(kvh == h, acc[:, h * HEAD_DIM:(h + 1) * HEAD_DIM], 0.0)
    o_ref[0] = out


def _b_sample(y, pool, page_table, Bd, T):
    nq = B_HEADS * HEAD_DIM
    nk = B_KV_HEADS * HEAD_DIM
    n_pages = page_table.shape[1]
    page_rows = pool.shape[1]
    assert B_BLOCK % page_rows == 0 and (n_pages * page_rows) % B_BLOCK == 0 and T <= B_BLOCK
    assert n_pages * page_rows // B_BLOCK <= LANES
    y3 = y.reshape(Bd, T, nq + 2 * nk)
    q = y3[:, :, :nq].reshape(Bd, T, B_HEADS, HEAD_DIM).transpose(0, 2, 1, 3)
    kvsel = jnp.asarray(np.kron(np.eye(B_KV_HEADS), np.ones((B_REP, 1))), F32)
    qbd = (q[:, :, :, None, :] * kvsel[None, :, None, :, None]).reshape(Bd, B_HEADS * T, nk)
    kn = y3[:, :, nq:nq + nk]
    vn = y3[:, :, nq + nk:]
    pool_t = pool.transpose(0, 2, 3, 4, 1).reshape(pool.shape[0], 2 * nk, page_rows)
    page_specs = [pl.BlockSpec((1, 2 * nk, page_rows), lambda b, pt, p=p: (pt[b, p], 0, 0))
                  for p in range(n_pages)]
    o = pl.pallas_call(
        functools.partial(_b_sample_kernel, n_pages, T),
        grid_spec=pltpu.PrefetchScalarGridSpec(
            num_scalar_prefetch=1,
            grid=(Bd,),
            in_specs=[
                pl.BlockSpec((1, B_HEADS * T, nk), lambda b, pt: (b, 0, 0)),
                pl.BlockSpec((1, T, nk), lambda b, pt: (b, 0, 0)),
                pl.BlockSpec((1, T, nk), lambda b, pt: (b, 0, 0)),
            ] + page_specs,
            out_specs=pl.BlockSpec((1, B_HEADS * T, HEAD_DIM), lambda b, pt: (b, 0, 0)),
        ),
        out_shape=jax.ShapeDtypeStruct((Bd, B_HEADS * T, HEAD_DIM), F32),
        compiler_params=_params("parallel"),
        name="b_sample",
    )(page_table, qbd, kn, vn, *([pool_t] * n_pages))
    return o.reshape(Bd, B_HEADS, T, HEAD_DIM).transpose(0, 2, 1, 3).reshape(Bd * T, nq)


def _mixer_b(xp, xs, gain, w_in, qk_gain, w_out, pool, page_table, tabs_p, tabs_s, Bn, S, Bd, T):
    tn = 512
    nq = B_HEADS * HEAD_DIM
    nk = B_KV_HEADS * HEAD_DIM
    flags = (True, True, True)
    ones = jnp.ones((nk,), F32)
    qg = jnp.tile(qk_gain[0], tn // HEAD_DIM)
    kg = jnp.concatenate([jnp.tile(qk_gain[1], B_KV_HEADS), ones])
    gain_rows = jnp.stack([qg, qg, kg]).reshape(3, 1, tn)
    mask_rows = jnp.stack([jnp.ones((tn,), F32), jnp.ones((tn,), F32),
                           jnp.concatenate([ones, 0.0 * ones])]).reshape(3, 1, tn)
    w = w_in.astype(BF16)
    yp = _proj_qk(xp, gain, w, flags, gain_rows, mask_rows, tabs_p, 1024, tn)
    ys = _proj_qk(xs, gain, w, flags, gain_rows, mask_rows, tabs_s, xs.shape[0], tn)
    wo = w_out.astype(BF16)
    op = _b_prompt(yp, Bn, S)
    xp = _row_call(_out_plain_kernel, "out_b", xp, [(op, nq, 0)], [wo], 512)
    osm = _b_sample(ys, pool, page_table, Bd, T)
    xs = _row_call(_out_plain_kernel, "out_b", xs, [(osm, nq, 0)], [wo], xs.shape[0])
    kv_p = yp[:, nq:].reshape(Bn, S, 2, B_KV_HEADS, HEAD_DIM)
    kv_s = ys[:, nq:].reshape(Bd, T, 2, B_KV_HEADS, HEAD_DIM)
    return xp, xs, kv_p, kv_s


def _gla_kernel(nv, q_ref, k_ref, v_ref, glr_ref, wg_ref, bg_ref, tri_ref, blk_ref, s0_ref,
                o_ref, sT_ref, s_ref, qp_ref, kp_ref, vp_ref, gp_ref):
    t = pl.program_id(1)
    Tt = SEQ_TILE

    @pl.when(t == 0)
    def _():
        s_ref[...] = s0_ref[0]

    if nv < Tt:
        qp_ref[...] = jnp.zeros_like(qp_ref)
        kp_ref[...] = jnp.zeros_like(kp_ref)
        vp_ref[...] = jnp.zeros_like(vp_ref)
        gp_ref[...] = jnp.zeros_like(gp_ref)
    qp_ref[0:nv, :] = q_ref[0]
    kp_ref[0:nv, :] = k_ref[0]
    vp_ref[0:nv, :] = v_ref[0]
    gp_ref[0:nv, :] = glr_ref[0]
    q, k, v = qp_ref[...], kp_ref[...], vp_ref[...]

    x = _dot(gp_ref[...].astype(BF16), wg_ref[...]) + bg_ref[...]
    la = (jnp.minimum(x, 0.0) - jnp.log(1.0 + jnp.exp(-jnp.abs(x)))) * (1.0 / C_TAU)
    row = lax.broadcasted_iota(jnp.int32, (Tt, C_KEY), 0)
    la = jnp.where(row < nv, la, 0.0)
    b = _sel_left(tri_ref[...], la, terms=3)
    bend = _sel_left(blk_ref[...], la, terms=3)
    qe = q * (C_DK ** -0.5) * jnp.exp(b)
    ke = k * jnp.exp(-b)
    kd = k * jnp.exp(bend - b)
    kdt = kd.T
    bendt = bend.T
    qeb, keb, vb = qe.astype(BF16), ke.astype(BF16), v.astype(BF16)
    causal = tri_ref[...] > 0
    lane_t = lax.broadcasted_iota(jnp.int32, (C_DK, Tt), 1)
    n_chunks = -(-nv // C_CHUNK)
    for h in range(C_HEADS):
        ks = slice(h * C_DK, (h + 1) * C_DK)
        vs = slice(h * C_DV, (h + 1) * C_DV)
        att = jnp.where(causal, _dot_nt(qeb[:, ks], keb[:, ks]), 0.0)
        o_intra = _dot(att.astype(BF16), vb[:, vs])
        st = s_ref[h]
        parts = []
        for c in range(n_chunks):
            rows = slice(c * C_CHUNK, (c + 1) * C_CHUNK)
            parts.append(o_intra[rows] + _dot(qeb[rows, ks], st.astype(BF16)))
            in_chunk = (lane_t >= c * C_CHUNK) & (lane_t < (c + 1) * C_CHUNK)
            last = lane_t == (c + 1) * C_CHUNK - 1
            dec = jnp.exp(jnp.sum(jnp.where(last, bendt[ks, :], 0.0), axis=1, keepdims=True))
            kdc = jnp.where(in_chunk, kdt[ks, :], 0.0).astype(BF16)
            st = dec * st + _dot(kdc, vb[:, vs])
        s_ref[h] = st
        oh = parts[0] if n_chunks == 1 else jnp.concatenate(parts, axis=0)
        o_ref[0, :, vs] = oh[0:nv]

    @pl.when(t == pl.num_programs(1) - 1)
    def _():
        sT_ref[0] = s_ref[...]


def _gla(y, w_gate2, b_gate, s0, nb, nt, nv):
    Tt = SEQ_TILE
    y3 = y.reshape(nb * nt, nv, C_NPAD)
    wg = jnp.zeros((LANES, C_KEY), F32).at[:C_RANK].set(w_gate2).astype(BF16)
    idx = np.arange(Tt)
    same = (idx[:, None] // C_CHUNK) == (idx[None, :] // C_CHUNK)
    tri = jnp.asarray(same & (idx[None, :] <= idx[:, None]), BF16)
    blk = jnp.asarray(same, BF16)

    def yspec(width, cb):
        return pl.BlockSpec((1, nv, width), lambda b, t: (b * nt + t, 0, cb))

    def cspec(a):
        return pl.BlockSpec(a.shape, lambda b, t, nd=a.ndim: (0,) * nd)

    bg = b_gate.reshape(1, C_KEY)
    o, sT = pl.pallas_call(
        functools.partial(_gla_kernel, nv),
        grid=(nb, nt),
        in_specs=[
            yspec(C_KEY, 0), yspec(C_KEY, 1), yspec(C_VAL, 1), yspec(LANES, (2 * C_KEY + 2 * C_VAL) // LANES),
            cspec(wg), cspec(bg), cspec(tri), cspec(blk),
            pl.BlockSpec((1, C_HEADS, C_DK, C_DV), lambda b, t: (b, 0, 0, 0)),
        ],
        out_specs=[
            pl.BlockSpec((1, nv, C_VAL), lambda b, t: (b * nt + t, 0, 0)),
            pl.BlockSpec((1, C_HEADS, C_DK, C_DV), lambda b, t: (b, 0, 0, 0)),
        ],
        out_shape=[
            jax.ShapeDtypeStruct((nb * nt, nv, C_VAL), F32),
            jax.ShapeDtypeStruct((nb, C_HEADS, C_DK, C_DV), F32),
        ],
        scratch_shapes=[
            pltpu.VMEM((C_HEADS, C_DK, C_DV), F32),
            pltpu.VMEM((Tt, C_KEY), F32), pltpu.VMEM((Tt, C_KEY), F32),
            pltpu.VMEM((Tt, C_VAL), F32), pltpu.VMEM((Tt, LANES), F32),
        ],
        compiler_params=_params("parallel", "arbitrary"),
        name="gla",
    )(y3, y3, y3, y3, wg, bg, tri, blk, s0)
    return o.reshape(nb * nt * nv, C_VAL), sT


def _mixer_c(xp, xs, gain, w_in, w_gate2, b_gate, norm_g, w_out, state, Bn, S, Bd, T):
    n_in = w_in.shape[1]
    w = jnp.zeros((D_MODEL, C_NPAD), BF16).at[:, :n_in].set(w_in.astype(BF16))
    yp = _proj(xp, gain, w, 1024)
    ys = _proj(xs, gain, w, xs.shape[0])
    zero = jnp.zeros((Bn, C_HEADS, C_DK, C_DV), F32)
    op, sp = _gla(yp, w_gate2, b_gate, zero, Bn, S // SEQ_TILE, SEQ_TILE)
    osm, ss = _gla(ys, w_gate2, b_gate, state, Bd, 1, T)
    wo = w_out.astype(BF16)
    ng = norm_g.reshape(1, C_DV)
    rcb = (2 * C_KEY + C_VAL) // C_VAL
    xp = _row_call(_out_c_kernel, "out_c", xp, [(op, C_VAL, 0), (yp, C_VAL, rcb)], [ng, wo], 512)
    xs = _row_call(_out_c_kernel, "out_c", xs, [(osm, C_VAL, 0), (ys, C_VAL, rcb)], [ng, wo], xs.shape[0])
    return xp, xs, sp, ss


def _ssd_kernel(nv, xa_ref, xb_ref, bc_ref, dt_ref, cs_ref, cw_ref, cb_ref, dtb_ref, alog_ref, dsk_ref,
                tri_ref, exp_ref, expt_ref, h0_ref, y_ref, hT_ref, h_ref, xp_ref, dtp_ref, yd_ref):
    t = pl.program_id(1)
    Tt = SEQ_TILE
    pre = 8

    @pl.when(t == 0)
    def _():
        h_ref[...] = h0_ref[0]
        xp_ref[0:pre, :] = cs_ref[0]

    if nv < Tt:
        xp_ref[pre:, :] = jnp.zeros((Tt, D_XBC), F32)
        dtp_ref[...] = jnp.zeros_like(dtp_ref)
    xp_ref[pre:pre + nv, 0:1024] = xa_ref[0]
    xp_ref[pre:pre + nv, 1024:2048] = xb_ref[0]
    xp_ref[pre:pre + nv, 2048:3072] = bc_ref[0]
    dtp_ref[0:nv, :] = dt_ref[0]

    conv = cb_ref[...]
    for w in range(D_CONV):
        conv = conv + xp_ref[pl.ds(pre - (D_CONV - 1) + w, Tt), :] * cw_ref[w:w + 1, :]
    xp_ref[0:pre, :] = xp_ref[Tt:Tt + pre, :]
    xbc = _silu(conv)
    x = xbc[:, :D_INNER]
    nbc = D_GROUPS * D_STATE
    bm = xbc[:, D_INNER:D_INNER + nbc].astype(BF16)
    cm = xbc[:, D_INNER + nbc:].astype(BF16)

    row = lax.broadcasted_iota(jnp.int32, (Tt, LANES), 0)
    lane = lax.broadcasted_iota(jnp.int32, (Tt, LANES), 1)
    live = (row < nv) & (lane < D_HEADS)
    dt = jnp.where(live, _softplus(dtp_ref[...] + dtb_ref[...]), 0.0)
    cum = _sel_left(tri_ref[...], dt * (-jnp.exp(alog_ref[...])), terms=3)
    cumt = cum.T
    cend = cum[Tt - 1:Tt, :]
    ex = exp_ref[...]
    stack = jnp.concatenate([jnp.exp(cend - cum) * dt, dt, jnp.exp(cum)], axis=0)
    wide = _sel_right(stack, ex)
    x_state = (x * wide[0:Tt]).astype(BF16)
    x_dt = (x * wide[Tt:2 * Tt]).astype(BF16)
    off_scale = wide[2 * Tt:]

    causal = tri_ref[...] > 0
    gw = D_INNER // D_GROUPS
    for g in range(D_GROUPS):
        bg = bm[:, g * D_STATE:(g + 1) * D_STATE]
        cg = cm[:, g * D_STATE:(g + 1) * D_STATE]
        cb = _dot_nt(cg, bg)
        for jj in range(0, D_HPG, 2):
            pair = []
            for j in (g * D_HPG + jj, g * D_HPG + jj + 1):
                seg = cum[:, j:j + 1] - cumt[j:j + 1, :]
                lmat = jnp.exp(jnp.where(causal, seg, -jnp.inf))
                pair.append(_dot((cb * lmat).astype(BF16), x_dt[:, j * D_HEADDIM:(j + 1) * D_HEADDIM]))
            c0 = (g * D_HPG + jj) * D_HEADDIM
            yd_ref[:, c0:c0 + 2 * D_HEADDIM] = jnp.concatenate(pair, axis=1)
        hg = h_ref[g * gw:(g + 1) * gw, :]
        y_off = _dot_nt(cg, hg.astype(BF16)) * off_scale[:, g * gw:(g + 1) * gw]
        yd_ref[:, g * gw:(g + 1) * gw] += y_off
    y = yd_ref[...] + dsk_ref[...] * x
    y_ref[0] = y[0:nv]

    dcol = jnp.broadcast_to(jnp.exp(cumt[:, Tt - 1:Tt]), (LANES, D_STATE))
    dfull = _sel_left(expt_ref[...], dcol)
    xst = x_state.astype(F32).T.astype(BF16)
    for g in range(D_GROUPS):
        rows = slice(g * gw, (g + 1) * gw)
        h_ref[rows, :] = dfull[rows] * h_ref[rows, :] + _dot(xst[rows], bm[:, g * D_STATE:(g + 1) * D_STATE])

    @pl.when(t == pl.num_programs(1) - 1)
    def _():
        hT_ref[0] = h_ref[...]


def _ssd(y, conv_w, conv_b, dt_bias, a_log, d_skip, h0, c0, nb, nt, nv):
    Tt = SEQ_TILE
    y3 = y.reshape(nb * nt, nv, D_NPAD)
    idx = np.arange(Tt)
    tri = jnp.asarray(idx[None, :] <= idx[:, None], BF16)
    ex_np = np.zeros((LANES, D_INNER), np.float32)
    ex_np[np.arange(D_INNER) // D_HEADDIM, np.arange(D_INNER)] = 1.0
    ex = jnp.asarray(ex_np, BF16)
    ext = jnp.asarray(ex_np.T, BF16)

    def pad_row(v):
        return jnp.zeros((1, LANES), F32).at[0, :D_HEADS].set(v)

    cs = jnp.zeros((nb, 8, D_XBC), F32).at[:, 8 - (D_CONV - 1):].set(c0)
    dsk = jnp.repeat(d_skip, D_HEADDIM).reshape(1, D_INNER)

    def yspec(width, cb):
        return pl.BlockSpec((1, nv, width), lambda b, t: (b * nt + t, 0, cb))

    def cspec(a):
        return pl.BlockSpec(a.shape, lambda b, t, nd=a.ndim: (0,) * nd)

    consts = [conv_w, conv_b.reshape(1, D_XBC), pad_row(dt_bias), pad_row(a_log), dsk, tri, ex, ext]
    yo, hT = pl.pallas_call(
        functools.partial(_ssd_kernel, nv),
        grid=(nb, nt),
        in_specs=[yspec(1024, 2), yspec(1024, 3), yspec(1024, 4), yspec(LANES, D_DT_COL),
                  pl.BlockSpec((1, 8, D_XBC), lambda b, t: (b, 0, 0))]
        + [cspec(a) for a in consts]
        + [pl.BlockSpec((1, D_INNER, D_STATE), lambda b, t: (b, 0, 0))],
        out_specs=[
            pl.BlockSpec((1, nv, D_INNER), lambda b, t: (b * nt + t, 0, 0)),
            pl.BlockSpec((1, D_INNER, D_STATE), lambda b, t: (b, 0, 0)),
        ],
        out_shape=[
            jax.ShapeDtypeStruct((nb * nt, nv, D_INNER), F32),
            jax.ShapeDtypeStruct((nb, D_INNER, D_STATE), F32),
        ],
        scratch_shapes=[
            pltpu.VMEM((D_INNER, D_STATE), F32),
            pltpu.VMEM((Tt + 8, D_XBC), F32),
            pltpu.VMEM((Tt, LANES), F32),
            pltpu.VMEM((Tt, D_INNER), F32),
        ],
        compiler_params=_params("parallel", "arbitrary"),
        name="ssd",
    )(y3, y3, y3, y3, cs, *consts, h0)
    return yo.reshape(nb * nt * nv, D_INNER), hT


def _mixer_d(xp, xs, gain, w_in, conv_w, conv_b, dt_bias, a_log, d_skip, norm_g, w_out,
             ssm_state, conv_state, Bn, S, Bd, T):
    n_in = w_in.shape[1]
    w = jnp.zeros((D_MODEL, D_NPAD), BF16).at[:, :n_in].set(w_in.astype(BF16))
    yp = _proj(xp, gain, w, 1024)
    ys = _proj(xs, gain, w, xs.shape[0])
    h0p = jnp.zeros((Bn, D_INNER, D_STATE), F32)
    c0p = jnp.zeros((Bn, D_CONV - 1, D_XBC), F32)
    op, hp = _ssd(yp, conv_w, conv_b, dt_bias, a_log, d_skip, h0p, c0p, Bn, S // SEQ_TILE, SEQ_TILE)
    h0s = ssm_state.reshape(Bd, D_INNER, D_STATE)
    osm, hs = _ssd(ys, conv_w, conv_b, dt_bias, a_log, d_skip, h0s, conv_state, Bd, 1, T)
    wo = w_out.astype(BF16)
    ng = norm_g.reshape(1, D_INNER)
    xp = _row_call(_out_d_kernel, "out_d", xp, [(op, D_INNER, 0), (yp, D_INNER, 0)], [ng, wo], 512)
    xs = _row_call(_out_d_kernel, "out_d", xs, [(osm, D_INNER, 0), (ys, D_INNER, 0)], [ng, wo], xs.shape[0])
    keep = D_CONV - 1
    xbc_p = yp.reshape(Bn, S, D_NPAD)[:, :, D_INNER:D_INNER + D_XBC]
    xbc_s = ys.reshape(Bd, T, D_NPAD)[:, :, D_INNER:D_INNER + D_XBC]
    cp = jnp.concatenate([c0p, xbc_p], axis=1)[:, -keep:] if S < keep else xbc_p[:, S - keep:]
    cs = jnp.concatenate([conv_state, xbc_s], axis=1)[:, -keep:]
    return (xp, xs, hp.reshape(Bn, D_HEADS, D_HEADDIM, D_STATE), hs.reshape(Bd, D_HEADS, D_HEADDIM, D_STATE),
            cp, cs)


def kernel(x_prompt, x_sample, cache_a_w1, cache_a_w2, cache_a_w3, cache_b_kv, page_table, state_c, state_d_ssm, state_d_conv, norm_gain, w_ffn_up, w_ffn_down, w_a_in, a_qk_gain, w_a_out, w_b_in, b_qk_gain, w_b_out, w_c_in, w_c_gate2, b_c_gate, c_norm_gain, w_c_out, w_d_in, d_conv_w, d_conv_b, d_dt_bias, d_a_log, d_skip, d_norm_gain, w_d_out):
    Bn, S, _ = x_prompt.shape
    Bd, T, _ = x_sample.shape
    depth = norm_gain.shape[0]
    past_len = page_table.shape[1] * cache_b_kv.shape[2]
    tabs_p = _rope_tables(jnp.arange(S, dtype=jnp.int32))
    tabs_s = _rope_tables(jnp.tile(past_len + jnp.arange(T, dtype=jnp.int32), Bd))
    xp = x_prompt.reshape(Bn * S, D_MODEL)
    xs = x_sample.reshape(Bd * T, D_MODEL)
    ts = xs.shape[0]
    w_up = w_ffn_up.astype(BF16)
    w_down = w_ffn_down.astype(BF16)
    outs = {k: [] for k in ("a0p", "a0s", "a1p", "a1s", "a2p", "a2s", "bp", "bs", "cp", "cs",
                            "hp", "hs", "dp", "ds")}
    for i in range(depth):
        m, j = i % 4, i // 4
        g = norm_gain[i]
        xp = _ffn(xp, g[0], w_up[i, 0], w_down[i, 0], 1024)
        xs = _ffn(xs, g[0], w_up[i, 0], w_down[i, 0], ts)
        if m == 0:
            xp, xs, new_p, new_s = _mixer_a(xp, xs, g[1], w_a_in[j], a_qk_gain[j], w_a_out[j],
                                            (cache_a_w1[j], cache_a_w2[j], cache_a_w3[j]),
                                            tabs_p, tabs_s, Bn, S, Bd, T)
            for gi in range(A_GROUPS):
                outs["a%dp" % gi].append(new_p[gi])
                outs["a%ds" % gi].append(new_s[gi])
        elif m == 1:
            xp, xs, kvp, kvs = _mixer_b(xp, xs, g[1], w_b_in[j], b_qk_gain[j], w_b_out[j],
                                        cache_b_kv[j], page_table, tabs_p, tabs_s, Bn, S, Bd, T)
            outs["bp"].append(kvp)
            outs["bs"].append(kvs)
        elif m == 2:
            xp, xs, sp, ss = _mixer_c(xp, xs, g[1], w_c_in[j], w_c_gate2[j], b_c_gate[j], c_norm_gain[j],
                                      w_c_out[j], state_c[j], Bn, S, Bd, T)
            outs["cp"].append(sp)
            outs["cs"].append(ss)
        else:
            xp, xs, hp, hs, cp, cs = _mixer_d(xp, xs, g[1], w_d_in[j], d_conv_w[j], d_conv_b[j], d_dt_bias[j],
                                              d_a_log[j], d_skip[j], d_norm_gain[j], w_d_out[j],
                                              state_d_ssm[j], state_d_conv[j], Bn, S, Bd, T)
            outs["hp"].append(hp)
            outs["hs"].append(hs)
            outs["dp"].append(cp)
            outs["ds"].append(cs)
        xp = _ffn(xp, g[2], w_up[i, 1], w_down[i, 1], 1024)
        xs = _ffn(xs, g[2], w_up[i, 1], w_down[i, 1], ts)
    st = {k: jnp.stack(v) for k, v in outs.items()}
    return (xp.reshape(Bn, S, D_MODEL), xs.reshape(Bd, T, D_MODEL),
            st["a0p"], st["a0s"], st["a1p"], st["a1s"], st["a2p"], st["a2s"],
            st["bp"], st["bs"], st["cp"], st["cs"], st["hp"], st["hs"], st["dp"], st["ds"])
```

```python
import functools
import math

import numpy as np
import jax
import jax.numpy as jnp
from jax import lax
from jax.experimental import pallas as pl
from jax.experimental.pallas import tpu as pltpu

F32 = jnp.float32
BF16 = jnp.bfloat16

D_MODEL = 1024
HEAD_DIM = 64
ROPE_DIM = HEAD_DIM // 4
ROPE_THETA = 500000.0
EPS = 1e-6
D_FF = 2816
NEG = -1e30

A_GROUPS = 3
A_DILATIONS = (1, 4, 16)
A_HEADS = 8
A_WIDTH = A_HEADS * HEAD_DIM
A_SPAN = 128

B_HEADS = 16
B_KV_HEADS = 4
B_REP = B_HEADS // B_KV_HEADS
B_BLOCK = 256
B_TOPK = 3
B_VT_ROWS = HEAD_DIM + 8
LOG2E = 1.4426950408889634

C_HEADS = 4
C_KEY = 512
C_VAL = 1024
C_DK = 128
C_DV = 256
C_RANK = 16
C_TAU = 16.0
C_CHUNK = 32
C_NPAD = 3200

D_INNER = 2048
D_HEADDIM = 64
D_HEADS = 32
D_GROUPS = 4
D_HPG = 8
D_STATE = 128
D_CONV = 4
D_XBC = D_INNER + 2 * D_GROUPS * D_STATE
D_NPAD = 5376
D_DT_COL = (D_INNER + D_XBC) // 128

LANES = 128
FFN_ROWS = 256
PROJ_ROWS = 256
SEQ_TILE = 128


def _params(*sem):
    return pltpu.CompilerParams(dimension_semantics=sem)


def _split2(a):
    hi = a.astype(BF16)
    lo = (a - hi.astype(F32)).astype(BF16)
    return hi, lo


def _split3(a):
    hi = a.astype(BF16)
    r = a - hi.astype(F32)
    mid = r.astype(BF16)
    lo = (r - mid.astype(F32)).astype(BF16)
    return hi, mid, lo


def _dot(a, b):
    return jnp.dot(a, b, preferred_element_type=F32)


def _dot_nt(a, b):
    return lax.dot_general(a, b, (((1,), (1,)), ((), ())), preferred_element_type=F32)


def _sel_right(a, sel01, terms=2):
    parts = _split2(a) if terms == 2 else _split3(a)
    out = None
    for p in reversed(parts):
        d = _dot(p, sel01)
        out = d if out is None else out + d
    return out


def _sel_left(sel01, a, terms=2):
    parts = _split2(a) if terms == 2 else _split3(a)
    out = None
    for p in reversed(parts):
        d = _dot(sel01, p)
        out = d if out is None else out + d
    return out


def _silu(x):
    return x * jax.nn.sigmoid(x)


def _softplus(x):
    return jnp.maximum(x, 0.0) + jnp.log(1.0 + jnp.exp(-jnp.abs(x)))


def _rms_rows(x, g):
    ms = jnp.mean(x * x, axis=-1, keepdims=True)
    return x * lax.rsqrt(ms + EPS) * g


def _ffn_kernel(x_ref, g_ref, wg_ref, wu_ref, wd_ref, o_ref):
    x = x_ref[...]
    h = _rms_rows(x, g_ref[...]).astype(BF16)
    a = _dot(h, wg_ref[...])
    u = _dot(h, wu_ref[...])
    act = (_silu(a) * u).astype(BF16)
    o_ref[...] = x + 0.5 * _dot(act, wd_ref[...])


def _ffn(x, gain, w_up, w_down, layer, which, tm):
    T = x.shape[0]
    tm = min(tm, FFN_ROWS)
    resident = dict(pipeline_mode=pl.Buffered(1))
    return pl.pallas_call(
        _ffn_kernel,
        grid=(T // tm,),
        in_specs=[
            pl.BlockSpec((tm, D_MODEL), lambda i: (i, 0)),
            pl.BlockSpec((1, D_MODEL), lambda i: (0, 0), **resident),
            pl.BlockSpec((None, None, D_MODEL, D_FF), lambda i: (layer, which, 0, 0), **resident),
            pl.BlockSpec((None, None, D_MODEL, D_FF), lambda i: (layer, which, 0, 1), **resident),
            pl.BlockSpec((None, None, D_FF, D_MODEL), lambda i: (layer, which, 0, 0), **resident),
        ],
        out_specs=pl.BlockSpec((tm, D_MODEL), lambda i: (i, 0)),
        out_shape=jax.ShapeDtypeStruct((T, D_MODEL), F32),
        compiler_params=_params("parallel"),
        name="ffn",
    )(x, gain.reshape(1, D_MODEL), w_up, w_up, w_down)


def _proj_kernel(x_ref, g_ref, w_ref, o_ref):
    o_ref[...] = _dot(_rms_rows(x_ref[...], g_ref[...]).astype(BF16), w_ref[...])


def _proj(x, gain, w, tm):
    T = x.shape[0]
    N = w.shape[1]
    tm = min(tm, PROJ_ROWS)
    resident = dict(pipeline_mode=pl.Buffered(1))
    return pl.pallas_call(
        _proj_kernel,
        grid=(T // tm,),
        in_specs=[
            pl.BlockSpec((tm, D_MODEL), lambda i: (i, 0)),
            pl.BlockSpec((1, D_MODEL), lambda i: (0, 0), **resident),
            pl.BlockSpec((D_MODEL, N), lambda i: (0, 0), **resident),
        ],
        out_specs=pl.BlockSpec((tm, N), lambda i: (i, 0)),
        out_shape=jax.ShapeDtypeStruct((T, N), F32),
        compiler_params=_params("parallel"),
        name="proj",
    )(x, gain.reshape(1, D_MODEL), w)


def _proj_qk_kernel(flags, x_ref, g_ref, w_ref, bd_ref, gain_ref, mask_ref, cos_ref, sa_ref, sb_ref, o_ref):
    h = _rms_rows(x_ref[...], g_ref[...]).astype(BF16)
    tn = bd_ref.shape[0]
    cos, sa, sb = cos_ref[...], sa_ref[...], sb_ref[...]
    for j, flag in enumerate(flags):
        y = _dot(h, w_ref[:, j * tn:(j + 1) * tn])
        if not flag:
            o_ref[:, j * tn:(j + 1) * tn] = y
            continue
        ss = _dot((y * y).astype(BF16), bd_ref[...])
        yn = y * lax.rsqrt(ss * (1.0 / HEAD_DIM) + EPS) * gain_ref[j]
        mask = mask_ref[j]
        for c in range(tn // LANES):
            sl = slice(c * LANES, (c + 1) * LANES)
            v = yn[:, sl]
            up = pltpu.roll(v, LANES - ROPE_DIM // 2, 1)
            dn = pltpu.roll(v, ROPE_DIM // 2, 1)
            rot = v * cos + up * sa + dn * sb
            o_ref[:, j * tn + c * LANES:j * tn + (c + 1) * LANES] = jnp.where(mask[:, sl] > 0.0, rot, y[:, sl])


def _proj_qk(x, gain, w, flags, gain_rows, mask_rows, tables, tm, tn):
    T = x.shape[0]
    N = w.shape[1]
    nj = N // tn
    tm = min(tm, PROJ_ROWS)
    cos, sa, sb = tables
    n_pos_tiles = cos.shape[0] // tm
    bd = jnp.asarray(np.kron(np.eye(tn // HEAD_DIM), np.ones((HEAD_DIM, HEAD_DIM))), BF16)
    resident = dict(pipeline_mode=pl.Buffered(1))
    tab_spec = pl.BlockSpec((tm, LANES), lambda i: (i % n_pos_tiles, 0))
    return pl.pallas_call(
        functools.partial(_proj_qk_kernel, flags),
        grid=(T // tm,),
        in_specs=[
            pl.BlockSpec((tm, D_MODEL), lambda i: (i, 0)),
            pl.BlockSpec((1, D_MODEL), lambda i: (0, 0), **resident),
            pl.BlockSpec((D_MODEL, N), lambda i: (0, 0), **resident),
            pl.BlockSpec((tn, tn), lambda i: (0, 0), **resident),
            pl.BlockSpec((nj, 1, tn), lambda i: (0, 0, 0), **resident),
            pl.BlockSpec((nj, 1, tn), lambda i: (0, 0, 0), **resident),
            tab_spec, tab_spec, tab_spec,
        ],
        out_specs=pl.BlockSpec((tm, N), lambda i: (i, 0)),
        out_shape=jax.ShapeDtypeStruct((T, N), F32),
        compiler_params=_params("parallel"),
        name="proj_qk",
    )(x, gain.reshape(1, D_MODEL), w, bd, gain_rows, mask_rows, cos, sa, sb)


def _rope_tables(pos):
    half = ROPE_DIM // 2
    inv = ROPE_THETA ** (-jnp.arange(half, dtype=F32) / half)
    ang = pos.astype(F32)[:, None] * inv[None, :]
    cos, sin = jnp.cos(ang), jnp.sin(ang)
    n = pos.shape[0]
    one = jnp.ones((n, HEAD_DIM - ROPE_DIM), F32)
    zero = jnp.zeros((n, HEAD_DIM - ROPE_DIM), F32)
    zh = jnp.zeros((n, half), F32)
    c64 = jnp.concatenate([cos, cos, one], axis=1)
    sa64 = jnp.concatenate([-sin, zh, zero], axis=1)
    sb64 = jnp.concatenate([zh, sin, zero], axis=1)
    rep = LANES // HEAD_DIM
    return tuple(jnp.tile(t, (1, rep)) for t in (c64, sa64, sb64))


def _out_plain_kernel(x_ref, o_ref, w_ref, y_ref):
    y_ref[...] = x_ref[...] + _dot(o_ref[...].astype(BF16), w_ref[...])


def _out_a_kernel(x_ref, o0, o1, o2, l0, l1, l2, w_ref, y_ref):
    la, lb, lc = l0[...], l1[...], l2[...]
    m = jnp.maximum(jnp.maximum(la, lb), lc)
    ea, eb, ec = jnp.exp(la - m), jnp.exp(lb - m), jnp.exp(lc - m)
    o = (ea * o0[...] + eb * o1[...] + ec * o2[...]) / (ea + eb + ec)
    y_ref[...] = x_ref[...] + _dot(o.astype(BF16), w_ref[...])


def _out_c_kernel(x_ref, o_ref, r_ref, g_ref, w_ref, y_ref):
    o = o_ref[...]
    g = g_ref[...]
    parts = [_rms_rows(o[:, h * C_DV:(h + 1) * C_DV], g) for h in range(C_HEADS)]
    on = jnp.concatenate(parts, axis=1) * _silu(r_ref[...])
    y_ref[...] = x_ref[...] + _dot(on.astype(BF16), w_ref[...])


def _out_d_kernel(x_ref, o_ref, z_ref, g_ref, w_ref, y_ref):
    gated = o_ref[...] * _silu(z_ref[...])
    gw = D_INNER // D_GROUPS
    g = g_ref[...]
    parts = [_rms_rows(gated[:, k * gw:(k + 1) * gw], g[:, k * gw:(k + 1) * gw]) for k in range(D_GROUPS)]
    y_ref[...] = x_ref[...] + _dot(jnp.concatenate(parts, axis=1).astype(BF16), w_ref[...])


def _row_call(kernel, name, x, row_inputs, const_inputs, tm):
    T = x.shape[0]
    in_specs = [pl.BlockSpec((tm, D_MODEL), lambda i: (i, 0))]
    args = [x]
    for arr, width, cb in row_inputs:
        in_specs.append(pl.BlockSpec((tm, width), lambda i, cb=cb: (i, cb)))
        args.append(arr)
    for arr in const_inputs:
        in_specs.append(pl.BlockSpec(arr.shape, lambda i, nd=arr.ndim: (0,) * nd))
        args.append(arr)
    return pl.pallas_call(
        kernel,
        grid=(T // tm,),
        in_specs=in_specs,
        out_specs=pl.BlockSpec((tm, D_MODEL), lambda i: (i, 0)),
        out_shape=jax.ShapeDtypeStruct((T, D_MODEL), F32),
        compiler_params=_params("parallel"),
        name=name,
    )(*args)


def _a_prompt_kernel(d, q_ref, kp_ref, kc_ref, vp_ref, vc_ref, o_ref, l_ref):
    j = pl.program_id(1)
    heads = q_ref.shape[2] // HEAD_DIM
    per_pass = max(1, A_HEADS // heads)
    n_prob = per_pass * heads
    scale = HEAD_DIM ** -0.5
    qq = lax.broadcasted_iota(jnp.int32, (n_prob * A_SPAN, 2 * A_SPAN), 0) & (A_SPAN - 1)
    kk = lax.broadcasted_iota(jnp.int32, (n_prob * A_SPAN, 2 * A_SPAN), 1)
    rel = qq + A_SPAN - kk
    valid = (rel >= 0) & (rel <= A_SPAN) & ((kk >= A_SPAN) | (j > 0))
    hs = [slice(h * HEAD_DIM, (h + 1) * HEAD_DIM) for h in range(heads)]
    for r0 in range(0, d, per_pass):
        rows = [pl.ds(r0 + e, A_SPAN, stride=d) if d > 1 else pl.ds(0, A_SPAN) for e in range(per_pass)]
        s, vs = [], []
        for rw in rows:
            q = (q_ref[0, rw, :] * scale).astype(BF16)
            k = jnp.concatenate([kp_ref[0, rw, :], kc_ref[0, rw, :]], axis=0).astype(BF16)
            vs.append(jnp.concatenate([vp_ref[0, rw, :], vc_ref[0, rw, :]], axis=0).astype(BF16))
            s += [_dot_nt(q[:, sl], k[:, sl]) for sl in hs]
        s = jnp.where(valid, jnp.concatenate(s, axis=0), -jnp.inf)
        m = jnp.max(s, axis=-1, keepdims=True)
        p = jnp.exp(s - m)
        den = jnp.sum(p, axis=-1, keepdims=True)
        pb = p.astype(BF16)
        lse = m + jnp.log(den)
        inv = 1.0 / den
        for e, rw in enumerate(rows):
            o, l = [], []
            for h, sl in enumerate(hs):
                blk = slice((e * heads + h) * A_SPAN, (e * heads + h + 1) * A_SPAN)
                o.append(_dot(pb[blk], vs[e][:, sl]) * inv[blk])
                l.append(jnp.broadcast_to(lse[blk], (A_SPAN, HEAD_DIM)))
            o_ref[0, rw, :] = jnp.concatenate(o, axis=1)
            l_ref[0, rw, :] = jnp.concatenate(l, axis=1)


def _a_prompt(y, g, Bn, S):
    d = A_DILATIONS[g]
    slab = A_SPAN * d
    heads = A_HEADS if d == 1 else LANES // HEAD_DIM
    cols = heads * HEAD_DIM
    ncb = A_WIDTH // cols
    y3 = y.reshape(Bn, S, A_GROUPS * 3 * A_WIDTH)

    def spec(which, prev):
        c0 = (3 * g + which) * ncb
        if prev:
            return pl.BlockSpec((1, slab, cols), lambda b, j, c: (b, jnp.maximum(j - 1, 0), c0 + c))
        return pl.BlockSpec((1, slab, cols), lambda b, j, c: (b, j, c0 + c))

    out_spec = pl.BlockSpec((1, slab, cols), lambda b, j, c: (b, j, c))
    o, l = pl.pallas_call(
        functools.partial(_a_prompt_kernel, d),
        grid=(Bn, S // slab, ncb),
        in_specs=[spec(0, False), spec(1, True), spec(1, False), spec(2, True), spec(2, False)],
        out_specs=[out_spec, out_spec],
        out_shape=[jax.ShapeDtypeStruct((Bn, S, A_WIDTH), F32)] * 2,
        compiler_params=_params("parallel", "arbitrary", "arbitrary"),
        name="a_prompt_d%d" % d,
    )(y3, y3, y3, y3, y3)
    return o.reshape(Bn * S, A_WIDTH), l.reshape(Bn * S, A_WIDTH)


def _nt_sel(mask, a):
    out = None
    for p in reversed(_split3(a)):
        d = _dot_nt(mask, p)
        out = d if out is None else out + d
    return out


def _a_sample_kernel(T, y_ref, c1_ref, c2_ref, c3_ref, hind_ref, hindt_ref, *out_refs):
    caches = (c1_ref, c2_ref, c3_ref)
    width = 3 * A_WIDTH
    scale = HEAD_DIM ** -0.5
    hind = hind_ref[...]
    hindt = hindt_ref[...]
    lane = lax.broadcasted_iota(jnp.int32, (8, LANES), 1)
    sub = lax.broadcasted_iota(jnp.int32, (8, LANES), 0)
    zrows = jnp.zeros((LANES - T, A_WIDTH), F32)
    zsq = jnp.zeros((LANES - 8, LANES), F32)

    def head_rows(prod):
        return _nt_sel(hind, jnp.concatenate([prod, zrows], axis=0))

    def sublane_total(x):
        x = x + pltpu.roll(x, 4, 0)
        x = x + pltpu.roll(x, 2, 0)
        return x + pltpu.roll(x, 1, 0)

    def head_scores(k_of, pat_of):
        out = jnp.zeros((8, LANES), F32)
        for h in range(A_HEADS):
            prod = k_of(h) * pat_of(h)
            part = prod[0:8]
            for j in range(1, HEAD_DIM // 8):
                part = part + prod[8 * j:8 * j + 8]
            out = jnp.where(sub == h, sublane_total(part), out)
        return out

    def rows_of(packed):
        return jnp.concatenate([packed, zsq], axis=0).T

    def widen(rows8):
        return _sel_right(rows8, hindt, terms=3)

    def place(col, at):
        return jnp.where(lane == at, col, 0.0)

    def patterns(q_t, pick):
        return _sel_right(q_t, pick.astype(BF16), terms=3)

    row_sq = lax.broadcasted_iota(jnp.int32, (LANES, LANES), 0)
    lane_sq = lax.broadcasted_iota(jnp.int32, (LANES, LANES), 1)

    for g in range(A_GROUPS):
        d = A_DILATIONS[g]
        c_ref = caches[g]
        n_tiles = c_ref.shape[-1] // LANES
        q_nat = y_ref[0, :, g * width:g * width + A_WIDTH] * scale
        kn_nat = y_ref[0, :, g * width + A_WIDTH:g * width + 2 * A_WIDTH]
        vn_nat = y_ref[0, :, g * width + 2 * A_WIDTH:(g + 1) * width]
        q_t = jnp.concatenate([q_nat, zrows], axis=0).T
        if d == 1:
            packed = jnp.zeros((8, LANES), F32)
            res = []
            for i in range(T):
                pat = patterns(q_t, row_sq == i)
                s = head_scores(lambda h: c_ref[0, 0, h], lambda h: pat[h * HEAD_DIM:(h + 1) * HEAD_DIM])
                s = jnp.where(lane >= i, s, -jnp.inf)
                sn = jnp.where(lane <= i, head_rows(q_nat[i:i + 1, :] * kn_nat), -jnp.inf)
                m = jnp.maximum(jnp.max(s, axis=1, keepdims=True), jnp.max(sn, axis=1, keepdims=True))
                p = jnp.exp(s - m)
                pn = jnp.exp(sn - m)
                den = jnp.sum(p, axis=1, keepdims=True) + jnp.sum(pn, axis=1, keepdims=True)
                acc = jnp.concatenate([c_ref[0, 1, h] * p[h:h + 1, :] for h in range(A_HEADS)], axis=0)
                res.append(_nt_sel((sub == 0).astype(BF16), acc)[0:1, :])
                packed = packed + place(den, i) + place(m + jnp.log(den), 8 + i)
                packed = packed + pltpu.roll(pn, 16 + 8 * i, 1)
            sq = rows_of(packed)
            denx = widen(sq[0:8])
            lsex = widen(sq[8:16])
            for i in range(T):
                pnx = widen(sq[16 + 8 * i:24 + 8 * i])
                o = res[i] + jnp.sum(pnx[0:T] * vn_nat, axis=0, keepdims=True)
                out_refs[g][0, i:i + 1, :] = o / denx[i:i + 1, :]
            out_refs[A_GROUPS + g][0] = lsex[0:T]
        else:
            cls = lane & (d - 1)
            self_s = head_rows(q_nat * kn_nat)
            pat = patterns(q_t, (row_sq == (lane_sq & (d - 1))) & (row_sq < T))
            s_tiles = [head_scores(lambda h: c_ref[0, 0, h, :, t * LANES:(t + 1) * LANES],
                                   lambda h: pat[h * HEAD_DIM:(h + 1) * HEAD_DIM])
                       for t in range(n_tiles)]
            smax = s_tiles[0]
            for t in range(1, n_tiles):
                smax = jnp.maximum(smax, s_tiles[t])
            mrow = jnp.full((8, LANES), jnp.inf, F32)
            m_cls = []
            for i in range(T):
                mi = jnp.maximum(jnp.max(jnp.where(cls == i, smax, -jnp.inf), axis=1, keepdims=True),
                                 self_s[:, i:i + 1])
                m_cls.append(mi)
                mrow = jnp.where(cls == i, mi, mrow)
            p_tiles = [jnp.exp(s_tiles[t] - mrow) for t in range(n_tiles)]
            psum = p_tiles[0]
            for t in range(1, n_tiles):
                psum = psum + p_tiles[t]
            accs = []
            for h in range(A_HEADS):
                acc = jnp.zeros((HEAD_DIM, LANES), F32)
                for t in range(n_tiles):
                    acc = acc + c_ref[0, 1, h, :, t * LANES:(t + 1) * LANES] * p_tiles[t][h:h + 1, :]
                accs.append(acc)
            res = _nt_sel((cls == sub).astype(BF16), jnp.concatenate(accs, axis=0))
            packed = jnp.zeros((8, LANES), F32)
            for i in range(T):
                ps = jnp.exp(self_s[:, i:i + 1] - m_cls[i])
                den = jnp.sum(jnp.where(cls == i, psum, 0.0), axis=1, keepdims=True) + ps
                packed = packed + place(ps, i) + place(den, 8 + i) + place(m_cls[i] + jnp.log(den), 16 + i)
            sq = rows_of(packed)
            psx, denx, lsex = widen(sq[0:8]), widen(sq[8:16]), widen(sq[16:24])
            out_refs[g][0] = (res[0:T] + psx[0:T] * vn_nat) / denx[0:T]
            out_refs[A_GROUPS + g][0] = lsex[0:T]


def _a_sample(y, caches, Bd, T):
    width = A_GROUPS * 3 * A_WIDTH
    y3 = y.reshape(Bd, T, width)
    views = []
    specs = [pl.BlockSpec((1, T, width), lambda b: (b, 0, 0))]
    for g, c in enumerate(caches):
        d = A_DILATIONS[g]
        assert c.shape[1] == A_SPAN * d and (d == 1 or T <= d) and T <= 8
        views.append(c.transpose(0, 2, 3, 4, 1))
        specs.append(pl.BlockSpec((1, 2, A_HEADS, HEAD_DIM, A_SPAN * d), lambda b: (b, 0, 0, 0, 0)))
    hind_np = np.zeros((LANES, A_WIDTH), np.float32)
    hind_np[np.arange(A_WIDTH) // HEAD_DIM, np.arange(A_WIDTH)] = 1.0
    hind = jnp.asarray(hind_np[:A_HEADS], BF16)
    hindt = jnp.asarray(hind_np, BF16)
    specs += [pl.BlockSpec(hind.shape, lambda b: (0, 0)), pl.BlockSpec(hindt.shape, lambda b: (0, 0))]
    out_spec = pl.BlockSpec((1, T, A_WIDTH), lambda b: (b, 0, 0))
    outs = pl.pallas_call(
        functools.partial(_a_sample_kernel, T),
        grid=(Bd,),
        in_specs=specs,
        out_specs=[out_spec] * (2 * A_GROUPS),
        out_shape=[jax.ShapeDtypeStruct((Bd, T, A_WIDTH), F32)] * (2 * A_GROUPS),
        compiler_params=_params("parallel"),
        name="a_sample",
    )(y3, *views, hind, hindt)
    return [o.reshape(Bd * T, A_WIDTH) for o in outs]


def _mixer_a(xp, xs, gain, w_in, qk_gain, w_out, caches, tabs_p, tabs_s, Bn, S, Bd, T):
    flags = (True, True, False) * A_GROUPS
    ones = jnp.ones((A_WIDTH,), F32)
    rows = [jnp.tile(qk_gain[0], A_HEADS), jnp.tile(qk_gain[1], A_HEADS), ones] * A_GROUPS
    gain_rows = jnp.stack(rows).reshape(3 * A_GROUPS, 1, A_WIDTH)
    mask_rows = jnp.stack([ones, ones, 0.0 * ones] * A_GROUPS).reshape(3 * A_GROUPS, 1, A_WIDTH)
    w = w_in.astype(BF16)
    yp = _proj_qk(xp, gain, w, flags, gain_rows, mask_rows, tabs_p, 1024, A_WIDTH)
    ys = _proj_qk(xs, gain, w, flags, gain_rows, mask_rows, tabs_s, xs.shape[0], A_WIDTH)
    wo = w_out.astype(BF16)
    pr = [_a_prompt(yp, g, Bn, S) for g in range(A_GROUPS)]
    xp = _row_call(_out_a_kernel, "out_a", xp,
                   [(o, A_WIDTH, 0) for o, _ in pr] + [(l, A_WIDTH, 0) for _, l in pr], [wo], 512)
    sr = _a_sample(ys, caches, Bd, T)
    xs = _row_call(_out_a_kernel, "out_a", xs, [(o, A_WIDTH, 0) for o in sr], [wo], xs.shape[0])
    yp3 = yp.reshape(Bn, S, A_GROUPS * 3 * A_WIDTH)
    ys3 = ys.reshape(Bd, T, A_GROUPS * 3 * A_WIDTH)
    new_p, new_s = [], []
    for g in range(A_GROUPS):
        win = min(A_SPAN * A_DILATIONS[g], S)
        c0 = (3 * g + 1) * A_WIDTH
        new_p.append(yp3[:, S - win:, c0:c0 + 2 * A_WIDTH].reshape(Bn, win, 2, A_HEADS, HEAD_DIM))
        new_s.append(ys3[:, :, c0:c0 + 2 * A_WIDTH].reshape(Bd, T, 2, A_HEADS, HEAD_DIM))
    return xp, xs, new_p, new_s


def _b_kmean_kernel(k_ref, o_ref):
    nblk = k_ref.shape[1] // B_BLOCK
    rows = [jnp.mean(k_ref[0, n * B_BLOCK:(n + 1) * B_BLOCK, :], axis=0, keepdims=True) for n in range(nblk)]
    o_ref[0] = jnp.concatenate(rows, axis=0)


def _b_gate_kernel(q_ref, kmh_ref, kml_ref, o_ref):
    i = pl.program_id(1)
    qh, ql = _split2(q_ref[0])
    kmh, kml = kmh_ref[0], kml_ref[0]
    gate = _dot_nt(kmh, qh) + (_dot_nt(kmh, ql) + _dot_nt(kml, qh))
    tq = gate.shape[1]
    nblk = gate.shape[0] // B_HEADS
    gate = gate.reshape(B_HEADS, nblk, tq)
    blk = lax.broadcasted_iota(jnp.int32, (B_HEADS, nblk, tq), 1)
    blkf = blk.astype(F32)
    gate = jnp.where(blk < i, gate, -jnp.inf)
    sel = jnp.zeros((B_HEADS, nblk, tq), jnp.bool_)
    for _ in range(B_TOPK):
        mx = jnp.max(gate, axis=1, keepdims=True)
        first = jnp.min(jnp.where(gate == mx, blkf, float(nblk)), axis=1, keepdims=True)
        hit = blkf == first
        sel = sel | (hit & (mx > -jnp.inf))
        gate = jnp.where(hit, -jnp.inf, gate)
    bias = jnp.where(sel | (blk >= i), 0.0, NEG).reshape(B_HEADS * nblk, tq)
    o_ref[0] = bias.T


def _b_attn_kernel(qi_ref, kn_ref, q_ref, k_ref, vt_ref, o_ref, *state):
    t = pl.program_id(1)
    i, n = qi_ref[t], kn_ref[t]
    tq = B_BLOCK
    wide = B_REP * tq
    m_refs, acc_refs = state[:B_KV_HEADS], state[B_KV_HEADS:]

    @pl.when(n == 0)
    def _():
        for kvh in range(B_KV_HEADS):
            m_refs[kvh][...] = jnp.full_like(m_refs[kvh], -jnp.inf)
            acc_refs[kvh][...] = jnp.zeros_like(acc_refs[kvh])

    def sweep(causal):
        if causal:
            kk = lax.broadcasted_iota(jnp.int32, (B_BLOCK, wide), 0)
            qq = lax.broadcasted_iota(jnp.int32, (B_BLOCK, wide), 1) & (tq - 1)
            keep = kk <= qq
        scores = []
        for kvh in range(B_KV_HEADS):
            k = k_ref[0, :, kvh * LANES:(kvh + 1) * LANES]
            s = _dot_nt(k, q_ref[0, 0, kvh])
            scores.append(jnp.where(keep, s, NEG) if causal else s)
        probs, scales = [], []
        for kvh in range(B_KV_HEADS):
            m_old = m_refs[kvh][...]
            m_new = jnp.maximum(m_old, jnp.max(scores[kvh], axis=0, keepdims=True))
            probs.append(jnp.exp2(scores[kvh] - m_new).astype(BF16))
            scales.append(jnp.exp2(m_old - m_new))
            m_refs[kvh][...] = m_new
        for kvh in range(B_KV_HEADS):
            acc_refs[kvh][...] = scales[kvh] * acc_refs[kvh][...] + _dot(vt_ref[0, kvh], probs[kvh])

    @pl.when(n < i)
    def _():
        sweep(False)

    @pl.when(n == i)
    def _():
        sweep(True)
        for kvh in range(B_KV_HEADS):
            acc = acc_refs[kvh][...]
            ot = acc[0:HEAD_DIM] / acc[HEAD_DIM:HEAD_DIM + 1]
            for r in range(0, B_REP, 2):
                pair = [ot[:, (r + e) * tq:(r + e + 1) * tq].T for e in range(2)]
                c0 = (kvh * B_REP + r) * HEAD_DIM
                o_ref[0, :, c0:c0 + 2 * HEAD_DIM] = jnp.concatenate(pair, axis=1)


def _b_prompt(y, Bn, S):
    nblk = S // B_BLOCK
    assert nblk % 8 == 0 and LANES % nblk == 0
    nq = B_HEADS * HEAD_DIM
    nk = B_KV_HEADS * HEAD_DIM
    y3 = y.reshape(Bn, S, nq + 2 * nk)
    kmean = pl.pallas_call(
        _b_kmean_kernel,
        grid=(Bn,),
        in_specs=[pl.BlockSpec((1, S, nk), lambda b: (b, 0, nq // nk))],
        out_specs=pl.BlockSpec((1, nblk, nk), lambda b: (b, 0, 0)),
        out_shape=jax.ShapeDtypeStruct((Bn, nblk, nk), F32),
        compiler_params=_params("parallel"),
        name="b_kmean",
    )(y3)
    km = kmean.reshape(Bn, nblk, B_KV_HEADS, HEAD_DIM)
    km = jnp.repeat(km, B_REP, axis=2)
    eye = jnp.eye(B_HEADS, dtype=F32)
    kmbd = jnp.einsum('bnhe,hg->bgnhe', km, eye).reshape(Bn, B_HEADS * nblk, nq)
    kmh = kmbd.astype(BF16)
    kml = (kmbd - kmh.astype(F32)).astype(BF16)
    bias = pl.pallas_call(
        _b_gate_kernel,
        grid=(Bn, nblk),
        in_specs=[
            pl.BlockSpec((1, B_BLOCK, nq), lambda b, i: (b, i, 0)),
            pl.BlockSpec((1, B_HEADS * nblk, nq), lambda b, i: (b, 0, 0)),
            pl.BlockSpec((1, B_HEADS * nblk, nq), lambda b, i: (b, 0, 0)),
        ],
        out_specs=pl.BlockSpec((1, B_BLOCK, B_HEADS * nblk), lambda b, i: (b, i, 0)),
        out_shape=jax.ShapeDtypeStruct((Bn, S, B_HEADS * nblk), F32),
        compiler_params=_params("parallel", "arbitrary"),
        name="b_gate",
    )(y3, kmh, kml)
    q = (y3[:, :, :nq] * (HEAD_DIM ** -0.5 * LOG2E)).astype(BF16).reshape(Bn, S, B_HEADS, HEAD_DIM)
    pad = LANES - HEAD_DIM - nblk
    q_aug = jnp.concatenate([q, bias.astype(BF16).reshape(Bn, S, B_HEADS, nblk),
                             jnp.zeros((Bn, S, B_HEADS, pad), BF16)], axis=-1)
    q_aug = q_aug.reshape(Bn, nblk, B_BLOCK, B_KV_HEADS, B_REP, LANES).transpose(0, 1, 3, 4, 2, 5)
    q_aug = q_aug.reshape(Bn, nblk, B_KV_HEADS, B_REP * B_BLOCK, LANES)
    k = y3[:, :, nq:nq + nk].astype(BF16).reshape(Bn, S, B_KV_HEADS, HEAD_DIM)
    onehot = jax.nn.one_hot(jnp.arange(S) // B_BLOCK, nblk, dtype=BF16)
    onehot = jnp.broadcast_to(onehot[None, :, None, :], (Bn, S, B_KV_HEADS, nblk))
    k_aug = jnp.concatenate([k, onehot, jnp.zeros((Bn, S, B_KV_HEADS, pad), BF16)], axis=-1)
    k_aug = k_aug.reshape(Bn, S, B_KV_HEADS * LANES)
    vt = y3[:, :, nq + nk:].astype(BF16).reshape(Bn, S, B_KV_HEADS, HEAD_DIM).transpose(0, 2, 3, 1)
    vt = jnp.concatenate([vt, jnp.ones((Bn, B_KV_HEADS, 1, S), BF16),
                          jnp.zeros((Bn, B_KV_HEADS, B_VT_ROWS - HEAD_DIM - 1, S), BF16)], axis=2)
    pairs = [(i, n) for i in range(nblk) for n in range(i + 1)]
    qi = jnp.asarray([p[0] for p in pairs], jnp.int32)
    kn = jnp.asarray([p[1] for p in pairs], jnp.int32)
    o = pl.pallas_call(
        _b_attn_kernel,
        grid_spec=pltpu.PrefetchScalarGridSpec(
            num_scalar_prefetch=2,
            grid=(Bn, len(pairs)),
            in_specs=[
                pl.BlockSpec((1, 1, B_KV_HEADS, B_REP * B_BLOCK, LANES),
                             lambda b, t, qi, kn: (b, qi[t], 0, 0, 0)),
                pl.BlockSpec((1, B_BLOCK, B_KV_HEADS * LANES), lambda b, t, qi, kn: (b, kn[t], 0)),
                pl.BlockSpec((1, B_KV_HEADS, B_VT_ROWS, B_BLOCK), lambda b, t, qi, kn: (b, 0, 0, kn[t])),
            ],
            out_specs=pl.BlockSpec((1, B_BLOCK, nq), lambda b, t, qi, kn: (b, qi[t], 0)),
            scratch_shapes=([pltpu.VMEM((1, B_REP * B_BLOCK), F32)] * B_KV_HEADS
                            + [pltpu.VMEM((B_VT_ROWS, B_REP * B_BLOCK), F32)] * B_KV_HEADS),
        ),
        out_shape=jax.ShapeDtypeStruct((Bn, S, nq), F32),
        compiler_params=_params("parallel", "arbitrary"),
        name="b_attn",
    )(qi, kn, q_aug, k_aug, vt)
    return o.reshape(Bn * S, nq)


def _b_sample_kernel(n_pages, T, pt_ref, q_ref, kn_ref, vn_ref, *refs):
    del pt_ref
    page_refs = refs[:n_pages]
    o_ref = refs[n_pages]
    nk = B_KV_HEADS * HEAD_DIM
    q = q_ref[0]
    qh, ql = _split2(q)
    rows = q.shape[0]
    page_rows = page_refs[0].shape[2]
    pages_per_block = B_BLOCK // page_rows
    n_blocks = n_pages // pages_per_block
    scale = HEAD_DIM ** -0.5
    lane = lax.broadcasted_iota(jnp.int32, (1, LANES), 1)
    lanef = lax.broadcasted_iota(jnp.int32, (rows, LANES), 1).astype(F32)
    scores = []
    km = jnp.zeros((nk, LANES), F32)
    for p in range(n_pages):
        kt = page_refs[p][0, :nk, :]
        scores.append(_dot(qh, kt.astype(BF16)) * scale)
        ksum = jnp.sum(kt, axis=1, keepdims=True) * (1.0 / B_BLOCK)
        km = km + ksum * (lane == p // pages_per_block).astype(F32)
    kmh, kml = _split2(km)
    gate = _dot(qh, kmh) + (_dot(qh, kml) + _dot(ql, kmh))
    gate = jnp.where(lanef < float(n_blocks), gate, -jnp.inf)
    sel = jnp.zeros((rows, LANES), jnp.bool_)
    for _ in range(min(B_TOPK, n_blocks)):
        mx = jnp.max(gate, axis=1, keepdims=True)
        first = jnp.min(jnp.where(gate == mx, lanef, float(LANES)), axis=1, keepdims=True)
        hit = lanef == first
        sel = sel | (hit & (mx > -jnp.inf))
        gate = jnp.where(hit, -jnp.inf, gate)
    self = sel.astype(F32)
    qidx = lax.broadcasted_iota(jnp.int32, (rows, 1), 0) % T
    kn, vn = kn_ref[0], vn_ref[0]
    own = []
    for j in range(T):
        sj = jnp.sum(q * kn[j:j + 1, :], axis=1, keepdims=True) * scale
        own.append(jnp.where(qidx >= j, sj, NEG))
    m = own[0]
    for j in range(1, T):
        m = jnp.maximum(m, own[j])
    for p in range(n_pages):
        b = p // pages_per_block
        scores[p] = jnp.where(self[:, b:b + 1] > 0.0, scores[p], NEG)
        m = jnp.maximum(m, jnp.max(scores[p], axis=-1, keepdims=True))
    den = jnp.zeros((rows, 1), F32)
    acc = jnp.zeros((rows, nk), F32)
    for j in range(T):
        pj = jnp.exp(own[j] - m)
        den = den + pj
        acc = acc + pj * vn[j:j + 1, :]
    for p in range(n_pages):
        pp = jnp.exp(scores[p] - m)
        den = den + jnp.sum(pp, axis=-1, keepdims=True)
        acc = acc + _dot_nt(pp.astype(BF16), page_refs[p][0, nk:, :].astype(BF16))
    acc = acc / den
    kvh = lax.broadcasted_iota(jnp.int32, (rows, 1), 0) // (T * B_REP)
    out = jnp.zeros((rows, HEAD_DIM), F32)
    for h in range(B_KV_HEADS):
        out = out + jnp.where(kvh == h, acc[:, h * HEAD_DIM:(h + 1) * HEAD_DIM], 0.0)
    o_ref[0] = out


def _b_sample(y, pool, page_table, Bd, T):
    nq = B_HEADS * HEAD_DIM
    nk = B_KV_HEADS * HEAD_DIM
    n_pages = page_table.shape[1]
    page_rows = pool.shape[1]
    assert B_BLOCK % page_rows == 0 and (n_pages * page_rows) % B_BLOCK == 0 and T <= B_BLOCK
    assert n_pages * page_rows // B_BLOCK <= LANES
    y3 = y.reshape(Bd, T, nq + 2 * nk)
    q = y3[:, :, :nq].reshape(Bd, T, B_HEADS, HEAD_DIM).transpose(0, 2, 1, 3)
    kvsel = jnp.asarray(np.kron(np.eye(B_KV_HEADS), np.ones((B_REP, 1))), F32)
    qbd = (q[:, :, :, None, :] * kvsel[None, :, None, :, None]).reshape(Bd, B_HEADS * T, nk)
    kn = y3[:, :, nq:nq + nk]
    vn = y3[:, :, nq + nk:]
    pool_t = pool.transpose(0, 2, 3, 4, 1).reshape(pool.shape[0], 2 * nk, page_rows)
    page_specs = [pl.BlockSpec((1, 2 * nk, page_rows), lambda b, pt, p=p: (pt[b, p], 0, 0))
                  for p in range(n_pages)]
    o = pl.pallas_call(
        functools.partial(_b_sample_kernel, n_pages, T),
        grid_spec=pltpu.PrefetchScalarGridSpec(
            num_scalar_prefetch=1,
            grid=(Bd,),
            in_specs=[
                pl.BlockSpec((1, B_HEADS * T, nk), lambda b, pt: (b, 0, 0)),
                pl.BlockSpec((1, T, nk), lambda b, pt: (b, 0, 0)),
                pl.BlockSpec((1, T, nk), lambda b, pt: (b, 0, 0)),
            ] + page_specs,
            out_specs=pl.BlockSpec((1, B_HEADS * T, HEAD_DIM), lambda b, pt: (b, 0, 0)),
        ),
        out_shape=jax.ShapeDtypeStruct((Bd, B_HEADS * T, HEAD_DIM), F32),
        compiler_params=_params("parallel"),
        name="b_sample",
    )(page_table, qbd, kn, vn, *([pool_t] * n_pages))
    return o.reshape(Bd, B_HEADS, T, HEAD_DIM).transpose(0, 2, 1, 3).reshape(Bd * T, nq)


def _mixer_b(xp, xs, gain, w_in, qk_gain, w_out, pool, page_table, tabs_p, tabs_s, Bn, S, Bd, T):
    tn = 512
    nq = B_HEADS * HEAD_DIM
    nk = B_KV_HEADS * HEAD_DIM
    flags = (True, True, True)
    ones = jnp.ones((nk,), F32)
    qg = jnp.tile(qk_gain[0], tn // HEAD_DIM)
    kg = jnp.concatenate([jnp.tile(qk_gain[1], B_KV_HEADS), ones])
    gain_rows = jnp.stack([qg, qg, kg]).reshape(3, 1, tn)
    mask_rows = jnp.stack([jnp.ones((tn,), F32), jnp.ones((tn,), F32),
                           jnp.concatenate([ones, 0.0 * ones])]).reshape(3, 1, tn)
    w = w_in.astype(BF16)
    yp = _proj_qk(xp, gain, w, flags, gain_rows, mask_rows, tabs_p, 1024, tn)
    ys = _proj_qk(xs, gain, w, flags, gain_rows, mask_rows, tabs_s, xs.shape[0], tn)
    wo = w_out.astype(BF16)
    op = _b_prompt(yp, Bn, S)
    xp = _row_call(_out_plain_kernel, "out_b", xp, [(op, nq, 0)], [wo], 512)
    osm = _b_sample(ys, pool, page_table, Bd, T)
    xs = _row_call(_out_plain_kernel, "out_b", xs, [(osm, nq, 0)], [wo], xs.shape[0])
    kv_p = yp[:, nq:].reshape(Bn, S, 2, B_KV_HEADS, HEAD_DIM)
    kv_s = ys[:, nq:].reshape(Bd, T, 2, B_KV_HEADS, HEAD_DIM)
    return xp, xs, kv_p, kv_s


def _gla_kernel(nv, q_ref, k_ref, v_ref, glr_ref, wg_ref, bg_ref, tri_ref, blk_ref, s0_ref,
                o_ref, sT_ref, s_ref, qp_ref, kp_ref, vp_ref, gp_ref):
    t = pl.program_id(1)
    Tt = SEQ_TILE

    @pl.when(t == 0)
    def _():
        s_ref[...] = s0_ref[0]

    if nv < Tt:
        qp_ref[...] = jnp.zeros_like(qp_ref)
        kp_ref[...] = jnp.zeros_like(kp_ref)
        vp_ref[...] = jnp.zeros_like(vp_ref)
        gp_ref[...] = jnp.zeros_like(gp_ref)
    qp_ref[0:nv, :] = q_ref[0]
    kp_ref[0:nv, :] = k_ref[0]
    vp_ref[0:nv, :] = v_ref[0]
    gp_ref[0:nv, :] = glr_ref[0]
    q, k, v = qp_ref[...], kp_ref[...], vp_ref[...]

    x = _dot(gp_ref[...].astype(BF16), wg_ref[...]) + bg_ref[...]
    la = (jnp.minimum(x, 0.0) - jnp.log(1.0 + jnp.exp(-jnp.abs(x)))) * (1.0 / C_TAU)
    row = lax.broadcasted_iota(jnp.int32, (Tt, C_KEY), 0)
    la = jnp.where(row < nv, la, 0.0)
    b = _sel_left(tri_ref[...], la, terms=3)
    bend = _sel_left(blk_ref[...], la, terms=3)
    qe = q * (C_DK ** -0.5) * jnp.exp(b)
    ke = k * jnp.exp(-b)
    kd = k * jnp.exp(bend - b)
    kdt = kd.T
    bendt = bend.T
    qeb, keb, vb = qe.astype(BF16), ke.astype(BF16), v.astype(BF16)
    causal = tri_ref[...] > 0
    lane_t = lax.broadcasted_iota(jnp.int32, (C_DK, Tt), 1)
    n_chunks = -(-nv // C_CHUNK)
    for h in range(C_HEADS):
        ks = slice(h * C_DK, (h + 1) * C_DK)
        vs = slice(h * C_DV, (h + 1) * C_DV)
        att = jnp.where(causal, _dot_nt(qeb[:, ks], keb[:, ks]), 0.0)
        o_intra = _dot(att.astype(BF16), vb[:, vs])
        st = s_ref[h]
        parts = []
        for c in range(n_chunks):
            rows = slice(c * C_CHUNK, (c + 1) * C_CHUNK)
            parts.append(o_intra[rows] + _dot(qeb[rows, ks], st.astype(BF16)))
            in_chunk = (lane_t >= c * C_CHUNK) & (lane_t < (c + 1) * C_CHUNK)
            last = lane_t == (c + 1) * C_CHUNK - 1
            dec = jnp.exp(jnp.sum(jnp.where(last, bendt[ks, :], 0.0), axis=1, keepdims=True))
            kdc = jnp.where(in_chunk, kdt[ks, :], 0.0).astype(BF16)
            st = dec * st + _dot(kdc, vb[:, vs])
        s_ref[h] = st
        oh = parts[0] if n_chunks == 1 else jnp.concatenate(parts, axis=0)
        o_ref[0, :, vs] = oh[0:nv]

    @pl.when(t == pl.num_programs(1) - 1)
    def _():
        sT_ref[0] = s_ref[...]


def _gla(y, w_gate2, b_gate, s0, nb, nt, nv):
    Tt = SEQ_TILE
    y3 = y.reshape(nb * nt, nv, C_NPAD)
    wg = jnp.zeros((LANES, C_KEY), F32).at[:C_RANK].set(w_gate2).astype(BF16)
    idx = np.arange(Tt)
    same = (idx[:, None] // C_CHUNK) == (idx[None, :] // C_CHUNK)
    tri = jnp.asarray(same & (idx[None, :] <= idx[:, None]), BF16)
    blk = jnp.asarray(same, BF16)

    def yspec(width, cb):
        return pl.BlockSpec((1, nv, width), lambda b, t: (b * nt + t, 0, cb))

    def cspec(a):
        return pl.BlockSpec(a.shape, lambda b, t, nd=a.ndim: (0,) * nd)

    bg = b_gate.reshape(1, C_KEY)
    o, sT = pl.pallas_call(
        functools.partial(_gla_kernel, nv),
        grid=(nb, nt),
        in_specs=[
            yspec(C_KEY, 0), yspec(C_KEY, 1), yspec(C_VAL, 1), yspec(LANES, (2 * C_KEY + 2 * C_VAL) // LANES),
            cspec(wg), cspec(bg), cspec(tri), cspec(blk),
            pl.BlockSpec((1, C_HEADS, C_DK, C_DV), lambda b, t: (b, 0, 0, 0)),
        ],
        out_specs=[
            pl.BlockSpec((1, nv, C_VAL), lambda b, t: (b * nt + t, 0, 0)),
            pl.BlockSpec((1, C_HEADS, C_DK, C_DV), lambda b, t: (b, 0, 0, 0)),
        ],
        out_shape=[
            jax.ShapeDtypeStruct((nb * nt, nv, C_VAL), F32),
            jax.ShapeDtypeStruct((nb, C_HEADS, C_DK, C_DV), F32),
        ],
        scratch_shapes=[
            pltpu.VMEM((C_HEADS, C_DK, C_DV), F32),
            pltpu.VMEM((Tt, C_KEY), F32), pltpu.VMEM((Tt, C_KEY), F32),
            pltpu.VMEM((Tt, C_VAL), F32), pltpu.VMEM((Tt, LANES), F32),
        ],
        compiler_params=_params("parallel", "arbitrary"),
        name="gla",
    )(y3, y3, y3, y3, wg, bg, tri, blk, s0)
    return o.reshape(nb * nt * nv, C_VAL), sT


def _mixer_c(xp, xs, gain, w_in, w_gate2, b_gate, norm_g, w_out, state, Bn, S, Bd, T):
    n_in = w_in.shape[1]
    w = jnp.zeros((D_MODEL, C_NPAD), BF16).at[:, :n_in].set(w_in.astype(BF16))
    yp = _proj(xp, gain, w, 1024)
    ys = _proj(xs, gain, w, xs.shape[0])
    zero = jnp.zeros((Bn, C_HEADS, C_DK, C_DV), F32)
    op, sp = _gla(yp, w_gate2, b_gate, zero, Bn, S // SEQ_TILE, SEQ_TILE)
    osm, ss = _gla(ys, w_gate2, b_gate, state, Bd, 1, T)
    wo = w_out.astype(BF16)
    ng = norm_g.reshape(1, C_DV)
    rcb = (2 * C_KEY + C_VAL) // C_VAL
    xp = _row_call(_out_c_kernel, "out_c", xp, [(op, C_VAL, 0), (yp, C_VAL, rcb)], [ng, wo], 512)
    xs = _row_call(_out_c_kernel, "out_c", xs, [(osm, C_VAL, 0), (ys, C_VAL, rcb)], [ng, wo], xs.shape[0])
    return xp, xs, sp, ss


def _ssd_kernel(nv, xa_ref, xb_ref, bc_ref, dt_ref, cs_ref, cw_ref, cb_ref, dtb_ref, alog_ref, dsk_ref,
                tri_ref, exp_ref, expt_ref, h0_ref, y_ref, hT_ref, h_ref, xp_ref, dtp_ref):
    t = pl.program_id(1)
    Tt = SEQ_TILE
    pre = 8

    @pl.when(t == 0)
    def _():
        h_ref[...] = h0_ref[0]
        xp_ref[0:pre, :] = cs_ref[0]

    if nv < Tt:
        xp_ref[pre:, :] = jnp.zeros((Tt, D_XBC), F32)
        dtp_ref[...] = jnp.zeros_like(dtp_ref)
    xp_ref[pre:pre + nv, 0:1024] = xa_ref[0]
    xp_ref[pre:pre + nv, 1024:2048] = xb_ref[0]
    xp_ref[pre:pre + nv, 2048:3072] = bc_ref[0]
    dtp_ref[0:nv, :] = dt_ref[0]

    full = xp_ref[...]
    conv = cb_ref[...] + full[pre:] * cw_ref[D_CONV - 1:D_CONV, :]
    for k in range(1, D_CONV):
        conv = conv + pltpu.roll(full, k, 0)[pre:] * cw_ref[D_CONV - 1 - k:D_CONV - k, :]
    xp_ref[0:pre, :] = xp_ref[Tt:Tt + pre, :]
    xbc = _silu(conv)
    x = xbc[:, :D_INNER]
    nbc = D_GROUPS * D_STATE
    bm = xbc[:, D_INNER:D_INNER + nbc].astype(BF16)
    cm = xbc[:, D_INNER + nbc:].astype(BF16)

    row = lax.broadcasted_iota(jnp.int32, (Tt, LANES), 0)
    lane = lax.broadcasted_iota(jnp.int32, (Tt, LANES), 1)
    live = (row < nv) & (lane < D_HEADS)
    dt = jnp.where(live, _softplus(dtp_ref[...] + dtb_ref[...]), 0.0)
    cum = _sel_left(tri_ref[...], dt * (-jnp.exp(alog_ref[...])), terms=3)
    cumt = cum.T
    cend = cum[Tt - 1:Tt, :]
    ex = exp_ref[...]
    stack = jnp.concatenate([jnp.exp(cend - cum) * dt, dt, jnp.exp(cum)], axis=0)
    wide = _sel_right(stack, ex)
    x_state = (x * wide[0:Tt]).astype(BF16)
    x_dt = (x * wide[Tt:2 * Tt]).astype(BF16)
    off_scale = wide[2 * Tt:]

    causal = tri_ref[...] > 0
    gw = D_INNER // D_GROUPS
    cbs, offs = [], []
    for g in range(D_GROUPS):
        cg = cm[:, g * D_STATE:(g + 1) * D_STATE]
        cbs.append(_dot_nt(cg, bm[:, g * D_STATE:(g + 1) * D_STATE]))
        offs.append(_dot_nt(cg, h_ref[g * gw:(g + 1) * gw, :].astype(BF16)))
    mats = []
    for j in range(D_HEADS):
        seg = cum[:, j:j + 1] - cumt[j:j + 1, :]
        mats.append((cbs[j // D_HPG] * jnp.exp(jnp.where(causal, seg, -jnp.inf))).astype(BF16))
    diag = [_dot(mats[j], x_dt[:, j * D_HEADDIM:(j + 1) * D_HEADDIM]) for j in range(D_HEADS)]
    y = jnp.concatenate(diag, axis=1) + jnp.concatenate(offs, axis=1) * off_scale + dsk_ref[...] * x
    y_ref[0] = y[0:nv]

    dcol = jnp.broadcast_to(jnp.exp(cumt[:, Tt - 1:Tt]), (LANES, D_STATE))
    dfull = _sel_left(expt_ref[...], dcol)
    xst = x_state.astype(F32).T.astype(BF16)
    for g in range(D_GROUPS):
        rows = slice(g * gw, (g + 1) * gw)
        h_ref[rows, :] = dfull[rows] * h_ref[rows, :] + _dot(xst[rows], bm[:, g * D_STATE:(g + 1) * D_STATE])

    @pl.when(t == pl.num_programs(1) - 1)
    def _():
        hT_ref[0] = h_ref[...]


def _ssd(y, conv_w, conv_b, dt_bias, a_log, d_skip, h0, c0, nb, nt, nv):
    Tt = SEQ_TILE
    y3 = y.reshape(nb * nt, nv, D_NPAD)
    idx = np.arange(Tt)
    tri = jnp.asarray(idx[None, :] <= idx[:, None], BF16)
    ex_np = np.zeros((LANES, D_INNER), np.float32)
    ex_np[np.arange(D_INNER) // D_HEADDIM, np.arange(D_INNER)] = 1.0
    ex = jnp.asarray(ex_np, BF16)
    ext = jnp.asarray(ex_np.T, BF16)

    def pad_row(v):
        return jnp.zeros((1, LANES), F32).at[0, :D_HEADS].set(v)

    cs = jnp.zeros((nb, 8, D_XBC), F32).at[:, 8 - (D_CONV - 1):].set(c0)
    dsk = jnp.repeat(d_skip, D_HEADDIM).reshape(1, D_INNER)

    def yspec(width, cb):
        return pl.BlockSpec((1, nv, width), lambda b, t: (b * nt + t, 0, cb))

    def cspec(a):
        return pl.BlockSpec(a.shape, lambda b, t, nd=a.ndim: (0,) * nd)

    consts = [conv_w, conv_b.reshape(1, D_XBC), pad_row(dt_bias), pad_row(a_log), dsk, tri, ex, ext]
    yo, hT = pl.pallas_call(
        functools.partial(_ssd_kernel, nv),
        grid=(nb, nt),
        in_specs=[yspec(1024, 2), yspec(1024, 3), yspec(1024, 4), yspec(LANES, D_DT_COL),
                  pl.BlockSpec((1, 8, D_XBC), lambda b, t: (b, 0, 0))]
        + [cspec(a) for a in consts]
        + [pl.BlockSpec((1, D_INNER, D_STATE), lambda b, t: (b, 0, 0))],
        out_specs=[
            pl.BlockSpec((1, nv, D_INNER), lambda b, t: (b * nt + t, 0, 0)),
            pl.BlockSpec((1, D_INNER, D_STATE), lambda b, t: (b, 0, 0)),
        ],
        out_shape=[
            jax.ShapeDtypeStruct((nb * nt, nv, D_INNER), F32),
            jax.ShapeDtypeStruct((nb, D_INNER, D_STATE), F32),
        ],
        scratch_shapes=[
            pltpu.VMEM((D_INNER, D_STATE), F32),
            pltpu.VMEM((Tt + 8, D_XBC), F32),
            pltpu.VMEM((Tt, LANES), F32),
        ],
        compiler_params=_params("parallel", "arbitrary"),
        name="ssd",
    )(y3, y3, y3, y3, cs, *consts, h0)
    return yo.reshape(nb * nt * nv, D_INNER), hT


def _ssd_step_kernel(nv, xa_ref, xb_ref, bc_ref, dt_ref, cs_ref, cw_ref, cb_ref, dtb_ref, alog_ref, dsk_ref,
                     eye_ref, h0_ref, y_ref, hT_ref, xp_ref):
    R = 8
    pre = 8
    xp_ref[0:pre, :] = cs_ref[0]
    xp_ref[pre:, :] = jnp.zeros((R, D_XBC), F32)
    xp_ref[pre:pre + nv, 0:1024] = xa_ref[0]
    xp_ref[pre:pre + nv, 1024:2048] = xb_ref[0]
    xp_ref[pre:pre + nv, 2048:3072] = bc_ref[0]
    conv = cb_ref[...]
    for w in range(D_CONV):
        conv = conv + xp_ref[pl.ds(pre - (D_CONV - 1) + w, R), :] * cw_ref[w:w + 1, :]
    xbc = _silu(conv)
    x = xbc[:, :D_INNER]
    nbc = D_GROUPS * D_STATE
    bmf = xbc[:, D_INNER:D_INNER + nbc]
    cmf = xbc[:, D_INNER + nbc:]

    row = lax.broadcasted_iota(jnp.int32, (R, LANES), 0)
    lane = lax.broadcasted_iota(jnp.int32, (R, LANES), 1)
    dtp = jnp.concatenate([dt_ref[0], jnp.zeros((R - nv, LANES), F32)], axis=0)
    dt = jnp.where((row < nv) & (lane < D_HEADS), _softplus(dtp + dtb_ref[...]), 0.0)
    cum = dt * (-jnp.exp(alog_ref[...]))
    for k in (1, 2, 4):
        cum = cum + jnp.where(row >= k, pltpu.roll(cum, k, 0), 0.0)
    cend = cum[R - 1:R, :]
    group_of_lane = lane // D_HPG
    low_half = lane < D_HEADDIM

    def widen(a):
        pairs = [jnp.where(low_half, a[:, j:j + 1], a[:, j + 1:j + 2]) for j in range(0, D_HEADS, 2)]
        return jnp.concatenate(pairs, axis=1)

    yd = jnp.zeros((R, D_INNER), F32)
    for s in range(nv):
        w_s = jnp.where(row >= s, jnp.exp(cum - cum[s:s + 1, :]), 0.0) * dt[s:s + 1, :]
        prod = cmf * bmf[s:s + 1, :]
        cb_s = jnp.zeros((R, LANES), F32)
        for g in range(D_GROUPS):
            tot = jnp.sum(prod[:, g * D_STATE:(g + 1) * D_STATE], axis=1, keepdims=True)
            cb_s = jnp.where(group_of_lane == g, tot, cb_s)
        yd = yd + widen(cb_s * w_s) * x[s:s + 1, :]
    off_scale = widen(jnp.exp(cum))
    x_state = (x * widen(jnp.exp(cend - cum) * dt)).astype(BF16)

    gw = D_INNER // D_GROUPS
    cm = cmf.astype(BF16)
    parts = []
    for g in range(D_GROUPS):
        hg = h0_ref[0, g * gw:(g + 1) * gw, :]
        parts.append(_dot_nt(cm[:, g * D_STATE:(g + 1) * D_STATE], hg.astype(BF16)))
    y = yd + jnp.concatenate(parts, axis=1) * off_scale + dsk_ref[...] * x
    y_ref[0] = y[0:nv]

    erow = jnp.broadcast_to(jnp.exp(cend), (LANES, LANES))
    eye128 = (lax.broadcasted_iota(jnp.int32, (LANES, LANES), 0)
              == lax.broadcasted_iota(jnp.int32, (LANES, LANES), 1)).astype(BF16)
    dcol = _nt_sel(eye128, erow)[0:D_HEADS].reshape(D_HEADS, 1, D_STATE)
    zpad = jnp.zeros((LANES - R, D_INNER), BF16)
    xs_pad = jnp.concatenate([x_state, zpad], axis=0)
    bm_pad = jnp.concatenate([bmf.astype(BF16), jnp.zeros((LANES - R, nbc), BF16)], axis=0)
    for g in range(D_GROUPS):
        rows = slice(g * gw, (g + 1) * gw)
        xt = _dot_nt(eye_ref[...], xs_pad[:, rows]).astype(BF16)
        kept = (h0_ref[0, rows, :].reshape(D_HPG, D_HEADDIM, D_STATE) * dcol[g * D_HPG:(g + 1) * D_HPG])
        hT_ref[0, rows, :] = kept.reshape(gw, D_STATE) + _dot(xt, bm_pad[:, g * D_STATE:(g + 1) * D_STATE])


def _ssd_step(y, conv_w, conv_b, dt_bias, a_log, d_skip, h0, c0, nb, nv):
    assert nv <= 8
    y3 = y.reshape(nb, nv, D_NPAD)
    gw = D_INNER // D_GROUPS
    eye = jnp.asarray(np.eye(gw), BF16)

    def pad_row(v):
        return jnp.zeros((1, LANES), F32).at[0, :D_HEADS].set(v)

    cs = jnp.zeros((nb, 8, D_XBC), F32).at[:, 8 - (D_CONV - 1):].set(c0)
    dsk = jnp.repeat(d_skip, D_HEADDIM).reshape(1, D_INNER)

    def yspec(width, cb):
        return pl.BlockSpec((1, nv, width), lambda b: (b, 0, cb))

    def cspec(a):
        return pl.BlockSpec(a.shape, lambda b, nd=a.ndim: (0,) * nd)

    consts = [conv_w, conv_b.reshape(1, D_XBC), pad_row(dt_bias), pad_row(a_log), dsk, eye]
    yo, hT = pl.pallas_call(
        functools.partial(_ssd_step_kernel, nv),
        grid=(nb,),
        in_specs=[yspec(1024, 2), yspec(1024, 3), yspec(1024, 4), yspec(LANES, D_DT_COL),
                  pl.BlockSpec((1, 8, D_XBC), lambda b: (b, 0, 0))]
        + [cspec(a) for a in consts]
        + [pl.BlockSpec((1, D_INNER, D_STATE), lambda b: (b, 0, 0))],
        out_specs=[
            pl.BlockSpec((1, nv, D_INNER), lambda b: (b, 0, 0)),
            pl.BlockSpec((1, D_INNER, D_STATE), lambda b: (b, 0, 0)),
        ],
        out_shape=[
            jax.ShapeDtypeStruct((nb, nv, D_INNER), F32),
            jax.ShapeDtypeStruct((nb, D_INNER, D_STATE), F32),
        ],
        scratch_shapes=[pltpu.VMEM((16, D_XBC), F32)],
        compiler_params=_params("parallel"),
        name="ssd_step",
    )(y3, y3, y3, y3, cs, *consts, h0)
    return yo.reshape(nb * nv, D_INNER), hT


def _mixer_d(xp, xs, gain, w_in, conv_w, conv_b, dt_bias, a_log, d_skip, norm_g, w_out,
             ssm_state, conv_state, Bn, S, Bd, T):
    n_in = w_in.shape[1]
    w = jnp.zeros((D_MODEL, D_NPAD), BF16).at[:, :n_in].set(w_in.astype(BF16))
    yp = _proj(xp, gain, w, 1024)
    ys = _proj(xs, gain, w, xs.shape[0])
    h0p = jnp.zeros((Bn, D_INNER, D_STATE), F32)
    c0p = jnp.zeros((Bn, D_CONV - 1, D_XBC), F32)
    op, hp = _ssd(yp, conv_w, conv_b, dt_bias, a_log, d_skip, h0p, c0p, Bn, S // SEQ_TILE, SEQ_TILE)
    h0s = ssm_state.reshape(Bd, D_INNER, D_STATE)
    osm, hs = _ssd_step(ys, conv_w, conv_b, dt_bias, a_log, d_skip, h0s, conv_state, Bd, T)
    wo = w_out.astype(BF16)
    ng = norm_g.reshape(1, D_INNER)
    xp = _row_call(_out_d_kernel, "out_d", xp, [(op, D_INNER, 0), (yp, D_INNER, 0)], [ng, wo], 512)
    xs = _row_call(_out_d_kernel, "out_d", xs, [(osm, D_INNER, 0), (ys, D_INNER, 0)], [ng, wo], xs.shape[0])
    keep = D_CONV - 1
    xbc_p = yp.reshape(Bn, S, D_NPAD)[:, :, D_INNER:D_INNER + D_XBC]
    xbc_s = ys.reshape(Bd, T, D_NPAD)[:, :, D_INNER:D_INNER + D_XBC]
    cp = jnp.concatenate([c0p, xbc_p], axis=1)[:, -keep:] if S < keep else xbc_p[:, S - keep:]
    cs = jnp.concatenate([conv_state, xbc_s], axis=1)[:, -keep:]
    return (xp, xs, hp.reshape(Bn, D_HEADS, D_HEADDIM, D_STATE), hs.reshape(Bd, D_HEADS, D_HEADDIM, D_STATE),
            cp, cs)


def kernel(x_prompt, x_sample, cache_a_w1, cache_a_w2, cache_a_w3, cache_b_kv, page_table, state_c, state_d_ssm, state_d_conv, norm_gain, w_ffn_up, w_ffn_down, w_a_in, a_qk_gain, w_a_out, w_b_in, b_qk_gain, w_b_out, w_c_in, w_c_gate2, b_c_gate, c_norm_gain, w_c_out, w_d_in, d_conv_w, d_conv_b, d_dt_bias, d_a_log, d_skip, d_norm_gain, w_d_out):
    Bn, S, _ = x_prompt.shape
    Bd, T, _ = x_sample.shape
    depth = norm_gain.shape[0]
    past_len = page_table.shape[1] * cache_b_kv.shape[2]
    tabs_p = _rope_tables(jnp.arange(S, dtype=jnp.int32))
    tabs_s = _rope_tables(jnp.tile(past_len + jnp.arange(T, dtype=jnp.int32), Bd))
    xp = x_prompt.reshape(Bn * S, D_MODEL)
    xs = x_sample.reshape(Bd * T, D_MODEL)
    ts = xs.shape[0]
    w_up = w_ffn_up.astype(BF16)
    w_down = w_ffn_down.astype(BF16)
    outs = {k: [] for k in ("a0p", "a0s", "a1p", "a1s", "a2p", "a2s", "bp", "bs", "cp", "cs",
                            "hp", "hs", "dp", "ds")}
    for i in range(depth):
        m, j = i % 4, i // 4
        g = norm_gain[i]
        xp = _ffn(xp, g[0], w_up, w_down, i, 0, 1024)
        xs = _ffn(xs, g[0], w_up, w_down, i, 0, ts)
        if m == 0:
            xp, xs, new_p, new_s = _mixer_a(xp, xs, g[1], w_a_in[j], a_qk_gain[j], w_a_out[j],
                                            (cache_a_w1[j], cache_a_w2[j], cache_a_w3[j]),
                                            tabs_p, tabs_s, Bn, S, Bd, T)
            for gi in range(A_GROUPS):
                outs["a%dp" % gi].append(new_p[gi])
                outs["a%ds" % gi].append(new_s[gi])
        elif m == 1:
            xp, xs, kvp, kvs = _mixer_b(xp, xs, g[1], w_b_in[j], b_qk_gain[j], w_b_out[j],
                                        cache_b_kv[j], page_table, tabs_p, tabs_s, Bn, S, Bd, T)
            outs["bp"].append(kvp)
            outs["bs"].append(kvs)
        elif m == 2:
            xp, xs, sp, ss = _mixer_c(xp, xs, g[1], w_c_in[j], w_c_gate2[j], b_c_gate[j], c_norm_gain[j],
                                      w_c_out[j], state_c[j], Bn, S, Bd, T)
            outs["cp"].append(sp)
            outs["cs"].append(ss)
        else:
            xp, xs, hp, hs, cp, cs = _mixer_d(xp, xs, g[1], w_d_in[j], d_conv_w[j], d_conv_b[j], d_dt_bias[j],
                                              d_a_log[j], d_skip[j], d_norm_gain[j], w_d_out[j],
                                              state_d_ssm[j], state_d_conv[j], Bn, S, Bd, T)
            outs["hp"].append(hp)
            outs["hs"].append(hs)
            outs["dp"].append(cp)
            outs["ds"].append(cs)
        xp = _ffn(xp, g[2], w_up, w_down, i, 1, 1024)
        xs = _ffn(xs, g[2], w_up, w_down, i, 1, ts)
    st = {k: jnp.stack(v) for k, v in outs.items()}
    return (xp.reshape(Bn, S, D_MODEL), xs.reshape(Bd, T, D_MODEL),
            st["a0p"], st["a0s"], st["a1p"], st["a1s"], st["a2p"], st["a2s"],
            st["bp"], st["bs"], st["cp"], st["cs"], st["hp"], st["hs"], st["dp"], st["ds"])
```

```python
import functools
import math

import numpy as np
import jax
import jax.numpy as jnp
from jax import lax
from jax.experimental import pallas as pl
from jax.experimental.pallas import tpu as pltpu

F32 = jnp.float32
BF16 = jnp.bfloat16

D_MODEL = 1024
HEAD_DIM = 64
ROPE_DIM = HEAD_DIM // 4
ROPE_THETA = 500000.0
EPS = 1e-6
D_FF = 2816
NEG = -1e30

A_GROUPS = 3
A_DILATIONS = (1, 4, 16)
A_HEADS = 8
A_WIDTH = A_HEADS * HEAD_DIM
A_SPAN = 128

B_HEADS = 16
B_KV_HEADS = 4
B_REP = B_HEADS // B_KV_HEADS
B_BLOCK = 256
B_TOPK = 3
B_VT_ROWS = HEAD_DIM + 8
LOG2E = 1.4426950408889634

C_HEADS = 4
C_KEY = 512
C_VAL = 1024
C_DK = 128
C_DV = 256
C_RANK = 16
C_TAU = 16.0
C_CHUNK = 32
C_NPAD = 3200

D_INNER = 2048
D_HEADDIM = 64
D_HEADS = 32
D_GROUPS = 4
D_HPG = 8
D_STATE = 128
D_CONV = 4
D_XBC = D_INNER + 2 * D_GROUPS * D_STATE
D_NPAD = 5376
D_DT_COL = (D_INNER + D_XBC) // 128

LANES = 128
FFN_ROWS = 256
PROJ_ROWS = 256
SEQ_TILE = 128


def _params(*sem):
    return pltpu.CompilerParams(dimension_semantics=sem)


def _split2(a):
    hi = a.astype(BF16)
    lo = (a - hi.astype(F32)).astype(BF16)
    return hi, lo


def _split3(a):
    hi = a.astype(BF16)
    r = a - hi.astype(F32)
    mid = r.astype(BF16)
    lo = (r - mid.astype(F32)).astype(BF16)
    return hi, mid, lo


def _dot(a, b):
    return jnp.dot(a, b, preferred_element_type=F32)


def _dot_nt(a, b):
    return lax.dot_general(a, b, (((1,), (1,)), ((), ())), preferred_element_type=F32)


def _sel_right(a, sel01, terms=2):
    parts = _split2(a) if terms == 2 else _split3(a)
    out = None
    for p in reversed(parts):
        d = _dot(p, sel01)
        out = d if out is None else out + d
    return out


def _sel_left(sel01, a, terms=2):
    parts = _split2(a) if terms == 2 else _split3(a)
    out = None
    for p in reversed(parts):
        d = _dot(sel01, p)
        out = d if out is None else out + d
    return out


def _silu(x):
    return x * jax.nn.sigmoid(x)


def _softplus(x):
    return jnp.maximum(x, 0.0) + jnp.log(1.0 + jnp.exp(-jnp.abs(x)))


def _rms_rows(x, g):
    ms = jnp.mean(x * x, axis=-1, keepdims=True)
    return x * lax.rsqrt(ms + EPS) * g


def _ffn_kernel(mix_fn, n_mix, x_ref, *refs):
    mix_refs = refs[:n_mix]
    g_ref, wg_ref, wu_ref, wd_ref, o_ref = refs[n_mix:]
    x = x_ref[...]
    if mix_fn is not None:
        x = x + mix_fn(*mix_refs)
    h = _rms_rows(x, g_ref[...]).astype(BF16)
    a = _dot(h, wg_ref[...])
    u = _dot(h, wu_ref[...])
    act = (_silu(a) * u).astype(BF16)
    o_ref[...] = x + 0.5 * _dot(act, wd_ref[...])


def _ffn(x, gain, w_up, w_down, layer, which, tm, mix=None):
    T = x.shape[0]
    tm = min(tm, FFN_ROWS)
    resident = dict(pipeline_mode=pl.Buffered(1))
    mix_fn, row_inputs, const_inputs = mix if mix is not None else (None, [], [])
    mix_specs, mix_args = [], []
    for arr, width, cb in row_inputs:
        mix_specs.append(pl.BlockSpec((tm, width), lambda i, cb=cb: (i, cb)))
        mix_args.append(arr)
    for arr in const_inputs:
        mix_specs.append(pl.BlockSpec(arr.shape, lambda i, nd=arr.ndim: (0,) * nd, **resident))
        mix_args.append(arr)
    return pl.pallas_call(
        functools.partial(_ffn_kernel, mix_fn, len(mix_args)),
        grid=(T // tm,),
        in_specs=[pl.BlockSpec((tm, D_MODEL), lambda i: (i, 0))] + mix_specs + [
            pl.BlockSpec((1, D_MODEL), lambda i: (0, 0), **resident),
            pl.BlockSpec((None, None, D_MODEL, D_FF), lambda i: (layer, which, 0, 0), **resident),
            pl.BlockSpec((None, None, D_MODEL, D_FF), lambda i: (layer, which, 0, 1), **resident),
            pl.BlockSpec((None, None, D_FF, D_MODEL), lambda i: (layer, which, 0, 0), **resident),
        ],
        out_specs=pl.BlockSpec((tm, D_MODEL), lambda i: (i, 0)),
        out_shape=jax.ShapeDtypeStruct((T, D_MODEL), F32),
        compiler_params=_params("parallel"),
        name="ffn",
    )(x, *mix_args, gain.reshape(1, D_MODEL), w_up, w_up, w_down)


def _proj_kernel(x_ref, g_ref, w_ref, o_ref):
    o_ref[...] = _dot(_rms_rows(x_ref[...], g_ref[...]).astype(BF16), w_ref[...])


def _proj(x, gain, w, tm):
    T = x.shape[0]
    N = w.shape[1]
    tm = min(tm, PROJ_ROWS)
    resident = dict(pipeline_mode=pl.Buffered(1))
    return pl.pallas_call(
        _proj_kernel,
        grid=(T // tm,),
        in_specs=[
            pl.BlockSpec((tm, D_MODEL), lambda i: (i, 0)),
            pl.BlockSpec((1, D_MODEL), lambda i: (0, 0), **resident),
            pl.BlockSpec((D_MODEL, N), lambda i: (0, 0), **resident),
        ],
        out_specs=pl.BlockSpec((tm, N), lambda i: (i, 0)),
        out_shape=jax.ShapeDtypeStruct((T, N), F32),
        compiler_params=_params("parallel"),
        name="proj",
    )(x, gain.reshape(1, D_MODEL), w)


def _proj_qk_kernel(flags, x_ref, g_ref, w_ref, bd_ref, gain_ref, mask_ref, cos_ref, sa_ref, sb_ref, o_ref):
    h = _rms_rows(x_ref[...], g_ref[...]).astype(BF16)
    tn = bd_ref.shape[0]
    cos, sa, sb = cos_ref[...], sa_ref[...], sb_ref[...]
    for j, flag in enumerate(flags):
        y = _dot(h, w_ref[:, j * tn:(j + 1) * tn])
        if not flag:
            o_ref[:, j * tn:(j + 1) * tn] = y
            continue
        ss = _dot((y * y).astype(BF16), bd_ref[...])
        yn = y * lax.rsqrt(ss * (1.0 / HEAD_DIM) + EPS) * gain_ref[j]
        mask = mask_ref[j]
        for c in range(tn // LANES):
            sl = slice(c * LANES, (c + 1) * LANES)
            v = yn[:, sl]
            up = pltpu.roll(v, LANES - ROPE_DIM // 2, 1)
            dn = pltpu.roll(v, ROPE_DIM // 2, 1)
            rot = v * cos + up * sa + dn * sb
            o_ref[:, j * tn + c * LANES:j * tn + (c + 1) * LANES] = jnp.where(mask[:, sl] > 0.0, rot, y[:, sl])


def _proj_qk(x, gain, w, flags, gain_rows, mask_rows, tables, tm, tn):
    T = x.shape[0]
    N = w.shape[1]
    nj = N // tn
    tm = min(tm, PROJ_ROWS)
    cos, sa, sb = tables
    n_pos_tiles = cos.shape[0] // tm
    bd = jnp.asarray(np.kron(np.eye(tn // HEAD_DIM), np.ones((HEAD_DIM, HEAD_DIM))), BF16)
    resident = dict(pipeline_mode=pl.Buffered(1))
    tab_spec = pl.BlockSpec((tm, LANES), lambda i: (i % n_pos_tiles, 0))
    return pl.pallas_call(
        functools.partial(_proj_qk_kernel, flags),
        grid=(T // tm,),
        in_specs=[
            pl.BlockSpec((tm, D_MODEL), lambda i: (i, 0)),
            pl.BlockSpec((1, D_MODEL), lambda i: (0, 0), **resident),
            pl.BlockSpec((D_MODEL, N), lambda i: (0, 0), **resident),
            pl.BlockSpec((tn, tn), lambda i: (0, 0), **resident),
            pl.BlockSpec((nj, 1, tn), lambda i: (0, 0, 0), **resident),
            pl.BlockSpec((nj, 1, tn), lambda i: (0, 0, 0), **resident),
            tab_spec, tab_spec, tab_spec,
        ],
        out_specs=pl.BlockSpec((tm, N), lambda i: (i, 0)),
        out_shape=jax.ShapeDtypeStruct((T, N), F32),
        compiler_params=_params("parallel"),
        name="proj_qk",
    )(x, gain.reshape(1, D_MODEL), w, bd, gain_rows, mask_rows, cos, sa, sb)


def _rope_tables(pos):
    half = ROPE_DIM // 2
    inv = ROPE_THETA ** (-jnp.arange(half, dtype=F32) / half)
    ang = pos.astype(F32)[:, None] * inv[None, :]
    cos, sin = jnp.cos(ang), jnp.sin(ang)
    n = pos.shape[0]
    one = jnp.ones((n, HEAD_DIM - ROPE_DIM), F32)
    zero = jnp.zeros((n, HEAD_DIM - ROPE_DIM), F32)
    zh = jnp.zeros((n, half), F32)
    c64 = jnp.concatenate([cos, cos, one], axis=1)
    sa64 = jnp.concatenate([-sin, zh, zero], axis=1)
    sb64 = jnp.concatenate([zh, sin, zero], axis=1)
    rep = LANES // HEAD_DIM
    return tuple(jnp.tile(t, (1, rep)) for t in (c64, sa64, sb64))


def _mix_plain(o_ref, w_ref):
    return _dot(o_ref[...].astype(BF16), w_ref[...])


def _mix_a(o0, o1, o2, l0, l1, l2, w_ref):
    la, lb, lc = l0[...], l1[...], l2[...]
    m = jnp.maximum(jnp.maximum(la, lb), lc)
    ea, eb, ec = jnp.exp(la - m), jnp.exp(lb - m), jnp.exp(lc - m)
    o = (ea * o0[...] + eb * o1[...] + ec * o2[...]) / (ea + eb + ec)
    return _dot(o.astype(BF16), w_ref[...])


def _mix_c(o_ref, r_ref, g_ref, w_ref):
    o = o_ref[...]
    g = g_ref[...]
    parts = [_rms_rows(o[:, h * C_DV:(h + 1) * C_DV], g) for h in range(C_HEADS)]
    on = jnp.concatenate(parts, axis=1) * _silu(r_ref[...])
    return _dot(on.astype(BF16), w_ref[...])


def _mix_d(o_ref, z_ref, g_ref, w_ref):
    gated = o_ref[...] * _silu(z_ref[...])
    gw = D_INNER // D_GROUPS
    g = g_ref[...]
    parts = [_rms_rows(gated[:, k * gw:(k + 1) * gw], g[:, k * gw:(k + 1) * gw]) for k in range(D_GROUPS)]
    return _dot(jnp.concatenate(parts, axis=1).astype(BF16), w_ref[...])


def _a_prompt_kernel(d, q_ref, kp_ref, kc_ref, vp_ref, vc_ref, o_ref, l_ref):
    j = pl.program_id(1)
    heads = q_ref.shape[2] // HEAD_DIM
    per_pass = max(1, A_HEADS // heads)
    n_prob = per_pass * heads
    scale = HEAD_DIM ** -0.5
    qq = lax.broadcasted_iota(jnp.int32, (n_prob * A_SPAN, 2 * A_SPAN), 0) & (A_SPAN - 1)
    kk = lax.broadcasted_iota(jnp.int32, (n_prob * A_SPAN, 2 * A_SPAN), 1)
    rel = qq + A_SPAN - kk
    valid = (rel >= 0) & (rel <= A_SPAN) & ((kk >= A_SPAN) | (j > 0))
    hs = [slice(h * HEAD_DIM, (h + 1) * HEAD_DIM) for h in range(heads)]
    for r0 in range(0, d, per_pass):
        rows = [pl.ds(r0 + e, A_SPAN, stride=d) if d > 1 else pl.ds(0, A_SPAN) for e in range(per_pass)]
        s, vs = [], []
        for rw in rows:
            q = (q_ref[0, rw, :] * scale).astype(BF16)
            k = jnp.concatenate([kp_ref[0, rw, :], kc_ref[0, rw, :]], axis=0).astype(BF16)
            vs.append(jnp.concatenate([vp_ref[0, rw, :], vc_ref[0, rw, :]], axis=0).astype(BF16))
            s += [_dot_nt(q[:, sl], k[:, sl]) for sl in hs]
        s = jnp.where(valid, jnp.concatenate(s, axis=0), -jnp.inf)
        m = jnp.max(s, axis=-1, keepdims=True)
        p = jnp.exp(s - m)
        den = jnp.sum(p, axis=-1, keepdims=True)
        pb = p.astype(BF16)
        lse = m + jnp.log(den)
        inv = 1.0 / den
        for e, rw in enumerate(rows):
            o, l = [], []
            for h, sl in enumerate(hs):
                blk = slice((e * heads + h) * A_SPAN, (e * heads + h + 1) * A_SPAN)
                o.append(_dot(pb[blk], vs[e][:, sl]) * inv[blk])
                l.append(jnp.broadcast_to(lse[blk], (A_SPAN, HEAD_DIM)))
            o_ref[0, rw, :] = jnp.concatenate(o, axis=1)
            l_ref[0, rw, :] = jnp.concatenate(l, axis=1)


def _a_prompt(y, g, Bn, S):
    d = A_DILATIONS[g]
    slab = A_SPAN * d
    heads = A_HEADS if d == 1 else LANES // HEAD_DIM
    cols = heads * HEAD_DIM
    ncb = A_WIDTH // cols
    y3 = y.reshape(Bn, S, A_GROUPS * 3 * A_WIDTH)

    def spec(which, prev):
        c0 = (3 * g + which) * ncb
        if prev:
            return pl.BlockSpec((1, slab, cols), lambda b, j, c: (b, jnp.maximum(j - 1, 0), c0 + c))
        return pl.BlockSpec((1, slab, cols), lambda b, j, c: (b, j, c0 + c))

    out_spec = pl.BlockSpec((1, slab, cols), lambda b, j, c: (b, j, c))
    o, l = pl.pallas_call(
        functools.partial(_a_prompt_kernel, d),
        grid=(Bn, S // slab, ncb),
        in_specs=[spec(0, False), spec(1, True), spec(1, False), spec(2, True), spec(2, False)],
        out_specs=[out_spec, out_spec],
        out_shape=[jax.ShapeDtypeStruct((Bn, S, A_WIDTH), F32)] * 2,
        compiler_params=_params("parallel", "arbitrary", "arbitrary"),
        name="a_prompt_d%d" % d,
    )(y3, y3, y3, y3, y3)
    return o.reshape(Bn * S, A_WIDTH), l.reshape(Bn * S, A_WIDTH)


def _nt_sel(mask, a):
    out = None
    for p in reversed(_split3(a)):
        d = _dot_nt(mask, p)
        out = d if out is None else out + d
    return out


def _a_sample_kernel(T, y_ref, c1_ref, c2_ref, c3_ref, hind_ref, hindt_ref, *out_refs):
    caches = (c1_ref, c2_ref, c3_ref)
    width = 3 * A_WIDTH
    scale = HEAD_DIM ** -0.5
    hind = hind_ref[...]
    hindt = hindt_ref[...]
    lane = lax.broadcasted_iota(jnp.int32, (8, LANES), 1)
    sub = lax.broadcasted_iota(jnp.int32, (8, LANES), 0)
    zrows = jnp.zeros((LANES - T, A_WIDTH), F32)
    zsq = jnp.zeros((LANES - 8, LANES), F32)

    def head_rows(prod):
        return _nt_sel(hind, jnp.concatenate([prod, zrows], axis=0))

    def sublane_total(x):
        x = x + pltpu.roll(x, 4, 0)
        x = x + pltpu.roll(x, 2, 0)
        return x + pltpu.roll(x, 1, 0)

    def head_scores(k_of, pat_of):
        out = jnp.zeros((8, LANES), F32)
        for h in range(A_HEADS):
            prod = k_of(h) * pat_of(h)
            part = prod[0:8]
            for j in range(1, HEAD_DIM // 8):
                part = part + prod[8 * j:8 * j + 8]
            out = jnp.where(sub == h, sublane_total(part), out)
        return out

    def rows_of(packed):
        return jnp.concatenate([packed, zsq], axis=0).T

    def widen(rows8):
        return _sel_right(rows8, hindt, terms=3)

    def place(col, at):
        return jnp.where(lane == at, col, 0.0)

    def patterns(q_t, pick):
        return _sel_right(q_t, pick.astype(BF16), terms=3)

    row_sq = lax.broadcasted_iota(jnp.int32, (LANES, LANES), 0)
    lane_sq = lax.broadcasted_iota(jnp.int32, (LANES, LANES), 1)

    for g in range(A_GROUPS):
        d = A_DILATIONS[g]
        c_ref = caches[g]
        n_tiles = c_ref.shape[-1] // LANES
        q_nat = y_ref[0, :, g * width:g * width + A_WIDTH] * scale
        kn_nat = y_ref[0, :, g * width + A_WIDTH:g * width + 2 * A_WIDTH]
        vn_nat = y_ref[0, :, g * width + 2 * A_WIDTH:(g + 1) * width]
        q_t = jnp.concatenate([q_nat, zrows], axis=0).T
        if d == 1:
            packed = jnp.zeros((8, LANES), F32)
            res = []
            for i in range(T):
                pat = patterns(q_t, row_sq == i)
                s = head_scores(lambda h: c_ref[0, 0, h], lambda h: pat[h * HEAD_DIM:(h + 1) * HEAD_DIM])
                s = jnp.where(lane >= i, s, -jnp.inf)
                sn = jnp.where(lane <= i, head_rows(q_nat[i:i + 1, :] * kn_nat), -jnp.inf)
                m = jnp.maximum(jnp.max(s, axis=1, keepdims=True), jnp.max(sn, axis=1, keepdims=True))
                p = jnp.exp(s - m)
                pn = jnp.exp(sn - m)
                den = jnp.sum(p, axis=1, keepdims=True) + jnp.sum(pn, axis=1, keepdims=True)
                acc = jnp.concatenate([c_ref[0, 1, h] * p[h:h + 1, :] for h in range(A_HEADS)], axis=0)
                res.append(_nt_sel((sub == 0).astype(BF16), acc)[0:1, :])
                packed = packed + place(den, i) + place(m + jnp.log(den), 8 + i)
                packed = packed + pltpu.roll(pn, 16 + 8 * i, 1)
            sq = rows_of(packed)
            denx = widen(sq[0:8])
            lsex = widen(sq[8:16])
            for i in range(T):
                pnx = widen(sq[16 + 8 * i:24 + 8 * i])
                o = res[i] + jnp.sum(pnx[0:T] * vn_nat, axis=0, keepdims=True)
                out_refs[g][0, i:i + 1, :] = o / denx[i:i + 1, :]
            out_refs[A_GROUPS + g][0] = lsex[0:T]
        else:
            cls = lane & (d - 1)
            self_s = head_rows(q_nat * kn_nat)
            pat = patterns(q_t, (row_sq == (lane_sq & (d - 1))) & (row_sq < T))
            s_tiles = [head_scores(lambda h: c_ref[0, 0, h, :, t * LANES:(t + 1) * LANES],
                                   lambda h: pat[h * HEAD_DIM:(h + 1) * HEAD_DIM])
                       for t in range(n_tiles)]
            smax = s_tiles[0]
            for t in range(1, n_tiles):
                smax = jnp.maximum(smax, s_tiles[t])
            mrow = jnp.full((8, LANES), jnp.inf, F32)
            m_cls = []
            for i in range(T):
                mi = jnp.maximum(jnp.max(jnp.where(cls == i, smax, -jnp.inf), axis=1, keepdims=True),
                                 self_s[:, i:i + 1])
                m_cls.append(mi)
                mrow = jnp.where(cls == i, mi, mrow)
            p_tiles = [jnp.exp(s_tiles[t] - mrow) for t in range(n_tiles)]
            psum = p_tiles[0]
            for t in range(1, n_tiles):
                psum = psum + p_tiles[t]
            accs = []
            for h in range(A_HEADS):
                acc = jnp.zeros((HEAD_DIM, LANES), F32)
                for t in range(n_tiles):
                    acc = acc + c_ref[0, 1, h, :, t * LANES:(t + 1) * LANES] * p_tiles[t][h:h + 1, :]
                accs.append(acc)
            res = _nt_sel((cls == sub).astype(BF16), jnp.concatenate(accs, axis=0))
            packed = jnp.zeros((8, LANES), F32)
            for i in range(T):
                ps = jnp.exp(self_s[:, i:i + 1] - m_cls[i])
                den = jnp.sum(jnp.where(cls == i, psum, 0.0), axis=1, keepdims=True) + ps
                packed = packed + place(ps, i) + place(den, 8 + i) + place(m_cls[i] + jnp.log(den), 16 + i)
            sq = rows_of(packed)
            psx, denx, lsex = widen(sq[0:8]), widen(sq[8:16]), widen(sq[16:24])
            out_refs[g][0] = (res[0:T] + psx[0:T] * vn_nat) / denx[0:T]
            out_refs[A_GROUPS + g][0] = lsex[0:T]


def _a_sample(y, caches, Bd, T):
    width = A_GROUPS * 3 * A_WIDTH
    y3 = y.reshape(Bd, T, width)
    views = []
    specs = [pl.BlockSpec((1, T, width), lambda b: (b, 0, 0))]
    for g, c in enumerate(caches):
        d = A_DILATIONS[g]
        assert c.shape[1] == A_SPAN * d and (d == 1 or T <= d) and T <= 8
        views.append(c.transpose(0, 2, 3, 4, 1))
        specs.append(pl.BlockSpec((1, 2, A_HEADS, HEAD_DIM, A_SPAN * d), lambda b: (b, 0, 0, 0, 0)))
    hind_np = np.zeros((LANES, A_WIDTH), np.float32)
    hind_np[np.arange(A_WIDTH) // HEAD_DIM, np.arange(A_WIDTH)] = 1.0
    hind = jnp.asarray(hind_np[:A_HEADS], BF16)
    hindt = jnp.asarray(hind_np, BF16)
    specs += [pl.BlockSpec(hind.shape, lambda b: (0, 0)), pl.BlockSpec(hindt.shape, lambda b: (0, 0))]
    out_spec = pl.BlockSpec((1, T, A_WIDTH), lambda b: (b, 0, 0))
    outs = pl.pallas_call(
        functools.partial(_a_sample_kernel, T),
        grid=(Bd,),
        in_specs=specs,
        out_specs=[out_spec] * (2 * A_GROUPS),
        out_shape=[jax.ShapeDtypeStruct((Bd, T, A_WIDTH), F32)] * (2 * A_GROUPS),
        compiler_params=_params("parallel"),
        name="a_sample",
    )(y3, *views, hind, hindt)
    return [o.reshape(Bd * T, A_WIDTH) for o in outs]


def _mixer_a(xp, xs, gain, w_in, qk_gain, w_out, caches, tabs_p, tabs_s, Bn, S, Bd, T):
    flags = (True, True, False) * A_GROUPS
    ones = jnp.ones((A_WIDTH,), F32)
    rows = [jnp.tile(qk_gain[0], A_HEADS), jnp.tile(qk_gain[1], A_HEADS), ones] * A_GROUPS
    gain_rows = jnp.stack(rows).reshape(3 * A_GROUPS, 1, A_WIDTH)
    mask_rows = jnp.stack([ones, ones, 0.0 * ones] * A_GROUPS).reshape(3 * A_GROUPS, 1, A_WIDTH)
    w = w_in.astype(BF16)
    yp = _proj_qk(xp, gain, w, flags, gain_rows, mask_rows, tabs_p, 1024, A_WIDTH)
    ys = _proj_qk(xs, gain, w, flags, gain_rows, mask_rows, tabs_s, xs.shape[0], A_WIDTH)
    wo = w_out.astype(BF16)
    pr = [_a_prompt(yp, g, Bn, S) for g in range(A_GROUPS)]
    mix_p = (_mix_a, [(o, A_WIDTH, 0) for o, _ in pr] + [(l, A_WIDTH, 0) for _, l in pr], [wo])
    sr = _a_sample(ys, caches, Bd, T)
    mix_s = (_mix_a, [(o, A_WIDTH, 0) for o in sr], [wo])
    yp3 = yp.reshape(Bn, S, A_GROUPS * 3 * A_WIDTH)
    ys3 = ys.reshape(Bd, T, A_GROUPS * 3 * A_WIDTH)
    new_p, new_s = [], []
    for g in range(A_GROUPS):
        win = min(A_SPAN * A_DILATIONS[g], S)
        c0 = (3 * g + 1) * A_WIDTH
        new_p.append(yp3[:, S - win:, c0:c0 + 2 * A_WIDTH].reshape(Bn, win, 2, A_HEADS, HEAD_DIM))
        new_s.append(ys3[:, :, c0:c0 + 2 * A_WIDTH].reshape(Bd, T, 2, A_HEADS, HEAD_DIM))
    return mix_p, mix_s, new_p, new_s


def _b_kmean_kernel(k_ref, o_ref):
    nblk = k_ref.shape[1] // B_BLOCK
    rows = [jnp.mean(k_ref[0, n * B_BLOCK:(n + 1) * B_BLOCK, :], axis=0, keepdims=True) for n in range(nblk)]
    o_ref[0] = jnp.concatenate(rows, axis=0)


def _b_gate_kernel(q_ref, kmh_ref, kml_ref, o_ref):
    i = pl.program_id(1)
    qh, ql = _split2(q_ref[0])
    kmh, kml = kmh_ref[0], kml_ref[0]
    gate = _dot_nt(kmh, qh) + (_dot_nt(kmh, ql) + _dot_nt(kml, qh))
    tq = gate.shape[1]
    nblk = gate.shape[0] // B_HEADS
    gate = gate.reshape(B_HEADS, nblk, tq)
    blk = lax.broadcasted_iota(jnp.int32, (B_HEADS, nblk, tq), 1)
    blkf = blk.astype(F32)
    gate = jnp.where(blk < i, gate, -jnp.inf)
    sel = jnp.zeros((B_HEADS, nblk, tq), jnp.bool_)
    for _ in range(B_TOPK):
        mx = jnp.max(gate, axis=1, keepdims=True)
        first = jnp.min(jnp.where(gate == mx, blkf, float(nblk)), axis=1, keepdims=True)
        hit = blkf == first
        sel = sel | (hit & (mx > -jnp.inf))
        gate = jnp.where(hit, -jnp.inf, gate)
    bias = jnp.where(sel | (blk >= i), 0.0, NEG).reshape(B_HEADS * nblk, tq)
    o_ref[0] = bias.T


def _b_attn_kernel(qi_ref, kn_ref, q_ref, k_ref, vt_ref, o_ref, *state):
    t = pl.program_id(1)
    i, n = qi_ref[t], kn_ref[t]
    tq = B_BLOCK
    wide = B_REP * tq
    m_refs, acc_refs = state[:B_KV_HEADS], state[B_KV_HEADS:]

    @pl.when(n == 0)
    def _():
        for kvh in range(B_KV_HEADS):
            m_refs[kvh][...] = jnp.full_like(m_refs[kvh], -jnp.inf)
            acc_refs[kvh][...] = jnp.zeros_like(acc_refs[kvh])

    def sweep(causal):
        if causal:
            kk = lax.broadcasted_iota(jnp.int32, (B_BLOCK, wide), 0)
            qq = lax.broadcasted_iota(jnp.int32, (B_BLOCK, wide), 1) & (tq - 1)
            keep = kk <= qq
        scores = []
        for kvh in range(B_KV_HEADS):
            k = k_ref[0, :, kvh * LANES:(kvh + 1) * LANES]
            s = _dot_nt(k, q_ref[0, 0, kvh])
            scores.append(jnp.where(keep, s, NEG) if causal else s)
        probs, scales = [], []
        for kvh in range(B_KV_HEADS):
            m_old = m_refs[kvh][...]
            m_new = jnp.maximum(m_old, jnp.max(scores[kvh], axis=0, keepdims=True))
            probs.append(jnp.exp2(scores[kvh] - m_new).astype(BF16))
            scales.append(jnp.exp2(m_old - m_new))
            m_refs[kvh][...] = m_new
        for kvh in range(B_KV_HEADS):
            acc_refs[kvh][...] = scales[kvh] * acc_refs[kvh][...] + _dot(vt_ref[0, kvh], probs[kvh])

    @pl.when(n < i)
    def _():
        sweep(False)

    @pl.when(n == i)
    def _():
        sweep(True)
        for kvh in range(B_KV_HEADS):
            acc = acc_refs[kvh][...]
            ot = acc[0:HEAD_DIM] / acc[HEAD_DIM:HEAD_DIM + 1]
            for r in range(0, B_REP, 2):
                pair = [ot[:, (r + e) * tq:(r + e + 1) * tq].T for e in range(2)]
                c0 = (kvh * B_REP + r) * HEAD_DIM
                o_ref[0, :, c0:c0 + 2 * HEAD_DIM] = jnp.concatenate(pair, axis=1)


def _b_prompt(y, Bn, S):
    nblk = S // B_BLOCK
    assert nblk % 8 == 0 and LANES % nblk == 0
    nq = B_HEADS * HEAD_DIM
    nk = B_KV_HEADS * HEAD_DIM
    y3 = y.reshape(Bn, S, nq + 2 * nk)
    kmean = pl.pallas_call(
        _b_kmean_kernel,
        grid=(Bn,),
        in_specs=[pl.BlockSpec((1, S, nk), lambda b: (b, 0, nq // nk))],
        out_specs=pl.BlockSpec((1, nblk, nk), lambda b: (b, 0, 0)),
        out_shape=jax.ShapeDtypeStruct((Bn, nblk, nk), F32),
        compiler_params=_params("parallel"),
        name="b_kmean",
    )(y3)
    km = kmean.reshape(Bn, nblk, B_KV_HEADS, HEAD_DIM)
    km = jnp.repeat(km, B_REP, axis=2)
    eye = jnp.eye(B_HEADS, dtype=F32)
    kmbd = jnp.einsum('bnhe,hg->bgnhe', km, eye).reshape(Bn, B_HEADS * nblk, nq)
    kmh = kmbd.astype(BF16)
    kml = (kmbd - kmh.astype(F32)).astype(BF16)
    bias = pl.pallas_call(
        _b_gate_kernel,
        grid=(Bn, nblk),
        in_specs=[
            pl.BlockSpec((1, B_BLOCK, nq), lambda b, i: (b, i, 0)),
            pl.BlockSpec((1, B_HEADS * nblk, nq), lambda b, i: (b, 0, 0)),
            pl.BlockSpec((1, B_HEADS * nblk, nq), lambda b, i: (b, 0, 0)),
        ],
        out_specs=pl.BlockSpec((1, B_BLOCK, B_HEADS * nblk), lambda b, i: (b, i, 0)),
        out_shape=jax.ShapeDtypeStruct((Bn, S, B_HEADS * nblk), F32),
        compiler_params=_params("parallel", "arbitrary"),
        name="b_gate",
    )(y3, kmh, kml)
    q = (y3[:, :, :nq] * (HEAD_DIM ** -0.5 * LOG2E)).astype(BF16).reshape(Bn, S, B_HEADS, HEAD_DIM)
    pad = LANES - HEAD_DIM - nblk
    q_aug = jnp.concatenate([q, bias.astype(BF16).reshape(Bn, S, B_HEADS, nblk),
                             jnp.zeros((Bn, S, B_HEADS, pad), BF16)], axis=-1)
    q_aug = q_aug.reshape(Bn, nblk, B_BLOCK, B_KV_HEADS, B_REP, LANES).transpose(0, 1, 3, 4, 2, 5)
    q_aug = q_aug.reshape(Bn, nblk, B_KV_HEADS, B_REP * B_BLOCK, LANES)
    k = y3[:, :, nq:nq + nk].astype(BF16).reshape(Bn, S, B_KV_HEADS, HEAD_DIM)
    onehot = jax.nn.one_hot(jnp.arange(S) // B_BLOCK, nblk, dtype=BF16)
    onehot = jnp.broadcast_to(onehot[None, :, None, :], (Bn, S, B_KV_HEADS, nblk))
    k_aug = jnp.concatenate([k, onehot, jnp.zeros((Bn, S, B_KV_HEADS, pad), BF16)], axis=-1)
    k_aug = k_aug.reshape(Bn, S, B_KV_HEADS * LANES)
    vt = y3[:, :, nq + nk:].astype(BF16).reshape(Bn, S, B_KV_HEADS, HEAD_DIM).transpose(0, 2, 3, 1)
    vt = jnp.concatenate([vt, jnp.ones((Bn, B_KV_HEADS, 1, S), BF16),
                          jnp.zeros((Bn, B_KV_HEADS, B_VT_ROWS - HEAD_DIM - 1, S), BF16)], axis=2)
    pairs = [(i, n) for i in range(nblk) for n in range(i + 1)]
    qi = jnp.asarray([p[0] for p in pairs], jnp.int32)
    kn = jnp.asarray([p[1] for p in pairs], jnp.int32)
    o = pl.pallas_call(
        _b_attn_kernel,
        grid_spec=pltpu.PrefetchScalarGridSpec(
            num_scalar_prefetch=2,
            grid=(Bn, len(pairs)),
            in_specs=[
                pl.BlockSpec((1, 1, B_KV_HEADS, B_REP * B_BLOCK, LANES),
                             lambda b, t, qi, kn: (b, qi[t], 0, 0, 0)),
                pl.BlockSpec((1, B_BLOCK, B_KV_HEADS * LANES), lambda b, t, qi, kn: (b, kn[t], 0)),
                pl.BlockSpec((1, B_KV_HEADS, B_VT_ROWS, B_BLOCK), lambda b, t, qi, kn: (b, 0, 0, kn[t])),
            ],
            out_specs=pl.BlockSpec((1, B_BLOCK, nq), lambda b, t, qi, kn: (b, qi[t], 0)),
            scratch_shapes=([pltpu.VMEM((1, B_REP * B_BLOCK), F32)] * B_KV_HEADS
                            + [pltpu.VMEM((B_VT_ROWS, B_REP * B_BLOCK), F32)] * B_KV_HEADS),
        ),
        out_shape=jax.ShapeDtypeStruct((Bn, S, nq), F32),
        compiler_params=_params("parallel", "arbitrary"),
        name="b_attn",
    )(qi, kn, q_aug, k_aug, vt)
    return o.reshape(Bn * S, nq)


def _b_sample_kernel(n_pages, T, pt_ref, q_ref, kn_ref, vn_ref, *refs):
    del pt_ref
    page_refs = refs[:n_pages]
    o_ref = refs[n_pages]
    nk = B_KV_HEADS * HEAD_DIM
    q = q_ref[0]
    qh, ql = _split2(q)
    rows = q.shape[0]
    page_rows = page_refs[0].shape[2]
    pages_per_block = B_BLOCK // page_rows
    n_blocks = n_pages // pages_per_block
    scale = HEAD_DIM ** -0.5
    lane = lax.broadcasted_iota(jnp.int32, (1, LANES), 1)
    lanef = lax.broadcasted_iota(jnp.int32, (rows, LANES), 1).astype(F32)
    scores = []
    km = jnp.zeros((nk, LANES), F32)
    for p in range(n_pages):
        kt = page_refs[p][0, :nk, :]
        scores.append(_dot(qh, kt.astype(BF16)) * scale)
        ksum = jnp.sum(kt, axis=1, keepdims=True) * (1.0 / B_BLOCK)
        km = km + ksum * (lane == p // pages_per_block).astype(F32)
    kmh, kml = _split2(km)
    gate = _dot(qh, kmh) + (_dot(qh, kml) + _dot(ql, kmh))
    gate = jnp.where(lanef < float(n_blocks), gate, -jnp.inf)
    sel = jnp.zeros((rows, LANES), jnp.bool_)
    for _ in range(min(B_TOPK, n_blocks)):
        mx = jnp.max(gate, axis=1, keepdims=True)
        first = jnp.min(jnp.where(gate == mx, lanef, float(LANES)), axis=1, keepdims=True)
        hit = lanef == first
        sel = sel | (hit & (mx > -jnp.inf))
        gate = jnp.where(hit, -jnp.inf, gate)
    self = sel.astype(F32)
    qidx = lax.broadcasted_iota(jnp.int32, (rows, 1), 0) % T
    kn, vn = kn_ref[0], vn_ref[0]
    own = []
    for j in range(T):
        sj = jnp.sum(q * kn[j:j + 1, :], axis=1, keepdims=True) * scale
        own.append(jnp.where(qidx >= j, sj, NEG))
    m = own[0]
    for j in range(1, T):
        m = jnp.maximum(m, own[j])
    for p in range(n_pages):
        b = p // pages_per_block
        scores[p] = jnp.where(self[:, b:b + 1] > 0.0, scores[p], NEG)
        m = jnp.maximum(m, jnp.max(scores[p], axis=-1, keepdims=True))
    den = jnp.zeros((rows, 1), F32)
    acc = jnp.zeros((rows, nk), F32)
    for j in range(T):
        pj = jnp.exp(own[j] - m)
        den = den + pj
        acc = acc + pj * vn[j:j + 1, :]
    for p in range(n_pages):
        pp = jnp.exp(scores[p] - m)
        den = den + jnp.sum(pp, axis=-1, keepdims=True)
        acc = acc + _dot_nt(pp.astype(BF16), page_refs[p][0, nk:, :].astype(BF16))
    acc = acc / den
    kvh = lax.broadcasted_iota(jnp.int32, (rows, 1), 0) // (T * B_REP)
    out = jnp.zeros((rows, HEAD_DIM), F32)
    for h in range(B_KV_HEADS):
        out = out + jnp.where(kvh == h, acc[:, h * HEAD_DIM:(h + 1) * HEAD_DIM], 0.0)
    o_ref[0] = out


def _b_sample(y, pool, page_table, Bd, T):
    nq = B_HEADS * HEAD_DIM
    nk = B_KV_HEADS * HEAD_DIM
    n_pages = page_table.shape[1]
    page_rows = pool.shape[1]
    assert B_BLOCK % page_rows == 0 and (n_pages * page_rows) % B_BLOCK == 0 and T <= B_BLOCK
    assert n_pages * page_rows // B_BLOCK <= LANES
    y3 = y.reshape(Bd, T, nq + 2 * nk)
    q = y3[:, :, :nq].reshape(Bd, T, B_HEADS, HEAD_DIM).transpose(0, 2, 1, 3)
    kvsel = jnp.asarray(np.kron(np.eye(B_KV_HEADS), np.ones((B_REP, 1))), F32)
    qbd = (q[:, :, :, None, :] * kvsel[None, :, None, :, None]).reshape(Bd, B_HEADS * T, nk)
    kn = y3[:, :, nq:nq + nk]
    vn = y3[:, :, nq + nk:]
    pool_t = pool.transpose(0, 2, 3, 4, 1).reshape(pool.shape[0], 2 * nk, page_rows)
    page_specs = [pl.BlockSpec((1, 2 * nk, page_rows), lambda b, pt, p=p: (pt[b, p], 0, 0))
                  for p in range(n_pages)]
    o = pl.pallas_call(
        functools.partial(_b_sample_kernel, n_pages, T),
        grid_spec=pltpu.PrefetchScalarGridSpec(
            num_scalar_prefetch=1,
            grid=(Bd,),
            in_specs=[
                pl.BlockSpec((1, B_HEADS * T, nk), lambda b, pt: (b, 0, 0)),
                pl.BlockSpec((1, T, nk), lambda b, pt: (b, 0, 0)),
                pl.BlockSpec((1, T, nk), lambda b, pt: (b, 0, 0)),
            ] + page_specs,
            out_specs=pl.BlockSpec((1, B_HEADS * T, HEAD_DIM), lambda b, pt: (b, 0, 0)),
        ),
        out_shape=jax.ShapeDtypeStruct((Bd, B_HEADS * T, HEAD_DIM), F32),
        compiler_params=_params("parallel"),
        name="b_sample",
    )(page_table, qbd, kn, vn, *([pool_t] * n_pages))
    return o.reshape(Bd, B_HEADS, T, HEAD_DIM).transpose(0, 2, 1, 3).reshape(Bd * T, nq)


def _mixer_b(xp, xs, gain, w_in, qk_gain, w_out, pool, page_table, tabs_p, tabs_s, Bn, S, Bd, T):
    tn = 512
    nq = B_HEADS * HEAD_DIM
    nk = B_KV_HEADS * HEAD_DIM
    flags = (True, True, True)
    ones = jnp.ones((nk,), F32)
    qg = jnp.tile(qk_gain[0], tn // HEAD_DIM)
    kg = jnp.concatenate([jnp.tile(qk_gain[1], B_KV_HEADS), ones])
    gain_rows = jnp.stack([qg, qg, kg]).reshape(3, 1, tn)
    mask_rows = jnp.stack([jnp.ones((tn,), F32), jnp.ones((tn,), F32),
                           jnp.concatenate([ones, 0.0 * ones])]).reshape(3, 1, tn)
    w = w_in.astype(BF16)
    yp = _proj_qk(xp, gain, w, flags, gain_rows, mask_rows, tabs_p, 1024, tn)
    ys = _proj_qk(xs, gain, w, flags, gain_rows, mask_rows, tabs_s, xs.shape[0], tn)
    wo = w_out.astype(BF16)
    op = _b_prompt(yp, Bn, S)
    mix_p = (_mix_plain, [(op, nq, 0)], [wo])
    osm = _b_sample(ys, pool, page_table, Bd, T)
    mix_s = (_mix_plain, [(osm, nq, 0)], [wo])
    kv_p = yp[:, nq:].reshape(Bn, S, 2, B_KV_HEADS, HEAD_DIM)
    kv_s = ys[:, nq:].reshape(Bd, T, 2, B_KV_HEADS, HEAD_DIM)
    return mix_p, mix_s, kv_p, kv_s


def _gla_kernel(nv, q_ref, k_ref, v_ref, glr_ref, wg_ref, bg_ref, tri_ref, blk_ref, s0_ref,
                o_ref, sT_ref, s_ref, qp_ref, kp_ref, vp_ref, gp_ref):
    t = pl.program_id(1)
    Tt = SEQ_TILE

    @pl.when(t == 0)
    def _():
        s_ref[...] = s0_ref[0]

    if nv < Tt:
        qp_ref[...] = jnp.zeros_like(qp_ref)
        kp_ref[...] = jnp.zeros_like(kp_ref)
        vp_ref[...] = jnp.zeros_like(vp_ref)
        gp_ref[...] = jnp.zeros_like(gp_ref)
    qp_ref[0:nv, :] = q_ref[0]
    kp_ref[0:nv, :] = k_ref[0]
    vp_ref[0:nv, :] = v_ref[0]
    gp_ref[0:nv, :] = glr_ref[0]
    q, k, v = qp_ref[...], kp_ref[...], vp_ref[...]

    x = _dot(gp_ref[...].astype(BF16), wg_ref[...]) + bg_ref[...]
    la = (jnp.minimum(x, 0.0) - jnp.log(1.0 + jnp.exp(-jnp.abs(x)))) * (1.0 / C_TAU)
    row = lax.broadcasted_iota(jnp.int32, (Tt, C_KEY), 0)
    la = jnp.where(row < nv, la, 0.0)
    b = _sel_left(tri_ref[...], la, terms=3)
    bend = _sel_left(blk_ref[...], la, terms=3)
    qe = q * (C_DK ** -0.5) * jnp.exp(b)
    ke = k * jnp.exp(-b)
    kd = k * jnp.exp(bend - b)
    kdt = kd.T
    bendt = bend.T
    qeb, keb, vb = qe.astype(BF16), ke.astype(BF16), v.astype(BF16)
    causal = tri_ref[...] > 0
    lane_t = lax.broadcasted_iota(jnp.int32, (C_DK, Tt), 1)
    n_chunks = -(-nv // C_CHUNK)
    ksl = [slice(h * C_DK, (h + 1) * C_DK) for h in range(C_HEADS)]
    vsl = [slice(h * C_DV, (h + 1) * C_DV) for h in range(C_HEADS)]
    att = [jnp.where(causal, _dot_nt(qeb[:, ks], keb[:, ks]), 0.0).astype(BF16) for ks in ksl]
    chunk_kv, chunk_dec = [], []
    for c in range(n_chunks):
        in_chunk = (lane_t >= c * C_CHUNK) & (lane_t < (c + 1) * C_CHUNK)
        last = lane_t == (c + 1) * C_CHUNK - 1
        chunk_kv.append([_dot(jnp.where(in_chunk, kdt[ks, :], 0.0).astype(BF16), vb[:, vs])
                         for ks, vs in zip(ksl, vsl)])
        chunk_dec.append([jnp.exp(jnp.sum(jnp.where(last, bendt[ks, :], 0.0), axis=1, keepdims=True))
                          for ks in ksl])
    o_intra = [_dot(att[h], vb[:, vsl[h]]) for h in range(C_HEADS)]
    for h in range(C_HEADS):
        st = s_ref[h]
        parts = []
        for c in range(n_chunks):
            rows = slice(c * C_CHUNK, (c + 1) * C_CHUNK)
            parts.append(o_intra[h][rows] + _dot(qeb[rows, ksl[h]], st.astype(BF16)))
            st = chunk_dec[c][h] * st + chunk_kv[c][h]
        s_ref[h] = st
        oh = parts[0] if n_chunks == 1 else jnp.concatenate(parts, axis=0)
        o_ref[0, :, vsl[h]] = oh[0:nv]

    @pl.when(t == pl.num_programs(1) - 1)
    def _():
        sT_ref[0] = s_ref[...]


def _gla(y, w_gate2, b_gate, s0, nb, nt, nv):
    Tt = SEQ_TILE
    y3 = y.reshape(nb * nt, nv, C_NPAD)
    wg = jnp.zeros((LANES, C_KEY), F32).at[:C_RANK].set(w_gate2).astype(BF16)
    idx = np.arange(Tt)
    same = (idx[:, None] // C_CHUNK) == (idx[None, :] // C_CHUNK)
    tri = jnp.asarray(same & (idx[None, :] <= idx[:, None]), BF16)
    blk = jnp.asarray(same, BF16)

    def yspec(width, cb):
        return pl.BlockSpec((1, nv, width), lambda b, t: (b * nt + t, 0, cb))

    def cspec(a):
        return pl.BlockSpec(a.shape, lambda b, t, nd=a.ndim: (0,) * nd)

    bg = b_gate.reshape(1, C_KEY)
    o, sT = pl.pallas_call(
        functools.partial(_gla_kernel, nv),
        grid=(nb, nt),
        in_specs=[
            yspec(C_KEY, 0), yspec(C_KEY, 1), yspec(C_VAL, 1), yspec(LANES, (2 * C_KEY + 2 * C_VAL) // LANES),
            cspec(wg), cspec(bg), cspec(tri), cspec(blk),
            pl.BlockSpec((1, C_HEADS, C_DK, C_DV), lambda b, t: (b, 0, 0, 0)),
        ],
        out_specs=[
            pl.BlockSpec((1, nv, C_VAL), lambda b, t: (b * nt + t, 0, 0)),
            pl.BlockSpec((1, C_HEADS, C_DK, C_DV), lambda b, t: (b, 0, 0, 0)),
        ],
        out_shape=[
            jax.ShapeDtypeStruct((nb * nt, nv, C_VAL), F32),
            jax.ShapeDtypeStruct((nb, C_HEADS, C_DK, C_DV), F32),
        ],
        scratch_shapes=[
            pltpu.VMEM((C_HEADS, C_DK, C_DV), F32),
            pltpu.VMEM((Tt, C_KEY), F32), pltpu.VMEM((Tt, C_KEY), F32),
            pltpu.VMEM((Tt, C_VAL), F32), pltpu.VMEM((Tt, LANES), F32),
        ],
        compiler_params=_params("parallel", "arbitrary"),
        name="gla",
    )(y3, y3, y3, y3, wg, bg, tri, blk, s0)
    return o.reshape(nb * nt * nv, C_VAL), sT


def _mixer_c(xp, xs, gain, w_in, w_gate2, b_gate, norm_g, w_out, state, Bn, S, Bd, T):
    n_in = w_in.shape[1]
    w = jnp.zeros((D_MODEL, C_NPAD), BF16).at[:, :n_in].set(w_in.astype(BF16))
    yp = _proj(xp, gain, w, 1024)
    ys = _proj(xs, gain, w, xs.shape[0])
    zero = jnp.zeros((Bn, C_HEADS, C_DK, C_DV), F32)
    op, sp = _gla(yp, w_gate2, b_gate, zero, Bn, S // SEQ_TILE, SEQ_TILE)
    osm, ss = _gla(ys, w_gate2, b_gate, state, Bd, 1, T)
    wo = w_out.astype(BF16)
    ng = norm_g.reshape(1, C_DV)
    rcb = (2 * C_KEY + C_VAL) // C_VAL
    mix_p = (_mix_c, [(op, C_VAL, 0), (yp, C_VAL, rcb)], [ng, wo])
    mix_s = (_mix_c, [(osm, C_VAL, 0), (ys, C_VAL, rcb)], [ng, wo])
    return mix_p, mix_s, sp, ss


def _ssd_kernel(nv, xa_ref, xb_ref, bc_ref, dt_ref, cs_ref, cw_ref, cb_ref, dtb_ref, alog_ref, dsk_ref,
                tri_ref, exp_ref, expt_ref, h0_ref, y_ref, hT_ref, h_ref, xp_ref, dtp_ref):
    t = pl.program_id(1)
    Tt = SEQ_TILE
    pre = 8

    @pl.when(t == 0)
    def _():
        h_ref[...] = h0_ref[0]
        xp_ref[0:pre, :] = cs_ref[0]

    if nv < Tt:
        xp_ref[pre:, :] = jnp.zeros((Tt, D_XBC), F32)
        dtp_ref[...] = jnp.zeros_like(dtp_ref)
    xp_ref[pre:pre + nv, 0:1024] = xa_ref[0]
    xp_ref[pre:pre + nv, 1024:2048] = xb_ref[0]
    xp_ref[pre:pre + nv, 2048:3072] = bc_ref[0]
    dtp_ref[0:nv, :] = dt_ref[0]

    full = xp_ref[...]
    conv = cb_ref[...] + full[pre:] * cw_ref[D_CONV - 1:D_CONV, :]
    for k in range(1, D_CONV):
        conv = conv + pltpu.roll(full, k, 0)[pre:] * cw_ref[D_CONV - 1 - k:D_CONV - k, :]
    xp_ref[0:pre, :] = xp_ref[Tt:Tt + pre, :]
    xbc = _silu(conv)
    x = xbc[:, :D_INNER]
    nbc = D_GROUPS * D_STATE
    bm = xbc[:, D_INNER:D_INNER + nbc].astype(BF16)
    cm = xbc[:, D_INNER + nbc:].astype(BF16)

    row = lax.broadcasted_iota(jnp.int32, (Tt, LANES), 0)
    lane = lax.broadcasted_iota(jnp.int32, (Tt, LANES), 1)
    live = (row < nv) & (lane < D_HEADS)
    dt = jnp.where(live, _softplus(dtp_ref[...] + dtb_ref[...]), 0.0)
    cum = _sel_left(tri_ref[...], dt * (-jnp.exp(alog_ref[...])), terms=3)
    cumt = cum.T
    cend = cum[Tt - 1:Tt, :]
    ex = exp_ref[...]
    stack = jnp.concatenate([jnp.exp(cend - cum) * dt, dt, jnp.exp(cum)], axis=0)
    wide = _sel_right(stack, ex)
    x_state = (x * wide[0:Tt]).astype(BF16)
    x_dt = (x * wide[Tt:2 * Tt]).astype(BF16)
    off_scale = wide[2 * Tt:]

    causal = tri_ref[...] > 0
    gw = D_INNER // D_GROUPS
    cbs, offs = [], []
    for g in range(D_GROUPS):
        cg = cm[:, g * D_STATE:(g + 1) * D_STATE]
        cbs.append(_dot_nt(cg, bm[:, g * D_STATE:(g + 1) * D_STATE]))
        offs.append(_dot_nt(cg, h_ref[g * gw:(g + 1) * gw, :].astype(BF16)))
    mats = []
    for j in range(D_HEADS):
        seg = cum[:, j:j + 1] - cumt[j:j + 1, :]
        mats.append((cbs[j // D_HPG] * jnp.exp(jnp.where(causal, seg, -jnp.inf))).astype(BF16))
    diag = [_dot(mats[j], x_dt[:, j * D_HEADDIM:(j + 1) * D_HEADDIM]) for j in range(D_HEADS)]
    y = jnp.concatenate(diag, axis=1) + jnp.concatenate(offs, axis=1) * off_scale + dsk_ref[...] * x
    y_ref[0] = y[0:nv]

    dcol = jnp.broadcast_to(jnp.exp(cumt[:, Tt - 1:Tt]), (LANES, D_STATE))
    dfull = _sel_left(expt_ref[...], dcol)
    xst = x_state.astype(F32).T.astype(BF16)
    for g in range(D_GROUPS):
        rows = slice(g * gw, (g + 1) * gw)
        h_ref[rows, :] = dfull[rows] * h_ref[rows, :] + _dot(xst[rows], bm[:, g * D_STATE:(g + 1) * D_STATE])

    @pl.when(t == pl.num_programs(1) - 1)
    def _():
        hT_ref[0] = h_ref[...]


def _ssd(y, conv_w, conv_b, dt_bias, a_log, d_skip, h0, c0, nb, nt, nv):
    Tt = SEQ_TILE
    y3 = y.reshape(nb * nt, nv, D_NPAD)
    idx = np.arange(Tt)
    tri = jnp.asarray(idx[None, :] <= idx[:, None], BF16)
    ex_np = np.zeros((LANES, D_INNER), np.float32)
    ex_np[np.arange(D_INNER) // D_HEADDIM, np.arange(D_INNER)] = 1.0
    ex = jnp.asarray(ex_np, BF16)
    ext = jnp.asarray(ex_np.T, BF16)

    def pad_row(v):
        return jnp.zeros((1, LANES), F32).at[0, :D_HEADS].set(v)

    cs = jnp.zeros((nb, 8, D_XBC), F32).at[:, 8 - (D_CONV - 1):].set(c0)
    dsk = jnp.repeat(d_skip, D_HEADDIM).reshape(1, D_INNER)

    def yspec(width, cb):
        return pl.BlockSpec((1, nv, width), lambda b, t: (b * nt + t, 0, cb))

    def cspec(a):
        return pl.BlockSpec(a.shape, lambda b, t, nd=a.ndim: (0,) * nd)

    consts = [conv_w, conv_b.reshape(1, D_XBC), pad_row(dt_bias), pad_row(a_log), dsk, tri, ex, ext]
    yo, hT = pl.pallas_call(
        functools.partial(_ssd_kernel, nv),
        grid=(nb, nt),
        in_specs=[yspec(1024, 2), yspec(1024, 3), yspec(1024, 4), yspec(LANES, D_DT_COL),
                  pl.BlockSpec((1, 8, D_XBC), lambda b, t: (b, 0, 0))]
        + [cspec(a) for a in consts]
        + [pl.BlockSpec((1, D_INNER, D_STATE), lambda b, t: (b, 0, 0))],
        out_specs=[
            pl.BlockSpec((1, nv, D_INNER), lambda b, t: (b * nt + t, 0, 0)),
            pl.BlockSpec((1, D_INNER, D_STATE), lambda b, t: (b, 0, 0)),
        ],
        out_shape=[
            jax.ShapeDtypeStruct((nb * nt, nv, D_INNER), F32),
            jax.ShapeDtypeStruct((nb, D_INNER, D_STATE), F32),
        ],
        scratch_shapes=[
            pltpu.VMEM((D_INNER, D_STATE), F32),
            pltpu.VMEM((Tt + 8, D_XBC), F32),
            pltpu.VMEM((Tt, LANES), F32),
        ],
        compiler_params=_params("parallel", "arbitrary"),
        name="ssd",
    )(y3, y3, y3, y3, cs, *consts, h0)
    return yo.reshape(nb * nt * nv, D_INNER), hT


def _ssd_step_kernel(nv, xa_ref, xb_ref, bc_ref, dt_ref, cs_ref, cw_ref, cb_ref, dtb_ref, alog_ref, dsk_ref,
                     eye_ref, h0_ref, y_ref, hT_ref, xp_ref):
    R = 8
    pre = 8
    xp_ref[0:pre, :] = cs_ref[0]
    xp_ref[pre:, :] = jnp.zeros((R, D_XBC), F32)
    xp_ref[pre:pre + nv, 0:1024] = xa_ref[0]
    xp_ref[pre:pre + nv, 1024:2048] = xb_ref[0]
    xp_ref[pre:pre + nv, 2048:3072] = bc_ref[0]
    conv = cb_ref[...]
    for w in range(D_CONV):
        conv = conv + xp_ref[pl.ds(pre - (D_CONV - 1) + w, R), :] * cw_ref[w:w + 1, :]
    xbc = _silu(conv)
    x = xbc[:, :D_INNER]
    nbc = D_GROUPS * D_STATE
    bmf = xbc[:, D_INNER:D_INNER + nbc]
    cmf = xbc[:, D_INNER + nbc:]

    row = lax.broadcasted_iota(jnp.int32, (R, LANES), 0)
    lane = lax.broadcasted_iota(jnp.int32, (R, LANES), 1)
    dtp = jnp.concatenate([dt_ref[0], jnp.zeros((R - nv, LANES), F32)], axis=0)
    dt = jnp.where((row < nv) & (lane < D_HEADS), _softplus(dtp + dtb_ref[...]), 0.0)
    cum = dt * (-jnp.exp(alog_ref[...]))
    for k in (1, 2, 4):
        cum = cum + jnp.where(row >= k, pltpu.roll(cum, k, 0), 0.0)
    cend = cum[R - 1:R, :]
    group_of_lane = lane // D_HPG
    low_half = lane < D_HEADDIM

    def widen(a):
        pairs = [jnp.where(low_half, a[:, j:j + 1], a[:, j + 1:j + 2]) for j in range(0, D_HEADS, 2)]
        return jnp.concatenate(pairs, axis=1)

    yd = jnp.zeros((R, D_INNER), F32)
    for s in range(nv):
        w_s = jnp.where(row >= s, jnp.exp(cum - cum[s:s + 1, :]), 0.0) * dt[s:s + 1, :]
        prod = cmf * bmf[s:s + 1, :]
        cb_s = jnp.zeros((R, LANES), F32)
        for g in range(D_GROUPS):
            tot = jnp.sum(prod[:, g * D_STATE:(g + 1) * D_STATE], axis=1, keepdims=True)
            cb_s = jnp.where(group_of_lane == g, tot, cb_s)
        yd = yd + widen(cb_s * w_s) * x[s:s + 1, :]
    off_scale = widen(jnp.exp(cum))
    x_state = (x * widen(jnp.exp(cend - cum) * dt)).astype(BF16)

    gw = D_INNER // D_GROUPS
    cm = cmf.astype(BF16)
    parts = []
    for g in range(D_GROUPS):
        hg = h0_ref[0, g * gw:(g + 1) * gw, :]
        parts.append(_dot_nt(cm[:, g * D_STATE:(g + 1) * D_STATE], hg.astype(BF16)))
    y = yd + jnp.concatenate(parts, axis=1) * off_scale + dsk_ref[...] * x
    y_ref[0] = y[0:nv]

    erow = jnp.broadcast_to(jnp.exp(cend), (LANES, LANES))
    eye128 = (lax.broadcasted_iota(jnp.int32, (LANES, LANES), 0)
              == lax.broadcasted_iota(jnp.int32, (LANES, LANES), 1)).astype(BF16)
    dcol = _nt_sel(eye128, erow)[0:D_HEADS].reshape(D_HEADS, 1, D_STATE)
    zpad = jnp.zeros((LANES - R, D_INNER), BF16)
    xs_pad = jnp.concatenate([x_state, zpad], axis=0)
    bm_pad = jnp.concatenate([bmf.astype(BF16), jnp.zeros((LANES - R, nbc), BF16)], axis=0)
    for g in range(D_GROUPS):
        rows = slice(g * gw, (g + 1) * gw)
        xt = _dot_nt(eye_ref[...], xs_pad[:, rows]).astype(BF16)
        kept = (h0_ref[0, rows, :].reshape(D_HPG, D_HEADDIM, D_STATE) * dcol[g * D_HPG:(g + 1) * D_HPG])
        hT_ref[0, rows, :] = kept.reshape(gw, D_STATE) + _dot(xt, bm_pad[:, g * D_STATE:(g + 1) * D_STATE])


def _ssd_step(y, conv_w, conv_b, dt_bias, a_log, d_skip, h0, c0, nb, nv):
    assert nv <= 8
    y3 = y.reshape(nb, nv, D_NPAD)
    gw = D_INNER // D_GROUPS
    eye = jnp.asarray(np.eye(gw), BF16)

    def pad_row(v):
        return jnp.zeros((1, LANES), F32).at[0, :D_HEADS].set(v)

    cs = jnp.zeros((nb, 8, D_XBC), F32).at[:, 8 - (D_CONV - 1):].set(c0)
    dsk = jnp.repeat(d_skip, D_HEADDIM).reshape(1, D_INNER)

    def yspec(width, cb):
        return pl.BlockSpec((1, nv, width), lambda b: (b, 0, cb))

    def cspec(a):
        return pl.BlockSpec(a.shape, lambda b, nd=a.ndim: (0,) * nd)

    consts = [conv_w, conv_b.reshape(1, D_XBC), pad_row(dt_bias), pad_row(a_log), dsk, eye]
    yo, hT = pl.pallas_call(
        functools.partial(_ssd_step_kernel, nv),
        grid=(nb,),
        in_specs=[yspec(1024, 2), yspec(1024, 3), yspec(1024, 4), yspec(LANES, D_DT_COL),
                  pl.BlockSpec((1, 8, D_XBC), lambda b: (b, 0, 0))]
        + [cspec(a) for a in consts]
        + [pl.BlockSpec((1, D_INNER, D_STATE), lambda b: (b, 0, 0))],
        out_specs=[
            pl.BlockSpec((1, nv, D_INNER), lambda b: (b, 0, 0)),
            pl.BlockSpec((1, D_INNER, D_STATE), lambda b: (b, 0, 0)),
        ],
        out_shape=[
            jax.ShapeDtypeStruct((nb, nv, D_INNER), F32),
            jax.ShapeDtypeStruct((nb, D_INNER, D_STATE), F32),
        ],
        scratch_shapes=[pltpu.VMEM((16, D_XBC), F32)],
        compiler_params=_params("parallel"),
        name="ssd_step",
    )(y3, y3, y3, y3, cs, *consts, h0)
    return yo.reshape(nb * nv, D_INNER), hT


def _mixer_d(xp, xs, gain, w_in, conv_w, conv_b, dt_bias, a_log, d_skip, norm_g, w_out,
             ssm_state, conv_state, Bn, S, Bd, T):
    n_in = w_in.shape[1]
    w = jnp.zeros((D_MODEL, D_NPAD), BF16).at[:, :n_in].set(w_in.astype(BF16))
    yp = _proj(xp, gain, w, 1024)
    ys = _proj(xs, gain, w, xs.shape[0])
    h0p = jnp.zeros((Bn, D_INNER, D_STATE), F32)
    c0p = jnp.zeros((Bn, D_CONV - 1, D_XBC), F32)
    op, hp = _ssd(yp, conv_w, conv_b, dt_bias, a_log, d_skip, h0p, c0p, Bn, S // SEQ_TILE, SEQ_TILE)
    h0s = ssm_state.reshape(Bd, D_INNER, D_STATE)
    osm, hs = _ssd_step(ys, conv_w, conv_b, dt_bias, a_log, d_skip, h0s, conv_state, Bd, T)
    wo = w_out.astype(BF16)
    ng = norm_g.reshape(1, D_INNER)
    mix_p = (_mix_d, [(op, D_INNER, 0), (yp, D_INNER, 0)], [ng, wo])
    mix_s = (_mix_d, [(osm, D_INNER, 0), (ys, D_INNER, 0)], [ng, wo])
    keep = D_CONV - 1
    xbc_p = yp.reshape(Bn, S, D_NPAD)[:, :, D_INNER:D_INNER + D_XBC]
    xbc_s = ys.reshape(Bd, T, D_NPAD)[:, :, D_INNER:D_INNER + D_XBC]
    cp = jnp.concatenate([c0p, xbc_p], axis=1)[:, -keep:] if S < keep else xbc_p[:, S - keep:]
    cs = jnp.concatenate([conv_state, xbc_s], axis=1)[:, -keep:]
    return (mix_p, mix_s, hp.reshape(Bn, D_HEADS, D_HEADDIM, D_STATE), hs.reshape(Bd, D_HEADS, D_HEADDIM, D_STATE),
            cp, cs)


def kernel(x_prompt, x_sample, cache_a_w1, cache_a_w2, cache_a_w3, cache_b_kv, page_table, state_c, state_d_ssm, state_d_conv, norm_gain, w_ffn_up, w_ffn_down, w_a_in, a_qk_gain, w_a_out, w_b_in, b_qk_gain, w_b_out, w_c_in, w_c_gate2, b_c_gate, c_norm_gain, w_c_out, w_d_in, d_conv_w, d_conv_b, d_dt_bias, d_a_log, d_skip, d_norm_gain, w_d_out):
    Bn, S, _ = x_prompt.shape
    Bd, T, _ = x_sample.shape
    depth = norm_gain.shape[0]
    past_len = page_table.shape[1] * cache_b_kv.shape[2]
    tabs_p = _rope_tables(jnp.arange(S, dtype=jnp.int32))
    tabs_s = _rope_tables(jnp.tile(past_len + jnp.arange(T, dtype=jnp.int32), Bd))
    xp = x_prompt.reshape(Bn * S, D_MODEL)
    xs = x_sample.reshape(Bd * T, D_MODEL)
    ts = xs.shape[0]
    w_up = w_ffn_up.astype(BF16)
    w_down = w_ffn_down.astype(BF16)
    outs = {k: [] for k in ("a0p", "a0s", "a1p", "a1s", "a2p", "a2s", "bp", "bs", "cp", "cs",
                            "hp", "hs", "dp", "ds")}
    for i in range(depth):
        m, j = i % 4, i // 4
        g = norm_gain[i]
        xp = _ffn(xp, g[0], w_up, w_down, i, 0, 1024)
        xs = _ffn(xs, g[0], w_up, w_down, i, 0, ts)
        if m == 0:
            mix_p, mix_s, new_p, new_s = _mixer_a(xp, xs, g[1], w_a_in[j], a_qk_gain[j], w_a_out[j],
                                            (cache_a_w1[j], cache_a_w2[j], cache_a_w3[j]),
                                            tabs_p, tabs_s, Bn, S, Bd, T)
            for gi in range(A_GROUPS):
                outs["a%dp" % gi].append(new_p[gi])
                outs["a%ds" % gi].append(new_s[gi])
        elif m == 1:
            mix_p, mix_s, kvp, kvs = _mixer_b(xp, xs, g[1], w_b_in[j], b_qk_gain[j], w_b_out[j],
                                        cache_b_kv[j], page_table, tabs_p, tabs_s, Bn, S, Bd, T)
            outs["bp"].append(kvp)
            outs["bs"].append(kvs)
        elif m == 2:
            mix_p, mix_s, sp, ss = _mixer_c(xp, xs, g[1], w_c_in[j], w_c_gate2[j], b_c_gate[j], c_norm_gain[j],
                                      w_c_out[j], state_c[j], Bn, S, Bd, T)
            outs["cp"].append(sp)
            outs["cs"].append(ss)
        else:
            mix_p, mix_s, hp, hs, cp, cs = _mixer_d(xp, xs, g[1], w_d_in[j], d_conv_w[j], d_conv_b[j], d_dt_bias[j],
                                              d_a_log[j], d_skip[j], d_norm_gain[j], w_d_out[j],
                                              state_d_ssm[j], state_d_conv[j], Bn, S, Bd, T)
            outs["hp"].append(hp)
            outs["hs"].append(hs)
            outs["dp"].append(cp)
            outs["ds"].append(cs)
        xp = _ffn(xp, g[2], w_up, w_down, i, 1, 1024, mix_p)
        xs = _ffn(xs, g[2], w_up, w_down, i, 1, ts, mix_s)
    st = {k: jnp.stack(v) for k, v in outs.items()}
    return (xp.reshape(Bn, S, D_MODEL), xs.reshape(Bd, T, D_MODEL),
            st["a0p"], st["a0s"], st["a1p"], st["a1s"], st["a2p"], st["a2s"],
            st["bp"], st["bs"], st["cp"], st["cs"], st["hp"], st["hs"], st["dp"], st["ds"])
```

```python
import functools
import math

import numpy as np
import jax
import jax.numpy as jnp
from jax import lax
from jax.experimental import pallas as pl
from jax.experimental.pallas import tpu as pltpu

F32 = jnp.float32
BF16 = jnp.bfloat16

D_MODEL = 1024
HEAD_DIM = 64
ROPE_DIM = HEAD_DIM // 4
ROPE_THETA = 500000.0
EPS = 1e-6
D_FF = 2816
NEG = -1e30

A_GROUPS = 3
A_DILATIONS = (1, 4, 16)
A_HEADS = 8
A_WIDTH = A_HEADS * HEAD_DIM
A_SPAN = 128

B_HEADS = 16
B_KV_HEADS = 4
B_REP = B_HEADS // B_KV_HEADS
B_BLOCK = 256
B_TOPK = 3
B_VT_ROWS = HEAD_DIM + 8
LOG2E = 1.4426950408889634

C_HEADS = 4
C_KEY = 512
C_VAL = 1024
C_DK = 128
C_DV = 256
C_RANK = 16
C_TAU = 16.0
C_CHUNK = 32
C_NPAD = 3200

D_INNER = 2048
D_HEADDIM = 64
D_HEADS = 32
D_GROUPS = 4
D_HPG = 8
D_STATE = 128
D_CONV = 4
D_XBC = D_INNER + 2 * D_GROUPS * D_STATE
D_NPAD = 5376
D_DT_COL = (D_INNER + D_XBC) // 128

LANES = 128
FFN_ROWS = 256
PROJ_ROWS = 256
SEQ_TILE = 128


def _params(*sem):
    return pltpu.CompilerParams(dimension_semantics=sem)


def _split2(a):
    hi = a.astype(BF16)
    lo = (a - hi.astype(F32)).astype(BF16)
    return hi, lo


def _split3(a):
    hi = a.astype(BF16)
    r = a - hi.astype(F32)
    mid = r.astype(BF16)
    lo = (r - mid.astype(F32)).astype(BF16)
    return hi, mid, lo


def _dot(a, b):
    return jnp.dot(a, b, preferred_element_type=F32)


def _dot_nt(a, b):
    return lax.dot_general(a, b, (((1,), (1,)), ((), ())), preferred_element_type=F32)


def _sel_right(a, sel01, terms=2):
    parts = _split2(a) if terms == 2 else _split3(a)
    out = None
    for p in reversed(parts):
        d = _dot(p, sel01)
        out = d if out is None else out + d
    return out


def _sel_left(sel01, a, terms=2):
    parts = _split2(a) if terms == 2 else _split3(a)
    out = None
    for p in reversed(parts):
        d = _dot(sel01, p)
        out = d if out is None else out + d
    return out


def _silu(x):
    return x * jax.nn.sigmoid(x)


def _softplus(x):
    return jnp.maximum(x, 0.0) + jnp.log(1.0 + jnp.exp(-jnp.abs(x)))


def _rms_rows(x, g):
    ms = jnp.mean(x * x, axis=-1, keepdims=True)
    return x * lax.rsqrt(ms + EPS) * g


def _ffn_kernel(mix_fn, n_mix, x_ref, *refs):
    mix_refs = refs[:n_mix]
    g_ref, wg_ref, wu_ref, wd_ref, o_ref = refs[n_mix:]
    x = x_ref[...]
    if mix_fn is not None:
        x = x + mix_fn(*mix_refs)
    h = _rms_rows(x, g_ref[...]).astype(BF16)
    a = _dot(h, wg_ref[...])
    u = _dot(h, wu_ref[...])
    act = (_silu(a) * u).astype(BF16)
    o_ref[...] = x + 0.5 * _dot(act, wd_ref[...])


def _ffn(x, gain, w_up, w_down, layer, which, tm, mix=None):
    T = x.shape[0]
    tm = min(tm, FFN_ROWS)
    resident = dict(pipeline_mode=pl.Buffered(1))
    mix_fn, row_inputs, const_inputs = mix if mix is not None else (None, [], [])
    mix_specs, mix_args = [], []
    for arr, width, cb in row_inputs:
        mix_specs.append(pl.BlockSpec((tm, width), lambda i, cb=cb: (i, cb)))
        mix_args.append(arr)
    for arr in const_inputs:
        mix_specs.append(pl.BlockSpec(arr.shape, lambda i, nd=arr.ndim: (0,) * nd, **resident))
        mix_args.append(arr)
    return pl.pallas_call(
        functools.partial(_ffn_kernel, mix_fn, len(mix_args)),
        grid=(T // tm,),
        in_specs=[pl.BlockSpec((tm, D_MODEL), lambda i: (i, 0))] + mix_specs + [
            pl.BlockSpec((1, D_MODEL), lambda i: (0, 0), **resident),
            pl.BlockSpec((None, None, D_MODEL, D_FF), lambda i: (layer, which, 0, 0), **resident),
            pl.BlockSpec((None, None, D_MODEL, D_FF), lambda i: (layer, which, 0, 1), **resident),
            pl.BlockSpec((None, None, D_FF, D_MODEL), lambda i: (layer, which, 0, 0), **resident),
        ],
        out_specs=pl.BlockSpec((tm, D_MODEL), lambda i: (i, 0)),
        out_shape=jax.ShapeDtypeStruct((T, D_MODEL), F32),
        compiler_params=_params("parallel"),
        name="ffn",
    )(x, *mix_args, gain.reshape(1, D_MODEL), w_up, w_up, w_down)


def _proj_kernel(x_ref, g_ref, w_ref, o_ref):
    o_ref[...] = _dot(_rms_rows(x_ref[...], g_ref[...]).astype(BF16), w_ref[...])


def _proj(x, gain, w, tm):
    T = x.shape[0]
    N = w.shape[1]
    tm = min(tm, PROJ_ROWS)
    resident = dict(pipeline_mode=pl.Buffered(1))
    return pl.pallas_call(
        _proj_kernel,
        grid=(T // tm,),
        in_specs=[
            pl.BlockSpec((tm, D_MODEL), lambda i: (i, 0)),
            pl.BlockSpec((1, D_MODEL), lambda i: (0, 0), **resident),
            pl.BlockSpec((D_MODEL, N), lambda i: (0, 0), **resident),
        ],
        out_specs=pl.BlockSpec((tm, N), lambda i: (i, 0)),
        out_shape=jax.ShapeDtypeStruct((T, N), F32),
        compiler_params=_params("parallel"),
        name="proj",
    )(x, gain.reshape(1, D_MODEL), w)


def _proj_qk_kernel(flags, x_ref, g_ref, w_ref, bd_ref, gain_ref, mask_ref, cos_ref, sa_ref, sb_ref, o_ref):
    h = _rms_rows(x_ref[...], g_ref[...]).astype(BF16)
    tn = bd_ref.shape[0]
    cos, sa, sb = cos_ref[...], sa_ref[...], sb_ref[...]
    for j, flag in enumerate(flags):
        y = _dot(h, w_ref[:, j * tn:(j + 1) * tn])
        if not flag:
            o_ref[:, j * tn:(j + 1) * tn] = y
            continue
        ss = _dot((y * y).astype(BF16), bd_ref[...])
        yn = y * lax.rsqrt(ss * (1.0 / HEAD_DIM) + EPS) * gain_ref[j]
        mask = mask_ref[j]
        for c in range(tn // LANES):
            sl = slice(c * LANES, (c + 1) * LANES)
            v = yn[:, sl]
            up = pltpu.roll(v, LANES - ROPE_DIM // 2, 1)
            dn = pltpu.roll(v, ROPE_DIM // 2, 1)
            rot = v * cos + up * sa + dn * sb
            o_ref[:, j * tn + c * LANES:j * tn + (c + 1) * LANES] = jnp.where(mask[:, sl] > 0.0, rot, y[:, sl])


def _proj_qk(x, gain, w, flags, gain_rows, mask_rows, tables, tm, tn):
    T = x.shape[0]
    N = w.shape[1]
    nj = N // tn
    tm = min(tm, PROJ_ROWS)
    cos, sa, sb = tables
    n_pos_tiles = cos.shape[0] // tm
    bd = jnp.asarray(np.kron(np.eye(tn // HEAD_DIM), np.ones((HEAD_DIM, HEAD_DIM))), BF16)
    resident = dict(pipeline_mode=pl.Buffered(1))
    tab_spec = pl.BlockSpec((tm, LANES), lambda i: (i % n_pos_tiles, 0))
    return pl.pallas_call(
        functools.partial(_proj_qk_kernel, flags),
        grid=(T // tm,),
        in_specs=[
            pl.BlockSpec((tm, D_MODEL), lambda i: (i, 0)),
            pl.BlockSpec((1, D_MODEL), lambda i: (0, 0), **resident),
            pl.BlockSpec((D_MODEL, N), lambda i: (0, 0), **resident),
            pl.BlockSpec((tn, tn), lambda i: (0, 0), **resident),
            pl.BlockSpec((nj, 1, tn), lambda i: (0, 0, 0), **resident),
            pl.BlockSpec((nj, 1, tn), lambda i: (0, 0, 0), **resident),
            tab_spec, tab_spec, tab_spec,
        ],
        out_specs=pl.BlockSpec((tm, N), lambda i: (i, 0)),
        out_shape=jax.ShapeDtypeStruct((T, N), F32),
        compiler_params=_params("parallel"),
        name="proj_qk",
    )(x, gain.reshape(1, D_MODEL), w, bd, gain_rows, mask_rows, cos, sa, sb)


def _rope_tables(pos):
    half = ROPE_DIM // 2
    inv = ROPE_THETA ** (-jnp.arange(half, dtype=F32) / half)
    ang = pos.astype(F32)[:, None] * inv[None, :]
    cos, sin = jnp.cos(ang), jnp.sin(ang)
    n = pos.shape[0]
    one = jnp.ones((n, HEAD_DIM - ROPE_DIM), F32)
    zero = jnp.zeros((n, HEAD_DIM - ROPE_DIM), F32)
    zh = jnp.zeros((n, half), F32)
    c64 = jnp.concatenate([cos, cos, one], axis=1)
    sa64 = jnp.concatenate([-sin, zh, zero], axis=1)
    sb64 = jnp.concatenate([zh, sin, zero], axis=1)
    rep = LANES // HEAD_DIM
    return tuple(jnp.tile(t, (1, rep)) for t in (c64, sa64, sb64))


def _mix_plain(o_ref, w_ref):
    return _dot(o_ref[...].astype(BF16), w_ref[...])


def _mix_a(o0, o1, o2, l0, l1, l2, w_ref):
    la, lb, lc = l0[...], l1[...], l2[...]
    m = jnp.maximum(jnp.maximum(la, lb), lc)
    ea, eb, ec = jnp.exp(la - m), jnp.exp(lb - m), jnp.exp(lc - m)
    o = (ea * o0[...] + eb * o1[...] + ec * o2[...]) / (ea + eb + ec)
    return _dot(o.astype(BF16), w_ref[...])


def _mix_c(o_ref, r_ref, g_ref, w_ref):
    o = o_ref[...]
    g = g_ref[...]
    parts = [_rms_rows(o[:, h * C_DV:(h + 1) * C_DV], g) for h in range(C_HEADS)]
    on = jnp.concatenate(parts, axis=1) * _silu(r_ref[...])
    return _dot(on.astype(BF16), w_ref[...])


def _mix_d(o_ref, z_ref, g_ref, w_ref):
    gated = o_ref[...] * _silu(z_ref[...])
    gw = D_INNER // D_GROUPS
    g = g_ref[...]
    parts = [_rms_rows(gated[:, k * gw:(k + 1) * gw], g[:, k * gw:(k + 1) * gw]) for k in range(D_GROUPS)]
    return _dot(jnp.concatenate(parts, axis=1).astype(BF16), w_ref[...])


def _a_prompt_kernel(d, q_ref, kp_ref, kc_ref, vp_ref, vc_ref, o_ref, l_ref):
    j = pl.program_id(1)
    heads = q_ref.shape[2] // HEAD_DIM
    per_pass = max(1, A_HEADS // heads)
    n_prob = per_pass * heads
    scale = HEAD_DIM ** -0.5
    qq = lax.broadcasted_iota(jnp.int32, (n_prob * A_SPAN, 2 * A_SPAN), 0) & (A_SPAN - 1)
    kk = lax.broadcasted_iota(jnp.int32, (n_prob * A_SPAN, 2 * A_SPAN), 1)
    rel = qq + A_SPAN - kk
    valid = (rel >= 0) & (rel <= A_SPAN) & ((kk >= A_SPAN) | (j > 0))
    hs = [slice(h * HEAD_DIM, (h + 1) * HEAD_DIM) for h in range(heads)]
    for r0 in range(0, d, per_pass):
        rows = [pl.ds(r0 + e, A_SPAN, stride=d) if d > 1 else pl.ds(0, A_SPAN) for e in range(per_pass)]
        s, vs = [], []
        for rw in rows:
            q = (q_ref[0, rw, :] * scale).astype(BF16)
            k = jnp.concatenate([kp_ref[0, rw, :], kc_ref[0, rw, :]], axis=0).astype(BF16)
            vs.append(jnp.concatenate([vp_ref[0, rw, :], vc_ref[0, rw, :]], axis=0).astype(BF16))
            s += [_dot_nt(q[:, sl], k[:, sl]) for sl in hs]
        s = jnp.where(valid, jnp.concatenate(s, axis=0), -jnp.inf)
        m = jnp.max(s, axis=-1, keepdims=True)
        p = jnp.exp(s - m)
        den = jnp.sum(p, axis=-1, keepdims=True)
        pb = p.astype(BF16)
        lse = m + jnp.log(den)
        inv = 1.0 / den
        for e, rw in enumerate(rows):
            o, l = [], []
            for h, sl in enumerate(hs):
                blk = slice((e * heads + h) * A_SPAN, (e * heads + h + 1) * A_SPAN)
                o.append(_dot(pb[blk], vs[e][:, sl]) * inv[blk])
                l.append(jnp.broadcast_to(lse[blk], (A_SPAN, HEAD_DIM)))
            o_ref[0, rw, :] = jnp.concatenate(o, axis=1)
            l_ref[0, rw, :] = jnp.concatenate(l, axis=1)


def _a_prompt(y, g, Bn, S):
    d = A_DILATIONS[g]
    slab = A_SPAN * d
    heads = A_HEADS if d == 1 else LANES // HEAD_DIM
    cols = heads * HEAD_DIM
    ncb = A_WIDTH // cols
    y3 = y.reshape(Bn, S, A_GROUPS * 3 * A_WIDTH)

    def spec(which, prev):
        c0 = (3 * g + which) * ncb
        if prev:
            return pl.BlockSpec((1, slab, cols), lambda b, j, c: (b, jnp.maximum(j - 1, 0), c0 + c))
        return pl.BlockSpec((1, slab, cols), lambda b, j, c: (b, j, c0 + c))

    out_spec = pl.BlockSpec((1, slab, cols), lambda b, j, c: (b, j, c))
    o, l = pl.pallas_call(
        functools.partial(_a_prompt_kernel, d),
        grid=(Bn, S // slab, ncb),
        in_specs=[spec(0, False), spec(1, True), spec(1, False), spec(2, True), spec(2, False)],
        out_specs=[out_spec, out_spec],
        out_shape=[jax.ShapeDtypeStruct((Bn, S, A_WIDTH), F32)] * 2,
        compiler_params=_params("parallel", "arbitrary", "arbitrary"),
        name="a_prompt_d%d" % d,
    )(y3, y3, y3, y3, y3)
    return o.reshape(Bn * S, A_WIDTH), l.reshape(Bn * S, A_WIDTH)


def _nt_sel(mask, a):
    out = None
    for p in reversed(_split3(a)):
        d = _dot_nt(mask, p)
        out = d if out is None else out + d
    return out


def _a_sample_kernel(T, y_ref, c1_ref, c2_ref, c3_ref, hind_ref, hindt_ref, *out_refs):
    caches = (c1_ref, c2_ref, c3_ref)
    width = 3 * A_WIDTH
    scale = HEAD_DIM ** -0.5
    hind = hind_ref[...]
    hindt = hindt_ref[...]
    lane = lax.broadcasted_iota(jnp.int32, (8, LANES), 1)
    sub = lax.broadcasted_iota(jnp.int32, (8, LANES), 0)
    zrows = jnp.zeros((LANES - T, A_WIDTH), F32)
    zsq = jnp.zeros((LANES - 8, LANES), F32)

    def head_rows(prod):
        return _nt_sel(hind, jnp.concatenate([prod, zrows], axis=0))

    def sublane_total(x):
        x = x + pltpu.roll(x, 4, 0)
        x = x + pltpu.roll(x, 2, 0)
        return x + pltpu.roll(x, 1, 0)

    def head_scores(k_of, pat_of):
        out = jnp.zeros((8, LANES), F32)
        for h in range(A_HEADS):
            prod = k_of(h) * pat_of(h)
            part = prod[0:8]
            for j in range(1, HEAD_DIM // 8):
                part = part + prod[8 * j:8 * j + 8]
            out = jnp.where(sub == h, sublane_total(part), out)
        return out

    def rows_of(packed):
        return jnp.concatenate([packed, zsq], axis=0).T

    def widen(rows8):
        return _sel_right(rows8, hindt, terms=3)

    def place(col, at):
        return jnp.where(lane == at, col, 0.0)

    def patterns(q_t, pick):
        return _sel_right(q_t, pick.astype(BF16), terms=3)

    row_sq = lax.broadcasted_iota(jnp.int32, (LANES, LANES), 0)
    lane_sq = lax.broadcasted_iota(jnp.int32, (LANES, LANES), 1)

    prep = []
    for g in range(A_GROUPS):
        d = A_DILATIONS[g]
        q_nat = y_ref[0, :, g * width:g * width + A_WIDTH] * scale
        kn_nat = y_ref[0, :, g * width + A_WIDTH:g * width + 2 * A_WIDTH]
        vn_nat = y_ref[0, :, g * width + 2 * A_WIDTH:(g + 1) * width]
        q_t = jnp.concatenate([q_nat, zrows], axis=0).T
        if d == 1:
            pats = [patterns(q_t, row_sq == i) for i in range(T)]
            news = [head_rows(q_nat[i:i + 1, :] * kn_nat) for i in range(T)]
        else:
            pats = [patterns(q_t, (row_sq == (lane_sq & (d - 1))) & (row_sq < T))]
            news = [head_rows(q_nat * kn_nat)]
        prep.append((vn_nat, pats, news))

    swept = []
    for g in range(A_GROUPS):
        d = A_DILATIONS[g]
        c_ref = caches[g]
        n_tiles = c_ref.shape[-1] // LANES
        _, pats, news = prep[g]
        if d == 1:
            packed = jnp.zeros((8, LANES), F32)
            accs = []
            for i in range(T):
                pat = pats[i]
                s = head_scores(lambda h: c_ref[0, 0, h], lambda h: pat[h * HEAD_DIM:(h + 1) * HEAD_DIM])
                s = jnp.where(lane >= i, s, -jnp.inf)
                sn = jnp.where(lane <= i, news[i], -jnp.inf)
                m = jnp.maximum(jnp.max(s, axis=1, keepdims=True), jnp.max(sn, axis=1, keepdims=True))
                p = jnp.exp(s - m)
                pn = jnp.exp(sn - m)
                den = jnp.sum(p, axis=1, keepdims=True) + jnp.sum(pn, axis=1, keepdims=True)
                accs.append(jnp.concatenate([c_ref[0, 1, h] * p[h:h + 1, :] for h in range(A_HEADS)], axis=0))
                packed = packed + place(den, i) + place(m + jnp.log(den), 8 + i)
                packed = packed + pltpu.roll(pn, 16 + 8 * i, 1)
            swept.append((accs, packed))
        else:
            cls = lane & (d - 1)
            self_s = news[0]
            pat = pats[0]
            s_tiles = [head_scores(lambda h: c_ref[0, 0, h, :, t * LANES:(t + 1) * LANES],
                                   lambda h: pat[h * HEAD_DIM:(h + 1) * HEAD_DIM])
                       for t in range(n_tiles)]
            smax = s_tiles[0]
            for t in range(1, n_tiles):
                smax = jnp.maximum(smax, s_tiles[t])
            mrow = jnp.full((8, LANES), jnp.inf, F32)
            m_cls = []
            for i in range(T):
                mi = jnp.maximum(jnp.max(jnp.where(cls == i, smax, -jnp.inf), axis=1, keepdims=True),
                                 self_s[:, i:i + 1])
                m_cls.append(mi)
                mrow = jnp.where(cls == i, mi, mrow)
            p_tiles = [jnp.exp(s_tiles[t] - mrow) for t in range(n_tiles)]
            psum = p_tiles[0]
            for t in range(1, n_tiles):
                psum = psum + p_tiles[t]
            accs = []
            for h in range(A_HEADS):
                acc = jnp.zeros((HEAD_DIM, LANES), F32)
                for t in range(n_tiles):
                    acc = acc + c_ref[0, 1, h, :, t * LANES:(t + 1) * LANES] * p_tiles[t][h:h + 1, :]
                accs.append(acc)
            packed = jnp.zeros((8, LANES), F32)
            for i in range(T):
                ps = jnp.exp(self_s[:, i:i + 1] - m_cls[i])
                den = jnp.sum(jnp.where(cls == i, psum, 0.0), axis=1, keepdims=True) + ps
                packed = packed + place(ps, i) + place(den, 8 + i) + place(m_cls[i] + jnp.log(den), 16 + i)
            swept.append(([jnp.concatenate(accs, axis=0)], packed))

    for g in range(A_GROUPS):
        d = A_DILATIONS[g]
        vn_nat = prep[g][0]
        accs, packed = swept[g]
        sq = rows_of(packed)
        if d == 1:
            first = (sub == 0).astype(BF16)
            denx = widen(sq[0:8])
            lsex = widen(sq[8:16])
            for i in range(T):
                pnx = widen(sq[16 + 8 * i:24 + 8 * i])
                o = _nt_sel(first, accs[i])[0:1, :] + jnp.sum(pnx[0:T] * vn_nat, axis=0, keepdims=True)
                out_refs[g][0, i:i + 1, :] = o / denx[i:i + 1, :]
            out_refs[A_GROUPS + g][0] = lsex[0:T]
        else:
            cls = lane & (d - 1)
            res = _nt_sel((cls == sub).astype(BF16), accs[0])
            psx, denx, lsex = widen(sq[0:8]), widen(sq[8:16]), widen(sq[16:24])
            out_refs[g][0] = (res[0:T] + psx[0:T] * vn_nat) / denx[0:T]
            out_refs[A_GROUPS + g][0] = lsex[0:T]


def _a_sample(y, caches, Bd, T):
    width = A_GROUPS * 3 * A_WIDTH
    y3 = y.reshape(Bd, T, width)
    views = []
    specs = [pl.BlockSpec((1, T, width), lambda b: (b, 0, 0))]
    for g, c in enumerate(caches):
        d = A_DILATIONS[g]
        assert c.shape[1] == A_SPAN * d and (d == 1 or T <= d) and T <= 8
        views.append(c.transpose(0, 2, 3, 4, 1))
        specs.append(pl.BlockSpec((1, 2, A_HEADS, HEAD_DIM, A_SPAN * d), lambda b: (b, 0, 0, 0, 0)))
    hind_np = np.zeros((LANES, A_WIDTH), np.float32)
    hind_np[np.arange(A_WIDTH) // HEAD_DIM, np.arange(A_WIDTH)] = 1.0
    hind = jnp.asarray(hind_np[:A_HEADS], BF16)
    hindt = jnp.asarray(hind_np, BF16)
    specs += [pl.BlockSpec(hind.shape, lambda b: (0, 0)), pl.BlockSpec(hindt.shape, lambda b: (0, 0))]
    out_spec = pl.BlockSpec((1, T, A_WIDTH), lambda b: (b, 0, 0))
    outs = pl.pallas_call(
        functools.partial(_a_sample_kernel, T),
        grid=(Bd,),
        in_specs=specs,
        out_specs=[out_spec] * (2 * A_GROUPS),
        out_shape=[jax.ShapeDtypeStruct((Bd, T, A_WIDTH), F32)] * (2 * A_GROUPS),
        compiler_params=_params("parallel"),
        name="a_sample",
    )(y3, *views, hind, hindt)
    return [o.reshape(Bd * T, A_WIDTH) for o in outs]


def _mixer_a(xp, xs, gain, w_in, qk_gain, w_out, caches, tabs_p, tabs_s, Bn, S, Bd, T):
    flags = (True, True, False) * A_GROUPS
    ones = jnp.ones((A_WIDTH,), F32)
    rows = [jnp.tile(qk_gain[0], A_HEADS), jnp.tile(qk_gain[1], A_HEADS), ones] * A_GROUPS
    gain_rows = jnp.stack(rows).reshape(3 * A_GROUPS, 1, A_WIDTH)
    mask_rows = jnp.stack([ones, ones, 0.0 * ones] * A_GROUPS).reshape(3 * A_GROUPS, 1, A_WIDTH)
    w = w_in.astype(BF16)
    yp = _proj_qk(xp, gain, w, flags, gain_rows, mask_rows, tabs_p, 1024, A_WIDTH)
    ys = _proj_qk(xs, gain, w, flags, gain_rows, mask_rows, tabs_s, xs.shape[0], A_WIDTH)
    wo = w_out.astype(BF16)
    pr = [_a_prompt(yp, g, Bn, S) for g in range(A_GROUPS)]
    mix_p = (_mix_a, [(o, A_WIDTH, 0) for o, _ in pr] + [(l, A_WIDTH, 0) for _, l in pr], [wo])
    sr = _a_sample(ys, caches, Bd, T)
    mix_s = (_mix_a, [(o, A_WIDTH, 0) for o in sr], [wo])
    yp3 = yp.reshape(Bn, S, A_GROUPS * 3 * A_WIDTH)
    ys3 = ys.reshape(Bd, T, A_GROUPS * 3 * A_WIDTH)
    new_p, new_s = [], []
    for g in range(A_GROUPS):
        win = min(A_SPAN * A_DILATIONS[g], S)
        c0 = (3 * g + 1) * A_WIDTH
        new_p.append(yp3[:, S - win:, c0:c0 + 2 * A_WIDTH].reshape(Bn, win, 2, A_HEADS, HEAD_DIM))
        new_s.append(ys3[:, :, c0:c0 + 2 * A_WIDTH].reshape(Bd, T, 2, A_HEADS, HEAD_DIM))
    return mix_p, mix_s, new_p, new_s


def _b_kmean_kernel(k_ref, o_ref):
    nblk = k_ref.shape[1] // B_BLOCK
    rows = [jnp.mean(k_ref[0, n * B_BLOCK:(n + 1) * B_BLOCK, :], axis=0, keepdims=True) for n in range(nblk)]
    o_ref[0] = jnp.concatenate(rows, axis=0)


def _b_gate_kernel(q_ref, kmh_ref, kml_ref, o_ref):
    i = pl.program_id(1)
    qh, ql = _split2(q_ref[0])
    kmh, kml = kmh_ref[0], kml_ref[0]
    gate = _dot_nt(kmh, qh) + (_dot_nt(kmh, ql) + _dot_nt(kml, qh))
    tq = gate.shape[1]
    nblk = gate.shape[0] // B_HEADS
    gate = gate.reshape(B_HEADS, nblk, tq)
    blk = lax.broadcasted_iota(jnp.int32, (B_HEADS, nblk, tq), 1)
    blkf = blk.astype(F32)
    gate = jnp.where(blk < i, gate, -jnp.inf)
    sel = jnp.zeros((B_HEADS, nblk, tq), jnp.bool_)
    for _ in range(B_TOPK):
        mx = jnp.max(gate, axis=1, keepdims=True)
        first = jnp.min(jnp.where(gate == mx, blkf, float(nblk)), axis=1, keepdims=True)
        hit = blkf == first
        sel = sel | (hit & (mx > -jnp.inf))
        gate = jnp.where(hit, -jnp.inf, gate)
    bias = jnp.where(sel | (blk >= i), 0.0, NEG).reshape(B_HEADS * nblk, tq)
    o_ref[0] = bias.T


def _b_attn_kernel(qi_ref, kn_ref, q_ref, k_ref, vt_ref, o_ref, *state):
    t = pl.program_id(1)
    i, n = qi_ref[t], kn_ref[t]
    tq = B_BLOCK
    wide = B_REP * tq
    m_refs, acc_refs = state[:B_KV_HEADS], state[B_KV_HEADS:]

    @pl.when(n == 0)
    def _():
        for kvh in range(B_KV_HEADS):
            m_refs[kvh][...] = jnp.full_like(m_refs[kvh], -jnp.inf)
            acc_refs[kvh][...] = jnp.zeros_like(acc_refs[kvh])

    def sweep(causal):
        if causal:
            kk = lax.broadcasted_iota(jnp.int32, (B_BLOCK, wide), 0)
            qq = lax.broadcasted_iota(jnp.int32, (B_BLOCK, wide), 1) & (tq - 1)
            keep = kk <= qq
        scores = []
        for kvh in range(B_KV_HEADS):
            k = k_ref[0, :, kvh * LANES:(kvh + 1) * LANES]
            s = _dot_nt(k, q_ref[0, 0, kvh])
            scores.append(jnp.where(keep, s, NEG) if causal else s)
        probs, scales = [], []
        for kvh in range(B_KV_HEADS):
            m_old = m_refs[kvh][...]
            m_new = jnp.maximum(m_old, jnp.max(scores[kvh], axis=0, keepdims=True))
            probs.append(jnp.exp2(scores[kvh] - m_new).astype(BF16))
            scales.append(jnp.exp2(m_old - m_new))
            m_refs[kvh][...] = m_new
        for kvh in range(B_KV_HEADS):
            acc_refs[kvh][...] = scales[kvh] * acc_refs[kvh][...] + _dot(vt_ref[0, kvh], probs[kvh])

    @pl.when(n < i)
    def _():
        sweep(False)

    @pl.when(n == i)
    def _():
        sweep(True)
        for kvh in range(B_KV_HEADS):
            acc = acc_refs[kvh][...]
            ot = acc[0:HEAD_DIM] / acc[HEAD_DIM:HEAD_DIM + 1]
            for r in range(0, B_REP, 2):
                pair = [ot[:, (r + e) * tq:(r + e + 1) * tq].T for e in range(2)]
                c0 = (kvh * B_REP + r) * HEAD_DIM
                o_ref[0, :, c0:c0 + 2 * HEAD_DIM] = jnp.concatenate(pair, axis=1)


def _b_prompt(y, Bn, S):
    nblk = S // B_BLOCK
    assert nblk % 8 == 0 and LANES % nblk == 0
    nq = B_HEADS * HEAD_DIM
    nk = B_KV_HEADS * HEAD_DIM
    y3 = y.reshape(Bn, S, nq + 2 * nk)
    kmean = pl.pallas_call(
        _b_kmean_kernel,
        grid=(Bn,),
        in_specs=[pl.BlockSpec((1, S, nk), lambda b: (b, 0, nq // nk))],
        out_specs=pl.BlockSpec((1, nblk, nk), lambda b: (b, 0, 0)),
        out_shape=jax.ShapeDtypeStruct((Bn, nblk, nk), F32),
        compiler_params=_params("parallel"),
        name="b_kmean",
    )(y3)
    km = kmean.reshape(Bn, nblk, B_KV_HEADS, HEAD_DIM)
    km = jnp.repeat(km, B_REP, axis=2)
    eye = jnp.eye(B_HEADS, dtype=F32)
    kmbd = jnp.einsum('bnhe,hg->bgnhe', km, eye).reshape(Bn, B_HEADS * nblk, nq)
    kmh = kmbd.astype(BF16)
    kml = (kmbd - kmh.astype(F32)).astype(BF16)
    bias = pl.pallas_call(
        _b_gate_kernel,
        grid=(Bn, nblk),
        in_specs=[
            pl.BlockSpec((1, B_BLOCK, nq), lambda b, i: (b, i, 0)),
            pl.BlockSpec((1, B_HEADS * nblk, nq), lambda b, i: (b, 0, 0)),
            pl.BlockSpec((1, B_HEADS * nblk, nq), lambda b, i: (b, 0, 0)),
        ],
        out_specs=pl.BlockSpec((1, B_BLOCK, B_HEADS * nblk), lambda b, i: (b, i, 0)),
        out_shape=jax.ShapeDtypeStruct((Bn, S, B_HEADS * nblk), F32),
        compiler_params=_params("parallel", "arbitrary"),
        name="b_gate",
    )(y3, kmh, kml)
    q = (y3[:, :, :nq] * (HEAD_DIM ** -0.5 * LOG2E)).astype(BF16).reshape(Bn, S, B_HEADS, HEAD_DIM)
    pad = LANES - HEAD_DIM - nblk
    q_aug = jnp.concatenate([q, bias.astype(BF16).reshape(Bn, S, B_HEADS, nblk),
                             jnp.zeros((Bn, S, B_HEADS, pad), BF16)], axis=-1)
    q_aug = q_aug.reshape(Bn, nblk, B_BLOCK, B_KV_HEADS, B_REP, LANES).transpose(0, 1, 3, 4, 2, 5)
    q_aug = q_aug.reshape(Bn, nblk, B_KV_HEADS, B_REP * B_BLOCK, LANES)
    k = y3[:, :, nq:nq + nk].astype(BF16).reshape(Bn, S, B_KV_HEADS, HEAD_DIM)
    onehot = jax.nn.one_hot(jnp.arange(S) // B_BLOCK, nblk, dtype=BF16)
    onehot = jnp.broadcast_to(onehot[None, :, None, :], (Bn, S, B_KV_HEADS, nblk))
    k_aug = jnp.concatenate([k, onehot, jnp.zeros((Bn, S, B_KV_HEADS, pad), BF16)], axis=-1)
    k_aug = k_aug.reshape(Bn, S, B_KV_HEADS * LANES)
    vt = y3[:, :, nq + nk:].astype(BF16).reshape(Bn, S, B_KV_HEADS, HEAD_DIM).transpose(0, 2, 3, 1)
    vt = jnp.concatenate([vt, jnp.ones((Bn, B_KV_HEADS, 1, S), BF16),
                          jnp.zeros((Bn, B_KV_HEADS, B_VT_ROWS - HEAD_DIM - 1, S), BF16)], axis=2)
    pairs = [(i, n) for i in range(nblk) for n in range(i + 1)]
    qi = jnp.asarray([p[0] for p in pairs], jnp.int32)
    kn = jnp.asarray([p[1] for p in pairs], jnp.int32)
    o = pl.pallas_call(
        _b_attn_kernel,
        grid_spec=pltpu.PrefetchScalarGridSpec(
            num_scalar_prefetch=2,
            grid=(Bn, len(pairs)),
            in_specs=[
                pl.BlockSpec((1, 1, B_KV_HEADS, B_REP * B_BLOCK, LANES),
                             lambda b, t, qi, kn: (b, qi[t], 0, 0, 0)),
                pl.BlockSpec((1, B_BLOCK, B_KV_HEADS * LANES), lambda b, t, qi, kn: (b, kn[t], 0)),
                pl.BlockSpec((1, B_KV_HEADS, B_VT_ROWS, B_BLOCK), lambda b, t, qi, kn: (b, 0, 0, kn[t])),
            ],
            out_specs=pl.BlockSpec((1, B_BLOCK, nq), lambda b, t, qi, kn: (b, qi[t], 0)),
            scratch_shapes=([pltpu.VMEM((1, B_REP * B_BLOCK), F32)] * B_KV_HEADS
                            + [pltpu.VMEM((B_VT_ROWS, B_REP * B_BLOCK), F32)] * B_KV_HEADS),
        ),
        out_shape=jax.ShapeDtypeStruct((Bn, S, nq), F32),
        compiler_params=_params("parallel", "arbitrary"),
        name="b_attn",
    )(qi, kn, q_aug, k_aug, vt)
    return o.reshape(Bn * S, nq)


def _b_sample_kernel(n_pages, T, pt_ref, q_ref, kn_ref, vn_ref, *refs):
    del pt_ref
    page_refs = refs[:n_pages]
    o_ref = refs[n_pages]
    nk = B_KV_HEADS * HEAD_DIM
    q = q_ref[0]
    qh, ql = _split2(q)
    rows = q.shape[0]
    page_rows = page_refs[0].shape[2]
    pages_per_block = B_BLOCK // page_rows
    n_blocks = n_pages // pages_per_block
    scale = HEAD_DIM ** -0.5
    lane = lax.broadcasted_iota(jnp.int32, (1, LANES), 1)
    lanef = lax.broadcasted_iota(jnp.int32, (rows, LANES), 1).astype(F32)
    scores = []
    block_sum = [None] * n_blocks
    for p in range(n_pages):
        kt = page_refs[p][0, :nk, :]
        scores.append(_dot(qh, kt.astype(BF16)) * scale)
        b = p // pages_per_block
        block_sum[b] = kt if block_sum[b] is None else block_sum[b] + kt
    km = jnp.zeros((nk, LANES), F32)
    for b in range(n_blocks):
        ksum = jnp.sum(block_sum[b], axis=1, keepdims=True) * (1.0 / B_BLOCK)
        km = km + ksum * (lane == b).astype(F32)
    kmh, kml = _split2(km)
    gate = _dot(qh, kmh) + (_dot(qh, kml) + _dot(ql, kmh))
    gate = jnp.where(lanef < float(n_blocks), gate, -jnp.inf)
    sel = jnp.zeros((rows, LANES), jnp.bool_)
    for _ in range(min(B_TOPK, n_blocks)):
        mx = jnp.max(gate, axis=1, keepdims=True)
        first = jnp.min(jnp.where(gate == mx, lanef, float(LANES)), axis=1, keepdims=True)
        hit = lanef == first
        sel = sel | (hit & (mx > -jnp.inf))
        gate = jnp.where(hit, -jnp.inf, gate)
    self = sel.astype(F32)
    qidx = lax.broadcasted_iota(jnp.int32, (rows, 1), 0) % T
    kn, vn = kn_ref[0], vn_ref[0]
    own = []
    for j in range(T):
        sj = jnp.sum(q * kn[j:j + 1, :], axis=1, keepdims=True) * scale
        own.append(jnp.where(qidx >= j, sj, NEG))
    m = own[0]
    for j in range(1, T):
        m = jnp.maximum(m, own[j])
    for p in range(n_pages):
        b = p // pages_per_block
        scores[p] = jnp.where(self[:, b:b + 1] > 0.0, scores[p], NEG)
        m = jnp.maximum(m, jnp.max(scores[p], axis=-1, keepdims=True))
    den = jnp.zeros((rows, 1), F32)
    acc = jnp.zeros((rows, nk), F32)
    for j in range(T):
        pj = jnp.exp(own[j] - m)
        den = den + pj
        acc = acc + pj * vn[j:j + 1, :]
    for p in range(n_pages):
        pp = jnp.exp(scores[p] - m)
        den = den + jnp.sum(pp, axis=-1, keepdims=True)
        acc = acc + _dot_nt(pp.astype(BF16), page_refs[p][0, nk:, :].astype(BF16))
    acc = acc / den
    kvh = lax.broadcasted_iota(jnp.int32, (rows, 1), 0) // (T * B_REP)
    out = jnp.zeros((rows, HEAD_DIM), F32)
    for h in range(B_KV_HEADS):
        out = out + jnp.where(kvh == h, acc[:, h * HEAD_DIM:(h + 1) * HEAD_DIM], 0.0)
    o_ref[0] = out


def _b_sample(y, pool, page_table, Bd, T):
    nq = B_HEADS * HEAD_DIM
    nk = B_KV_HEADS * HEAD_DIM
    n_pages = page_table.shape[1]
    page_rows = pool.shape[1]
    assert B_BLOCK % page_rows == 0 and (n_pages * page_rows) % B_BLOCK == 0 and T <= B_BLOCK
    assert n_pages * page_rows // B_BLOCK <= LANES
    y3 = y.reshape(Bd, T, nq + 2 * nk)
    q = y3[:, :, :nq].reshape(Bd, T, B_HEADS, HEAD_DIM).transpose(0, 2, 1, 3)
    kvsel = jnp.asarray(np.kron(np.eye(B_KV_HEADS), np.ones((B_REP, 1))), F32)
    qbd = (q[:, :, :, None, :] * kvsel[None, :, None, :, None]).reshape(Bd, B_HEADS * T, nk)
    kn = y3[:, :, nq:nq + nk]
    vn = y3[:, :, nq + nk:]
    pool_t = pool.transpose(0, 2, 3, 4, 1).reshape(pool.shape[0], 2 * nk, page_rows)
    page_specs = [pl.BlockSpec((1, 2 * nk, page_rows), lambda b, pt, p=p: (pt[b, p], 0, 0))
                  for p in range(n_pages)]
    o = pl.pallas_call(
        functools.partial(_b_sample_kernel, n_pages, T),
        grid_spec=pltpu.PrefetchScalarGridSpec(
            num_scalar_prefetch=1,
            grid=(Bd,),
            in_specs=[
                pl.BlockSpec((1, B_HEADS * T, nk), lambda b, pt: (b, 0, 0)),
                pl.BlockSpec((1, T, nk), lambda b, pt: (b, 0, 0)),
                pl.BlockSpec((1, T, nk), lambda b, pt: (b, 0, 0)),
            ] + page_specs,
            out_specs=pl.BlockSpec((1, B_HEADS * T, HEAD_DIM), lambda b, pt: (b, 0, 0)),
        ),
        out_shape=jax.ShapeDtypeStruct((Bd, B_HEADS * T, HEAD_DIM), F32),
        compiler_params=_params("parallel"),
        name="b_sample",
    )(page_table, qbd, kn, vn, *([pool_t] * n_pages))
    return o.reshape(Bd, B_HEADS, T, HEAD_DIM).transpose(0, 2, 1, 3).reshape(Bd * T, nq)


def _mixer_b(xp, xs, gain, w_in, qk_gain, w_out, pool, page_table, tabs_p, tabs_s, Bn, S, Bd, T):
    tn = 512
    nq = B_HEADS * HEAD_DIM
    nk = B_KV_HEADS * HEAD_DIM
    flags = (True, True, True)
    ones = jnp.ones((nk,), F32)
    qg = jnp.tile(qk_gain[0], tn // HEAD_DIM)
    kg = jnp.concatenate([jnp.tile(qk_gain[1], B_KV_HEADS), ones])
    gain_rows = jnp.stack([qg, qg, kg]).reshape(3, 1, tn)
    mask_rows = jnp.stack([jnp.ones((tn,), F32), jnp.ones((tn,), F32),
                           jnp.concatenate([ones, 0.0 * ones])]).reshape(3, 1, tn)
    w = w_in.astype(BF16)
    yp = _proj_qk(xp, gain, w, flags, gain_rows, mask_rows, tabs_p, 1024, tn)
    ys = _proj_qk(xs, gain, w, flags, gain_rows, mask_rows, tabs_s, xs.shape[0], tn)
    wo = w_out.astype(BF16)
    op = _b_prompt(yp, Bn, S)
    mix_p = (_mix_plain, [(op, nq, 0)], [wo])
    osm = _b_sample(ys, pool, page_table, Bd, T)
    mix_s = (_mix_plain, [(osm, nq, 0)], [wo])
    kv_p = yp[:, nq:].reshape(Bn, S, 2, B_KV_HEADS, HEAD_DIM)
    kv_s = ys[:, nq:].reshape(Bd, T, 2, B_KV_HEADS, HEAD_DIM)
    return mix_p, mix_s, kv_p, kv_s


def _gla_kernel(nv, q_ref, k_ref, v_ref, glr_ref, wg_ref, bg_ref, tri_ref, blk_ref, s0_ref,
                o_ref, sT_ref, s_ref, qp_ref, kp_ref, vp_ref, gp_ref):
    t = pl.program_id(1)
    Tt = SEQ_TILE

    @pl.when(t == 0)
    def _():
        s_ref[...] = s0_ref[0]

    if nv < Tt:
        qp_ref[...] = jnp.zeros_like(qp_ref)
        kp_ref[...] = jnp.zeros_like(kp_ref)
        vp_ref[...] = jnp.zeros_like(vp_ref)
        gp_ref[...] = jnp.zeros_like(gp_ref)
    qp_ref[0:nv, :] = q_ref[0]
    kp_ref[0:nv, :] = k_ref[0]
    vp_ref[0:nv, :] = v_ref[0]
    gp_ref[0:nv, :] = glr_ref[0]
    q, k, v = qp_ref[...], kp_ref[...], vp_ref[...]

    x = _dot(gp_ref[...].astype(BF16), wg_ref[...]) + bg_ref[...]
    la = (jnp.minimum(x, 0.0) - jnp.log(1.0 + jnp.exp(-jnp.abs(x)))) * (1.0 / C_TAU)
    row = lax.broadcasted_iota(jnp.int32, (Tt, C_KEY), 0)
    la = jnp.where(row < nv, la, 0.0)
    b = _sel_left(tri_ref[...], la, terms=3)
    bend = _sel_left(blk_ref[...], la, terms=3)
    qe = q * (C_DK ** -0.5) * jnp.exp(b)
    ke = k * jnp.exp(-b)
    kd = k * jnp.exp(bend - b)
    kdt = kd.T
    bendt = bend.T
    qeb, keb, vb = qe.astype(BF16), ke.astype(BF16), v.astype(BF16)
    causal = tri_ref[...] > 0
    lane_t = lax.broadcasted_iota(jnp.int32, (C_DK, Tt), 1)
    n_chunks = -(-nv // C_CHUNK)
    ksl = [slice(h * C_DK, (h + 1) * C_DK) for h in range(C_HEADS)]
    vsl = [slice(h * C_DV, (h + 1) * C_DV) for h in range(C_HEADS)]
    att = [jnp.where(causal, _dot_nt(qeb[:, ks], keb[:, ks]), 0.0).astype(BF16) for ks in ksl]
    chunk_kv, chunk_dec = [], []
    for c in range(n_chunks):
        in_chunk = (lane_t >= c * C_CHUNK) & (lane_t < (c + 1) * C_CHUNK)
        last = lane_t == (c + 1) * C_CHUNK - 1
        chunk_kv.append([_dot(jnp.where(in_chunk, kdt[ks, :], 0.0).astype(BF16), vb[:, vs])
                         for ks, vs in zip(ksl, vsl)])
        chunk_dec.append([jnp.exp(jnp.sum(jnp.where(last, bendt[ks, :], 0.0), axis=1, keepdims=True))
                          for ks in ksl])
    o_intra = [_dot(att[h], vb[:, vsl[h]]) for h in range(C_HEADS)]
    for h in range(C_HEADS):
        st = s_ref[h]
        parts = []
        for c in range(n_chunks):
            rows = slice(c * C_CHUNK, (c + 1) * C_CHUNK)
            parts.append(o_intra[h][rows] + _dot(qeb[rows, ksl[h]], st.astype(BF16)))
            st = chunk_dec[c][h] * st + chunk_kv[c][h]
        s_ref[h] = st
        oh = parts[0] if n_chunks == 1 else jnp.concatenate(parts, axis=0)
        o_ref[0, :, vsl[h]] = oh[0:nv]

    @pl.when(t == pl.num_programs(1) - 1)
    def _():
        sT_ref[0] = s_ref[...]


def _gla(y, w_gate2, b_gate, s0, nb, nt, nv):
    Tt = SEQ_TILE
    y3 = y.reshape(nb * nt, nv, C_NPAD)
    wg = jnp.zeros((LANES, C_KEY), F32).at[:C_RANK].set(w_gate2).astype(BF16)
    idx = np.arange(Tt)
    same = (idx[:, None] // C_CHUNK) == (idx[None, :] // C_CHUNK)
    tri = jnp.asarray(same & (idx[None, :] <= idx[:, None]), BF16)
    blk = jnp.asarray(same, BF16)

    def yspec(width, cb):
        return pl.BlockSpec((1, nv, width), lambda b, t: (b * nt + t, 0, cb))

    def cspec(a):
        return pl.BlockSpec(a.shape, lambda b, t, nd=a.ndim: (0,) * nd)

    bg = b_gate.reshape(1, C_KEY)
    o, sT = pl.pallas_call(
        functools.partial(_gla_kernel, nv),
        grid=(nb, nt),
        in_specs=[
            yspec(C_KEY, 0), yspec(C_KEY, 1), yspec(C_VAL, 1), yspec(LANES, (2 * C_KEY + 2 * C_VAL) // LANES),
            cspec(wg), cspec(bg), cspec(tri), cspec(blk),
            pl.BlockSpec((1, C_HEADS, C_DK, C_DV), lambda b, t: (b, 0, 0, 0)),
        ],
        out_specs=[
            pl.BlockSpec((1, nv, C_VAL), lambda b, t: (b * nt + t, 0, 0)),
            pl.BlockSpec((1, C_HEADS, C_DK, C_DV), lambda b, t: (b, 0, 0, 0)),
        ],
        out_shape=[
            jax.ShapeDtypeStruct((nb * nt, nv, C_VAL), F32),
            jax.ShapeDtypeStruct((nb, C_HEADS, C_DK, C_DV), F32),
        ],
        scratch_shapes=[
            pltpu.VMEM((C_HEADS, C_DK, C_DV), F32),
            pltpu.VMEM((Tt, C_KEY), F32), pltpu.VMEM((Tt, C_KEY), F32),
            pltpu.VMEM((Tt, C_VAL), F32), pltpu.VMEM((Tt, LANES), F32),
        ],
        compiler_params=_params("parallel", "arbitrary"),
        name="gla",
    )(y3, y3, y3, y3, wg, bg, tri, blk, s0)
    return o.reshape(nb * nt * nv, C_VAL), sT


def _mixer_c(xp, xs, gain, w_in, w_gate2, b_gate, norm_g, w_out, state, Bn, S, Bd, T):
    n_in = w_in.shape[1]
    w = jnp.zeros((D_MODEL, C_NPAD), BF16).at[:, :n_in].set(w_in.astype(BF16))
    yp = _proj(xp, gain, w, 1024)
    ys = _proj(xs, gain, w, xs.shape[0])
    zero = jnp.zeros((Bn, C_HEADS, C_DK, C_DV), F32)
    op, sp = _gla(yp, w_gate2, b_gate, zero, Bn, S // SEQ_TILE, SEQ_TILE)
    osm, ss = _gla(ys, w_gate2, b_gate, state, Bd, 1, T)
    wo = w_out.astype(BF16)
    ng = norm_g.reshape(1, C_DV)
    rcb = (2 * C_KEY + C_VAL) // C_VAL
    mix_p = (_mix_c, [(op, C_VAL, 0), (yp, C_VAL, rcb)], [ng, wo])
    mix_s = (_mix_c, [(osm, C_VAL, 0), (ys, C_VAL, rcb)], [ng, wo])
    return mix_p, mix_s, sp, ss


def _ssd_kernel(nv, xa_ref, xb_ref, bc_ref, dt_ref, cs_ref, cw_ref, cb_ref, dtb_ref, alog_ref, dsk_ref,
                tri_ref, exp_ref, expt_ref, h0_ref, y_ref, hT_ref, h_ref, xp_ref, dtp_ref):
    t = pl.program_id(1)
    Tt = SEQ_TILE
    pre = 8

    @pl.when(t == 0)
    def _():
        h_ref[...] = h0_ref[0]
        xp_ref[0:pre, :] = cs_ref[0]

    if nv < Tt:
        xp_ref[pre:, :] = jnp.zeros((Tt, D_XBC), F32)
        dtp_ref[...] = jnp.zeros_like(dtp_ref)
    xp_ref[pre:pre + nv, 0:1024] = xa_ref[0]
    xp_ref[pre:pre + nv, 1024:2048] = xb_ref[0]
    xp_ref[pre:pre + nv, 2048:3072] = bc_ref[0]
    dtp_ref[0:nv, :] = dt_ref[0]

    full = xp_ref[...]
    conv = cb_ref[...] + full[pre:] * cw_ref[D_CONV - 1:D_CONV, :]
    for k in range(1, D_CONV):
        conv = conv + pltpu.roll(full, k, 0)[pre:] * cw_ref[D_CONV - 1 - k:D_CONV - k, :]
    xp_ref[0:pre, :] = xp_ref[Tt:Tt + pre, :]
    xbc = _silu(conv)
    x = xbc[:, :D_INNER]
    nbc = D_GROUPS * D_STATE
    bm = xbc[:, D_INNER:D_INNER + nbc].astype(BF16)
    cm = xbc[:, D_INNER + nbc:].astype(BF16)

    row = lax.broadcasted_iota(jnp.int32, (Tt, LANES), 0)
    lane = lax.broadcasted_iota(jnp.int32, (Tt, LANES), 1)
    live = (row < nv) & (lane < D_HEADS)
    dt = jnp.where(live, _softplus(dtp_ref[...] + dtb_ref[...]), 0.0)
    cum = _sel_left(tri_ref[...], dt * (-jnp.exp(alog_ref[...])), terms=3)
    cumt = cum.T
    cend = cum[Tt - 1:Tt, :]
    ex = exp_ref[...]
    stack = jnp.concatenate([jnp.exp(cend - cum) * dt, dt, jnp.exp(cum)], axis=0)
    wide = _sel_right(stack, ex)
    x_state = (x * wide[0:Tt]).astype(BF16)
    x_dt = (x * wide[Tt:2 * Tt]).astype(BF16)
    off_scale = wide[2 * Tt:]

    causal = tri_ref[...] > 0
    gw = D_INNER // D_GROUPS
    cbs, offs = [], []
    for g in range(D_GROUPS):
        cg = cm[:, g * D_STATE:(g + 1) * D_STATE]
        cbs.append(_dot_nt(cg, bm[:, g * D_STATE:(g + 1) * D_STATE]))
        offs.append(_dot_nt(cg, h_ref[g * gw:(g + 1) * gw, :].astype(BF16)))
    mats = []
    for j in range(D_HEADS):
        seg = cum[:, j:j + 1] - cumt[j:j + 1, :]
        mats.append((cbs[j // D_HPG] * jnp.exp(jnp.where(causal, seg, -jnp.inf))).astype(BF16))
    diag = [_dot(mats[j], x_dt[:, j * D_HEADDIM:(j + 1) * D_HEADDIM]) for j in range(D_HEADS)]
    y = jnp.concatenate(diag, axis=1) + jnp.concatenate(offs, axis=1) * off_scale + dsk_ref[...] * x
    y_ref[0] = y[0:nv]

    dcol = jnp.broadcast_to(jnp.exp(cumt[:, Tt - 1:Tt]), (LANES, D_STATE))
    dfull = _sel_left(expt_ref[...], dcol)
    xst = x_state.astype(F32).T.astype(BF16)
    for g in range(D_GROUPS):
        rows = slice(g * gw, (g + 1) * gw)
        h_ref[rows, :] = dfull[rows] * h_ref[rows, :] + _dot(xst[rows], bm[:, g * D_STATE:(g + 1) * D_STATE])

    @pl.when(t == pl.num_programs(1) - 1)
    def _():
        hT_ref[0] = h_ref[...]


def _ssd(y, conv_w, conv_b, dt_bias, a_log, d_skip, h0, c0, nb, nt, nv):
    Tt = SEQ_TILE
    y3 = y.reshape(nb * nt, nv, D_NPAD)
    idx = np.arange(Tt)
    tri = jnp.asarray(idx[None, :] <= idx[:, None], BF16)
    ex_np = np.zeros((LANES, D_INNER), np.float32)
    ex_np[np.arange(D_INNER) // D_HEADDIM, np.arange(D_INNER)] = 1.0
    ex = jnp.asarray(ex_np, BF16)
    ext = jnp.asarray(ex_np.T, BF16)

    def pad_row(v):
        return jnp.zeros((1, LANES), F32).at[0, :D_HEADS].set(v)

    cs = jnp.zeros((nb, 8, D_XBC), F32).at[:, 8 - (D_CONV - 1):].set(c0)
    dsk = jnp.repeat(d_skip, D_HEADDIM).reshape(1, D_INNER)

    def yspec(width, cb):
        return pl.BlockSpec((1, nv, width), lambda b, t: (b * nt + t, 0, cb))

    def cspec(a):
        return pl.BlockSpec(a.shape, lambda b, t, nd=a.ndim: (0,) * nd)

    consts = [conv_w, conv_b.reshape(1, D_XBC), pad_row(dt_bias), pad_row(a_log), dsk, tri, ex, ext]
    yo, hT = pl.pallas_call(
        functools.partial(_ssd_kernel, nv),
        grid=(nb, nt),
        in_specs=[yspec(1024, 2), yspec(1024, 3), yspec(1024, 4), yspec(LANES, D_DT_COL),
                  pl.BlockSpec((1, 8, D_XBC), lambda b, t: (b, 0, 0))]
        + [cspec(a) for a in consts]
        + [pl.BlockSpec((1, D_INNER, D_STATE), lambda b, t: (b, 0, 0))],
        out_specs=[
            pl.BlockSpec((1, nv, D_INNER), lambda b, t: (b * nt + t, 0, 0)),
            pl.BlockSpec((1, D_INNER, D_STATE), lambda b, t: (b, 0, 0)),
        ],
        out_shape=[
            jax.ShapeDtypeStruct((nb * nt, nv, D_INNER), F32),
            jax.ShapeDtypeStruct((nb, D_INNER, D_STATE), F32),
        ],
        scratch_shapes=[
            pltpu.VMEM((D_INNER, D_STATE), F32),
            pltpu.VMEM((Tt + 8, D_XBC), F32),
            pltpu.VMEM((Tt, LANES), F32),
        ],
        compiler_params=_params("parallel", "arbitrary"),
        name="ssd",
    )(y3, y3, y3, y3, cs, *consts, h0)
    return yo.reshape(nb * nt * nv, D_INNER), hT


def _ssd_step_kernel(nv, xa_ref, xb_ref, bc_ref, dt_ref, cs_ref, cw_ref, cb_ref, dtb_ref, alog_ref, dsk_ref,
                     eye_ref, h0_ref, y_ref, hT_ref, xp_ref):
    R = 8
    pre = 8
    xp_ref[0:pre, :] = cs_ref[0]
    xp_ref[pre:, :] = jnp.zeros((R, D_XBC), F32)
    xp_ref[pre:pre + nv, 0:1024] = xa_ref[0]
    xp_ref[pre:pre + nv, 1024:2048] = xb_ref[0]
    xp_ref[pre:pre + nv, 2048:3072] = bc_ref[0]
    conv = cb_ref[...]
    for w in range(D_CONV):
        conv = conv + xp_ref[pl.ds(pre - (D_CONV - 1) + w, R), :] * cw_ref[w:w + 1, :]
    xbc = _silu(conv)
    x = xbc[:, :D_INNER]
    nbc = D_GROUPS * D_STATE
    bmf = xbc[:, D_INNER:D_INNER + nbc]
    cmf = xbc[:, D_INNER + nbc:]

    row = lax.broadcasted_iota(jnp.int32, (R, LANES), 0)
    lane = lax.broadcasted_iota(jnp.int32, (R, LANES), 1)
    dtp = jnp.concatenate([dt_ref[0], jnp.zeros((R - nv, LANES), F32)], axis=0)
    dt = jnp.where((row < nv) & (lane < D_HEADS), _softplus(dtp + dtb_ref[...]), 0.0)
    cum = dt * (-jnp.exp(alog_ref[...]))
    for k in (1, 2, 4):
        cum = cum + jnp.where(row >= k, pltpu.roll(cum, k, 0), 0.0)
    cend = cum[R - 1:R, :]
    group_of_lane = lane // D_HPG
    low_half = lane < D_HEADDIM

    def widen(a):
        pairs = [jnp.where(low_half, a[:, j:j + 1], a[:, j + 1:j + 2]) for j in range(0, D_HEADS, 2)]
        return jnp.concatenate(pairs, axis=1)

    yd = jnp.zeros((R, D_INNER), F32)
    for s in range(nv):
        w_s = jnp.where(row >= s, jnp.exp(cum - cum[s:s + 1, :]), 0.0) * dt[s:s + 1, :]
        prod = cmf * bmf[s:s + 1, :]
        cb_s = jnp.zeros((R, LANES), F32)
        for g in range(D_GROUPS):
            tot = jnp.sum(prod[:, g * D_STATE:(g + 1) * D_STATE], axis=1, keepdims=True)
            cb_s = jnp.where(group_of_lane == g, tot, cb_s)
        yd = yd + widen(cb_s * w_s) * x[s:s + 1, :]
    off_scale = widen(jnp.exp(cum))
    x_state = (x * widen(jnp.exp(cend - cum) * dt)).astype(BF16)

    gw = D_INNER // D_GROUPS
    cm = cmf.astype(BF16)
    parts = []
    for g in range(D_GROUPS):
        hg = h0_ref[0, g * gw:(g + 1) * gw, :]
        parts.append(_dot_nt(cm[:, g * D_STATE:(g + 1) * D_STATE], hg.astype(BF16)))
    y = yd + jnp.concatenate(parts, axis=1) * off_scale + dsk_ref[...] * x
    y_ref[0] = y[0:nv]

    erow = jnp.broadcast_to(jnp.exp(cend), (LANES, LANES))
    eye128 = (lax.broadcasted_iota(jnp.int32, (LANES, LANES), 0)
              == lax.broadcasted_iota(jnp.int32, (LANES, LANES), 1)).astype(BF16)
    dcol = _nt_sel(eye128, erow)[0:D_HEADS].reshape(D_HEADS, 1, D_STATE)
    zpad = jnp.zeros((LANES - R, D_INNER), BF16)
    xs_pad = jnp.concatenate([x_state, zpad], axis=0)
    bm_pad = jnp.concatenate([bmf.astype(BF16), jnp.zeros((LANES - R, nbc), BF16)], axis=0)
    for g in range(D_GROUPS):
        rows = slice(g * gw, (g + 1) * gw)
        xt = _dot_nt(eye_ref[...], xs_pad[:, rows]).astype(BF16)
        kept = (h0_ref[0, rows, :].reshape(D_HPG, D_HEADDIM, D_STATE) * dcol[g * D_HPG:(g + 1) * D_HPG])
        hT_ref[0, rows, :] = kept.reshape(gw, D_STATE) + _dot(xt, bm_pad[:, g * D_STATE:(g + 1) * D_STATE])


def _ssd_step(y, conv_w, conv_b, dt_bias, a_log, d_skip, h0, c0, nb, nv):
    assert nv <= 8
    y3 = y.reshape(nb, nv, D_NPAD)
    gw = D_INNER // D_GROUPS
    eye = jnp.asarray(np.eye(gw), BF16)

    def pad_row(v):
        return jnp.zeros((1, LANES), F32).at[0, :D_HEADS].set(v)

    cs = jnp.zeros((nb, 8, D_XBC), F32).at[:, 8 - (D_CONV - 1):].set(c0)
    dsk = jnp.repeat(d_skip, D_HEADDIM).reshape(1, D_INNER)

    def yspec(width, cb):
        return pl.BlockSpec((1, nv, width), lambda b: (b, 0, cb))

    def cspec(a):
        return pl.BlockSpec(a.shape, lambda b, nd=a.ndim: (0,) * nd)

    consts = [conv_w, conv_b.reshape(1, D_XBC), pad_row(dt_bias), pad_row(a_log), dsk, eye]
    yo, hT = pl.pallas_call(
        functools.partial(_ssd_step_kernel, nv),
        grid=(nb,),
        in_specs=[yspec(1024, 2), yspec(1024, 3), yspec(1024, 4), yspec(LANES, D_DT_COL),
                  pl.BlockSpec((1, 8, D_XBC), lambda b: (b, 0, 0))]
        + [cspec(a) for a in consts]
        + [pl.BlockSpec((1, D_INNER, D_STATE), lambda b: (b, 0, 0))],
        out_specs=[
            pl.BlockSpec((1, nv, D_INNER), lambda b: (b, 0, 0)),
            pl.BlockSpec((1, D_INNER, D_STATE), lambda b: (b, 0, 0)),
        ],
        out_shape=[
            jax.ShapeDtypeStruct((nb, nv, D_INNER), F32),
            jax.ShapeDtypeStruct((nb, D_INNER, D_STATE), F32),
        ],
        scratch_shapes=[pltpu.VMEM((16, D_XBC), F32)],
        compiler_params=_params("parallel"),
        name="ssd_step",
    )(y3, y3, y3, y3, cs, *consts, h0)
    return yo.reshape(nb * nv, D_INNER), hT


def _mixer_d(xp, xs, gain, w_in, conv_w, conv_b, dt_bias, a_log, d_skip, norm_g, w_out,
             ssm_state, conv_state, Bn, S, Bd, T):
    n_in = w_in.shape[1]
    w = jnp.zeros((D_MODEL, D_NPAD), BF16).at[:, :n_in].set(w_in.astype(BF16))
    yp = _proj(xp, gain, w, 1024)
    ys = _proj(xs, gain, w, xs.shape[0])
    h0p = jnp.zeros((Bn, D_INNER, D_STATE), F32)
    c0p = jnp.zeros((Bn, D_CONV - 1, D_XBC), F32)
    op, hp = _ssd(yp, conv_w, conv_b, dt_bias, a_log, d_skip, h0p, c0p, Bn, S // SEQ_TILE, SEQ_TILE)
    h0s = ssm_state.reshape(Bd, D_INNER, D_STATE)
    osm, hs = _ssd_step(ys, conv_w, conv_b, dt_bias, a_log, d_skip, h0s, conv_state, Bd, T)
    wo = w_out.astype(BF16)
    ng = norm_g.reshape(1, D_INNER)
    mix_p = (_mix_d, [(op, D_INNER, 0), (yp, D_INNER, 0)], [ng, wo])
    mix_s = (_mix_d, [(osm, D_INNER, 0), (ys, D_INNER, 0)], [ng, wo])
    keep = D_CONV - 1
    xbc_p = yp.reshape(Bn, S, D_NPAD)[:, :, D_INNER:D_INNER + D_XBC]
    xbc_s = ys.reshape(Bd, T, D_NPAD)[:, :, D_INNER:D_INNER + D_XBC]
    cp = jnp.concatenate([c0p, xbc_p], axis=1)[:, -keep:] if S < keep else xbc_p[:, S - keep:]
    cs = jnp.concatenate([conv_state, xbc_s], axis=1)[:, -keep:]
    return (mix_p, mix_s, hp.reshape(Bn, D_HEADS, D_HEADDIM, D_STATE), hs.reshape(Bd, D_HEADS, D_HEADDIM, D_STATE),
            cp, cs)


def kernel(x_prompt, x_sample, cache_a_w1, cache_a_w2, cache_a_w3, cache_b_kv, page_table, state_c, state_d_ssm, state_d_conv, norm_gain, w_ffn_up, w_ffn_down, w_a_in, a_qk_gain, w_a_out, w_b_in, b_qk_gain, w_b_out, w_c_in, w_c_gate2, b_c_gate, c_norm_gain, w_c_out, w_d_in, d_conv_w, d_conv_b, d_dt_bias, d_a_log, d_skip, d_norm_gain, w_d_out):
    Bn, S, _ = x_prompt.shape
    Bd, T, _ = x_sample.shape
    depth = norm_gain.shape[0]
    past_len = page_table.shape[1] * cache_b_kv.shape[2]
    tabs_p = _rope_tables(jnp.arange(S, dtype=jnp.int32))
    tabs_s = _rope_tables(jnp.tile(past_len + jnp.arange(T, dtype=jnp.int32), Bd))
    xp = x_prompt.reshape(Bn * S, D_MODEL)
    xs = x_sample.reshape(Bd * T, D_MODEL)
    ts = xs.shape[0]
    w_up = w_ffn_up.astype(BF16)
    w_down = w_ffn_down.astype(BF16)
    outs = {k: [] for k in ("a0p", "a0s", "a1p", "a1s", "a2p", "a2s", "bp", "bs", "cp", "cs",
                            "hp", "hs", "dp", "ds")}
    for i in range(depth):
        m, j = i % 4, i // 4
        g = norm_gain[i]
        xp = _ffn(xp, g[0], w_up, w_down, i, 0, 1024)
        xs = _ffn(xs, g[0], w_up, w_down, i, 0, ts)
        if m == 0:
            mix_p, mix_s, new_p, new_s = _mixer_a(xp, xs, g[1], w_a_in[j], a_qk_gain[j], w_a_out[j],
                                            (cache_a_w1[j], cache_a_w2[j], cache_a_w3[j]),
                                            tabs_p, tabs_s, Bn, S, Bd, T)
            for gi in range(A_GROUPS):
                outs["a%dp" % gi].append(new_p[gi])
                outs["a%ds" % gi].append(new_s[gi])
        elif m == 1:
            mix_p, mix_s, kvp, kvs = _mixer_b(xp, xs, g[1], w_b_in[j], b_qk_gain[j], w_b_out[j],
                                        cache_b_kv[j], page_table, tabs_p, tabs_s, Bn, S, Bd, T)
            outs["bp"].append(kvp)
            outs["bs"].append(kvs)
        elif m == 2:
            mix_p, mix_s, sp, ss = _mixer_c(xp, xs, g[1], w_c_in[j], w_c_gate2[j], b_c_gate[j], c_norm_gain[j],
                                      w_c_out[j], state_c[j], Bn, S, Bd, T)
            outs["cp"].append(sp)
            outs["cs"].append(ss)
        else:
            mix_p, mix_s, hp, hs, cp, cs = _mixer_d(xp, xs, g[1], w_d_in[j], d_conv_w[j], d_conv_b[j], d_dt_bias[j],
                                              d_a_log[j], d_skip[j], d_norm_gain[j], w_d_out[j],
                                              state_d_ssm[j], state_d_conv[j], Bn, S, Bd, T)
            outs["hp"].append(hp)
            outs["hs"].append(hs)
            outs["dp"].append(cp)
            outs["ds"].append(cs)
        xp = _ffn(xp, g[2], w_up, w_down, i, 1, 1024, mix_p)
        xs = _ffn(xs, g[2], w_up, w_down, i, 1, ts, mix_s)
    st = {k: jnp.stack(v) for k, v in outs.items()}
    return (xp.reshape(Bn, S, D_MODEL), xs.reshape(Bd, T, D_MODEL),
            st["a0p"], st["a0s"], st["a1p"], st["a1s"], st["a2p"], st["a2s"],
            st["bp"], st["bs"], st["cp"], st["cs"], st["hp"], st["hs"], st["dp"], st["ds"])
```

```python
import functools
import math

import numpy as np
import jax
import jax.numpy as jnp
from jax import lax
from jax.experimental import pallas as pl
from jax.experimental.pallas import tpu as pltpu

F32 = jnp.float32
BF16 = jnp.bfloat16

D_MODEL = 1024
HEAD_DIM = 64
ROPE_DIM = HEAD_DIM // 4
ROPE_THETA = 500000.0
EPS = 1e-6
D_FF = 2816
NEG = -1e30

A_GROUPS = 3
A_DILATIONS = (1, 4, 16)
A_HEADS = 8
A_WIDTH = A_HEADS * HEAD_DIM
A_SPAN = 128

B_HEADS = 16
B_KV_HEADS = 4
B_REP = B_HEADS // B_KV_HEADS
B_BLOCK = 256
B_TOPK = 3
B_VT_ROWS = HEAD_DIM + 8
LOG2E = 1.4426950408889634

C_HEADS = 4
C_KEY = 512
C_VAL = 1024
C_DK = 128
C_DV = 256
C_RANK = 16
C_TAU = 16.0
C_CHUNK = 32
C_NPAD = 3200

D_INNER = 2048
D_HEADDIM = 64
D_HEADS = 32
D_GROUPS = 4
D_HPG = 8
D_STATE = 128
D_CONV = 4
D_XBC = D_INNER + 2 * D_GROUPS * D_STATE
D_NPAD = 5376
D_DT_COL = (D_INNER + D_XBC) // 128

LANES = 128
FFN_ROWS = 256
PROJ_ROWS = 256
SEQ_TILE = 128


def _params(*sem):
    return pltpu.CompilerParams(dimension_semantics=sem)


def _split2(a):
    hi = a.astype(BF16)
    lo = (a - hi.astype(F32)).astype(BF16)
    return hi, lo


def _split3(a):
    hi = a.astype(BF16)
    r = a - hi.astype(F32)
    mid = r.astype(BF16)
    lo = (r - mid.astype(F32)).astype(BF16)
    return hi, mid, lo


def _dot(a, b):
    return jnp.dot(a, b, preferred_element_type=F32)


def _dot_nt(a, b):
    return lax.dot_general(a, b, (((1,), (1,)), ((), ())), preferred_element_type=F32)


def _sel_right(a, sel01, terms=2):
    parts = _split2(a) if terms == 2 else _split3(a)
    out = None
    for p in reversed(parts):
        d = _dot(p, sel01)
        out = d if out is None else out + d
    return out


def _sel_left(sel01, a, terms=2):
    parts = _split2(a) if terms == 2 else _split3(a)
    out = None
    for p in reversed(parts):
        d = _dot(sel01, p)
        out = d if out is None else out + d
    return out


def _silu(x):
    return x * jax.nn.sigmoid(x)


def _softplus(x):
    return jnp.maximum(x, 0.0) + jnp.log(1.0 + jnp.exp(-jnp.abs(x)))


def _rms_rows(x, g):
    ms = jnp.mean(x * x, axis=-1, keepdims=True)
    return x * lax.rsqrt(ms + EPS) * g


def _ffn_kernel(mix_fn, n_mix, x_ref, *refs):
    mix_refs = refs[:n_mix]
    g_ref, wg_ref, wu_ref, wd_ref, o_ref = refs[n_mix:]
    x = x_ref[...]
    if mix_fn is not None:
        x = x + mix_fn(*mix_refs)
    h = _rms_rows(x, g_ref[...]).astype(BF16)
    a = _dot(h, wg_ref[...])
    u = _dot(h, wu_ref[...])
    act = (_silu(a) * u).astype(BF16)
    o_ref[...] = x + 0.5 * _dot(act, wd_ref[...])


def _ffn(x, gain, w_up, w_down, layer, which, tm, mix=None):
    T = x.shape[0]
    tm = min(tm, FFN_ROWS)
    resident = dict(pipeline_mode=pl.Buffered(1))
    mix_fn, row_inputs, const_inputs = mix if mix is not None else (None, [], [])
    mix_specs, mix_args = [], []
    for arr, width, cb in row_inputs:
        mix_specs.append(pl.BlockSpec((tm, width), lambda i, cb=cb: (i, cb)))
        mix_args.append(arr)
    for arr in const_inputs:
        mix_specs.append(pl.BlockSpec(arr.shape, lambda i, nd=arr.ndim: (0,) * nd, **resident))
        mix_args.append(arr)
    return pl.pallas_call(
        functools.partial(_ffn_kernel, mix_fn, len(mix_args)),
        grid=(T // tm,),
        in_specs=[pl.BlockSpec((tm, D_MODEL), lambda i: (i, 0))] + mix_specs + [
            pl.BlockSpec((1, D_MODEL), lambda i: (0, 0), **resident),
            pl.BlockSpec((None, None, D_MODEL, D_FF), lambda i: (layer, which, 0, 0), **resident),
            pl.BlockSpec((None, None, D_MODEL, D_FF), lambda i: (layer, which, 0, 1), **resident),
            pl.BlockSpec((None, None, D_FF, D_MODEL), lambda i: (layer, which, 0, 0), **resident),
        ],
        out_specs=pl.BlockSpec((tm, D_MODEL), lambda i: (i, 0)),
        out_shape=jax.ShapeDtypeStruct((T, D_MODEL), F32),
        compiler_params=_params("parallel"),
        name="ffn",
    )(x, *mix_args, gain.reshape(1, D_MODEL), w_up, w_up, w_down)


def _proj_kernel(x_ref, g_ref, w_ref, o_ref):
    o_ref[...] = _dot(_rms_rows(x_ref[...], g_ref[...]).astype(BF16), w_ref[...])


def _proj(x, gain, w, tm):
    T = x.shape[0]
    N = w.shape[1]
    tm = min(tm, PROJ_ROWS)
    resident = dict(pipeline_mode=pl.Buffered(1))
    return pl.pallas_call(
        _proj_kernel,
        grid=(T // tm,),
        in_specs=[
            pl.BlockSpec((tm, D_MODEL), lambda i: (i, 0)),
            pl.BlockSpec((1, D_MODEL), lambda i: (0, 0), **resident),
            pl.BlockSpec((D_MODEL, N), lambda i: (0, 0), **resident),
        ],
        out_specs=pl.BlockSpec((tm, N), lambda i: (i, 0)),
        out_shape=jax.ShapeDtypeStruct((T, N), F32),
        compiler_params=_params("parallel"),
        name="proj",
    )(x, gain.reshape(1, D_MODEL), w)


def _proj_qk_kernel(flags, t_chunks, x_ref, g_ref, w_ref, bd_ref, gain_ref, mask_ref, cos_ref, sa_ref, sb_ref,
                    o_ref, *t_refs):
    h = _rms_rows(x_ref[...], g_ref[...]).astype(BF16)
    tn = bd_ref.shape[0]
    cos, sa, sb = cos_ref[...], sa_ref[...], sb_ref[...]
    for j, flag in enumerate(flags):
        y = _dot(h, w_ref[:, j * tn:(j + 1) * tn])
        if not flag:
            o_ref[:, j * tn:(j + 1) * tn] = y
            continue
        ss = _dot((y * y).astype(BF16), bd_ref[...])
        yn = y * lax.rsqrt(ss * (1.0 / HEAD_DIM) + EPS) * gain_ref[j]
        mask = mask_ref[j]
        for c in range(tn // LANES):
            sl = slice(c * LANES, (c + 1) * LANES)
            v = yn[:, sl]
            up = pltpu.roll(v, LANES - ROPE_DIM // 2, 1)
            dn = pltpu.roll(v, ROPE_DIM // 2, 1)
            rot = v * cos + up * sa + dn * sb
            if flag == 2:
                rot = jnp.where(mask[:, sl] > 0.0, rot, y[:, sl])
            o_ref[:, j * tn + c * LANES:j * tn + (c + 1) * LANES] = rot
    for k, j in enumerate(t_chunks):
        t_refs[0][0, k * tn:(k + 1) * tn, :] = o_ref[:, j * tn:(j + 1) * tn].T


def _proj_qk(x, gain, w, flags, gain_rows, mask_rows, tables, tm, tn, t_chunks=(), seq=None):
    T = x.shape[0]
    N = w.shape[1]
    nj = N // tn
    tm = min(tm, PROJ_ROWS)
    cos, sa, sb = tables
    n_pos_tiles = cos.shape[0] // tm
    bd = jnp.asarray(np.kron(np.eye(tn // HEAD_DIM), np.ones((HEAD_DIM, HEAD_DIM))), BF16)
    resident = dict(pipeline_mode=pl.Buffered(1))
    tab_spec = pl.BlockSpec((tm, LANES), lambda i: (i % n_pos_tiles, 0))
    out_specs = pl.BlockSpec((tm, N), lambda i: (i, 0))
    out_shape = jax.ShapeDtypeStruct((T, N), F32)
    if t_chunks:
        per_seq = seq // tm
        out_specs = [out_specs, pl.BlockSpec((1, len(t_chunks) * tn, tm), lambda i: (i // per_seq, 0, i % per_seq))]
        out_shape = [out_shape, jax.ShapeDtypeStruct((T // seq, len(t_chunks) * tn, seq), F32)]
    return pl.pallas_call(
        functools.partial(_proj_qk_kernel, flags, tuple(t_chunks)),
        grid=(T // tm,),
        in_specs=[
            pl.BlockSpec((tm, D_MODEL), lambda i: (i, 0)),
            pl.BlockSpec((1, D_MODEL), lambda i: (0, 0), **resident),
            pl.BlockSpec((D_MODEL, N), lambda i: (0, 0), **resident),
            pl.BlockSpec((tn, tn), lambda i: (0, 0), **resident),
            pl.BlockSpec((nj, 1, tn), lambda i: (0, 0, 0), **resident),
            pl.BlockSpec((nj, 1, tn), lambda i: (0, 0, 0), **resident),
            tab_spec, tab_spec, tab_spec,
        ],
        out_specs=out_specs,
        out_shape=out_shape,
        compiler_params=_params("parallel"),
        name="proj_qk",
    )(x, gain.reshape(1, D_MODEL), w, bd, gain_rows, mask_rows, cos, sa, sb)


def _rope_tables(pos):
    half = ROPE_DIM // 2
    inv = ROPE_THETA ** (-jnp.arange(half, dtype=F32) / half)
    ang = pos.astype(F32)[:, None] * inv[None, :]
    cos, sin = jnp.cos(ang), jnp.sin(ang)
    n = pos.shape[0]
    one = jnp.ones((n, HEAD_DIM - ROPE_DIM), F32)
    zero = jnp.zeros((n, HEAD_DIM - ROPE_DIM), F32)
    zh = jnp.zeros((n, half), F32)
    c64 = jnp.concatenate([cos, cos, one], axis=1)
    sa64 = jnp.concatenate([-sin, zh, zero], axis=1)
    sb64 = jnp.concatenate([zh, sin, zero], axis=1)
    rep = LANES // HEAD_DIM
    return tuple(jnp.tile(t, (1, rep)) for t in (c64, sa64, sb64))


def _mix_plain(o_ref, w_ref):
    return _dot(o_ref[...].astype(BF16), w_ref[...])


def _mix_a(o0, o1, o2, l0, l1, l2, w_ref):
    la, lb, lc = l0[...], l1[...], l2[...]
    m = jnp.maximum(jnp.maximum(la, lb), lc)
    ea, eb, ec = jnp.exp(la - m), jnp.exp(lb - m), jnp.exp(lc - m)
    o = (ea * o0[...] + eb * o1[...] + ec * o2[...]) / (ea + eb + ec)
    return _dot(o.astype(BF16), w_ref[...])


def _mix_c(o_ref, r_ref, g_ref, w_ref):
    o = o_ref[...]
    g = g_ref[...]
    parts = [_rms_rows(o[:, h * C_DV:(h + 1) * C_DV], g) for h in range(C_HEADS)]
    on = jnp.concatenate(parts, axis=1) * _silu(r_ref[...])
    return _dot(on.astype(BF16), w_ref[...])


def _mix_d(o_ref, z_ref, g_ref, w_ref):
    gated = o_ref[...] * _silu(z_ref[...])
    gw = D_INNER // D_GROUPS
    g = g_ref[...]
    parts = [_rms_rows(gated[:, k * gw:(k + 1) * gw], g[:, k * gw:(k + 1) * gw]) for k in range(D_GROUPS)]
    return _dot(jnp.concatenate(parts, axis=1).astype(BF16), w_ref[...])


def _a_prompt_kernel(d, q_ref, kp_ref, kc_ref, vp_ref, vc_ref, o_ref, l_ref):
    j = pl.program_id(1)
    heads = q_ref.shape[2] // HEAD_DIM
    per_pass = max(1, A_HEADS // heads)
    n_prob = per_pass * heads
    scale = HEAD_DIM ** -0.5
    qq = lax.broadcasted_iota(jnp.int32, (n_prob * A_SPAN, 2 * A_SPAN), 0) & (A_SPAN - 1)
    kk = lax.broadcasted_iota(jnp.int32, (n_prob * A_SPAN, 2 * A_SPAN), 1)
    rel = qq + A_SPAN - kk
    valid = (rel >= 0) & (rel <= A_SPAN) & ((kk >= A_SPAN) | (j > 0))
    hs = [slice(h * HEAD_DIM, (h + 1) * HEAD_DIM) for h in range(heads)]
    for r0 in range(0, d, per_pass):
        rows = [pl.ds(r0 + e, A_SPAN, stride=d) if d > 1 else pl.ds(0, A_SPAN) for e in range(per_pass)]
        s, vs = [], []
        for rw in rows:
            q = (q_ref[0, rw, :] * scale).astype(BF16)
            k = jnp.concatenate([kp_ref[0, rw, :], kc_ref[0, rw, :]], axis=0).astype(BF16)
            vs.append(jnp.concatenate([vp_ref[0, rw, :], vc_ref[0, rw, :]], axis=0).astype(BF16))
            s += [_dot_nt(q[:, sl], k[:, sl]) for sl in hs]
        s = jnp.where(valid, jnp.concatenate(s, axis=0), -jnp.inf)
        m = jnp.max(s, axis=-1, keepdims=True)
        p = jnp.exp(s - m)
        den = jnp.sum(p, axis=-1, keepdims=True)
        pb = p.astype(BF16)
        lse = m + jnp.log(den)
        inv = 1.0 / den
        for e, rw in enumerate(rows):
            o, l = [], []
            for h, sl in enumerate(hs):
                blk = slice((e * heads + h) * A_SPAN, (e * heads + h + 1) * A_SPAN)
                o.append(_dot(pb[blk], vs[e][:, sl]) * inv[blk])
                l.append(jnp.broadcast_to(lse[blk], (A_SPAN, HEAD_DIM)))
            o_ref[0, rw, :] = jnp.concatenate(o, axis=1)
            l_ref[0, rw, :] = jnp.concatenate(l, axis=1)


def _a_prompt(y, g, Bn, S):
    d = A_DILATIONS[g]
    slab = A_SPAN * d
    heads = A_HEADS if d == 1 else LANES // HEAD_DIM
    cols = heads * HEAD_DIM
    ncb = A_WIDTH // cols
    y3 = y.reshape(Bn, S, A_GROUPS * 3 * A_WIDTH)

    def spec(which, prev):
        c0 = (3 * g + which) * ncb
        if prev:
            return pl.BlockSpec((1, slab, cols), lambda b, j, c: (b, jnp.maximum(j - 1, 0), c0 + c))
        return pl.BlockSpec((1, slab, cols), lambda b, j, c: (b, j, c0 + c))

    out_spec = pl.BlockSpec((1, slab, cols), lambda b, j, c: (b, j, c))
    o, l = pl.pallas_call(
        functools.partial(_a_prompt_kernel, d),
        grid=(Bn, S // slab, ncb),
        in_specs=[spec(0, False), spec(1, True), spec(1, False), spec(2, True), spec(2, False)],
        out_specs=[out_spec, out_spec],
        out_shape=[jax.ShapeDtypeStruct((Bn, S, A_WIDTH), F32)] * 2,
        compiler_params=_params("parallel", "arbitrary", "arbitrary"),
        name="a_prompt_d%d" % d,
    )(y3, y3, y3, y3, y3)
    return o.reshape(Bn * S, A_WIDTH), l.reshape(Bn * S, A_WIDTH)


def _nt_sel(mask, a):
    out = None
    for p in reversed(_split3(a)):
        d = _dot_nt(mask, p)
        out = d if out is None else out + d
    return out


def _a_sample_kernel(T, y_ref, c1_ref, c2_ref, c3_ref, hind_ref, hindt_ref, *out_refs):
    caches = (c1_ref, c2_ref, c3_ref)
    width = 3 * A_WIDTH
    scale = HEAD_DIM ** -0.5
    hind = hind_ref[...]
    hindt = hindt_ref[...]
    lane = lax.broadcasted_iota(jnp.int32, (8, LANES), 1)
    sub = lax.broadcasted_iota(jnp.int32, (8, LANES), 0)
    zrows = jnp.zeros((LANES - T, A_WIDTH), F32)
    zsq = jnp.zeros((LANES - 8, LANES), F32)

    def head_rows(prod):
        return _nt_sel(hind, jnp.concatenate([prod, zrows], axis=0))

    def sublane_total(x):
        x = x + pltpu.roll(x, 4, 0)
        x = x + pltpu.roll(x, 2, 0)
        return x + pltpu.roll(x, 1, 0)

    def head_scores(k_of, pat_of):
        out = jnp.zeros((8, LANES), F32)
        for h in range(A_HEADS):
            prod = k_of(h) * pat_of(h)
            part = prod[0:8]
            for j in range(1, HEAD_DIM // 8):
                part = part + prod[8 * j:8 * j + 8]
            out = jnp.where(sub == h, sublane_total(part), out)
        return out

    def rows_of(packed):
        return jnp.concatenate([packed, zsq], axis=0).T

    def widen(rows8):
        return _sel_right(rows8, hindt, terms=3)

    def place(col, at):
        return jnp.where(lane == at, col, 0.0)

    def patterns(q_t, pick):
        return _sel_right(q_t, pick.astype(BF16), terms=3)

    row_sq = lax.broadcasted_iota(jnp.int32, (LANES, LANES), 0)
    lane_sq = lax.broadcasted_iota(jnp.int32, (LANES, LANES), 1)

    prep = []
    for g in range(A_GROUPS):
        d = A_DILATIONS[g]
        q_nat = y_ref[0, :, g * width:g * width + A_WIDTH] * scale
        kn_nat = y_ref[0, :, g * width + A_WIDTH:g * width + 2 * A_WIDTH]
        vn_nat = y_ref[0, :, g * width + 2 * A_WIDTH:(g + 1) * width]
        q_t = jnp.concatenate([q_nat, zrows], axis=0).T
        if d == 1:
            pats = [patterns(q_t, row_sq == i) for i in range(T)]
            news = [head_rows(q_nat[i:i + 1, :] * kn_nat) for i in range(T)]
        else:
            pats = [patterns(q_t, (row_sq == (lane_sq & (d - 1))) & (row_sq < T))]
            news = [head_rows(q_nat * kn_nat)]
        prep.append((vn_nat, pats, news))

    swept = []
    for g in range(A_GROUPS):
        d = A_DILATIONS[g]
        c_ref = caches[g]
        n_tiles = c_ref.shape[-1] // LANES
        _, pats, news = prep[g]
        if d == 1:
            packed = jnp.zeros((8, LANES), F32)
            accs = []
            for i in range(T):
                pat = pats[i]
                s = head_scores(lambda h: c_ref[0, 0, h], lambda h: pat[h * HEAD_DIM:(h + 1) * HEAD_DIM])
                s = jnp.where(lane >= i, s, -jnp.inf)
                sn = jnp.where(lane <= i, news[i], -jnp.inf)
                m = jnp.maximum(jnp.max(s, axis=1, keepdims=True), jnp.max(sn, axis=1, keepdims=True))
                p = jnp.exp(s - m)
                pn = jnp.exp(sn - m)
                den = jnp.sum(p, axis=1, keepdims=True) + jnp.sum(pn, axis=1, keepdims=True)
                accs.append(jnp.concatenate([c_ref[0, 1, h] * p[h:h + 1, :] for h in range(A_HEADS)], axis=0))
                packed = packed + place(den, i) + place(m + jnp.log(den), 8 + i)
                packed = packed + pltpu.roll(pn, 16 + 8 * i, 1)
            swept.append((accs, packed))
        else:
            cls = lane & (d - 1)
            self_s = news[0]
            pat = pats[0]
            s_tiles = [head_scores(lambda h: c_ref[0, 0, h, :, t * LANES:(t + 1) * LANES],
                                   lambda h: pat[h * HEAD_DIM:(h + 1) * HEAD_DIM])
                       for t in range(n_tiles)]
            smax = s_tiles[0]
            for t in range(1, n_tiles):
                smax = jnp.maximum(smax, s_tiles[t])
            mrow = jnp.full((8, LANES), jnp.inf, F32)
            m_cls = []
            for i in range(T):
                mi = jnp.maximum(jnp.max(jnp.where(cls == i, smax, -jnp.inf), axis=1, keepdims=True),
                                 self_s[:, i:i + 1])
                m_cls.append(mi)
                mrow = jnp.where(cls == i, mi, mrow)
            p_tiles = [jnp.exp(s_tiles[t] - mrow) for t in range(n_tiles)]
            psum = p_tiles[0]
            for t in range(1, n_tiles):
                psum = psum + p_tiles[t]
            accs = []
            for h in range(A_HEADS):
                acc = jnp.zeros((HEAD_DIM, LANES), F32)
                for t in range(n_tiles):
                    acc = acc + c_ref[0, 1, h, :, t * LANES:(t + 1) * LANES] * p_tiles[t][h:h + 1, :]
                accs.append(acc)
            packed = jnp.zeros((8, LANES), F32)
            for i in range(T):
                ps = jnp.exp(self_s[:, i:i + 1] - m_cls[i])
                den = jnp.sum(jnp.where(cls == i, psum, 0.0), axis=1, keepdims=True) + ps
                packed = packed + place(ps, i) + place(den, 8 + i) + place(m_cls[i] + jnp.log(den), 16 + i)
            swept.append(([jnp.concatenate(accs, axis=0)], packed))

    for g in range(A_GROUPS):
        d = A_DILATIONS[g]
        vn_nat = prep[g][0]
        accs, packed = swept[g]
        sq = rows_of(packed)
        if d == 1:
            first = (sub == 0).astype(BF16)
            denx = widen(sq[0:8])
            lsex = widen(sq[8:16])
            for i in range(T):
                pnx = widen(sq[16 + 8 * i:24 + 8 * i])
                o = _nt_sel(first, accs[i])[0:1, :] + jnp.sum(pnx[0:T] * vn_nat, axis=0, keepdims=True)
                out_refs[g][0, i:i + 1, :] = o / denx[i:i + 1, :]
            out_refs[A_GROUPS + g][0] = lsex[0:T]
        else:
            cls = lane & (d - 1)
            res = _nt_sel((cls == sub).astype(BF16), accs[0])
            psx, denx, lsex = widen(sq[0:8]), widen(sq[8:16]), widen(sq[16:24])
            out_refs[g][0] = (res[0:T] + psx[0:T] * vn_nat) / denx[0:T]
            out_refs[A_GROUPS + g][0] = lsex[0:T]


def _a_sample(y, caches, Bd, T):
    width = A_GROUPS * 3 * A_WIDTH
    y3 = y.reshape(Bd, T, width)
    views = []
    specs = [pl.BlockSpec((1, T, width), lambda b: (b, 0, 0))]
    for g, c in enumerate(caches):
        d = A_DILATIONS[g]
        assert c.shape[1] == A_SPAN * d and (d == 1 or T <= d) and T <= 8
        views.append(c.transpose(0, 2, 3, 4, 1))
        specs.append(pl.BlockSpec((1, 2, A_HEADS, HEAD_DIM, A_SPAN * d), lambda b: (b, 0, 0, 0, 0)))
    hind_np = np.zeros((LANES, A_WIDTH), np.float32)
    hind_np[np.arange(A_WIDTH) // HEAD_DIM, np.arange(A_WIDTH)] = 1.0
    hind = jnp.asarray(hind_np[:A_HEADS], BF16)
    hindt = jnp.asarray(hind_np, BF16)
    specs += [pl.BlockSpec(hind.shape, lambda b: (0, 0)), pl.BlockSpec(hindt.shape, lambda b: (0, 0))]
    out_spec = pl.BlockSpec((1, T, A_WIDTH), lambda b: (b, 0, 0))
    outs = pl.pallas_call(
        functools.partial(_a_sample_kernel, T),
        grid=(Bd,),
        in_specs=specs,
        out_specs=[out_spec] * (2 * A_GROUPS),
        out_shape=[jax.ShapeDtypeStruct((Bd, T, A_WIDTH), F32)] * (2 * A_GROUPS),
        compiler_params=_params("parallel"),
        name="a_sample",
    )(y3, *views, hind, hindt)
    return [o.reshape(Bd * T, A_WIDTH) for o in outs]


def _mixer_a(xp, xs, gain, w_in, qk_gain, w_out, caches, tabs_p, tabs_s, Bn, S, Bd, T):
    flags = (1, 1, 0) * A_GROUPS
    ones = jnp.ones((A_WIDTH,), F32)
    rows = [jnp.tile(qk_gain[0], A_HEADS), jnp.tile(qk_gain[1], A_HEADS), ones] * A_GROUPS
    gain_rows = jnp.stack(rows).reshape(3 * A_GROUPS, 1, A_WIDTH)
    mask_rows = jnp.stack([ones, ones, 0.0 * ones] * A_GROUPS).reshape(3 * A_GROUPS, 1, A_WIDTH)
    w = w_in.astype(BF16)
    yp = _proj_qk(xp, gain, w, flags, gain_rows, mask_rows, tabs_p, 1024, A_WIDTH)
    ys = _proj_qk(xs, gain, w, flags, gain_rows, mask_rows, tabs_s, xs.shape[0], A_WIDTH)
    wo = w_out.astype(BF16)
    pr = [_a_prompt(yp, g, Bn, S) for g in range(A_GROUPS)]
    mix_p = (_mix_a, [(o, A_WIDTH, 0) for o, _ in pr] + [(l, A_WIDTH, 0) for _, l in pr], [wo])
    sr = _a_sample(ys, caches, Bd, T)
    mix_s = (_mix_a, [(o, A_WIDTH, 0) for o in sr], [wo])
    yp3 = yp.reshape(Bn, S, A_GROUPS * 3 * A_WIDTH)
    ys3 = ys.reshape(Bd, T, A_GROUPS * 3 * A_WIDTH)
    new_p, new_s = [], []
    for g in range(A_GROUPS):
        win = min(A_SPAN * A_DILATIONS[g], S)
        c0 = (3 * g + 1) * A_WIDTH
        new_p.append(yp3[:, S - win:, c0:c0 + 2 * A_WIDTH].reshape(Bn, win, 2, A_HEADS, HEAD_DIM))
        new_s.append(ys3[:, :, c0:c0 + 2 * A_WIDTH].reshape(Bd, T, 2, A_HEADS, HEAD_DIM))
    return mix_p, mix_s, new_p, new_s


def _b_kmean_kernel(k_ref, o_ref):
    nblk = k_ref.shape[1] // B_BLOCK
    rows = [jnp.mean(k_ref[0, n * B_BLOCK:(n + 1) * B_BLOCK, :], axis=0, keepdims=True) for n in range(nblk)]
    o_ref[0] = jnp.concatenate(rows, axis=0)


def _b_gate_kernel(q_ref, kmh_ref, kml_ref, o_ref):
    i = pl.program_id(1)
    qh, ql = _split2(q_ref[0])
    kmh, kml = kmh_ref[0], kml_ref[0]
    gate = _dot_nt(kmh, qh) + (_dot_nt(kmh, ql) + _dot_nt(kml, qh))
    tq = gate.shape[1]
    nblk = gate.shape[0] // B_HEADS
    gate = gate.reshape(B_HEADS, nblk, tq)
    blk = lax.broadcasted_iota(jnp.int32, (B_HEADS, nblk, tq), 1)
    blkf = blk.astype(F32)
    gate = jnp.where(blk < i, gate, -jnp.inf)
    sel = jnp.zeros((B_HEADS, nblk, tq), jnp.bool_)
    for _ in range(B_TOPK):
        mx = jnp.max(gate, axis=1, keepdims=True)
        first = jnp.min(jnp.where(gate == mx, blkf, float(nblk)), axis=1, keepdims=True)
        hit = blkf == first
        sel = sel | (hit & (mx > -jnp.inf))
        gate = jnp.where(hit, -jnp.inf, gate)
    bias = jnp.where(sel | (blk >= i), 0.0, NEG).reshape(B_HEADS * nblk, tq)
    o_ref[0] = bias.T


def _b_attn_kernel(qi_ref, kn_ref, q_ref, k_ref, vt_ref, o_ref, *state):
    t = pl.program_id(1)
    i, n = qi_ref[t], kn_ref[t]
    tq = B_BLOCK
    wide = B_REP * tq
    m_refs, acc_refs = state[:B_KV_HEADS], state[B_KV_HEADS:]

    @pl.when(n == 0)
    def _():
        for kvh in range(B_KV_HEADS):
            m_refs[kvh][...] = jnp.full_like(m_refs[kvh], -jnp.inf)
            acc_refs[kvh][...] = jnp.zeros_like(acc_refs[kvh])

    def sweep(causal):
        if causal:
            kk = lax.broadcasted_iota(jnp.int32, (B_BLOCK, wide), 0)
            qq = lax.broadcasted_iota(jnp.int32, (B_BLOCK, wide), 1) & (tq - 1)
            keep = kk <= qq
        scores = []
        for kvh in range(B_KV_HEADS):
            k = k_ref[0, :, kvh * LANES:(kvh + 1) * LANES]
            s = _dot_nt(k, q_ref[0, 0, kvh])
            scores.append(jnp.where(keep, s, NEG) if causal else s)
        probs, scales = [], []
        for kvh in range(B_KV_HEADS):
            m_old = m_refs[kvh][...]
            m_new = jnp.maximum(m_old, jnp.max(scores[kvh], axis=0, keepdims=True))
            probs.append(jnp.exp2(scores[kvh] - m_new).astype(BF16))
            scales.append(jnp.exp2(m_old - m_new))
            m_refs[kvh][...] = m_new
        for kvh in range(B_KV_HEADS):
            acc_refs[kvh][...] = scales[kvh] * acc_refs[kvh][...] + _dot(vt_ref[0, kvh], probs[kvh])

    @pl.when(n < i)
    def _():
        sweep(False)

    @pl.when(n == i)
    def _():
        sweep(True)
        for kvh in range(B_KV_HEADS):
            acc = acc_refs[kvh][...]
            ot = acc[0:HEAD_DIM] / acc[HEAD_DIM:HEAD_DIM + 1]
            for r in range(0, B_REP, 2):
                pair = [ot[:, (r + e) * tq:(r + e + 1) * tq].T for e in range(2)]
                c0 = (kvh * B_REP + r) * HEAD_DIM
                o_ref[0, :, c0:c0 + 2 * HEAD_DIM] = jnp.concatenate(pair, axis=1)


def _b_prompt(y, Bn, S):
    nblk = S // B_BLOCK
    assert nblk % 8 == 0 and LANES % nblk == 0
    nq = B_HEADS * HEAD_DIM
    nk = B_KV_HEADS * HEAD_DIM
    y3 = y.reshape(Bn, S, nq + 2 * nk)
    kmean = pl.pallas_call(
        _b_kmean_kernel,
        grid=(Bn,),
        in_specs=[pl.BlockSpec((1, S, nk), lambda b: (b, 0, nq // nk))],
        out_specs=pl.BlockSpec((1, nblk, nk), lambda b: (b, 0, 0)),
        out_shape=jax.ShapeDtypeStruct((Bn, nblk, nk), F32),
        compiler_params=_params("parallel"),
        name="b_kmean",
    )(y3)
    km = kmean.reshape(Bn, nblk, B_KV_HEADS, HEAD_DIM)
    km = jnp.repeat(km, B_REP, axis=2)
    eye = jnp.eye(B_HEADS, dtype=F32)
    kmbd = jnp.einsum('bnhe,hg->bgnhe', km, eye).reshape(Bn, B_HEADS * nblk, nq)
    kmh = kmbd.astype(BF16)
    kml = (kmbd - kmh.astype(F32)).astype(BF16)
    bias = pl.pallas_call(
        _b_gate_kernel,
        grid=(Bn, nblk),
        in_specs=[
            pl.BlockSpec((1, B_BLOCK, nq), lambda b, i: (b, i, 0)),
            pl.BlockSpec((1, B_HEADS * nblk, nq), lambda b, i: (b, 0, 0)),
            pl.BlockSpec((1, B_HEADS * nblk, nq), lambda b, i: (b, 0, 0)),
        ],
        out_specs=pl.BlockSpec((1, B_BLOCK, B_HEADS * nblk), lambda b, i: (b, i, 0)),
        out_shape=jax.ShapeDtypeStruct((Bn, S, B_HEADS * nblk), F32),
        compiler_params=_params("parallel", "arbitrary"),
        name="b_gate",
    )(y3, kmh, kml)
    q = (y3[:, :, :nq] * (HEAD_DIM ** -0.5 * LOG2E)).astype(BF16).reshape(Bn, S, B_HEADS, HEAD_DIM)
    pad = LANES - HEAD_DIM - nblk
    q_aug = jnp.concatenate([q, bias.astype(BF16).reshape(Bn, S, B_HEADS, nblk),
                             jnp.zeros((Bn, S, B_HEADS, pad), BF16)], axis=-1)
    q_aug = q_aug.reshape(Bn, nblk, B_BLOCK, B_KV_HEADS, B_REP, LANES).transpose(0, 1, 3, 4, 2, 5)
    q_aug = q_aug.reshape(Bn, nblk, B_KV_HEADS, B_REP * B_BLOCK, LANES)
    k = y3[:, :, nq:nq + nk].astype(BF16).reshape(Bn, S, B_KV_HEADS, HEAD_DIM)
    onehot = jax.nn.one_hot(jnp.arange(S) // B_BLOCK, nblk, dtype=BF16)
    onehot = jnp.broadcast_to(onehot[None, :, None, :], (Bn, S, B_KV_HEADS, nblk))
    k_aug = jnp.concatenate([k, onehot, jnp.zeros((Bn, S, B_KV_HEADS, pad), BF16)], axis=-1)
    k_aug = k_aug.reshape(Bn, S, B_KV_HEADS * LANES)
    vt = y3[:, :, nq + nk:].astype(BF16).reshape(Bn, S, B_KV_HEADS, HEAD_DIM).transpose(0, 2, 3, 1)
    vt = jnp.concatenate([vt, jnp.ones((Bn, B_KV_HEADS, 1, S), BF16),
                          jnp.zeros((Bn, B_KV_HEADS, B_VT_ROWS - HEAD_DIM - 1, S), BF16)], axis=2)
    pairs = [(i, n) for i in range(nblk) for n in range(i + 1)]
    qi = jnp.asarray([p[0] for p in pairs], jnp.int32)
    kn = jnp.asarray([p[1] for p in pairs], jnp.int32)
    o = pl.pallas_call(
        _b_attn_kernel,
        grid_spec=pltpu.PrefetchScalarGridSpec(
            num_scalar_prefetch=2,
            grid=(Bn, len(pairs)),
            in_specs=[
                pl.BlockSpec((1, 1, B_KV_HEADS, B_REP * B_BLOCK, LANES),
                             lambda b, t, qi, kn: (b, qi[t], 0, 0, 0)),
                pl.BlockSpec((1, B_BLOCK, B_KV_HEADS * LANES), lambda b, t, qi, kn: (b, kn[t], 0)),
                pl.BlockSpec((1, B_KV_HEADS, B_VT_ROWS, B_BLOCK), lambda b, t, qi, kn: (b, 0, 0, kn[t])),
            ],
            out_specs=pl.BlockSpec((1, B_BLOCK, nq), lambda b, t, qi, kn: (b, qi[t], 0)),
            scratch_shapes=([pltpu.VMEM((1, B_REP * B_BLOCK), F32)] * B_KV_HEADS
                            + [pltpu.VMEM((B_VT_ROWS, B_REP * B_BLOCK), F32)] * B_KV_HEADS),
        ),
        out_shape=jax.ShapeDtypeStruct((Bn, S, nq), F32),
        compiler_params=_params("parallel", "arbitrary"),
        name="b_attn",
    )(qi, kn, q_aug, k_aug, vt)
    return o.reshape(Bn * S, nq)


def _b_sample_kernel(n_pages, T, pt_ref, q_ref, kn_ref, vn_ref, *refs):
    del pt_ref
    page_refs = refs[:n_pages]
    o_ref = refs[n_pages]
    nk = B_KV_HEADS * HEAD_DIM
    q = q_ref[0]
    qh, ql = _split2(q)
    rows = q.shape[0]
    page_rows = page_refs[0].shape[2]
    pages_per_block = B_BLOCK // page_rows
    n_blocks = n_pages // pages_per_block
    scale = HEAD_DIM ** -0.5
    lane = lax.broadcasted_iota(jnp.int32, (1, LANES), 1)
    lanef = lax.broadcasted_iota(jnp.int32, (rows, LANES), 1).astype(F32)
    scores = []
    block_sum = [None] * n_blocks
    for p in range(n_pages):
        kt = page_refs[p][0, :nk, :]
        scores.append(_dot(qh, kt.astype(BF16)) * scale)
        b = p // pages_per_block
        block_sum[b] = kt if block_sum[b] is None else block_sum[b] + kt
    km = jnp.zeros((nk, LANES), F32)
    for b in range(n_blocks):
        ksum = jnp.sum(block_sum[b], axis=1, keepdims=True) * (1.0 / B_BLOCK)
        km = km + ksum * (lane == b).astype(F32)
    kmh, kml = _split2(km)
    gate = _dot(qh, kmh) + (_dot(qh, kml) + _dot(ql, kmh))
    gate = jnp.where(lanef < float(n_blocks), gate, -jnp.inf)
    sel = jnp.zeros((rows, LANES), jnp.bool_)
    for _ in range(min(B_TOPK, n_blocks)):
        mx = jnp.max(gate, axis=1, keepdims=True)
        first = jnp.min(jnp.where(gate == mx, lanef, float(LANES)), axis=1, keepdims=True)
        hit = lanef == first
        sel = sel | (hit & (mx > -jnp.inf))
        gate = jnp.where(hit, -jnp.inf, gate)
    self = sel.astype(F32)
    qidx = lax.broadcasted_iota(jnp.int32, (rows, 1), 0) % T
    kn, vn = kn_ref[0], vn_ref[0]
    own = []
    for j in range(T):
        sj = jnp.sum(q * kn[j:j + 1, :], axis=1, keepdims=True) * scale
        own.append(jnp.where(qidx >= j, sj, NEG))
    m = own[0]
    for j in range(1, T):
        m = jnp.maximum(m, own[j])
    for p in range(n_pages):
        b = p // pages_per_block
        scores[p] = jnp.where(self[:, b:b + 1] > 0.0, scores[p], NEG)
        m = jnp.maximum(m, jnp.max(scores[p], axis=-1, keepdims=True))
    den = jnp.zeros((rows, 1), F32)
    acc = jnp.zeros((rows, nk), F32)
    for j in range(T):
        pj = jnp.exp(own[j] - m)
        den = den + pj
        acc = acc + pj * vn[j:j + 1, :]
    for p in range(n_pages):
        pp = jnp.exp(scores[p] - m)
        den = den + jnp.sum(pp, axis=-1, keepdims=True)
        acc = acc + _dot_nt(pp.astype(BF16), page_refs[p][0, nk:, :].astype(BF16))
    acc = acc / den
    kvh = lax.broadcasted_iota(jnp.int32, (rows, 1), 0) // (T * B_REP)
    out = jnp.zeros((rows, HEAD_DIM), F32)
    for h in range(B_KV_HEADS):
        out = out + jnp.where(kvh == h, acc[:, h * HEAD_DIM:(h + 1) * HEAD_DIM], 0.0)
    o_ref[0] = out


def _b_sample(y, pool, page_table, Bd, T):
    nq = B_HEADS * HEAD_DIM
    nk = B_KV_HEADS * HEAD_DIM
    n_pages = page_table.shape[1]
    page_rows = pool.shape[1]
    assert B_BLOCK % page_rows == 0 and (n_pages * page_rows) % B_BLOCK == 0 and T <= B_BLOCK
    assert n_pages * page_rows // B_BLOCK <= LANES
    y3 = y.reshape(Bd, T, nq + 2 * nk)
    q = y3[:, :, :nq].reshape(Bd, T, B_HEADS, HEAD_DIM).transpose(0, 2, 1, 3)
    kvsel = jnp.asarray(np.kron(np.eye(B_KV_HEADS), np.ones((B_REP, 1))), F32)
    qbd = (q[:, :, :, None, :] * kvsel[None, :, None, :, None]).reshape(Bd, B_HEADS * T, nk)
    kn = y3[:, :, nq:nq + nk]
    vn = y3[:, :, nq + nk:]
    pool_t = pool.transpose(0, 2, 3, 4, 1).reshape(pool.shape[0], 2 * nk, page_rows)
    page_specs = [pl.BlockSpec((1, 2 * nk, page_rows), lambda b, pt, p=p: (pt[b, p], 0, 0))
                  for p in range(n_pages)]
    o = pl.pallas_call(
        functools.partial(_b_sample_kernel, n_pages, T),
        grid_spec=pltpu.PrefetchScalarGridSpec(
            num_scalar_prefetch=1,
            grid=(Bd,),
            in_specs=[
                pl.BlockSpec((1, B_HEADS * T, nk), lambda b, pt: (b, 0, 0)),
                pl.BlockSpec((1, T, nk), lambda b, pt: (b, 0, 0)),
                pl.BlockSpec((1, T, nk), lambda b, pt: (b, 0, 0)),
            ] + page_specs,
            out_specs=pl.BlockSpec((1, B_HEADS * T, HEAD_DIM), lambda b, pt: (b, 0, 0)),
        ),
        out_shape=jax.ShapeDtypeStruct((Bd, B_HEADS * T, HEAD_DIM), F32),
        compiler_params=_params("parallel"),
        name="b_sample",
    )(page_table, qbd, kn, vn, *([pool_t] * n_pages))
    return o.reshape(Bd, B_HEADS, T, HEAD_DIM).transpose(0, 2, 1, 3).reshape(Bd * T, nq)


def _mixer_b(xp, xs, gain, w_in, qk_gain, w_out, pool, page_table, tabs_p, tabs_s, Bn, S, Bd, T):
    tn = 512
    nq = B_HEADS * HEAD_DIM
    nk = B_KV_HEADS * HEAD_DIM
    flags = (1, 1, 2)
    ones = jnp.ones((nk,), F32)
    qg = jnp.tile(qk_gain[0], tn // HEAD_DIM)
    kg = jnp.concatenate([jnp.tile(qk_gain[1], B_KV_HEADS), ones])
    gain_rows = jnp.stack([qg, qg, kg]).reshape(3, 1, tn)
    mask_rows = jnp.stack([jnp.ones((tn,), F32), jnp.ones((tn,), F32),
                           jnp.concatenate([ones, 0.0 * ones])]).reshape(3, 1, tn)
    w = w_in.astype(BF16)
    yp, kvt = _proj_qk(xp, gain, w, flags, gain_rows, mask_rows, tabs_p, 1024, tn, t_chunks=(2,), seq=S)
    ys = _proj_qk(xs, gain, w, flags, gain_rows, mask_rows, tabs_s, xs.shape[0], tn)
    wo = w_out.astype(BF16)
    op = _b_prompt(yp, Bn, S)
    mix_p = (_mix_plain, [(op, nq, 0)], [wo])
    osm = _b_sample(ys, pool, page_table, Bd, T)
    mix_s = (_mix_plain, [(osm, nq, 0)], [wo])
    kv_p = kvt.reshape(Bn, 2, B_KV_HEADS, HEAD_DIM, S).transpose(0, 4, 1, 2, 3)
    kv_s = ys[:, nq:].reshape(Bd, T, 2, B_KV_HEADS, HEAD_DIM)
    return mix_p, mix_s, kv_p, kv_s


def _gla_kernel(nv, q_ref, k_ref, v_ref, glr_ref, wg_ref, bg_ref, tri_ref, blk_ref, s0_ref,
                o_ref, sT_ref, s_ref, qp_ref, kp_ref, vp_ref, gp_ref):
    t = pl.program_id(1)
    Tt = SEQ_TILE

    @pl.when(t == 0)
    def _():
        s_ref[...] = s0_ref[0]

    if nv < Tt:
        qp_ref[...] = jnp.zeros_like(qp_ref)
        kp_ref[...] = jnp.zeros_like(kp_ref)
        vp_ref[...] = jnp.zeros_like(vp_ref)
        gp_ref[...] = jnp.zeros_like(gp_ref)
    qp_ref[0:nv, :] = q_ref[0]
    kp_ref[0:nv, :] = k_ref[0]
    vp_ref[0:nv, :] = v_ref[0]
    gp_ref[0:nv, :] = glr_ref[0]
    q, k, v = qp_ref[...], kp_ref[...], vp_ref[...]

    x = _dot(gp_ref[...].astype(BF16), wg_ref[...]) + bg_ref[...]
    la = (jnp.minimum(x, 0.0) - jnp.log(1.0 + jnp.exp(-jnp.abs(x)))) * (1.0 / C_TAU)
    row = lax.broadcasted_iota(jnp.int32, (Tt, C_KEY), 0)
    la = jnp.where(row < nv, la, 0.0)
    b = _sel_left(tri_ref[...], la, terms=3)
    bend = _sel_left(blk_ref[...], la, terms=3)
    qe = q * (C_DK ** -0.5) * jnp.exp(b)
    ke = k * jnp.exp(-b)
    kd = k * jnp.exp(bend - b)
    kdt = kd.T
    bendt = bend.T
    qeb, keb, vb = qe.astype(BF16), ke.astype(BF16), v.astype(BF16)
    causal = tri_ref[...] > 0
    lane_t = lax.broadcasted_iota(jnp.int32, (C_DK, Tt), 1)
    n_chunks = -(-nv // C_CHUNK)
    ksl = [slice(h * C_DK, (h + 1) * C_DK) for h in range(C_HEADS)]
    vsl = [slice(h * C_DV, (h + 1) * C_DV) for h in range(C_HEADS)]
    att = [jnp.where(causal, _dot_nt(qeb[:, ks], keb[:, ks]), 0.0).astype(BF16) for ks in ksl]
    chunk_kv, chunk_dec = [], []
    for c in range(n_chunks):
        in_chunk = (lane_t >= c * C_CHUNK) & (lane_t < (c + 1) * C_CHUNK)
        last = lane_t == (c + 1) * C_CHUNK - 1
        chunk_kv.append([_dot(jnp.where(in_chunk, kdt[ks, :], 0.0).astype(BF16), vb[:, vs])
                         for ks, vs in zip(ksl, vsl)])
        chunk_dec.append([jnp.exp(jnp.sum(jnp.where(last, bendt[ks, :], 0.0), axis=1, keepdims=True))
                          for ks in ksl])
    o_intra = [_dot(att[h], vb[:, vsl[h]]) for h in range(C_HEADS)]
    for h in range(C_HEADS):
        st = s_ref[h]
        parts = []
        for c in range(n_chunks):
            rows = slice(c * C_CHUNK, (c + 1) * C_CHUNK)
            parts.append(o_intra[h][rows] + _dot(qeb[rows, ksl[h]], st.astype(BF16)))
            st = chunk_dec[c][h] * st + chunk_kv[c][h]
        s_ref[h] = st
        oh = parts[0] if n_chunks == 1 else jnp.concatenate(parts, axis=0)
        o_ref[0, :, vsl[h]] = oh[0:nv]

    @pl.when(t == pl.num_programs(1) - 1)
    def _():
        sT_ref[0] = s_ref[...]


def _gla(y, w_gate2, b_gate, s0, nb, nt, nv):
    Tt = SEQ_TILE
    y3 = y.reshape(nb * nt, nv, C_NPAD)
    wg = jnp.zeros((LANES, C_KEY), F32).at[:C_RANK].set(w_gate2).astype(BF16)
    idx = np.arange(Tt)
    same = (idx[:, None] // C_CHUNK) == (idx[None, :] // C_CHUNK)
    tri = jnp.asarray(same & (idx[None, :] <= idx[:, None]), BF16)
    blk = jnp.asarray(same, BF16)

    def yspec(width, cb):
        return pl.BlockSpec((1, nv, width), lambda b, t: (b * nt + t, 0, cb))

    def cspec(a):
        return pl.BlockSpec(a.shape, lambda b, t, nd=a.ndim: (0,) * nd)

    bg = b_gate.reshape(1, C_KEY)
    o, sT = pl.pallas_call(
        functools.partial(_gla_kernel, nv),
        grid=(nb, nt),
        in_specs=[
            yspec(C_KEY, 0), yspec(C_KEY, 1), yspec(C_VAL, 1), yspec(LANES, (2 * C_KEY + 2 * C_VAL) // LANES),
            cspec(wg), cspec(bg), cspec(tri), cspec(blk),
            pl.BlockSpec((1, C_HEADS, C_DK, C_DV), lambda b, t: (b, 0, 0, 0)),
        ],
        out_specs=[
            pl.BlockSpec((1, nv, C_VAL), lambda b, t: (b * nt + t, 0, 0)),
            pl.BlockSpec((1, C_HEADS, C_DK, C_DV), lambda b, t: (b, 0, 0, 0)),
        ],
        out_shape=[
            jax.ShapeDtypeStruct((nb * nt, nv, C_VAL), F32),
            jax.ShapeDtypeStruct((nb, C_HEADS, C_DK, C_DV), F32),
        ],
        scratch_shapes=[
            pltpu.VMEM((C_HEADS, C_DK, C_DV), F32),
            pltpu.VMEM((Tt, C_KEY), F32), pltpu.VMEM((Tt, C_KEY), F32),
            pltpu.VMEM((Tt, C_VAL), F32), pltpu.VMEM((Tt, LANES), F32),
        ],
        compiler_params=_params("parallel", "arbitrary"),
        name="gla",
    )(y3, y3, y3, y3, wg, bg, tri, blk, s0)
    return o.reshape(nb * nt * nv, C_VAL), sT


def _mixer_c(xp, xs, gain, w_in, w_gate2, b_gate, norm_g, w_out, state, Bn, S, Bd, T):
    n_in = w_in.shape[1]
    w = jnp.zeros((D_MODEL, C_NPAD), BF16).at[:, :n_in].set(w_in.astype(BF16))
    yp = _proj(xp, gain, w, 1024)
    ys = _proj(xs, gain, w, xs.shape[0])
    zero = jnp.zeros((Bn, C_HEADS, C_DK, C_DV), F32)
    op, sp = _gla(yp, w_gate2, b_gate, zero, Bn, S // SEQ_TILE, SEQ_TILE)
    osm, ss = _gla(ys, w_gate2, b_gate, state, Bd, 1, T)
    wo = w_out.astype(BF16)
    ng = norm_g.reshape(1, C_DV)
    rcb = (2 * C_KEY + C_VAL) // C_VAL
    mix_p = (_mix_c, [(op, C_VAL, 0), (yp, C_VAL, rcb)], [ng, wo])
    mix_s = (_mix_c, [(osm, C_VAL, 0), (ys, C_VAL, rcb)], [ng, wo])
    return mix_p, mix_s, sp, ss


def _ssd_kernel(nv, xa_ref, xb_ref, bc_ref, dt_ref, cs_ref, cw_ref, cb_ref, dtb_ref, alog_ref, dsk_ref,
                tri_ref, exp_ref, expt_ref, h0_ref, y_ref, hT_ref, h_ref, xp_ref, dtp_ref):
    t = pl.program_id(1)
    Tt = SEQ_TILE
    pre = 8

    @pl.when(t == 0)
    def _():
        h_ref[...] = h0_ref[0]
        xp_ref[0:pre, :] = cs_ref[0]

    if nv < Tt:
        xp_ref[pre:, :] = jnp.zeros((Tt, D_XBC), F32)
        dtp_ref[...] = jnp.zeros_like(dtp_ref)
    xp_ref[pre:pre + nv, 0:1024] = xa_ref[0]
    xp_ref[pre:pre + nv, 1024:2048] = xb_ref[0]
    xp_ref[pre:pre + nv, 2048:3072] = bc_ref[0]
    dtp_ref[0:nv, :] = dt_ref[0]

    full = xp_ref[...]
    conv = cb_ref[...] + full[pre:] * cw_ref[D_CONV - 1:D_CONV, :]
    for k in range(1, D_CONV):
        conv = conv + pltpu.roll(full, k, 0)[pre:] * cw_ref[D_CONV - 1 - k:D_CONV - k, :]
    xp_ref[0:pre, :] = xp_ref[Tt:Tt + pre, :]
    xbc = _silu(conv)
    x = xbc[:, :D_INNER]
    nbc = D_GROUPS * D_STATE
    bm = xbc[:, D_INNER:D_INNER + nbc].astype(BF16)
    cm = xbc[:, D_INNER + nbc:].astype(BF16)

    row = lax.broadcasted_iota(jnp.int32, (Tt, LANES), 0)
    lane = lax.broadcasted_iota(jnp.int32, (Tt, LANES), 1)
    live = (row < nv) & (lane < D_HEADS)
    dt = jnp.where(live, _softplus(dtp_ref[...] + dtb_ref[...]), 0.0)
    cum = _sel_left(tri_ref[...], dt * (-jnp.exp(alog_ref[...])), terms=3)
    cumt = cum.T
    cend = cum[Tt - 1:Tt, :]
    ex = exp_ref[...]
    stack = jnp.concatenate([jnp.exp(cend - cum) * dt, dt, jnp.exp(cum)], axis=0)
    wide = _sel_right(stack, ex)
    x_state = (x * wide[0:Tt]).astype(BF16)
    x_dt = (x * wide[Tt:2 * Tt]).astype(BF16)
    off_scale = wide[2 * Tt:]

    causal = tri_ref[...] > 0
    gw = D_INNER // D_GROUPS
    cbs, offs = [], []
    for g in range(D_GROUPS):
        cg = cm[:, g * D_STATE:(g + 1) * D_STATE]
        cbs.append(_dot_nt(cg, bm[:, g * D_STATE:(g + 1) * D_STATE]))
        offs.append(_dot_nt(cg, h_ref[g * gw:(g + 1) * gw, :].astype(BF16)))
    mats = []
    for j in range(D_HEADS):
        seg = cum[:, j:j + 1] - cumt[j:j + 1, :]
        mats.append((cbs[j // D_HPG] * jnp.exp(jnp.where(causal, seg, -jnp.inf))).astype(BF16))
    diag = [_dot(mats[j], x_dt[:, j * D_HEADDIM:(j + 1) * D_HEADDIM]) for j in range(D_HEADS)]
    y = jnp.concatenate(diag, axis=1) + jnp.concatenate(offs, axis=1) * off_scale + dsk_ref[...] * x
    y_ref[0] = y[0:nv]

    dcol = jnp.broadcast_to(jnp.exp(cumt[:, Tt - 1:Tt]), (LANES, D_STATE))
    dfull = _sel_left(expt_ref[...], dcol)
    xst = x_state.astype(F32).T.astype(BF16)
    for g in range(D_GROUPS):
        rows = slice(g * gw, (g + 1) * gw)
        h_ref[rows, :] = dfull[rows] * h_ref[rows, :] + _dot(xst[rows], bm[:, g * D_STATE:(g + 1) * D_STATE])

    @pl.when(t == pl.num_programs(1) - 1)
    def _():
        hT_ref[0] = h_ref[...]


def _ssd(y, conv_w, conv_b, dt_bias, a_log, d_skip, h0, c0, nb, nt, nv):
    Tt = SEQ_TILE
    y3 = y.reshape(nb * nt, nv, D_NPAD)
    idx = np.arange(Tt)
    tri = jnp.asarray(idx[None, :] <= idx[:, None], BF16)
    ex_np = np.zeros((LANES, D_INNER), np.float32)
    ex_np[np.arange(D_INNER) // D_HEADDIM, np.arange(D_INNER)] = 1.0
    ex = jnp.asarray(ex_np, BF16)
    ext = jnp.asarray(ex_np.T, BF16)

    def pad_row(v):
        return jnp.zeros((1, LANES), F32).at[0, :D_HEADS].set(v)

    cs = jnp.zeros((nb, 8, D_XBC), F32).at[:, 8 - (D_CONV - 1):].set(c0)
    dsk = jnp.repeat(d_skip, D_HEADDIM).reshape(1, D_INNER)

    def yspec(width, cb):
        return pl.BlockSpec((1, nv, width), lambda b, t: (b * nt + t, 0, cb))

    def cspec(a):
        return pl.BlockSpec(a.shape, lambda b, t, nd=a.ndim: (0,) * nd)

    consts = [conv_w, conv_b.reshape(1, D_XBC), pad_row(dt_bias), pad_row(a_log), dsk, tri, ex, ext]
    yo, hT = pl.pallas_call(
        functools.partial(_ssd_kernel, nv),
        grid=(nb, nt),
        in_specs=[yspec(1024, 2), yspec(1024, 3), yspec(1024, 4), yspec(LANES, D_DT_COL),
                  pl.BlockSpec((1, 8, D_XBC), lambda b, t: (b, 0, 0))]
        + [cspec(a) for a in consts]
        + [pl.BlockSpec((1, D_INNER, D_STATE), lambda b, t: (b, 0, 0))],
        out_specs=[
            pl.BlockSpec((1, nv, D_INNER), lambda b, t: (b * nt + t, 0, 0)),
            pl.BlockSpec((1, D_INNER, D_STATE), lambda b, t: (b, 0, 0)),
        ],
        out_shape=[
            jax.ShapeDtypeStruct((nb * nt, nv, D_INNER), F32),
            jax.ShapeDtypeStruct((nb, D_INNER, D_STATE), F32),
        ],
        scratch_shapes=[
            pltpu.VMEM((D_INNER, D_STATE), F32),
            pltpu.VMEM((Tt + 8, D_XBC), F32),
            pltpu.VMEM((Tt, LANES), F32),
        ],
        compiler_params=_params("parallel", "arbitrary"),
        name="ssd",
    )(y3, y3, y3, y3, cs, *consts, h0)
    return yo.reshape(nb * nt * nv, D_INNER), hT


def _ssd_step_kernel(nv, xa_ref, xb_ref, bc_ref, dt_ref, cs_ref, cw_ref, cb_ref, dtb_ref, alog_ref, dsk_ref,
                     eye_ref, h0_ref, y_ref, hT_ref, xp_ref):
    R = 8
    pre = 8
    xp_ref[0:pre, :] = cs_ref[0]
    xp_ref[pre:, :] = jnp.zeros((R, D_XBC), F32)
    xp_ref[pre:pre + nv, 0:1024] = xa_ref[0]
    xp_ref[pre:pre + nv, 1024:2048] = xb_ref[0]
    xp_ref[pre:pre + nv, 2048:3072] = bc_ref[0]
    conv = cb_ref[...]
    for w in range(D_CONV):
        conv = conv + xp_ref[pl.ds(pre - (D_CONV - 1) + w, R), :] * cw_ref[w:w + 1, :]
    xbc = _silu(conv)
    x = xbc[:, :D_INNER]
    nbc = D_GROUPS * D_STATE
    bmf = xbc[:, D_INNER:D_INNER + nbc]
    cmf = xbc[:, D_INNER + nbc:]

    row = lax.broadcasted_iota(jnp.int32, (R, LANES), 0)
    lane = lax.broadcasted_iota(jnp.int32, (R, LANES), 1)
    dtp = jnp.concatenate([dt_ref[0], jnp.zeros((R - nv, LANES), F32)], axis=0)
    dt = jnp.where((row < nv) & (lane < D_HEADS), _softplus(dtp + dtb_ref[...]), 0.0)
    cum = dt * (-jnp.exp(alog_ref[...]))
    for k in (1, 2, 4):
        cum = cum + jnp.where(row >= k, pltpu.roll(cum, k, 0), 0.0)
    cend = cum[R - 1:R, :]
    group_of_lane = lane // D_HPG
    low_half = lane < D_HEADDIM

    def widen(a):
        pairs = [jnp.where(low_half, a[:, j:j + 1], a[:, j + 1:j + 2]) for j in range(0, D_HEADS, 2)]
        return jnp.concatenate(pairs, axis=1)

    yd = jnp.zeros((R, D_INNER), F32)
    for s in range(nv):
        w_s = jnp.where(row >= s, jnp.exp(cum - cum[s:s + 1, :]), 0.0) * dt[s:s + 1, :]
        prod = cmf * bmf[s:s + 1, :]
        cb_s = jnp.zeros((R, LANES), F32)
        for g in range(D_GROUPS):
            tot = jnp.sum(prod[:, g * D_STATE:(g + 1) * D_STATE], axis=1, keepdims=True)
            cb_s = jnp.where(group_of_lane == g, tot, cb_s)
        yd = yd + widen(cb_s * w_s) * x[s:s + 1, :]
    off_scale = widen(jnp.exp(cum))
    x_state = (x * widen(jnp.exp(cend - cum) * dt)).astype(BF16)

    gw = D_INNER // D_GROUPS
    cm = cmf.astype(BF16)
    parts = []
    for g in range(D_GROUPS):
        hg = h0_ref[0, g * gw:(g + 1) * gw, :]
        parts.append(_dot_nt(cm[:, g * D_STATE:(g + 1) * D_STATE], hg.astype(BF16)))
    y = yd + jnp.concatenate(parts, axis=1) * off_scale + dsk_ref[...] * x
    y_ref[0] = y[0:nv]

    erow = jnp.broadcast_to(jnp.exp(cend), (LANES, LANES))
    eye128 = (lax.broadcasted_iota(jnp.int32, (LANES, LANES), 0)
              == lax.broadcasted_iota(jnp.int32, (LANES, LANES), 1)).astype(BF16)
    dcol = _nt_sel(eye128, erow)[0:D_HEADS].reshape(D_HEADS, 1, D_STATE)
    zpad = jnp.zeros((LANES - R, D_INNER), BF16)
    xs_pad = jnp.concatenate([x_state, zpad], axis=0)
    bm_pad = jnp.concatenate([bmf.astype(BF16), jnp.zeros((LANES - R, nbc), BF16)], axis=0)
    for g in range(D_GROUPS):
        rows = slice(g * gw, (g + 1) * gw)
        xt = _dot_nt(eye_ref[...], xs_pad[:, rows]).astype(BF16)
        kept = (h0_ref[0, rows, :].reshape(D_HPG, D_HEADDIM, D_STATE) * dcol[g * D_HPG:(g + 1) * D_HPG])
        hT_ref[0, rows, :] = kept.reshape(gw, D_STATE) + _dot(xt, bm_pad[:, g * D_STATE:(g + 1) * D_STATE])


def _ssd_step(y, conv_w, conv_b, dt_bias, a_log, d_skip, h0, c0, nb, nv):
    assert nv <= 8
    y3 = y.reshape(nb, nv, D_NPAD)
    gw = D_INNER // D_GROUPS
    eye = jnp.asarray(np.eye(gw), BF16)

    def pad_row(v):
        return jnp.zeros((1, LANES), F32).at[0, :D_HEADS].set(v)

    cs = jnp.zeros((nb, 8, D_XBC), F32).at[:, 8 - (D_CONV - 1):].set(c0)
    dsk = jnp.repeat(d_skip, D_HEADDIM).reshape(1, D_INNER)

    def yspec(width, cb):
        return pl.BlockSpec((1, nv, width), lambda b: (b, 0, cb))

    def cspec(a):
        return pl.BlockSpec(a.shape, lambda b, nd=a.ndim: (0,) * nd)

    consts = [conv_w, conv_b.reshape(1, D_XBC), pad_row(dt_bias), pad_row(a_log), dsk, eye]
    yo, hT = pl.pallas_call(
        functools.partial(_ssd_step_kernel, nv),
        grid=(nb,),
        in_specs=[yspec(1024, 2), yspec(1024, 3), yspec(1024, 4), yspec(LANES, D_DT_COL),
                  pl.BlockSpec((1, 8, D_XBC), lambda b: (b, 0, 0))]
        + [cspec(a) for a in consts]
        + [pl.BlockSpec((1, D_INNER, D_STATE), lambda b: (b, 0, 0))],
        out_specs=[
            pl.BlockSpec((1, nv, D_INNER), lambda b: (b, 0, 0)),
            pl.BlockSpec((1, D_INNER, D_STATE), lambda b: (b, 0, 0)),
        ],
        out_shape=[
            jax.ShapeDtypeStruct((nb, nv, D_INNER), F32),
            jax.ShapeDtypeStruct((nb, D_INNER, D_STATE), F32),
        ],
        scratch_shapes=[pltpu.VMEM((16, D_XBC), F32)],
        compiler_params=_params("parallel"),
        name="ssd_step",
    )(y3, y3, y3, y3, cs, *consts, h0)
    return yo.reshape(nb * nv, D_INNER), hT


def _mixer_d(xp, xs, gain, w_in, conv_w, conv_b, dt_bias, a_log, d_skip, norm_g, w_out,
             ssm_state, conv_state, Bn, S, Bd, T):
    n_in = w_in.shape[1]
    w = jnp.zeros((D_MODEL, D_NPAD), BF16).at[:, :n_in].set(w_in.astype(BF16))
    yp = _proj(xp, gain, w, 1024)
    ys = _proj(xs, gain, w, xs.shape[0])
    h0p = jnp.zeros((Bn, D_INNER, D_STATE), F32)
    c0p = jnp.zeros((Bn, D_CONV - 1, D_XBC), F32)
    op, hp = _ssd(yp, conv_w, conv_b, dt_bias, a_log, d_skip, h0p, c0p, Bn, S // SEQ_TILE, SEQ_TILE)
    h0s = ssm_state.reshape(Bd, D_INNER, D_STATE)
    osm, hs = _ssd_step(ys, conv_w, conv_b, dt_bias, a_log, d_skip, h0s, conv_state, Bd, T)
    wo = w_out.astype(BF16)
    ng = norm_g.reshape(1, D_INNER)
    mix_p = (_mix_d, [(op, D_INNER, 0), (yp, D_INNER, 0)], [ng, wo])
    mix_s = (_mix_d, [(osm, D_INNER, 0), (ys, D_INNER, 0)], [ng, wo])
    keep = D_CONV - 1
    xbc_p = yp.reshape(Bn, S, D_NPAD)[:, :, D_INNER:D_INNER + D_XBC]
    xbc_s = ys.reshape(Bd, T, D_NPAD)[:, :, D_INNER:D_INNER + D_XBC]
    cp = jnp.concatenate([c0p, xbc_p], axis=1)[:, -keep:] if S < keep else xbc_p[:, S - keep:]
    cs = jnp.concatenate([conv_state, xbc_s], axis=1)[:, -keep:]
    return (mix_p, mix_s, hp.reshape(Bn, D_HEADS, D_HEADDIM, D_STATE), hs.reshape(Bd, D_HEADS, D_HEADDIM, D_STATE),
            cp, cs)


def kernel(x_prompt, x_sample, cache_a_w1, cache_a_w2, cache_a_w3, cache_b_kv, page_table, state_c, state_d_ssm, state_d_conv, norm_gain, w_ffn_up, w_ffn_down, w_a_in, a_qk_gain, w_a_out, w_b_in, b_qk_gain, w_b_out, w_c_in, w_c_gate2, b_c_gate, c_norm_gain, w_c_out, w_d_in, d_conv_w, d_conv_b, d_dt_bias, d_a_log, d_skip, d_norm_gain, w_d_out):
    Bn, S, _ = x_prompt.shape
    Bd, T, _ = x_sample.shape
    depth = norm_gain.shape[0]
    past_len = page_table.shape[1] * cache_b_kv.shape[2]
    tabs_p = _rope_tables(jnp.arange(S, dtype=jnp.int32))
    tabs_s = _rope_tables(jnp.tile(past_len + jnp.arange(T, dtype=jnp.int32), Bd))
    xp = x_prompt.reshape(Bn * S, D_MODEL)
    xs = x_sample.reshape(Bd * T, D_MODEL)
    ts = xs.shape[0]
    w_up = w_ffn_up.astype(BF16)
    w_down = w_ffn_down.astype(BF16)
    outs = {k: [] for k in ("a0p", "a0s", "a1p", "a1s", "a2p", "a2s", "bp", "bs", "cp", "cs",
                            "hp", "hs", "dp", "ds")}
    for i in range(depth):
        m, j = i % 4, i // 4
        g = norm_gain[i]
        xp = _ffn(xp, g[0], w_up, w_down, i, 0, 1024)
        xs = _ffn(xs, g[0], w_up, w_down, i, 0, ts)
        if m == 0:
            mix_p, mix_s, new_p, new_s = _mixer_a(xp, xs, g[1], w_a_in[j], a_qk_gain[j], w_a_out[j],
                                            (cache_a_w1[j], cache_a_w2[j], cache_a_w3[j]),
                                            tabs_p, tabs_s, Bn, S, Bd, T)
            for gi in range(A_GROUPS):
                outs["a%dp" % gi].append(new_p[gi])
                outs["a%ds" % gi].append(new_s[gi])
        elif m == 1:
            mix_p, mix_s, kvp, kvs = _mixer_b(xp, xs, g[1], w_b_in[j], b_qk_gain[j], w_b_out[j],
                                        cache_b_kv[j], page_table, tabs_p, tabs_s, Bn, S, Bd, T)
            outs["bp"].append(kvp)
            outs["bs"].append(kvs)
        elif m == 2:
            mix_p, mix_s, sp, ss = _mixer_c(xp, xs, g[1], w_c_in[j], w_c_gate2[j], b_c_gate[j], c_norm_gain[j],
                                      w_c_out[j], state_c[j], Bn, S, Bd, T)
            outs["cp"].append(sp)
            outs["cs"].append(ss)
        else:
            mix_p, mix_s, hp, hs, cp, cs = _mixer_d(xp, xs, g[1], w_d_in[j], d_conv_w[j], d_conv_b[j], d_dt_bias[j],
                                              d_a_log[j], d_skip[j], d_norm_gain[j], w_d_out[j],
                                              state_d_ssm[j], state_d_conv[j], Bn, S, Bd, T)
            outs["hp"].append(hp)
            outs["hs"].append(hs)
            outs["dp"].append(cp)
            outs["ds"].append(cs)
        xp = _ffn(xp, g[2], w_up, w_down, i, 1, 1024, mix_p)
        xs = _ffn(xs, g[2], w_up, w_down, i, 1, ts, mix_s)
    st = {k: jnp.stack(v) for k, v in outs.items()}
    return (xp.reshape(Bn, S, D_MODEL), xs.reshape(Bd, T, D_MODEL),
            st["a0p"], st["a0s"], st["a1p"], st["a1s"], st["a2p"], st["a2s"],
            st["bp"], st["bs"], st["cp"], st["cs"], st["hp"], st["hs"], st["dp"], st["ds"])
```

```python
import functools
import math

import numpy as np
import jax
import jax.numpy as jnp
from jax import lax
from jax.experimental import pallas as pl
from jax.experimental.pallas import tpu as pltpu

F32 = jnp.float32
BF16 = jnp.bfloat16

D_MODEL = 1024
HEAD_DIM = 64
ROPE_DIM = HEAD_DIM // 4
ROPE_THETA = 500000.0
EPS = 1e-6
D_FF = 2816
NEG = -1e30

A_GROUPS = 3
A_DILATIONS = (1, 4, 16)
A_HEADS = 8
A_WIDTH = A_HEADS * HEAD_DIM
A_SPAN = 128

B_HEADS = 16
B_KV_HEADS = 4
B_REP = B_HEADS // B_KV_HEADS
B_BLOCK = 256
B_TOPK = 3
B_VT_ROWS = HEAD_DIM + 8
LOG2E = 1.4426950408889634

C_HEADS = 4
C_KEY = 512
C_VAL = 1024
C_DK = 128
C_DV = 256
C_RANK = 16
C_TAU = 16.0
C_CHUNK = 32
C_NPAD = 3200

D_INNER = 2048
D_HEADDIM = 64
D_HEADS = 32
D_GROUPS = 4
D_HPG = 8
D_STATE = 128
D_CONV = 4
D_XBC = D_INNER + 2 * D_GROUPS * D_STATE
D_NPAD = 5376
D_DT_COL = (D_INNER + D_XBC) // 128

LANES = 128
FFN_ROWS = 256
PROJ_ROWS = 256
SEQ_TILE = 128


def _params(*sem):
    return pltpu.CompilerParams(dimension_semantics=sem)


def _split2(a):
    hi = a.astype(BF16)
    lo = (a - hi.astype(F32)).astype(BF16)
    return hi, lo


def _split3(a):
    hi = a.astype(BF16)
    r = a - hi.astype(F32)
    mid = r.astype(BF16)
    lo = (r - mid.astype(F32)).astype(BF16)
    return hi, mid, lo


def _dot(a, b):
    return jnp.dot(a, b, preferred_element_type=F32)


def _dot_nt(a, b):
    return lax.dot_general(a, b, (((1,), (1,)), ((), ())), preferred_element_type=F32)


def _sel_right(a, sel01, terms=2):
    parts = _split2(a) if terms == 2 else _split3(a)
    out = None
    for p in reversed(parts):
        d = _dot(p, sel01)
        out = d if out is None else out + d
    return out


def _sel_left(sel01, a, terms=2):
    parts = _split2(a) if terms == 2 else _split3(a)
    out = None
    for p in reversed(parts):
        d = _dot(sel01, p)
        out = d if out is None else out + d
    return out


def _silu(x):
    return x * jax.nn.sigmoid(x)


def _softplus(x):
    return jnp.maximum(x, 0.0) + jnp.log(1.0 + jnp.exp(-jnp.abs(x)))


def _rms_rows(x, g):
    ms = jnp.mean(x * x, axis=-1, keepdims=True)
    return x * lax.rsqrt(ms + EPS) * g


def _ffn_kernel(mix_fn, n_mix, x_ref, *refs):
    mix_refs = refs[:n_mix]
    g_ref, wg_ref, wu_ref, wd_ref, o_ref = refs[n_mix:]
    x = x_ref[...]
    if mix_fn is not None:
        x = x + mix_fn(*mix_refs)
    h = _rms_rows(x, g_ref[...]).astype(BF16)
    a = _dot(h, wg_ref[...])
    u = _dot(h, wu_ref[...])
    act = (_silu(a) * u).astype(BF16)
    o_ref[...] = x + 0.5 * _dot(act, wd_ref[...])


def _ffn(x, gain, w_up, w_down, layer, which, tm, mix=None):
    T = x.shape[0]
    tm = min(tm, FFN_ROWS)
    resident = dict(pipeline_mode=pl.Buffered(1))
    mix_fn, row_inputs, const_inputs = mix if mix is not None else (None, [], [])
    mix_specs, mix_args = [], []
    for arr, width, cb in row_inputs:
        mix_specs.append(pl.BlockSpec((tm, width), lambda i, cb=cb: (i, cb)))
        mix_args.append(arr)
    for arr in const_inputs:
        mix_specs.append(pl.BlockSpec(arr.shape, lambda i, nd=arr.ndim: (0,) * nd, **resident))
        mix_args.append(arr)
    return pl.pallas_call(
        functools.partial(_ffn_kernel, mix_fn, len(mix_args)),
        grid=(T // tm,),
        in_specs=[pl.BlockSpec((tm, D_MODEL), lambda i: (i, 0))] + mix_specs + [
            pl.BlockSpec((1, D_MODEL), lambda i: (0, 0), **resident),
            pl.BlockSpec((None, None, D_MODEL, D_FF), lambda i: (layer, which, 0, 0), **resident),
            pl.BlockSpec((None, None, D_MODEL, D_FF), lambda i: (layer, which, 0, 1), **resident),
            pl.BlockSpec((None, None, D_FF, D_MODEL), lambda i: (layer, which, 0, 0), **resident),
        ],
        out_specs=pl.BlockSpec((tm, D_MODEL), lambda i: (i, 0)),
        out_shape=jax.ShapeDtypeStruct((T, D_MODEL), F32),
        compiler_params=_params("parallel"),
        name="ffn",
    )(x, *mix_args, gain.reshape(1, D_MODEL), w_up, w_up, w_down)


def _proj_kernel(x_ref, g_ref, w_ref, o_ref):
    o_ref[...] = _dot(_rms_rows(x_ref[...], g_ref[...]).astype(BF16), w_ref[...])


def _proj(x, gain, w, tm):
    T = x.shape[0]
    N = w.shape[1]
    tm = min(tm, PROJ_ROWS)
    resident = dict(pipeline_mode=pl.Buffered(1))
    return pl.pallas_call(
        _proj_kernel,
        grid=(T // tm,),
        in_specs=[
            pl.BlockSpec((tm, D_MODEL), lambda i: (i, 0)),
            pl.BlockSpec((1, D_MODEL), lambda i: (0, 0), **resident),
            pl.BlockSpec((D_MODEL, N), lambda i: (0, 0), **resident),
        ],
        out_specs=pl.BlockSpec((tm, N), lambda i: (i, 0)),
        out_shape=jax.ShapeDtypeStruct((T, N), F32),
        compiler_params=_params("parallel"),
        name="proj",
    )(x, gain.reshape(1, D_MODEL), w)


def _proj_qk_kernel(flags, t_chunks, x_ref, g_ref, w_ref, bd_ref, gain_ref, mask_ref, cos_ref, sa_ref, sb_ref,
                    o_ref, *t_refs):
    h = _rms_rows(x_ref[...], g_ref[...]).astype(BF16)
    tn = bd_ref.shape[0]
    cos, sa, sb = cos_ref[...], sa_ref[...], sb_ref[...]
    for j, flag in enumerate(flags):
        y = _dot(h, w_ref[:, j * tn:(j + 1) * tn])
        if not flag:
            o_ref[:, j * tn:(j + 1) * tn] = y
            continue
        ss = _dot((y * y).astype(BF16), bd_ref[...])
        yn = y * lax.rsqrt(ss * (1.0 / HEAD_DIM) + EPS) * gain_ref[j]
        mask = mask_ref[j]
        for c in range(tn // LANES):
            sl = slice(c * LANES, (c + 1) * LANES)
            v = yn[:, sl]
            up = pltpu.roll(v, LANES - ROPE_DIM // 2, 1)
            dn = pltpu.roll(v, ROPE_DIM // 2, 1)
            rot = v * cos + up * sa + dn * sb
            if flag == 2:
                rot = jnp.where(mask[:, sl] > 0.0, rot, y[:, sl])
            o_ref[:, j * tn + c * LANES:j * tn + (c + 1) * LANES] = rot
    for t_ref, chunks in zip(t_refs, t_chunks):
        for k, j in enumerate(chunks):
            t_ref[0, k * tn:(k + 1) * tn, :] = o_ref[:, j * tn:(j + 1) * tn].T


def _proj_qk(x, gain, w, flags, gain_rows, mask_rows, tables, tm, tn, t_chunks=(), seq=None):
    T = x.shape[0]
    N = w.shape[1]
    nj = N // tn
    tm = min(tm, PROJ_ROWS)
    cos, sa, sb = tables
    n_pos_tiles = cos.shape[0] // tm
    bd = jnp.asarray(np.kron(np.eye(tn // HEAD_DIM), np.ones((HEAD_DIM, HEAD_DIM))), BF16)
    resident = dict(pipeline_mode=pl.Buffered(1))
    tab_spec = pl.BlockSpec((tm, LANES), lambda i: (i % n_pos_tiles, 0))
    out_specs = pl.BlockSpec((tm, N), lambda i: (i, 0))
    out_shape = jax.ShapeDtypeStruct((T, N), F32)
    per_seq = seq // tm if t_chunks else 1
    windows = []
    if t_chunks:
        out_specs, out_shape = [out_specs], [out_shape]
        for chunks, window in t_chunks:
            n_tiles = -(-window // tm)
            first = per_seq - n_tiles
            windows.append(tuple(chunks))
            out_specs.append(pl.BlockSpec((1, len(chunks) * tn, tm),
                                          lambda i, first=first: (i // per_seq, 0, jnp.maximum(i % per_seq - first, 0))))
            out_shape.append(jax.ShapeDtypeStruct((T // seq, len(chunks) * tn, n_tiles * tm), F32))
    return pl.pallas_call(
        functools.partial(_proj_qk_kernel, flags, tuple(windows)),
        grid=(T // tm,),
        in_specs=[
            pl.BlockSpec((tm, D_MODEL), lambda i: (i, 0)),
            pl.BlockSpec((1, D_MODEL), lambda i: (0, 0), **resident),
            pl.BlockSpec((D_MODEL, N), lambda i: (0, 0), **resident),
            pl.BlockSpec((tn, tn), lambda i: (0, 0), **resident),
            pl.BlockSpec((nj, 1, tn), lambda i: (0, 0, 0), **resident),
            pl.BlockSpec((nj, 1, tn), lambda i: (0, 0, 0), **resident),
            tab_spec, tab_spec, tab_spec,
        ],
        out_specs=out_specs,
        out_shape=out_shape,
        compiler_params=_params("arbitrary"),
        name="proj_qk",
    )(x, gain.reshape(1, D_MODEL), w, bd, gain_rows, mask_rows, cos, sa, sb)


def _rope_tables(pos):
    half = ROPE_DIM // 2
    inv = ROPE_THETA ** (-jnp.arange(half, dtype=F32) / half)
    ang = pos.astype(F32)[:, None] * inv[None, :]
    cos, sin = jnp.cos(ang), jnp.sin(ang)
    n = pos.shape[0]
    one = jnp.ones((n, HEAD_DIM - ROPE_DIM), F32)
    zero = jnp.zeros((n, HEAD_DIM - ROPE_DIM), F32)
    zh = jnp.zeros((n, half), F32)
    c64 = jnp.concatenate([cos, cos, one], axis=1)
    sa64 = jnp.concatenate([-sin, zh, zero], axis=1)
    sb64 = jnp.concatenate([zh, sin, zero], axis=1)
    rep = LANES // HEAD_DIM
    return tuple(jnp.tile(t, (1, rep)) for t in (c64, sa64, sb64))


def _mix_plain(o_ref, w_ref):
    return _dot(o_ref[...].astype(BF16), w_ref[...])


def _mix_a(o0, o1, o2, l0, l1, l2, w_ref):
    la, lb, lc = l0[...], l1[...], l2[...]
    m = jnp.maximum(jnp.maximum(la, lb), lc)
    ea, eb, ec = jnp.exp(la - m), jnp.exp(lb - m), jnp.exp(lc - m)
    o = (ea * o0[...] + eb * o1[...] + ec * o2[...]) / (ea + eb + ec)
    return _dot(o.astype(BF16), w_ref[...])


def _mix_c(o_ref, r_ref, g_ref, w_ref):
    o = o_ref[...]
    g = g_ref[...]
    parts = [_rms_rows(o[:, h * C_DV:(h + 1) * C_DV], g) for h in range(C_HEADS)]
    on = jnp.concatenate(parts, axis=1) * _silu(r_ref[...])
    return _dot(on.astype(BF16), w_ref[...])


def _mix_d(o_ref, z_ref, g_ref, w_ref):
    gated = o_ref[...] * _silu(z_ref[...])
    gw = D_INNER // D_GROUPS
    g = g_ref[...]
    parts = [_rms_rows(gated[:, k * gw:(k + 1) * gw], g[:, k * gw:(k + 1) * gw]) for k in range(D_GROUPS)]
    return _dot(jnp.concatenate(parts, axis=1).astype(BF16), w_ref[...])


def _a_prompt_kernel(d, q_ref, kp_ref, kc_ref, vp_ref, vc_ref, o_ref, l_ref):
    j = pl.program_id(1)
    heads = q_ref.shape[2] // HEAD_DIM
    per_pass = max(1, A_HEADS // heads)
    n_prob = per_pass * heads
    scale = HEAD_DIM ** -0.5
    qq = lax.broadcasted_iota(jnp.int32, (n_prob * A_SPAN, 2 * A_SPAN), 0) & (A_SPAN - 1)
    kk = lax.broadcasted_iota(jnp.int32, (n_prob * A_SPAN, 2 * A_SPAN), 1)
    rel = qq + A_SPAN - kk
    valid = (rel >= 0) & (rel <= A_SPAN) & ((kk >= A_SPAN) | (j > 0))
    hs = [slice(h * HEAD_DIM, (h + 1) * HEAD_DIM) for h in range(heads)]
    for r0 in range(0, d, per_pass):
        rows = [pl.ds(r0 + e, A_SPAN, stride=d) if d > 1 else pl.ds(0, A_SPAN) for e in range(per_pass)]
        s, vs = [], []
        for rw in rows:
            q = (q_ref[0, rw, :] * scale).astype(BF16)
            k = jnp.concatenate([kp_ref[0, rw, :], kc_ref[0, rw, :]], axis=0).astype(BF16)
            vs.append(jnp.concatenate([vp_ref[0, rw, :], vc_ref[0, rw, :]], axis=0).astype(BF16))
            s += [_dot_nt(q[:, sl], k[:, sl]) for sl in hs]
        s = jnp.where(valid, jnp.concatenate(s, axis=0), -jnp.inf)
        m = jnp.max(s, axis=-1, keepdims=True)
        p = jnp.exp(s - m)
        den = jnp.sum(p, axis=-1, keepdims=True)
        pb = p.astype(BF16)
        lse = m + jnp.log(den)
        inv = 1.0 / den
        for e, rw in enumerate(rows):
            o, l = [], []
            for h, sl in enumerate(hs):
                blk = slice((e * heads + h) * A_SPAN, (e * heads + h + 1) * A_SPAN)
                o.append(_dot(pb[blk], vs[e][:, sl]) * inv[blk])
                l.append(jnp.broadcast_to(lse[blk], (A_SPAN, HEAD_DIM)))
            o_ref[0, rw, :] = jnp.concatenate(o, axis=1)
            l_ref[0, rw, :] = jnp.concatenate(l, axis=1)


def _a_prompt(y, g, Bn, S):
    d = A_DILATIONS[g]
    slab = A_SPAN * d
    heads = A_HEADS if d == 1 else LANES // HEAD_DIM
    cols = heads * HEAD_DIM
    ncb = A_WIDTH // cols
    y3 = y.reshape(Bn, S, A_GROUPS * 3 * A_WIDTH)

    def spec(which, prev):
        c0 = (3 * g + which) * ncb
        if prev:
            return pl.BlockSpec((1, slab, cols), lambda b, j, c: (b, jnp.maximum(j - 1, 0), c0 + c))
        return pl.BlockSpec((1, slab, cols), lambda b, j, c: (b, j, c0 + c))

    out_spec = pl.BlockSpec((1, slab, cols), lambda b, j, c: (b, j, c))
    o, l = pl.pallas_call(
        functools.partial(_a_prompt_kernel, d),
        grid=(Bn, S // slab, ncb),
        in_specs=[spec(0, False), spec(1, True), spec(1, False), spec(2, True), spec(2, False)],
        out_specs=[out_spec, out_spec],
        out_shape=[jax.ShapeDtypeStruct((Bn, S, A_WIDTH), F32)] * 2,
        compiler_params=_params("parallel", "arbitrary", "arbitrary"),
        name="a_prompt_d%d" % d,
    )(y3, y3, y3, y3, y3)
    return o.reshape(Bn * S, A_WIDTH), l.reshape(Bn * S, A_WIDTH)


def _nt_sel(mask, a):
    out = None
    for p in reversed(_split3(a)):
        d = _dot_nt(mask, p)
        out = d if out is None else out + d
    return out


def _a_sample_kernel(T, y_ref, c1_ref, c2_ref, c3_ref, hind_ref, hindt_ref, *out_refs):
    caches = (c1_ref, c2_ref, c3_ref)
    width = 3 * A_WIDTH
    scale = HEAD_DIM ** -0.5
    hind = hind_ref[...]
    hindt = hindt_ref[...]
    lane = lax.broadcasted_iota(jnp.int32, (8, LANES), 1)
    sub = lax.broadcasted_iota(jnp.int32, (8, LANES), 0)
    zrows = jnp.zeros((LANES - T, A_WIDTH), F32)
    zsq = jnp.zeros((LANES - 8, LANES), F32)

    def head_rows(prod):
        return _nt_sel(hind, jnp.concatenate([prod, zrows], axis=0))

    def sublane_total(x):
        x = x + pltpu.roll(x, 4, 0)
        x = x + pltpu.roll(x, 2, 0)
        return x + pltpu.roll(x, 1, 0)

    def head_scores(k_of, pat_of):
        out = jnp.zeros((8, LANES), F32)
        for h in range(A_HEADS):
            prod = k_of(h) * pat_of(h)
            part = prod[0:8]
            for j in range(1, HEAD_DIM // 8):
                part = part + prod[8 * j:8 * j + 8]
            out = jnp.where(sub == h, sublane_total(part), out)
        return out

    def rows_of(packed):
        return jnp.concatenate([packed, zsq], axis=0).T

    def widen(rows8):
        return _sel_right(rows8, hindt, terms=3)

    def place(col, at):
        return jnp.where(lane == at, col, 0.0)

    def patterns(q_t, pick):
        return _sel_right(q_t, pick.astype(BF16), terms=3)

    row_sq = lax.broadcasted_iota(jnp.int32, (LANES, LANES), 0)
    lane_sq = lax.broadcasted_iota(jnp.int32, (LANES, LANES), 1)

    prep = []
    for g in range(A_GROUPS):
        d = A_DILATIONS[g]
        q_nat = y_ref[0, :, g * width:g * width + A_WIDTH] * scale
        kn_nat = y_ref[0, :, g * width + A_WIDTH:g * width + 2 * A_WIDTH]
        vn_nat = y_ref[0, :, g * width + 2 * A_WIDTH:(g + 1) * width]
        q_t = jnp.concatenate([q_nat, zrows], axis=0).T
        if d == 1:
            pats = [patterns(q_t, row_sq == i) for i in range(T)]
            news = [head_rows(q_nat[i:i + 1, :] * kn_nat) for i in range(T)]
        else:
            pats = [patterns(q_t, (row_sq == (lane_sq & (d - 1))) & (row_sq < T))]
            news = [head_rows(q_nat * kn_nat)]
        prep.append((vn_nat, pats, news))

    swept = []
    for g in range(A_GROUPS):
        d = A_DILATIONS[g]
        c_ref = caches[g]
        n_tiles = c_ref.shape[-1] // LANES
        _, pats, news = prep[g]
        if d == 1:
            packed = jnp.zeros((8, LANES), F32)
            accs = []
            for i in range(T):
                pat = pats[i]
                s = head_scores(lambda h: c_ref[0, 0, h], lambda h: pat[h * HEAD_DIM:(h + 1) * HEAD_DIM])
                s = jnp.where(lane >= i, s, -jnp.inf)
                sn = jnp.where(lane <= i, news[i], -jnp.inf)
                m = jnp.maximum(jnp.max(s, axis=1, keepdims=True), jnp.max(sn, axis=1, keepdims=True))
                p = jnp.exp(s - m)
                pn = jnp.exp(sn - m)
                den = jnp.sum(p, axis=1, keepdims=True) + jnp.sum(pn, axis=1, keepdims=True)
                accs.append(jnp.concatenate([c_ref[0, 1, h] * p[h:h + 1, :] for h in range(A_HEADS)], axis=0))
                packed = packed + place(den, i) + place(m + jnp.log(den), 8 + i)
                packed = packed + pltpu.roll(pn, 16 + 8 * i, 1)
            swept.append((accs, packed))
        else:
            cls = lane & (d - 1)
            self_s = news[0]
            pat = pats[0]
            s_tiles = [head_scores(lambda h: c_ref[0, 0, h, :, t * LANES:(t + 1) * LANES],
                                   lambda h: pat[h * HEAD_DIM:(h + 1) * HEAD_DIM])
                       for t in range(n_tiles)]
            smax = s_tiles[0]
            for t in range(1, n_tiles):
                smax = jnp.maximum(smax, s_tiles[t])
            mrow = jnp.full((8, LANES), jnp.inf, F32)
            m_cls = []
            for i in range(T):
                mi = jnp.maximum(jnp.max(jnp.where(cls == i, smax, -jnp.inf), axis=1, keepdims=True),
                                 self_s[:, i:i + 1])
                m_cls.append(mi)
                mrow = jnp.where(cls == i, mi, mrow)
            p_tiles = [jnp.exp(s_tiles[t] - mrow) for t in range(n_tiles)]
            psum = p_tiles[0]
            for t in range(1, n_tiles):
                psum = psum + p_tiles[t]
            accs = []
            for h in range(A_HEADS):
                acc = jnp.zeros((HEAD_DIM, LANES), F32)
                for t in range(n_tiles):
                    acc = acc + c_ref[0, 1, h, :, t * LANES:(t + 1) * LANES] * p_tiles[t][h:h + 1, :]
                accs.append(acc)
            packed = jnp.zeros((8, LANES), F32)
            for i in range(T):
                ps = jnp.exp(self_s[:, i:i + 1] - m_cls[i])
                den = jnp.sum(jnp.where(cls == i, psum, 0.0), axis=1, keepdims=True) + ps
                packed = packed + place(ps, i) + place(den, 8 + i) + place(m_cls[i] + jnp.log(den), 16 + i)
            swept.append(([jnp.concatenate(accs, axis=0)], packed))

    for g in range(A_GROUPS):
        d = A_DILATIONS[g]
        vn_nat = prep[g][0]
        accs, packed = swept[g]
        sq = rows_of(packed)
        if d == 1:
            first = (sub == 0).astype(BF16)
            denx = widen(sq[0:8])
            lsex = widen(sq[8:16])
            for i in range(T):
                pnx = widen(sq[16 + 8 * i:24 + 8 * i])
                o = _nt_sel(first, accs[i])[0:1, :] + jnp.sum(pnx[0:T] * vn_nat, axis=0, keepdims=True)
                out_refs[g][0, i:i + 1, :] = o / denx[i:i + 1, :]
            out_refs[A_GROUPS + g][0] = lsex[0:T]
        else:
            cls = lane & (d - 1)
            res = _nt_sel((cls == sub).astype(BF16), accs[0])
            psx, denx, lsex = widen(sq[0:8]), widen(sq[8:16]), widen(sq[16:24])
            out_refs[g][0] = (res[0:T] + psx[0:T] * vn_nat) / denx[0:T]
            out_refs[A_GROUPS + g][0] = lsex[0:T]


def _a_sample(y, caches, Bd, T):
    width = A_GROUPS * 3 * A_WIDTH
    y3 = y.reshape(Bd, T, width)
    views = []
    specs = [pl.BlockSpec((1, T, width), lambda b: (b, 0, 0))]
    for g, c in enumerate(caches):
        d = A_DILATIONS[g]
        assert c.shape[1] == A_SPAN * d and (d == 1 or T <= d) and T <= 8
        views.append(c.transpose(0, 2, 3, 4, 1))
        specs.append(pl.BlockSpec((1, 2, A_HEADS, HEAD_DIM, A_SPAN * d), lambda b: (b, 0, 0, 0, 0)))
    hind_np = np.zeros((LANES, A_WIDTH), np.float32)
    hind_np[np.arange(A_WIDTH) // HEAD_DIM, np.arange(A_WIDTH)] = 1.0
    hind = jnp.asarray(hind_np[:A_HEADS], BF16)
    hindt = jnp.asarray(hind_np, BF16)
    specs += [pl.BlockSpec(hind.shape, lambda b: (0, 0)), pl.BlockSpec(hindt.shape, lambda b: (0, 0))]
    out_spec = pl.BlockSpec((1, T, A_WIDTH), lambda b: (b, 0, 0))
    outs = pl.pallas_call(
        functools.partial(_a_sample_kernel, T),
        grid=(Bd,),
        in_specs=specs,
        out_specs=[out_spec] * (2 * A_GROUPS),
        out_shape=[jax.ShapeDtypeStruct((Bd, T, A_WIDTH), F32)] * (2 * A_GROUPS),
        compiler_params=_params("parallel"),
        name="a_sample",
    )(y3, *views, hind, hindt)
    return [o.reshape(Bd * T, A_WIDTH) for o in outs]


def _mixer_a(xp, xs, gain, w_in, qk_gain, w_out, caches, tabs_p, tabs_s, Bn, S, Bd, T):
    flags = (1, 1, 0) * A_GROUPS
    ones = jnp.ones((A_WIDTH,), F32)
    rows = [jnp.tile(qk_gain[0], A_HEADS), jnp.tile(qk_gain[1], A_HEADS), ones] * A_GROUPS
    gain_rows = jnp.stack(rows).reshape(3 * A_GROUPS, 1, A_WIDTH)
    mask_rows = jnp.stack([ones, ones, 0.0 * ones] * A_GROUPS).reshape(3 * A_GROUPS, 1, A_WIDTH)
    w = w_in.astype(BF16)
    wins = [min(A_SPAN * A_DILATIONS[g], S) for g in range(A_GROUPS)]
    yp, *kvts = _proj_qk(xp, gain, w, flags, gain_rows, mask_rows, tabs_p, 1024, A_WIDTH,
                         t_chunks=[((3 * g + 1, 3 * g + 2), wins[g]) for g in range(A_GROUPS)], seq=S)
    ys = _proj_qk(xs, gain, w, flags, gain_rows, mask_rows, tabs_s, xs.shape[0], A_WIDTH)
    wo = w_out.astype(BF16)
    pr = [_a_prompt(yp, g, Bn, S) for g in range(A_GROUPS)]
    mix_p = (_mix_a, [(o, A_WIDTH, 0) for o, _ in pr] + [(l, A_WIDTH, 0) for _, l in pr], [wo])
    sr = _a_sample(ys, caches, Bd, T)
    mix_s = (_mix_a, [(o, A_WIDTH, 0) for o in sr], [wo])
    ys3 = ys.reshape(Bd, T, A_GROUPS * 3 * A_WIDTH)
    new_p, new_s = [], []
    for g in range(A_GROUPS):
        c0 = (3 * g + 1) * A_WIDTH
        kvt = kvts[g][:, :, kvts[g].shape[2] - wins[g]:]
        new_p.append(kvt.reshape(Bn, 2, A_HEADS, HEAD_DIM, wins[g]).transpose(0, 4, 1, 2, 3))
        new_s.append(ys3[:, :, c0:c0 + 2 * A_WIDTH].reshape(Bd, T, 2, A_HEADS, HEAD_DIM))
    return mix_p, mix_s, new_p, new_s


def _b_kmean_kernel(k_ref, o_ref):
    nblk = k_ref.shape[1] // B_BLOCK
    rows = [jnp.mean(k_ref[0, n * B_BLOCK:(n + 1) * B_BLOCK, :], axis=0, keepdims=True) for n in range(nblk)]
    o_ref[0] = jnp.concatenate(rows, axis=0)


def _b_gate_kernel(q_ref, kmh_ref, kml_ref, o_ref):
    i = pl.program_id(1)
    qh, ql = _split2(q_ref[0])
    kmh, kml = kmh_ref[0], kml_ref[0]
    gate = _dot_nt(kmh, qh) + (_dot_nt(kmh, ql) + _dot_nt(kml, qh))
    tq = gate.shape[1]
    nblk = gate.shape[0] // B_HEADS
    gate = gate.reshape(B_HEADS, nblk, tq)
    blk = lax.broadcasted_iota(jnp.int32, (B_HEADS, nblk, tq), 1)
    blkf = blk.astype(F32)
    gate = jnp.where(blk < i, gate, -jnp.inf)
    sel = jnp.zeros((B_HEADS, nblk, tq), jnp.bool_)
    for _ in range(B_TOPK):
        mx = jnp.max(gate, axis=1, keepdims=True)
        first = jnp.min(jnp.where(gate == mx, blkf, float(nblk)), axis=1, keepdims=True)
        hit = blkf == first
        sel = sel | (hit & (mx > -jnp.inf))
        gate = jnp.where(hit, -jnp.inf, gate)
    bias = jnp.where(sel | (blk >= i), 0.0, NEG).reshape(B_HEADS * nblk, tq)
    o_ref[0] = bias.T


def _b_attn_kernel(qi_ref, kn_ref, q_ref, k_ref, vt_ref, o_ref, *state):
    t = pl.program_id(1)
    i, n = qi_ref[t], kn_ref[t]
    tq = B_BLOCK
    wide = B_REP * tq
    m_refs, acc_refs = state[:B_KV_HEADS], state[B_KV_HEADS:]

    @pl.when(n == 0)
    def _():
        for kvh in range(B_KV_HEADS):
            m_refs[kvh][...] = jnp.full_like(m_refs[kvh], -jnp.inf)
            acc_refs[kvh][...] = jnp.zeros_like(acc_refs[kvh])

    def sweep(causal):
        if causal:
            kk = lax.broadcasted_iota(jnp.int32, (B_BLOCK, wide), 0)
            qq = lax.broadcasted_iota(jnp.int32, (B_BLOCK, wide), 1) & (tq - 1)
            keep = kk <= qq
        scores = []
        for kvh in range(B_KV_HEADS):
            k = k_ref[0, :, kvh * LANES:(kvh + 1) * LANES]
            s = _dot_nt(k, q_ref[0, 0, kvh])
            scores.append(jnp.where(keep, s, NEG) if causal else s)
        probs, scales = [], []
        for kvh in range(B_KV_HEADS):
            m_old = m_refs[kvh][...]
            m_new = jnp.maximum(m_old, jnp.max(scores[kvh], axis=0, keepdims=True))
            probs.append(jnp.exp2(scores[kvh] - m_new).astype(BF16))
            scales.append(jnp.exp2(m_old - m_new))
            m_refs[kvh][...] = m_new
        for kvh in range(B_KV_HEADS):
            acc_refs[kvh][...] = scales[kvh] * acc_refs[kvh][...] + _dot(vt_ref[0, kvh], probs[kvh])

    @pl.when(n < i)
    def _():
        sweep(False)

    @pl.when(n == i)
    def _():
        sweep(True)
        for kvh in range(B_KV_HEADS):
            acc = acc_refs[kvh][...]
            ot = acc[0:HEAD_DIM] / acc[HEAD_DIM:HEAD_DIM + 1]
            for r in range(0, B_REP, 2):
                pair = [ot[:, (r + e) * tq:(r + e + 1) * tq].T for e in range(2)]
                c0 = (kvh * B_REP + r) * HEAD_DIM
                o_ref[0, :, c0:c0 + 2 * HEAD_DIM] = jnp.concatenate(pair, axis=1)


def _b_prompt(y, Bn, S):
    nblk = S // B_BLOCK
    assert nblk % 8 == 0 and LANES % nblk == 0
    nq = B_HEADS * HEAD_DIM
    nk = B_KV_HEADS * HEAD_DIM
    y3 = y.reshape(Bn, S, nq + 2 * nk)
    kmean = pl.pallas_call(
        _b_kmean_kernel,
        grid=(Bn,),
        in_specs=[pl.BlockSpec((1, S, nk), lambda b: (b, 0, nq // nk))],
        out_specs=pl.BlockSpec((1, nblk, nk), lambda b: (b, 0, 0)),
        out_shape=jax.ShapeDtypeStruct((Bn, nblk, nk), F32),
        compiler_params=_params("parallel"),
        name="b_kmean",
    )(y3)
    km = kmean.reshape(Bn, nblk, B_KV_HEADS, HEAD_DIM)
    km = jnp.repeat(km, B_REP, axis=2)
    eye = jnp.eye(B_HEADS, dtype=F32)
    kmbd = jnp.einsum('bnhe,hg->bgnhe', km, eye).reshape(Bn, B_HEADS * nblk, nq)
    kmh = kmbd.astype(BF16)
    kml = (kmbd - kmh.astype(F32)).astype(BF16)
    bias = pl.pallas_call(
        _b_gate_kernel,
        grid=(Bn, nblk),
        in_specs=[
            pl.BlockSpec((1, B_BLOCK, nq), lambda b, i: (b, i, 0)),
            pl.BlockSpec((1, B_HEADS * nblk, nq), lambda b, i: (b, 0, 0)),
            pl.BlockSpec((1, B_HEADS * nblk, nq), lambda b, i: (b, 0, 0)),
        ],
        out_specs=pl.BlockSpec((1, B_BLOCK, B_HEADS * nblk), lambda b, i: (b, i, 0)),
        out_shape=jax.ShapeDtypeStruct((Bn, S, B_HEADS * nblk), F32),
        compiler_params=_params("parallel", "arbitrary"),
        name="b_gate",
    )(y3, kmh, kml)
    q = (y3[:, :, :nq] * (HEAD_DIM ** -0.5 * LOG2E)).astype(BF16).reshape(Bn, S, B_HEADS, HEAD_DIM)
    pad = LANES - HEAD_DIM - nblk
    q_aug = jnp.concatenate([q, bias.astype(BF16).reshape(Bn, S, B_HEADS, nblk),
                             jnp.zeros((Bn, S, B_HEADS, pad), BF16)], axis=-1)
    q_aug = q_aug.reshape(Bn, nblk, B_BLOCK, B_KV_HEADS, B_REP, LANES).transpose(0, 1, 3, 4, 2, 5)
    q_aug = q_aug.reshape(Bn, nblk, B_KV_HEADS, B_REP * B_BLOCK, LANES)
    k = y3[:, :, nq:nq + nk].astype(BF16).reshape(Bn, S, B_KV_HEADS, HEAD_DIM)
    onehot = jax.nn.one_hot(jnp.arange(S) // B_BLOCK, nblk, dtype=BF16)
    onehot = jnp.broadcast_to(onehot[None, :, None, :], (Bn, S, B_KV_HEADS, nblk))
    k_aug = jnp.concatenate([k, onehot, jnp.zeros((Bn, S, B_KV_HEADS, pad), BF16)], axis=-1)
    k_aug = k_aug.reshape(Bn, S, B_KV_HEADS * LANES)
    vt = y3[:, :, nq + nk:].astype(BF16).reshape(Bn, S, B_KV_HEADS, HEAD_DIM).transpose(0, 2, 3, 1)
    vt = jnp.concatenate([vt, jnp.ones((Bn, B_KV_HEADS, 1, S), BF16),
                          jnp.zeros((Bn, B_KV_HEADS, B_VT_ROWS - HEAD_DIM - 1, S), BF16)], axis=2)
    pairs = [(i, n) for i in range(nblk) for n in range(i + 1)]
    qi = jnp.asarray([p[0] for p in pairs], jnp.int32)
    kn = jnp.asarray([p[1] for p in pairs], jnp.int32)
    o = pl.pallas_call(
        _b_attn_kernel,
        grid_spec=pltpu.PrefetchScalarGridSpec(
            num_scalar_prefetch=2,
            grid=(Bn, len(pairs)),
            in_specs=[
                pl.BlockSpec((1, 1, B_KV_HEADS, B_REP * B_BLOCK, LANES),
                             lambda b, t, qi, kn: (b, qi[t], 0, 0, 0)),
                pl.BlockSpec((1, B_BLOCK, B_KV_HEADS * LANES), lambda b, t, qi, kn: (b, kn[t], 0)),
                pl.BlockSpec((1, B_KV_HEADS, B_VT_ROWS, B_BLOCK), lambda b, t, qi, kn: (b, 0, 0, kn[t])),
            ],
            out_specs=pl.BlockSpec((1, B_BLOCK, nq), lambda b, t, qi, kn: (b, qi[t], 0)),
            scratch_shapes=([pltpu.VMEM((1, B_REP * B_BLOCK), F32)] * B_KV_HEADS
                            + [pltpu.VMEM((B_VT_ROWS, B_REP * B_BLOCK), F32)] * B_KV_HEADS),
        ),
        out_shape=jax.ShapeDtypeStruct((Bn, S, nq), F32),
        compiler_params=_params("parallel", "arbitrary"),
        name="b_attn",
    )(qi, kn, q_aug, k_aug, vt)
    return o.reshape(Bn * S, nq)


def _b_sample_kernel(n_pages, T, pt_ref, q_ref, kn_ref, vn_ref, *refs):
    del pt_ref
    page_refs = refs[:n_pages]
    o_ref = refs[n_pages]
    nk = B_KV_HEADS * HEAD_DIM
    q = q_ref[0]
    qh, ql = _split2(q)
    rows = q.shape[0]
    page_rows = page_refs[0].shape[2]
    pages_per_block = B_BLOCK // page_rows
    n_blocks = n_pages // pages_per_block
    scale = HEAD_DIM ** -0.5
    lane = lax.broadcasted_iota(jnp.int32, (1, LANES), 1)
    lanef = lax.broadcasted_iota(jnp.int32, (rows, LANES), 1).astype(F32)
    scores = []
    block_sum = [None] * n_blocks
    for p in range(n_pages):
        kt = page_refs[p][0, :nk, :]
        scores.append(_dot(qh, kt.astype(BF16)) * scale)
        b = p // pages_per_block
        block_sum[b] = kt if block_sum[b] is None else block_sum[b] + kt
    km = jnp.zeros((nk, LANES), F32)
    for b in range(n_blocks):
        ksum = jnp.sum(block_sum[b], axis=1, keepdims=True) * (1.0 / B_BLOCK)
        km = km + ksum * (lane == b).astype(F32)
    kmh, kml = _split2(km)
    gate = _dot(qh, kmh) + (_dot(qh, kml) + _dot(ql, kmh))
    gate = jnp.where(lanef < float(n_blocks), gate, -jnp.inf)
    sel = jnp.zeros((rows, LANES), jnp.bool_)
    for _ in range(min(B_TOPK, n_blocks)):
        mx = jnp.max(gate, axis=1, keepdims=True)
        first = jnp.min(jnp.where(gate == mx, lanef, float(LANES)), axis=1, keepdims=True)
        hit = lanef == first
        sel = sel | (hit & (mx > -jnp.inf))
        gate = jnp.where(hit, -jnp.inf, gate)
    self = sel.astype(F32)
    qidx = lax.broadcasted_iota(jnp.int32, (rows, 1), 0) % T
    kn, vn = kn_ref[0], vn_ref[0]
    own = []
    for j in range(T):
        sj = jnp.sum(q * kn[j:j + 1, :], axis=1, keepdims=True) * scale
        own.append(jnp.where(qidx >= j, sj, NEG))
    m = own[0]
    for j in range(1, T):
        m = jnp.maximum(m, own[j])
    for p in range(n_pages):
        b = p // pages_per_block
        scores[p] = jnp.where(self[:, b:b + 1] > 0.0, scores[p], NEG)
        m = jnp.maximum(m, jnp.max(scores[p], axis=-1, keepdims=True))
    den = jnp.zeros((rows, 1), F32)
    acc = jnp.zeros((rows, nk), F32)
    for j in range(T):
        pj = jnp.exp(own[j] - m)
        den = den + pj
        acc = acc + pj * vn[j:j + 1, :]
    for p in range(n_pages):
        pp = jnp.exp(scores[p] - m)
        den = den + jnp.sum(pp, axis=-1, keepdims=True)
        acc = acc + _dot_nt(pp.astype(BF16), page_refs[p][0, nk:, :].astype(BF16))
    acc = acc / den
    kvh = lax.broadcasted_iota(jnp.int32, (rows, 1), 0) // (T * B_REP)
    out = jnp.zeros((rows, HEAD_DIM), F32)
    for h in range(B_KV_HEADS):
        out = out + jnp.where(kvh == h, acc[:, h * HEAD_DIM:(h + 1) * HEAD_DIM], 0.0)
    o_ref[0] = out


def _b_sample(y, pool, page_table, Bd, T):
    nq = B_HEADS * HEAD_DIM
    nk = B_KV_HEADS * HEAD_DIM
    n_pages = page_table.shape[1]
    page_rows = pool.shape[1]
    assert B_BLOCK % page_rows == 0 and (n_pages * page_rows) % B_BLOCK == 0 and T <= B_BLOCK
    assert n_pages * page_rows // B_BLOCK <= LANES
    y3 = y.reshape(Bd, T, nq + 2 * nk)
    q = y3[:, :, :nq].reshape(Bd, T, B_HEADS, HEAD_DIM).transpose(0, 2, 1, 3)
    kvsel = jnp.asarray(np.kron(np.eye(B_KV_HEADS), np.ones((B_REP, 1))), F32)
    qbd = (q[:, :, :, None, :] * kvsel[None, :, None, :, None]).reshape(Bd, B_HEADS * T, nk)
    kn = y3[:, :, nq:nq + nk]
    vn = y3[:, :, nq + nk:]
    pool_t = pool.transpose(0, 2, 3, 4, 1).reshape(pool.shape[0], 2 * nk, page_rows)
    page_specs = [pl.BlockSpec((1, 2 * nk, page_rows), lambda b, pt, p=p: (pt[b, p], 0, 0))
                  for p in range(n_pages)]
    o = pl.pallas_call(
        functools.partial(_b_sample_kernel, n_pages, T),
        grid_spec=pltpu.PrefetchScalarGridSpec(
            num_scalar_prefetch=1,
            grid=(Bd,),
            in_specs=[
                pl.BlockSpec((1, B_HEADS * T, nk), lambda b, pt: (b, 0, 0)),
                pl.BlockSpec((1, T, nk), lambda b, pt: (b, 0, 0)),
                pl.BlockSpec((1, T, nk), lambda b, pt: (b, 0, 0)),
            ] + page_specs,
            out_specs=pl.BlockSpec((1, B_HEADS * T, HEAD_DIM), lambda b, pt: (b, 0, 0)),
        ),
        out_shape=jax.ShapeDtypeStruct((Bd, B_HEADS * T, HEAD_DIM), F32),
        compiler_params=_params("parallel"),
        name="b_sample",
    )(page_table, qbd, kn, vn, *([pool_t] * n_pages))
    return o.reshape(Bd, B_HEADS, T, HEAD_DIM).transpose(0, 2, 1, 3).reshape(Bd * T, nq)


def _mixer_b(xp, xs, gain, w_in, qk_gain, w_out, pool, page_table, tabs_p, tabs_s, Bn, S, Bd, T):
    tn = 512
    nq = B_HEADS * HEAD_DIM
    nk = B_KV_HEADS * HEAD_DIM
    flags = (1, 1, 2)
    ones = jnp.ones((nk,), F32)
    qg = jnp.tile(qk_gain[0], tn // HEAD_DIM)
    kg = jnp.concatenate([jnp.tile(qk_gain[1], B_KV_HEADS), ones])
    gain_rows = jnp.stack([qg, qg, kg]).reshape(3, 1, tn)
    mask_rows = jnp.stack([jnp.ones((tn,), F32), jnp.ones((tn,), F32),
                           jnp.concatenate([ones, 0.0 * ones])]).reshape(3, 1, tn)
    w = w_in.astype(BF16)
    yp, kvt = _proj_qk(xp, gain, w, flags, gain_rows, mask_rows, tabs_p, 1024, tn, t_chunks=[((2,), S)], seq=S)
    ys = _proj_qk(xs, gain, w, flags, gain_rows, mask_rows, tabs_s, xs.shape[0], tn)
    wo = w_out.astype(BF16)
    op = _b_prompt(yp, Bn, S)
    mix_p = (_mix_plain, [(op, nq, 0)], [wo])
    osm = _b_sample(ys, pool, page_table, Bd, T)
    mix_s = (_mix_plain, [(osm, nq, 0)], [wo])
    kv_p = kvt.reshape(Bn, 2, B_KV_HEADS, HEAD_DIM, S).transpose(0, 4, 1, 2, 3)
    kv_s = ys[:, nq:].reshape(Bd, T, 2, B_KV_HEADS, HEAD_DIM)
    return mix_p, mix_s, kv_p, kv_s


def _gla_kernel(nv, q_ref, k_ref, v_ref, glr_ref, wg_ref, bg_ref, tri_ref, blk_ref, s0_ref,
                o_ref, sT_ref, s_ref, qp_ref, kp_ref, vp_ref, gp_ref):
    t = pl.program_id(1)
    Tt = SEQ_TILE

    @pl.when(t == 0)
    def _():
        s_ref[...] = s0_ref[0]

    if nv < Tt:
        qp_ref[...] = jnp.zeros_like(qp_ref)
        kp_ref[...] = jnp.zeros_like(kp_ref)
        vp_ref[...] = jnp.zeros_like(vp_ref)
        gp_ref[...] = jnp.zeros_like(gp_ref)
    qp_ref[0:nv, :] = q_ref[0]
    kp_ref[0:nv, :] = k_ref[0]
    vp_ref[0:nv, :] = v_ref[0]
    gp_ref[0:nv, :] = glr_ref[0]
    q, k, v = qp_ref[...], kp_ref[...], vp_ref[...]

    x = _dot(gp_ref[...].astype(BF16), wg_ref[...]) + bg_ref[...]
    la = (jnp.minimum(x, 0.0) - jnp.log(1.0 + jnp.exp(-jnp.abs(x)))) * (1.0 / C_TAU)
    row = lax.broadcasted_iota(jnp.int32, (Tt, C_KEY), 0)
    la = jnp.where(row < nv, la, 0.0)
    b = _sel_left(tri_ref[...], la, terms=3)
    bend = _sel_left(blk_ref[...], la, terms=3)
    qe = q * (C_DK ** -0.5) * jnp.exp(b)
    ke = k * jnp.exp(-b)
    kd = k * jnp.exp(bend - b)
    kdt = kd.T
    bendt = bend.T
    qeb, keb, vb = qe.astype(BF16), ke.astype(BF16), v.astype(BF16)
    causal = tri_ref[...] > 0
    lane_t = lax.broadcasted_iota(jnp.int32, (C_DK, Tt), 1)
    n_chunks = -(-nv // C_CHUNK)
    ksl = [slice(h * C_DK, (h + 1) * C_DK) for h in range(C_HEADS)]
    vsl = [slice(h * C_DV, (h + 1) * C_DV) for h in range(C_HEADS)]
    att = [jnp.where(causal, _dot_nt(qeb[:, ks], keb[:, ks]), 0.0).astype(BF16) for ks in ksl]
    chunk_kv, chunk_dec = [], []
    for c in range(n_chunks):
        in_chunk = (lane_t >= c * C_CHUNK) & (lane_t < (c + 1) * C_CHUNK)
        last = lane_t == (c + 1) * C_CHUNK - 1
        chunk_kv.append([_dot(jnp.where(in_chunk, kdt[ks, :], 0.0).astype(BF16), vb[:, vs])
                         for ks, vs in zip(ksl, vsl)])
        chunk_dec.append([jnp.exp(jnp.sum(jnp.where(last, bendt[ks, :], 0.0), axis=1, keepdims=True))
                          for ks in ksl])
    o_intra = [_dot(att[h], vb[:, vsl[h]]) for h in range(C_HEADS)]
    for h in range(C_HEADS):
        st = s_ref[h]
        parts = []
        for c in range(n_chunks):
            rows = slice(c * C_CHUNK, (c + 1) * C_CHUNK)
            parts.append(o_intra[h][rows] + _dot(qeb[rows, ksl[h]], st.astype(BF16)))
            st = chunk_dec[c][h] * st + chunk_kv[c][h]
        s_ref[h] = st
        oh = parts[0] if n_chunks == 1 else jnp.concatenate(parts, axis=0)
        o_ref[0, :, vsl[h]] = oh[0:nv]

    @pl.when(t == pl.num_programs(1) - 1)
    def _():
        sT_ref[0] = s_ref[...]


def _gla(y, w_gate2, b_gate, s0, nb, nt, nv):
    Tt = SEQ_TILE
    y3 = y.reshape(nb * nt, nv, C_NPAD)
    wg = jnp.zeros((LANES, C_KEY), F32).at[:C_RANK].set(w_gate2).astype(BF16)
    idx = np.arange(Tt)
    same = (idx[:, None] // C_CHUNK) == (idx[None, :] // C_CHUNK)
    tri = jnp.asarray(same & (idx[None, :] <= idx[:, None]), BF16)
    blk = jnp.asarray(same, BF16)

    def yspec(width, cb):
        return pl.BlockSpec((1, nv, width), lambda b, t: (b * nt + t, 0, cb))

    def cspec(a):
        return pl.BlockSpec(a.shape, lambda b, t, nd=a.ndim: (0,) * nd)

    bg = b_gate.reshape(1, C_KEY)
    o, sT = pl.pallas_call(
        functools.partial(_gla_kernel, nv),
        grid=(nb, nt),
        in_specs=[
            yspec(C_KEY, 0), yspec(C_KEY, 1), yspec(C_VAL, 1), yspec(LANES, (2 * C_KEY + 2 * C_VAL) // LANES),
            cspec(wg), cspec(bg), cspec(tri), cspec(blk),
            pl.BlockSpec((1, C_HEADS, C_DK, C_DV), lambda b, t: (b, 0, 0, 0)),
        ],
        out_specs=[
            pl.BlockSpec((1, nv, C_VAL), lambda b, t: (b * nt + t, 0, 0)),
            pl.BlockSpec((1, C_HEADS, C_DK, C_DV), lambda b, t: (b, 0, 0, 0)),
        ],
        out_shape=[
            jax.ShapeDtypeStruct((nb * nt, nv, C_VAL), F32),
            jax.ShapeDtypeStruct((nb, C_HEADS, C_DK, C_DV), F32),
        ],
        scratch_shapes=[
            pltpu.VMEM((C_HEADS, C_DK, C_DV), F32),
            pltpu.VMEM((Tt, C_KEY), F32), pltpu.VMEM((Tt, C_KEY), F32),
            pltpu.VMEM((Tt, C_VAL), F32), pltpu.VMEM((Tt, LANES), F32),
        ],
        compiler_params=_params("parallel", "arbitrary"),
        name="gla",
    )(y3, y3, y3, y3, wg, bg, tri, blk, s0)
    return o.reshape(nb * nt * nv, C_VAL), sT


def _mixer_c(xp, xs, gain, w_in, w_gate2, b_gate, norm_g, w_out, state, Bn, S, Bd, T):
    n_in = w_in.shape[1]
    w = jnp.zeros((D_MODEL, C_NPAD), BF16).at[:, :n_in].set(w_in.astype(BF16))
    yp = _proj(xp, gain, w, 1024)
    ys = _proj(xs, gain, w, xs.shape[0])
    zero = jnp.zeros((Bn, C_HEADS, C_DK, C_DV), F32)
    op, sp = _gla(yp, w_gate2, b_gate, zero, Bn, S // SEQ_TILE, SEQ_TILE)
    osm, ss = _gla(ys, w_gate2, b_gate, state, Bd, 1, T)
    wo = w_out.astype(BF16)
    ng = norm_g.reshape(1, C_DV)
    rcb = (2 * C_KEY + C_VAL) // C_VAL
    mix_p = (_mix_c, [(op, C_VAL, 0), (yp, C_VAL, rcb)], [ng, wo])
    mix_s = (_mix_c, [(osm, C_VAL, 0), (ys, C_VAL, rcb)], [ng, wo])
    return mix_p, mix_s, sp, ss


def _ssd_kernel(nv, xa_ref, xb_ref, bc_ref, dt_ref, cs_ref, cw_ref, cb_ref, dtb_ref, alog_ref, dsk_ref,
                tri_ref, exp_ref, expt_ref, h0_ref, y_ref, hT_ref, h_ref, xp_ref, dtp_ref):
    t = pl.program_id(1)
    Tt = SEQ_TILE
    pre = 8

    @pl.when(t == 0)
    def _():
        h_ref[...] = h0_ref[0]
        xp_ref[0:pre, :] = cs_ref[0]

    if nv < Tt:
        xp_ref[pre:, :] = jnp.zeros((Tt, D_XBC), F32)
        dtp_ref[...] = jnp.zeros_like(dtp_ref)
    xp_ref[pre:pre + nv, 0:1024] = xa_ref[0]
    xp_ref[pre:pre + nv, 1024:2048] = xb_ref[0]
    xp_ref[pre:pre + nv, 2048:3072] = bc_ref[0]
    dtp_ref[0:nv, :] = dt_ref[0]

    full = xp_ref[...]
    conv = cb_ref[...] + full[pre:] * cw_ref[D_CONV - 1:D_CONV, :]
    for k in range(1, D_CONV):
        conv = conv + pltpu.roll(full, k, 0)[pre:] * cw_ref[D_CONV - 1 - k:D_CONV - k, :]
    xp_ref[0:pre, :] = xp_ref[Tt:Tt + pre, :]
    xbc = _silu(conv)
    x = xbc[:, :D_INNER]
    nbc = D_GROUPS * D_STATE
    bm = xbc[:, D_INNER:D_INNER + nbc].astype(BF16)
    cm = xbc[:, D_INNER + nbc:].astype(BF16)

    row = lax.broadcasted_iota(jnp.int32, (Tt, LANES), 0)
    lane = lax.broadcasted_iota(jnp.int32, (Tt, LANES), 1)
    live = (row < nv) & (lane < D_HEADS)
    dt = jnp.where(live, _softplus(dtp_ref[...] + dtb_ref[...]), 0.0)
    cum = _sel_left(tri_ref[...], dt * (-jnp.exp(alog_ref[...])), terms=3)
    cumt = cum.T
    cend = cum[Tt - 1:Tt, :]
    ex = exp_ref[...]
    stack = jnp.concatenate([jnp.exp(cend - cum) * dt, dt, jnp.exp(cum)], axis=0)
    wide = _sel_right(stack, ex)
    x_state = (x * wide[0:Tt]).astype(BF16)
    x_dt = (x * wide[Tt:2 * Tt]).astype(BF16)
    off_scale = wide[2 * Tt:]

    causal = tri_ref[...] > 0
    gw = D_INNER // D_GROUPS
    cbs, offs = [], []
    for g in range(D_GROUPS):
        cg = cm[:, g * D_STATE:(g + 1) * D_STATE]
        cbs.append(_dot_nt(cg, bm[:, g * D_STATE:(g + 1) * D_STATE]))
        offs.append(_dot_nt(cg, h_ref[g * gw:(g + 1) * gw, :].astype(BF16)))
    mats = []
    for j in range(D_HEADS):
        seg = cum[:, j:j + 1] - cumt[j:j + 1, :]
        mats.append((cbs[j // D_HPG] * jnp.exp(jnp.where(causal, seg, -jnp.inf))).astype(BF16))
    diag = [_dot(mats[j], x_dt[:, j * D_HEADDIM:(j + 1) * D_HEADDIM]) for j in range(D_HEADS)]
    y = jnp.concatenate(diag, axis=1) + jnp.concatenate(offs, axis=1) * off_scale + dsk_ref[...] * x
    y_ref[0] = y[0:nv]

    dcol = jnp.broadcast_to(jnp.exp(cumt[:, Tt - 1:Tt]), (LANES, D_STATE))
    dfull = _sel_left(expt_ref[...], dcol)
    xst = x_state.astype(F32).T.astype(BF16)
    for g in range(D_GROUPS):
        rows = slice(g * gw, (g + 1) * gw)
        h_ref[rows, :] = dfull[rows] * h_ref[rows, :] + _dot(xst[rows], bm[:, g * D_STATE:(g + 1) * D_STATE])

    @pl.when(t == pl.num_programs(1) - 1)
    def _():
        hT_ref[0] = h_ref[...]


def _ssd(y, conv_w, conv_b, dt_bias, a_log, d_skip, h0, c0, nb, nt, nv):
    Tt = SEQ_TILE
    y3 = y.reshape(nb * nt, nv, D_NPAD)
    idx = np.arange(Tt)
    tri = jnp.asarray(idx[None, :] <= idx[:, None], BF16)
    ex_np = np.zeros((LANES, D_INNER), np.float32)
    ex_np[np.arange(D_INNER) // D_HEADDIM, np.arange(D_INNER)] = 1.0
    ex = jnp.asarray(ex_np, BF16)
    ext = jnp.asarray(ex_np.T, BF16)

    def pad_row(v):
        return jnp.zeros((1, LANES), F32).at[0, :D_HEADS].set(v)

    cs = jnp.zeros((nb, 8, D_XBC), F32).at[:, 8 - (D_CONV - 1):].set(c0)
    dsk = jnp.repeat(d_skip, D_HEADDIM).reshape(1, D_INNER)

    def yspec(width, cb):
        return pl.BlockSpec((1, nv, width), lambda b, t: (b * nt + t, 0, cb))

    def cspec(a):
        return pl.BlockSpec(a.shape, lambda b, t, nd=a.ndim: (0,) * nd)

    consts = [conv_w, conv_b.reshape(1, D_XBC), pad_row(dt_bias), pad_row(a_log), dsk, tri, ex, ext]
    yo, hT = pl.pallas_call(
        functools.partial(_ssd_kernel, nv),
        grid=(nb, nt),
        in_specs=[yspec(1024, 2), yspec(1024, 3), yspec(1024, 4), yspec(LANES, D_DT_COL),
                  pl.BlockSpec((1, 8, D_XBC), lambda b, t: (b, 0, 0))]
        + [cspec(a) for a in consts]
        + [pl.BlockSpec((1, D_INNER, D_STATE), lambda b, t: (b, 0, 0))],
        out_specs=[
            pl.BlockSpec((1, nv, D_INNER), lambda b, t: (b * nt + t, 0, 0)),
            pl.BlockSpec((1, D_INNER, D_STATE), lambda b, t: (b, 0, 0)),
        ],
        out_shape=[
            jax.ShapeDtypeStruct((nb * nt, nv, D_INNER), F32),
            jax.ShapeDtypeStruct((nb, D_INNER, D_STATE), F32),
        ],
        scratch_shapes=[
            pltpu.VMEM((D_INNER, D_STATE), F32),
            pltpu.VMEM((Tt + 8, D_XBC), F32),
            pltpu.VMEM((Tt, LANES), F32),
        ],
        compiler_params=_params("parallel", "arbitrary"),
        name="ssd",
    )(y3, y3, y3, y3, cs, *consts, h0)
    return yo.reshape(nb * nt * nv, D_INNER), hT


def _ssd_step_kernel(nv, xa_ref, xb_ref, bc_ref, dt_ref, cs_ref, cw_ref, cb_ref, dtb_ref, alog_ref, dsk_ref,
                     eye_ref, h0_ref, y_ref, hT_ref, xp_ref):
    R = 8
    pre = 8
    xp_ref[0:pre, :] = cs_ref[0]
    xp_ref[pre:, :] = jnp.zeros((R, D_XBC), F32)
    xp_ref[pre:pre + nv, 0:1024] = xa_ref[0]
    xp_ref[pre:pre + nv, 1024:2048] = xb_ref[0]
    xp_ref[pre:pre + nv, 2048:3072] = bc_ref[0]
    conv = cb_ref[...]
    for w in range(D_CONV):
        conv = conv + xp_ref[pl.ds(pre - (D_CONV - 1) + w, R), :] * cw_ref[w:w + 1, :]
    xbc = _silu(conv)
    x = xbc[:, :D_INNER]
    nbc = D_GROUPS * D_STATE
    bmf = xbc[:, D_INNER:D_INNER + nbc]
    cmf = xbc[:, D_INNER + nbc:]

    row = lax.broadcasted_iota(jnp.int32, (R, LANES), 0)
    lane = lax.broadcasted_iota(jnp.int32, (R, LANES), 1)
    dtp = jnp.concatenate([dt_ref[0], jnp.zeros((R - nv, LANES), F32)], axis=0)
    dt = jnp.where((row < nv) & (lane < D_HEADS), _softplus(dtp + dtb_ref[...]), 0.0)
    cum = dt * (-jnp.exp(alog_ref[...]))
    for k in (1, 2, 4):
        cum = cum + jnp.where(row >= k, pltpu.roll(cum, k, 0), 0.0)
    cend = cum[R - 1:R, :]
    group_of_lane = lane // D_HPG
    low_half = lane < D_HEADDIM

    def widen(a):
        pairs = [jnp.where(low_half, a[:, j:j + 1], a[:, j + 1:j + 2]) for j in range(0, D_HEADS, 2)]
        return jnp.concatenate(pairs, axis=1)

    yd = jnp.zeros((R, D_INNER), F32)
    for s in range(nv):
        w_s = jnp.where(row >= s, jnp.exp(cum - cum[s:s + 1, :]), 0.0) * dt[s:s + 1, :]
        prod = cmf * bmf[s:s + 1, :]
        cb_s = jnp.zeros((R, LANES), F32)
        for g in range(D_GROUPS):
            tot = jnp.sum(prod[:, g * D_STATE:(g + 1) * D_STATE], axis=1, keepdims=True)
            cb_s = jnp.where(group_of_lane == g, tot, cb_s)
        yd = yd + widen(cb_s * w_s) * x[s:s + 1, :]
    off_scale = widen(jnp.exp(cum))
    x_state = (x * widen(jnp.exp(cend - cum) * dt)).astype(BF16)

    gw = D_INNER // D_GROUPS
    cm = cmf.astype(BF16)
    parts = []
    for g in range(D_GROUPS):
        hg = h0_ref[0, g * gw:(g + 1) * gw, :]
        parts.append(_dot_nt(cm[:, g * D_STATE:(g + 1) * D_STATE], hg.astype(BF16)))
    y = yd + jnp.concatenate(parts, axis=1) * off_scale + dsk_ref[...] * x
    y_ref[0] = y[0:nv]

    erow = jnp.broadcast_to(jnp.exp(cend), (LANES, LANES))
    eye128 = (lax.broadcasted_iota(jnp.int32, (LANES, LANES), 0)
              == lax.broadcasted_iota(jnp.int32, (LANES, LANES), 1)).astype(BF16)
    dcol = _nt_sel(eye128, erow)[0:D_HEADS].reshape(D_HEADS, 1, D_STATE)
    zpad = jnp.zeros((LANES - R, D_INNER), BF16)
    xs_pad = jnp.concatenate([x_state, zpad], axis=0)
    bm_pad = jnp.concatenate([bmf.astype(BF16), jnp.zeros((LANES - R, nbc), BF16)], axis=0)
    for g in range(D_GROUPS):
        rows = slice(g * gw, (g + 1) * gw)
        xt = _dot_nt(eye_ref[...], xs_pad[:, rows]).astype(BF16)
        kept = (h0_ref[0, rows, :].reshape(D_HPG, D_HEADDIM, D_STATE) * dcol[g * D_HPG:(g + 1) * D_HPG])
        hT_ref[0, rows, :] = kept.reshape(gw, D_STATE) + _dot(xt, bm_pad[:, g * D_STATE:(g + 1) * D_STATE])


def _ssd_step(y, conv_w, conv_b, dt_bias, a_log, d_skip, h0, c0, nb, nv):
    assert nv <= 8
    y3 = y.reshape(nb, nv, D_NPAD)
    gw = D_INNER // D_GROUPS
    eye = jnp.asarray(np.eye(gw), BF16)

    def pad_row(v):
        return jnp.zeros((1, LANES), F32).at[0, :D_HEADS].set(v)

    cs = jnp.zeros((nb, 8, D_XBC), F32).at[:, 8 - (D_CONV - 1):].set(c0)
    dsk = jnp.repeat(d_skip, D_HEADDIM).reshape(1, D_INNER)

    def yspec(width, cb):
        return pl.BlockSpec((1, nv, width), lambda b: (b, 0, cb))

    def cspec(a):
        return pl.BlockSpec(a.shape, lambda b, nd=a.ndim: (0,) * nd)

    consts = [conv_w, conv_b.reshape(1, D_XBC), pad_row(dt_bias), pad_row(a_log), dsk, eye]
    yo, hT = pl.pallas_call(
        functools.partial(_ssd_step_kernel, nv),
        grid=(nb,),
        in_specs=[yspec(1024, 2), yspec(1024, 3), yspec(1024, 4), yspec(LANES, D_DT_COL),
                  pl.BlockSpec((1, 8, D_XBC), lambda b: (b, 0, 0))]
        + [cspec(a) for a in consts]
        + [pl.BlockSpec((1, D_INNER, D_STATE), lambda b: (b, 0, 0))],
        out_specs=[
            pl.BlockSpec((1, nv, D_INNER), lambda b: (b, 0, 0)),
            pl.BlockSpec((1, D_INNER, D_STATE), lambda b: (b, 0, 0)),
        ],
        out_shape=[
            jax.ShapeDtypeStruct((nb, nv, D_INNER), F32),
            jax.ShapeDtypeStruct((nb, D_INNER, D_STATE), F32),
        ],
        scratch_shapes=[pltpu.VMEM((16, D_XBC), F32)],
        compiler_params=_params("parallel"),
        name="ssd_step",
    )(y3, y3, y3, y3, cs, *consts, h0)
    return yo.reshape(nb * nv, D_INNER), hT


def _mixer_d(xp, xs, gain, w_in, conv_w, conv_b, dt_bias, a_log, d_skip, norm_g, w_out,
             ssm_state, conv_state, Bn, S, Bd, T):
    n_in = w_in.shape[1]
    w = jnp.zeros((D_MODEL, D_NPAD), BF16).at[:, :n_in].set(w_in.astype(BF16))
    yp = _proj(xp, gain, w, 1024)
    ys = _proj(xs, gain, w, xs.shape[0])
    h0p = jnp.zeros((Bn, D_INNER, D_STATE), F32)
    c0p = jnp.zeros((Bn, D_CONV - 1, D_XBC), F32)
    op, hp = _ssd(yp, conv_w, conv_b, dt_bias, a_log, d_skip, h0p, c0p, Bn, S // SEQ_TILE, SEQ_TILE)
    h0s = ssm_state.reshape(Bd, D_INNER, D_STATE)
    osm, hs = _ssd_step(ys, conv_w, conv_b, dt_bias, a_log, d_skip, h0s, conv_state, Bd, T)
    wo = w_out.astype(BF16)
    ng = norm_g.reshape(1, D_INNER)
    mix_p = (_mix_d, [(op, D_INNER, 0), (yp, D_INNER, 0)], [ng, wo])
    mix_s = (_mix_d, [(osm, D_INNER, 0), (ys, D_INNER, 0)], [ng, wo])
    keep = D_CONV - 1
    xbc_p = yp.reshape(Bn, S, D_NPAD)[:, :, D_INNER:D_INNER + D_XBC]
    xbc_s = ys.reshape(Bd, T, D_NPAD)[:, :, D_INNER:D_INNER + D_XBC]
    cp = jnp.concatenate([c0p, xbc_p], axis=1)[:, -keep:] if S < keep else xbc_p[:, S - keep:]
    cs = jnp.concatenate([conv_state, xbc_s], axis=1)[:, -keep:]
    return (mix_p, mix_s, hp.reshape(Bn, D_HEADS, D_HEADDIM, D_STATE), hs.reshape(Bd, D_HEADS, D_HEADDIM, D_STATE),
            cp, cs)


def kernel(x_prompt, x_sample, cache_a_w1, cache_a_w2, cache_a_w3, cache_b_kv, page_table, state_c, state_d_ssm, state_d_conv, norm_gain, w_ffn_up, w_ffn_down, w_a_in, a_qk_gain, w_a_out, w_b_in, b_qk_gain, w_b_out, w_c_in, w_c_gate2, b_c_gate, c_norm_gain, w_c_out, w_d_in, d_conv_w, d_conv_b, d_dt_bias, d_a_log, d_skip, d_norm_gain, w_d_out):
    Bn, S, _ = x_prompt.shape
    Bd, T, _ = x_sample.shape
    depth = norm_gain.shape[0]
    past_len = page_table.shape[1] * cache_b_kv.shape[2]
    tabs_p = _rope_tables(jnp.arange(S, dtype=jnp.int32))
    tabs_s = _rope_tables(jnp.tile(past_len + jnp.arange(T, dtype=jnp.int32), Bd))
    xp = x_prompt.reshape(Bn * S, D_MODEL)
    xs = x_sample.reshape(Bd * T, D_MODEL)
    ts = xs.shape[0]
    w_up = w_ffn_up.astype(BF16)
    w_down = w_ffn_down.astype(BF16)
    outs = {k: [] for k in ("a0p", "a0s", "a1p", "a1s", "a2p", "a2s", "bp", "bs", "cp", "cs",
                            "hp", "hs", "dp", "ds")}
    for i in range(depth):
        m, j = i % 4, i // 4
        g = norm_gain[i]
        xp = _ffn(xp, g[0], w_up, w_down, i, 0, 1024)
        xs = _ffn(xs, g[0], w_up, w_down, i, 0, ts)
        if m == 0:
            mix_p, mix_s, new_p, new_s = _mixer_a(xp, xs, g[1], w_a_in[j], a_qk_gain[j], w_a_out[j],
                                            (cache_a_w1[j], cache_a_w2[j], cache_a_w3[j]),
                                            tabs_p, tabs_s, Bn, S, Bd, T)
            for gi in range(A_GROUPS):
                outs["a%dp" % gi].append(new_p[gi])
                outs["a%ds" % gi].append(new_s[gi])
        elif m == 1:
            mix_p, mix_s, kvp, kvs = _mixer_b(xp, xs, g[1], w_b_in[j], b_qk_gain[j], w_b_out[j],
                                        cache_b_kv[j], page_table, tabs_p, tabs_s, Bn, S, Bd, T)
            outs["bp"].append(kvp)
            outs["bs"].append(kvs)
        elif m == 2:
            mix_p, mix_s, sp, ss = _mixer_c(xp, xs, g[1], w_c_in[j], w_c_gate2[j], b_c_gate[j], c_norm_gain[j],
                                      w_c_out[j], state_c[j], Bn, S, Bd, T)
            outs["cp"].append(sp)
            outs["cs"].append(ss)
        else:
            mix_p, mix_s, hp, hs, cp, cs = _mixer_d(xp, xs, g[1], w_d_in[j], d_conv_w[j], d_conv_b[j], d_dt_bias[j],
                                              d_a_log[j], d_skip[j], d_norm_gain[j], w_d_out[j],
                                              state_d_ssm[j], state_d_conv[j], Bn, S, Bd, T)
            outs["hp"].append(hp)
            outs["hs"].append(hs)
            outs["dp"].append(cp)
            outs["ds"].append(cs)
        xp = _ffn(xp, g[2], w_up, w_down, i, 1, 1024, mix_p)
        xs = _ffn(xs, g[2], w_up, w_down, i, 1, ts, mix_s)
    st = {k: jnp.stack(v) for k, v in outs.items()}
    return (xp.reshape(Bn, S, D_MODEL), xs.reshape(Bd, T, D_MODEL),
            st["a0p"], st["a0s"], st["a1p"], st["a1s"], st["a2p"], st["a2s"],
            st["bp"], st["bs"], st["cp"], st["cs"], st["hp"], st["hs"], st["dp"], st["ds"])
```

```python
import functools
import math

import numpy as np
import jax
import jax.numpy as jnp
from jax import lax
from jax.experimental import pallas as pl
from jax.experimental.pallas import tpu as pltpu

F32 = jnp.float32
BF16 = jnp.bfloat16

D_MODEL = 1024
HEAD_DIM = 64
ROPE_DIM = HEAD_DIM // 4
ROPE_THETA = 500000.0
EPS = 1e-6
D_FF = 2816
NEG = -1e30

A_GROUPS = 3
A_DILATIONS = (1, 4, 16)
A_HEADS = 8
A_WIDTH = A_HEADS * HEAD_DIM
A_SPAN = 128

B_HEADS = 16
B_KV_HEADS = 4
B_REP = B_HEADS // B_KV_HEADS
B_BLOCK = 256
B_TOPK = 3
B_VT_ROWS = HEAD_DIM + 8
LOG2E = 1.4426950408889634

C_HEADS = 4
C_KEY = 512
C_VAL = 1024
C_DK = 128
C_DV = 256
C_RANK = 16
C_TAU = 16.0
C_CHUNK = 32
C_NPAD = 3200

D_INNER = 2048
D_HEADDIM = 64
D_HEADS = 32
D_GROUPS = 4
D_HPG = 8
D_STATE = 128
D_CONV = 4
D_XBC = D_INNER + 2 * D_GROUPS * D_STATE
D_NPAD = 5376
D_DT_COL = (D_INNER + D_XBC) // 128

LANES = 128
FFN_ROWS = 512
PROJ_ROWS = 256
SEQ_TILE = 128


def _params(*sem):
    return pltpu.CompilerParams(dimension_semantics=sem)


def _split2(a):
    hi = a.astype(BF16)
    lo = (a - hi.astype(F32)).astype(BF16)
    return hi, lo


def _split3(a):
    hi = a.astype(BF16)
    r = a - hi.astype(F32)
    mid = r.astype(BF16)
    lo = (r - mid.astype(F32)).astype(BF16)
    return hi, mid, lo


def _dot(a, b):
    return jnp.dot(a, b, preferred_element_type=F32)


def _dot_nt(a, b):
    return lax.dot_general(a, b, (((1,), (1,)), ((), ())), preferred_element_type=F32)


def _sel_right(a, sel01, terms=2):
    parts = _split2(a) if terms == 2 else _split3(a)
    out = None
    for p in reversed(parts):
        d = _dot(p, sel01)
        out = d if out is None else out + d
    return out


def _sel_left(sel01, a, terms=2):
    parts = _split2(a) if terms == 2 else _split3(a)
    out = None
    for p in reversed(parts):
        d = _dot(sel01, p)
        out = d if out is None else out + d
    return out


def _silu(x):
    return x * jax.nn.sigmoid(x)


def _softplus(x):
    return jnp.maximum(x, 0.0) + jnp.log(1.0 + jnp.exp(-jnp.abs(x)))


def _rms_rows(x, g):
    ms = jnp.mean(x * x, axis=-1, keepdims=True)
    return x * lax.rsqrt(ms + EPS) * g


def _ffn_kernel(mix_fn, n_mix, x_ref, *refs):
    mix_refs = refs[:n_mix]
    g_ref, wg_ref, wu_ref, wd_ref, o_ref = refs[n_mix:]
    x = x_ref[...]
    if mix_fn is not None:
        x = x + mix_fn(*mix_refs)
    h = _rms_rows(x, g_ref[...]).astype(BF16)
    a = _dot(h, wg_ref[...])
    u = _dot(h, wu_ref[...])
    act = (_silu(a) * u).astype(BF16)
    o_ref[...] = x + 0.5 * _dot(act, wd_ref[...])


def _ffn(x, gain, w_up, w_down, layer, which, tm, mix=None):
    T = x.shape[0]
    tm = min(tm, FFN_ROWS)
    resident = dict(pipeline_mode=pl.Buffered(1))
    mix_fn, row_inputs, const_inputs = mix if mix is not None else (None, [], [])
    mix_specs, mix_args = [], []
    for arr, width, cb in row_inputs:
        mix_specs.append(pl.BlockSpec((tm, width), lambda i, cb=cb: (i, cb)))
        mix_args.append(arr)
    for arr in const_inputs:
        mix_specs.append(pl.BlockSpec(arr.shape, lambda i, nd=arr.ndim: (0,) * nd, **resident))
        mix_args.append(arr)
    return pl.pallas_call(
        functools.partial(_ffn_kernel, mix_fn, len(mix_args)),
        grid=(T // tm,),
        in_specs=[pl.BlockSpec((tm, D_MODEL), lambda i: (i, 0))] + mix_specs + [
            pl.BlockSpec((1, D_MODEL), lambda i: (0, 0), **resident),
            pl.BlockSpec((None, None, D_MODEL, D_FF), lambda i: (layer, which, 0, 0), **resident),
            pl.BlockSpec((None, None, D_MODEL, D_FF), lambda i: (layer, which, 0, 1), **resident),
            pl.BlockSpec((None, None, D_FF, D_MODEL), lambda i: (layer, which, 0, 0), **resident),
        ],
        out_specs=pl.BlockSpec((tm, D_MODEL), lambda i: (i, 0)),
        out_shape=jax.ShapeDtypeStruct((T, D_MODEL), F32),
        compiler_params=_params("parallel"),
        name="ffn",
    )(x, *mix_args, gain.reshape(1, D_MODEL), w_up, w_up, w_down)


def _proj_kernel(x_ref, g_ref, w_ref, o_ref):
    o_ref[...] = _dot(_rms_rows(x_ref[...], g_ref[...]).astype(BF16), w_ref[...])


def _proj(x, gain, w, tm):
    T = x.shape[0]
    N = w.shape[1]
    tm = min(tm, PROJ_ROWS)
    resident = dict(pipeline_mode=pl.Buffered(1))
    return pl.pallas_call(
        _proj_kernel,
        grid=(T // tm,),
        in_specs=[
            pl.BlockSpec((tm, D_MODEL), lambda i: (i, 0)),
            pl.BlockSpec((1, D_MODEL), lambda i: (0, 0), **resident),
            pl.BlockSpec((D_MODEL, N), lambda i: (0, 0), **resident),
        ],
        out_specs=pl.BlockSpec((tm, N), lambda i: (i, 0)),
        out_shape=jax.ShapeDtypeStruct((T, N), F32),
        compiler_params=_params("parallel"),
        name="proj",
    )(x, gain.reshape(1, D_MODEL), w)


def _proj_qk_kernel(flags, t_chunks, x_ref, g_ref, w_ref, bd_ref, gain_ref, mask_ref, cos_ref, sa_ref, sb_ref,
                    o_ref, *t_refs):
    h = _rms_rows(x_ref[...], g_ref[...]).astype(BF16)
    tn = bd_ref.shape[0]
    cos, sa, sb = cos_ref[...], sa_ref[...], sb_ref[...]
    for j, flag in enumerate(flags):
        y = _dot(h, w_ref[:, j * tn:(j + 1) * tn])
        if not flag:
            o_ref[:, j * tn:(j + 1) * tn] = y
            continue
        ss = _dot((y * y).astype(BF16), bd_ref[...])
        yn = y * lax.rsqrt(ss * (1.0 / HEAD_DIM) + EPS) * gain_ref[j]
        mask = mask_ref[j]
        for c in range(tn // LANES):
            sl = slice(c * LANES, (c + 1) * LANES)
            v = yn[:, sl]
            up = pltpu.roll(v, LANES - ROPE_DIM // 2, 1)
            dn = pltpu.roll(v, ROPE_DIM // 2, 1)
            rot = v * cos + up * sa + dn * sb
            if flag == 2:
                rot = jnp.where(mask[:, sl] > 0.0, rot, y[:, sl])
            o_ref[:, j * tn + c * LANES:j * tn + (c + 1) * LANES] = rot
    for t_ref, chunks in zip(t_refs, t_chunks):
        for k, j in enumerate(chunks):
            t_ref[0, k * tn:(k + 1) * tn, :] = o_ref[:, j * tn:(j + 1) * tn].T


def _proj_qk(x, gain, w, flags, gain_rows, mask_rows, tables, tm, tn, t_chunks=(), seq=None):
    T = x.shape[0]
    N = w.shape[1]
    nj = N // tn
    tm = min(tm, PROJ_ROWS)
    cos, sa, sb = tables
    n_pos_tiles = cos.shape[0] // tm
    bd = jnp.asarray(np.kron(np.eye(tn // HEAD_DIM), np.ones((HEAD_DIM, HEAD_DIM))), BF16)
    resident = dict(pipeline_mode=pl.Buffered(1))
    tab_spec = pl.BlockSpec((tm, LANES), lambda i: (i % n_pos_tiles, 0))
    out_specs = pl.BlockSpec((tm, N), lambda i: (i, 0))
    out_shape = jax.ShapeDtypeStruct((T, N), F32)
    per_seq = seq // tm if t_chunks else 1
    windows = []
    if t_chunks:
        out_specs, out_shape = [out_specs], [out_shape]
        for chunks, window in t_chunks:
            n_tiles = -(-window // tm)
            first = per_seq - n_tiles
            windows.append(tuple(chunks))
            out_specs.append(pl.BlockSpec((1, len(chunks) * tn, tm),
                                          lambda i, first=first: (i // per_seq, 0, jnp.maximum(i % per_seq - first, 0))))
            out_shape.append(jax.ShapeDtypeStruct((T // seq, len(chunks) * tn, n_tiles * tm), F32))
    return pl.pallas_call(
        functools.partial(_proj_qk_kernel, flags, tuple(windows)),
        grid=(T // tm,),
        in_specs=[
            pl.BlockSpec((tm, D_MODEL), lambda i: (i, 0)),
            pl.BlockSpec((1, D_MODEL), lambda i: (0, 0), **resident),
            pl.BlockSpec((D_MODEL, N), lambda i: (0, 0), **resident),
            pl.BlockSpec((tn, tn), lambda i: (0, 0), **resident),
            pl.BlockSpec((nj, 1, tn), lambda i: (0, 0, 0), **resident),
            pl.BlockSpec((nj, 1, tn), lambda i: (0, 0, 0), **resident),
            tab_spec, tab_spec, tab_spec,
        ],
        out_specs=out_specs,
        out_shape=out_shape,
        compiler_params=_params("arbitrary"),
        name="proj_qk",
    )(x, gain.reshape(1, D_MODEL), w, bd, gain_rows, mask_rows, cos, sa, sb)


def _rope_tables(pos):
    half = ROPE_DIM // 2
    inv = ROPE_THETA ** (-jnp.arange(half, dtype=F32) / half)
    ang = pos.astype(F32)[:, None] * inv[None, :]
    cos, sin = jnp.cos(ang), jnp.sin(ang)
    n = pos.shape[0]
    one = jnp.ones((n, HEAD_DIM - ROPE_DIM), F32)
    zero = jnp.zeros((n, HEAD_DIM - ROPE_DIM), F32)
    zh = jnp.zeros((n, half), F32)
    c64 = jnp.concatenate([cos, cos, one], axis=1)
    sa64 = jnp.concatenate([-sin, zh, zero], axis=1)
    sb64 = jnp.concatenate([zh, sin, zero], axis=1)
    rep = LANES // HEAD_DIM
    return tuple(jnp.tile(t, (1, rep)) for t in (c64, sa64, sb64))


def _mix_plain(o_ref, w_ref):
    return _dot(o_ref[...].astype(BF16), w_ref[...])


def _mix_a(o0, o1, o2, l0, l1, l2, w_ref):
    la, lb, lc = l0[...], l1[...], l2[...]
    m = jnp.maximum(jnp.maximum(la, lb), lc)
    ea, eb, ec = jnp.exp(la - m), jnp.exp(lb - m), jnp.exp(lc - m)
    o = (ea * o0[...] + eb * o1[...] + ec * o2[...]) / (ea + eb + ec)
    return _dot(o.astype(BF16), w_ref[...])


def _mix_c(o_ref, r_ref, g_ref, w_ref):
    o = o_ref[...]
    g = g_ref[...]
    parts = [_rms_rows(o[:, h * C_DV:(h + 1) * C_DV], g) for h in range(C_HEADS)]
    on = jnp.concatenate(parts, axis=1) * _silu(r_ref[...])
    return _dot(on.astype(BF16), w_ref[...])


def _mix_d(o_ref, z_ref, g_ref, w_ref):
    gated = o_ref[...] * _silu(z_ref[...])
    gw = D_INNER // D_GROUPS
    g = g_ref[...]
    parts = [_rms_rows(gated[:, k * gw:(k + 1) * gw], g[:, k * gw:(k + 1) * gw]) for k in range(D_GROUPS)]
    return _dot(jnp.concatenate(parts, axis=1).astype(BF16), w_ref[...])


def _a_prompt_kernel(d, q_ref, kp_ref, kc_ref, vp_ref, vc_ref, o_ref, l_ref):
    j = pl.program_id(1)
    heads = q_ref.shape[2] // HEAD_DIM
    per_pass = max(1, A_HEADS // heads)
    n_prob = per_pass * heads
    scale = HEAD_DIM ** -0.5
    qq = lax.broadcasted_iota(jnp.int32, (n_prob * A_SPAN, 2 * A_SPAN), 0) & (A_SPAN - 1)
    kk = lax.broadcasted_iota(jnp.int32, (n_prob * A_SPAN, 2 * A_SPAN), 1)
    rel = qq + A_SPAN - kk
    valid = (rel >= 0) & (rel <= A_SPAN) & ((kk >= A_SPAN) | (j > 0))
    hs = [slice(h * HEAD_DIM, (h + 1) * HEAD_DIM) for h in range(heads)]
    for r0 in range(0, d, per_pass):
        rows = [pl.ds(r0 + e, A_SPAN, stride=d) if d > 1 else pl.ds(0, A_SPAN) for e in range(per_pass)]
        s, vs = [], []
        for rw in rows:
            q = (q_ref[0, rw, :] * scale).astype(BF16)
            k = jnp.concatenate([kp_ref[0, rw, :], kc_ref[0, rw, :]], axis=0).astype(BF16)
            vs.append(jnp.concatenate([vp_ref[0, rw, :], vc_ref[0, rw, :]], axis=0).astype(BF16))
            s += [_dot_nt(q[:, sl], k[:, sl]) for sl in hs]
        s = jnp.where(valid, jnp.concatenate(s, axis=0), -jnp.inf)
        m = jnp.max(s, axis=-1, keepdims=True)
        p = jnp.exp(s - m)
        den = jnp.sum(p, axis=-1, keepdims=True)
        pb = p.astype(BF16)
        lse = m + jnp.log(den)
        inv = 1.0 / den
        for e, rw in enumerate(rows):
            o, l = [], []
            for h, sl in enumerate(hs):
                blk = slice((e * heads + h) * A_SPAN, (e * heads + h + 1) * A_SPAN)
                o.append(_dot(pb[blk], vs[e][:, sl]) * inv[blk])
                l.append(jnp.broadcast_to(lse[blk], (A_SPAN, HEAD_DIM)))
            o_ref[0, rw, :] = jnp.concatenate(o, axis=1)
            l_ref[0, rw, :] = jnp.concatenate(l, axis=1)


def _a_prompt(y, g, Bn, S):
    d = A_DILATIONS[g]
    slab = A_SPAN * d
    heads = A_HEADS if d == 1 else LANES // HEAD_DIM
    cols = heads * HEAD_DIM
    ncb = A_WIDTH // cols
    y3 = y.reshape(Bn, S, A_GROUPS * 3 * A_WIDTH)

    def spec(which, prev):
        c0 = (3 * g + which) * ncb
        if prev:
            return pl.BlockSpec((1, slab, cols), lambda b, j, c: (b, jnp.maximum(j - 1, 0), c0 + c))
        return pl.BlockSpec((1, slab, cols), lambda b, j, c: (b, j, c0 + c))

    out_spec = pl.BlockSpec((1, slab, cols), lambda b, j, c: (b, j, c))
    o, l = pl.pallas_call(
        functools.partial(_a_prompt_kernel, d),
        grid=(Bn, S // slab, ncb),
        in_specs=[spec(0, False), spec(1, True), spec(1, False), spec(2, True), spec(2, False)],
        out_specs=[out_spec, out_spec],
        out_shape=[jax.ShapeDtypeStruct((Bn, S, A_WIDTH), F32)] * 2,
        compiler_params=_params("parallel", "arbitrary", "arbitrary"),
        name="a_prompt_d%d" % d,
    )(y3, y3, y3, y3, y3)
    return o.reshape(Bn * S, A_WIDTH), l.reshape(Bn * S, A_WIDTH)


def _nt_sel(mask, a):
    out = None
    for p in reversed(_split3(a)):
        d = _dot_nt(mask, p)
        out = d if out is None else out + d
    return out


def _a_sample_kernel(T, y_ref, c1_ref, c2_ref, c3_ref, hind_ref, hindt_ref, *out_refs):
    caches = (c1_ref, c2_ref, c3_ref)
    width = 3 * A_WIDTH
    scale = HEAD_DIM ** -0.5
    hind = hind_ref[...]
    hindt = hindt_ref[...]
    lane = lax.broadcasted_iota(jnp.int32, (8, LANES), 1)
    sub = lax.broadcasted_iota(jnp.int32, (8, LANES), 0)
    zrows = jnp.zeros((LANES - T, A_WIDTH), F32)
    zsq = jnp.zeros((LANES - 8, LANES), F32)

    def head_rows(prod):
        return _nt_sel(hind, jnp.concatenate([prod, zrows], axis=0))

    def sublane_total(x):
        x = x + pltpu.roll(x, 4, 0)
        x = x + pltpu.roll(x, 2, 0)
        return x + pltpu.roll(x, 1, 0)

    def head_scores(k_of, pat_of):
        out = jnp.zeros((8, LANES), F32)
        for h in range(A_HEADS):
            prod = k_of(h) * pat_of(h)
            part = prod[0:8]
            for j in range(1, HEAD_DIM // 8):
                part = part + prod[8 * j:8 * j + 8]
            out = jnp.where(sub == h, sublane_total(part), out)
        return out

    def rows_of(packed):
        return jnp.concatenate([packed, zsq], axis=0).T

    def widen(rows8):
        return _sel_right(rows8, hindt, terms=3)

    def place(col, at):
        return jnp.where(lane == at, col, 0.0)

    def patterns(q_t, pick):
        return _sel_right(q_t, pick.astype(BF16), terms=3)

    row_sq = lax.broadcasted_iota(jnp.int32, (LANES, LANES), 0)
    lane_sq = lax.broadcasted_iota(jnp.int32, (LANES, LANES), 1)

    prep = []
    for g in range(A_GROUPS):
        d = A_DILATIONS[g]
        q_nat = y_ref[0, :, g * width:g * width + A_WIDTH] * scale
        kn_nat = y_ref[0, :, g * width + A_WIDTH:g * width + 2 * A_WIDTH]
        vn_nat = y_ref[0, :, g * width + 2 * A_WIDTH:(g + 1) * width]
        q_t = jnp.concatenate([q_nat, zrows], axis=0).T
        if d == 1:
            pats = [patterns(q_t, row_sq == i) for i in range(T)]
            news = [head_rows(q_nat[i:i + 1, :] * kn_nat) for i in range(T)]
        else:
            pats = [patterns(q_t, (row_sq == (lane_sq & (d - 1))) & (row_sq < T))]
            news = [head_rows(q_nat * kn_nat)]
        prep.append((vn_nat, pats, news))

    swept = []
    for g in range(A_GROUPS):
        d = A_DILATIONS[g]
        c_ref = caches[g]
        n_tiles = c_ref.shape[-1] // LANES
        _, pats, news = prep[g]
        if d == 1:
            packed = jnp.zeros((8, LANES), F32)
            accs = []
            for i in range(T):
                pat = pats[i]
                s = head_scores(lambda h: c_ref[0, 0, h], lambda h: pat[h * HEAD_DIM:(h + 1) * HEAD_DIM])
                s = jnp.where(lane >= i, s, -jnp.inf)
                sn = jnp.where(lane <= i, news[i], -jnp.inf)
                m = jnp.maximum(jnp.max(s, axis=1, keepdims=True), jnp.max(sn, axis=1, keepdims=True))
                p = jnp.exp(s - m)
                pn = jnp.exp(sn - m)
                den = jnp.sum(p, axis=1, keepdims=True) + jnp.sum(pn, axis=1, keepdims=True)
                accs.append(jnp.concatenate([c_ref[0, 1, h] * p[h:h + 1, :] for h in range(A_HEADS)], axis=0))
                packed = packed + place(den, i) + place(m + jnp.log(den), 8 + i)
                packed = packed + pltpu.roll(pn, 16 + 8 * i, 1)
            swept.append((accs, packed))
        else:
            cls = lane & (d - 1)
            self_s = news[0]
            pat = pats[0]
            s_tiles = [head_scores(lambda h: c_ref[0, 0, h, :, t * LANES:(t + 1) * LANES],
                                   lambda h: pat[h * HEAD_DIM:(h + 1) * HEAD_DIM])
                       for t in range(n_tiles)]
            smax = s_tiles[0]
            for t in range(1, n_tiles):
                smax = jnp.maximum(smax, s_tiles[t])
            mrow = jnp.full((8, LANES), jnp.inf, F32)
            m_cls = []
            for i in range(T):
                mi = jnp.maximum(jnp.max(jnp.where(cls == i, smax, -jnp.inf), axis=1, keepdims=True),
                                 self_s[:, i:i + 1])
                m_cls.append(mi)
                mrow = jnp.where(cls == i, mi, mrow)
            p_tiles = [jnp.exp(s_tiles[t] - mrow) for t in range(n_tiles)]
            psum = p_tiles[0]
            for t in range(1, n_tiles):
                psum = psum + p_tiles[t]
            accs = []
            for h in range(A_HEADS):
                acc = jnp.zeros((HEAD_DIM, LANES), F32)
                for t in range(n_tiles):
                    acc = acc + c_ref[0, 1, h, :, t * LANES:(t + 1) * LANES] * p_tiles[t][h:h + 1, :]
                accs.append(acc)
            packed = jnp.zeros((8, LANES), F32)
            for i in range(T):
                ps = jnp.exp(self_s[:, i:i + 1] - m_cls[i])
                den = jnp.sum(jnp.where(cls == i, psum, 0.0), axis=1, keepdims=True) + ps
                packed = packed + place(ps, i) + place(den, 8 + i) + place(m_cls[i] + jnp.log(den), 16 + i)
            swept.append(([jnp.concatenate(accs, axis=0)], packed))

    for g in range(A_GROUPS):
        d = A_DILATIONS[g]
        vn_nat = prep[g][0]
        accs, packed = swept[g]
        sq = rows_of(packed)
        if d == 1:
            first = (sub == 0).astype(BF16)
            denx = widen(sq[0:8])
            lsex = widen(sq[8:16])
            for i in range(T):
                pnx = widen(sq[16 + 8 * i:24 + 8 * i])
                o = _nt_sel(first, accs[i])[0:1, :] + jnp.sum(pnx[0:T] * vn_nat, axis=0, keepdims=True)
                out_refs[g][0, i:i + 1, :] = o / denx[i:i + 1, :]
            out_refs[A_GROUPS + g][0] = lsex[0:T]
        else:
            cls = lane & (d - 1)
            res = _nt_sel((cls == sub).astype(BF16), accs[0])
            psx, denx, lsex = widen(sq[0:8]), widen(sq[8:16]), widen(sq[16:24])
            out_refs[g][0] = (res[0:T] + psx[0:T] * vn_nat) / denx[0:T]
            out_refs[A_GROUPS + g][0] = lsex[0:T]


def _a_sample(y, caches, Bd, T):
    width = A_GROUPS * 3 * A_WIDTH
    y3 = y.reshape(Bd, T, width)
    views = []
    specs = [pl.BlockSpec((1, T, width), lambda b: (b, 0, 0))]
    for g, c in enumerate(caches):
        d = A_DILATIONS[g]
        assert c.shape[1] == A_SPAN * d and (d == 1 or T <= d) and T <= 8
        views.append(c.transpose(0, 2, 3, 4, 1))
        specs.append(pl.BlockSpec((1, 2, A_HEADS, HEAD_DIM, A_SPAN * d), lambda b: (b, 0, 0, 0, 0)))
    hind_np = np.zeros((LANES, A_WIDTH), np.float32)
    hind_np[np.arange(A_WIDTH) // HEAD_DIM, np.arange(A_WIDTH)] = 1.0
    hind = jnp.asarray(hind_np[:A_HEADS], BF16)
    hindt = jnp.asarray(hind_np, BF16)
    specs += [pl.BlockSpec(hind.shape, lambda b: (0, 0)), pl.BlockSpec(hindt.shape, lambda b: (0, 0))]
    out_spec = pl.BlockSpec((1, T, A_WIDTH), lambda b: (b, 0, 0))
    outs = pl.pallas_call(
        functools.partial(_a_sample_kernel, T),
        grid=(Bd,),
        in_specs=specs,
        out_specs=[out_spec] * (2 * A_GROUPS),
        out_shape=[jax.ShapeDtypeStruct((Bd, T, A_WIDTH), F32)] * (2 * A_GROUPS),
        compiler_params=_params("parallel"),
        name="a_sample",
    )(y3, *views, hind, hindt)
    return [o.reshape(Bd * T, A_WIDTH) for o in outs]


def _mixer_a(xp, xs, gain, w_in, qk_gain, w_out, caches, tabs_p, tabs_s, Bn, S, Bd, T):
    flags = (1, 1, 0) * A_GROUPS
    ones = jnp.ones((A_WIDTH,), F32)
    rows = [jnp.tile(qk_gain[0], A_HEADS), jnp.tile(qk_gain[1], A_HEADS), ones] * A_GROUPS
    gain_rows = jnp.stack(rows).reshape(3 * A_GROUPS, 1, A_WIDTH)
    mask_rows = jnp.stack([ones, ones, 0.0 * ones] * A_GROUPS).reshape(3 * A_GROUPS, 1, A_WIDTH)
    w = w_in.astype(BF16)
    wins = [min(A_SPAN * A_DILATIONS[g], S) for g in range(A_GROUPS)]
    yp, *kvts = _proj_qk(xp, gain, w, flags, gain_rows, mask_rows, tabs_p, 1024, A_WIDTH,
                         t_chunks=[((3 * g + 1, 3 * g + 2), wins[g]) for g in range(A_GROUPS)], seq=S)
    ys = _proj_qk(xs, gain, w, flags, gain_rows, mask_rows, tabs_s, xs.shape[0], A_WIDTH)
    wo = w_out.astype(BF16)
    pr = [_a_prompt(yp, g, Bn, S) for g in range(A_GROUPS)]
    mix_p = (_mix_a, [(o, A_WIDTH, 0) for o, _ in pr] + [(l, A_WIDTH, 0) for _, l in pr], [wo])
    sr = _a_sample(ys, caches, Bd, T)
    mix_s = (_mix_a, [(o, A_WIDTH, 0) for o in sr], [wo])
    ys3 = ys.reshape(Bd, T, A_GROUPS * 3 * A_WIDTH)
    new_p, new_s = [], []
    for g in range(A_GROUPS):
        c0 = (3 * g + 1) * A_WIDTH
        kvt = kvts[g][:, :, kvts[g].shape[2] - wins[g]:]
        new_p.append(kvt.reshape(Bn, 2, A_HEADS, HEAD_DIM, wins[g]).transpose(0, 4, 1, 2, 3))
        new_s.append(ys3[:, :, c0:c0 + 2 * A_WIDTH].reshape(Bd, T, 2, A_HEADS, HEAD_DIM))
    return mix_p, mix_s, new_p, new_s


def _b_kmean_kernel(k_ref, o_ref):
    nblk = k_ref.shape[1] // B_BLOCK
    rows = [jnp.mean(k_ref[0, n * B_BLOCK:(n + 1) * B_BLOCK, :], axis=0, keepdims=True) for n in range(nblk)]
    o_ref[0] = jnp.concatenate(rows, axis=0)


def _b_gate_kernel(q_ref, kmh_ref, kml_ref, o_ref):
    i = pl.program_id(1)
    qh, ql = _split2(q_ref[0])
    kmh, kml = kmh_ref[0], kml_ref[0]
    gate = _dot_nt(kmh, qh) + (_dot_nt(kmh, ql) + _dot_nt(kml, qh))
    tq = gate.shape[1]
    nblk = gate.shape[0] // B_HEADS
    gate = gate.reshape(B_HEADS, nblk, tq)
    blk = lax.broadcasted_iota(jnp.int32, (B_HEADS, nblk, tq), 1)
    blkf = blk.astype(F32)
    gate = jnp.where(blk < i, gate, -jnp.inf)
    sel = jnp.zeros((B_HEADS, nblk, tq), jnp.bool_)
    for _ in range(B_TOPK):
        mx = jnp.max(gate, axis=1, keepdims=True)
        first = jnp.min(jnp.where(gate == mx, blkf, float(nblk)), axis=1, keepdims=True)
        hit = blkf == first
        sel = sel | (hit & (mx > -jnp.inf))
        gate = jnp.where(hit, -jnp.inf, gate)
    bias = jnp.where(sel | (blk >= i), 0.0, NEG).reshape(B_HEADS * nblk, tq)
    o_ref[0] = bias.T


def _b_attn_kernel(qi_ref, kn_ref, q_ref, k_ref, vt_ref, o_ref, *state):
    t = pl.program_id(1)
    i, n = qi_ref[t], kn_ref[t]
    tq = B_BLOCK
    wide = B_REP * tq
    m_refs, acc_refs = state[:B_KV_HEADS], state[B_KV_HEADS:]

    @pl.when(n == 0)
    def _():
        for kvh in range(B_KV_HEADS):
            m_refs[kvh][...] = jnp.full_like(m_refs[kvh], -jnp.inf)
            acc_refs[kvh][...] = jnp.zeros_like(acc_refs[kvh])

    def sweep(causal):
        if causal:
            kk = lax.broadcasted_iota(jnp.int32, (B_BLOCK, wide), 0)
            qq = lax.broadcasted_iota(jnp.int32, (B_BLOCK, wide), 1) & (tq - 1)
            keep = kk <= qq
        scores = []
        for kvh in range(B_KV_HEADS):
            k = k_ref[0, :, kvh * LANES:(kvh + 1) * LANES]
            s = _dot_nt(k, q_ref[0, 0, kvh])
            scores.append(jnp.where(keep, s, NEG) if causal else s)
        probs, scales = [], []
        for kvh in range(B_KV_HEADS):
            m_old = m_refs[kvh][...]
            m_new = jnp.maximum(m_old, jnp.max(scores[kvh], axis=0, keepdims=True))
            probs.append(jnp.exp2(scores[kvh] - m_new).astype(BF16))
            scales.append(jnp.exp2(m_old - m_new))
            m_refs[kvh][...] = m_new
        for kvh in range(B_KV_HEADS):
            acc_refs[kvh][...] = scales[kvh] * acc_refs[kvh][...] + _dot(vt_ref[0, kvh], probs[kvh])

    @pl.when(n < i)
    def _():
        sweep(False)

    @pl.when(n == i)
    def _():
        sweep(True)
        for kvh in range(B_KV_HEADS):
            acc = acc_refs[kvh][...]
            ot = acc[0:HEAD_DIM] / acc[HEAD_DIM:HEAD_DIM + 1]
            for r in range(0, B_REP, 2):
                pair = [ot[:, (r + e) * tq:(r + e + 1) * tq].T for e in range(2)]
                c0 = (kvh * B_REP + r) * HEAD_DIM
                o_ref[0, :, c0:c0 + 2 * HEAD_DIM] = jnp.concatenate(pair, axis=1)


def _b_prompt(y, Bn, S):
    nblk = S // B_BLOCK
    assert nblk % 8 == 0 and LANES % nblk == 0
    nq = B_HEADS * HEAD_DIM
    nk = B_KV_HEADS * HEAD_DIM
    y3 = y.reshape(Bn, S, nq + 2 * nk)
    kmean = pl.pallas_call(
        _b_kmean_kernel,
        grid=(Bn,),
        in_specs=[pl.BlockSpec((1, S, nk), lambda b: (b, 0, nq // nk))],
        out_specs=pl.BlockSpec((1, nblk, nk), lambda b: (b, 0, 0)),
        out_shape=jax.ShapeDtypeStruct((Bn, nblk, nk), F32),
        compiler_params=_params("parallel"),
        name="b_kmean",
    )(y3)
    km = kmean.reshape(Bn, nblk, B_KV_HEADS, HEAD_DIM)
    km = jnp.repeat(km, B_REP, axis=2)
    eye = jnp.eye(B_HEADS, dtype=F32)
    kmbd = jnp.einsum('bnhe,hg->bgnhe', km, eye).reshape(Bn, B_HEADS * nblk, nq)
    kmh = kmbd.astype(BF16)
    kml = (kmbd - kmh.astype(F32)).astype(BF16)
    bias = pl.pallas_call(
        _b_gate_kernel,
        grid=(Bn, nblk),
        in_specs=[
            pl.BlockSpec((1, B_BLOCK, nq), lambda b, i: (b, i, 0)),
            pl.BlockSpec((1, B_HEADS * nblk, nq), lambda b, i: (b, 0, 0)),
            pl.BlockSpec((1, B_HEADS * nblk, nq), lambda b, i: (b, 0, 0)),
        ],
        out_specs=pl.BlockSpec((1, B_BLOCK, B_HEADS * nblk), lambda b, i: (b, i, 0)),
        out_shape=jax.ShapeDtypeStruct((Bn, S, B_HEADS * nblk), F32),
        compiler_params=_params("parallel", "arbitrary"),
        name="b_gate",
    )(y3, kmh, kml)
    q = (y3[:, :, :nq] * (HEAD_DIM ** -0.5 * LOG2E)).astype(BF16).reshape(Bn, S, B_HEADS, HEAD_DIM)
    pad = LANES - HEAD_DIM - nblk
    q_aug = jnp.concatenate([q, bias.astype(BF16).reshape(Bn, S, B_HEADS, nblk),
                             jnp.zeros((Bn, S, B_HEADS, pad), BF16)], axis=-1)
    q_aug = q_aug.reshape(Bn, nblk, B_BLOCK, B_KV_HEADS, B_REP, LANES).transpose(0, 1, 3, 4, 2, 5)
    q_aug = q_aug.reshape(Bn, nblk, B_KV_HEADS, B_REP * B_BLOCK, LANES)
    k = y3[:, :, nq:nq + nk].astype(BF16).reshape(Bn, S, B_KV_HEADS, HEAD_DIM)
    onehot = jax.nn.one_hot(jnp.arange(S) // B_BLOCK, nblk, dtype=BF16)
    onehot = jnp.broadcast_to(onehot[None, :, None, :], (Bn, S, B_KV_HEADS, nblk))
    k_aug = jnp.concatenate([k, onehot, jnp.zeros((Bn, S, B_KV_HEADS, pad), BF16)], axis=-1)
    k_aug = k_aug.reshape(Bn, S, B_KV_HEADS * LANES)
    vt = y3[:, :, nq + nk:].astype(BF16).reshape(Bn, S, B_KV_HEADS, HEAD_DIM).transpose(0, 2, 3, 1)
    vt = jnp.concatenate([vt, jnp.ones((Bn, B_KV_HEADS, 1, S), BF16),
                          jnp.zeros((Bn, B_KV_HEADS, B_VT_ROWS - HEAD_DIM - 1, S), BF16)], axis=2)
    pairs = [(i, n) for i in range(nblk) for n in range(i + 1)]
    qi = jnp.asarray([p[0] for p in pairs], jnp.int32)
    kn = jnp.asarray([p[1] for p in pairs], jnp.int32)
    o = pl.pallas_call(
        _b_attn_kernel,
        grid_spec=pltpu.PrefetchScalarGridSpec(
            num_scalar_prefetch=2,
            grid=(Bn, len(pairs)),
            in_specs=[
                pl.BlockSpec((1, 1, B_KV_HEADS, B_REP * B_BLOCK, LANES),
                             lambda b, t, qi, kn: (b, qi[t], 0, 0, 0)),
                pl.BlockSpec((1, B_BLOCK, B_KV_HEADS * LANES), lambda b, t, qi, kn: (b, kn[t], 0)),
                pl.BlockSpec((1, B_KV_HEADS, B_VT_ROWS, B_BLOCK), lambda b, t, qi, kn: (b, 0, 0, kn[t])),
            ],
            out_specs=pl.BlockSpec((1, B_BLOCK, nq), lambda b, t, qi, kn: (b, qi[t], 0)),
            scratch_shapes=([pltpu.VMEM((1, B_REP * B_BLOCK), F32)] * B_KV_HEADS
                            + [pltpu.VMEM((B_VT_ROWS, B_REP * B_BLOCK), F32)] * B_KV_HEADS),
        ),
        out_shape=jax.ShapeDtypeStruct((Bn, S, nq), F32),
        compiler_params=_params("parallel", "arbitrary"),
        name="b_attn",
    )(qi, kn, q_aug, k_aug, vt)
    return o.reshape(Bn * S, nq)


def _b_sample_kernel(n_pages, T, pt_ref, q_ref, kn_ref, vn_ref, *refs):
    del pt_ref
    page_refs = refs[:n_pages]
    o_ref = refs[n_pages]
    nk = B_KV_HEADS * HEAD_DIM
    q = q_ref[0]
    qh, ql = _split2(q)
    rows = q.shape[0]
    page_rows = page_refs[0].shape[2]
    pages_per_block = B_BLOCK // page_rows
    n_blocks = n_pages // pages_per_block
    scale = HEAD_DIM ** -0.5
    lane = lax.broadcasted_iota(jnp.int32, (1, LANES), 1)
    lanef = lax.broadcasted_iota(jnp.int32, (rows, LANES), 1).astype(F32)
    scores = []
    block_sum = [None] * n_blocks
    for p in range(n_pages):
        kt = page_refs[p][0, :nk, :]
        scores.append(_dot(qh, kt.astype(BF16)) * scale)
        b = p // pages_per_block
        block_sum[b] = kt if block_sum[b] is None else block_sum[b] + kt
    km = jnp.zeros((nk, LANES), F32)
    for b in range(n_blocks):
        ksum = jnp.sum(block_sum[b], axis=1, keepdims=True) * (1.0 / B_BLOCK)
        km = km + ksum * (lane == b).astype(F32)
    kmh, kml = _split2(km)
    gate = _dot(qh, kmh) + (_dot(qh, kml) + _dot(ql, kmh))
    gate = jnp.where(lanef < float(n_blocks), gate, -jnp.inf)
    sel = jnp.zeros((rows, LANES), jnp.bool_)
    for _ in range(min(B_TOPK, n_blocks)):
        mx = jnp.max(gate, axis=1, keepdims=True)
        first = jnp.min(jnp.where(gate == mx, lanef, float(LANES)), axis=1, keepdims=True)
        hit = lanef == first
        sel = sel | (hit & (mx > -jnp.inf))
        gate = jnp.where(hit, -jnp.inf, gate)
    self = sel.astype(F32)
    qidx = lax.broadcasted_iota(jnp.int32, (rows, 1), 0) % T
    kn, vn = kn_ref[0], vn_ref[0]
    own = []
    for j in range(T):
        sj = jnp.sum(q * kn[j:j + 1, :], axis=1, keepdims=True) * scale
        own.append(jnp.where(qidx >= j, sj, NEG))
    m = own[0]
    for j in range(1, T):
        m = jnp.maximum(m, own[j])
    for p in range(n_pages):
        b = p // pages_per_block
        scores[p] = jnp.where(self[:, b:b + 1] > 0.0, scores[p], NEG)
        m = jnp.maximum(m, jnp.max(scores[p], axis=-1, keepdims=True))
    den = jnp.zeros((rows, 1), F32)
    acc = jnp.zeros((rows, nk), F32)
    for j in range(T):
        pj = jnp.exp(own[j] - m)
        den = den + pj
        acc = acc + pj * vn[j:j + 1, :]
    for p in range(n_pages):
        pp = jnp.exp(scores[p] - m)
        den = den + jnp.sum(pp, axis=-1, keepdims=True)
        acc = acc + _dot_nt(pp.astype(BF16), page_refs[p][0, nk:, :].astype(BF16))
    acc = acc / den
    kvh = lax.broadcasted_iota(jnp.int32, (rows, 1), 0) // (T * B_REP)
    out = jnp.zeros((rows, HEAD_DIM), F32)
    for h in range(B_KV_HEADS):
        out = out + jnp.where(kvh == h, acc[:, h * HEAD_DIM:(h + 1) * HEAD_DIM], 0.0)
    o_ref[0] = out


def _b_sample(y, pool, page_table, Bd, T):
    nq = B_HEADS * HEAD_DIM
    nk = B_KV_HEADS * HEAD_DIM
    n_pages = page_table.shape[1]
    page_rows = pool.shape[1]
    assert B_BLOCK % page_rows == 0 and (n_pages * page_rows) % B_BLOCK == 0 and T <= B_BLOCK
    assert n_pages * page_rows // B_BLOCK <= LANES
    y3 = y.reshape(Bd, T, nq + 2 * nk)
    q = y3[:, :, :nq].reshape(Bd, T, B_HEADS, HEAD_DIM).transpose(0, 2, 1, 3)
    kvsel = jnp.asarray(np.kron(np.eye(B_KV_HEADS), np.ones((B_REP, 1))), F32)
    qbd = (q[:, :, :, None, :] * kvsel[None, :, None, :, None]).reshape(Bd, B_HEADS * T, nk)
    kn = y3[:, :, nq:nq + nk]
    vn = y3[:, :, nq + nk:]
    pool_t = pool.transpose(0, 2, 3, 4, 1).reshape(pool.shape[0], 2 * nk, page_rows)
    page_specs = [pl.BlockSpec((1, 2 * nk, page_rows), lambda b, pt, p=p: (pt[b, p], 0, 0))
                  for p in range(n_pages)]
    o = pl.pallas_call(
        functools.partial(_b_sample_kernel, n_pages, T),
        grid_spec=pltpu.PrefetchScalarGridSpec(
            num_scalar_prefetch=1,
            grid=(Bd,),
            in_specs=[
                pl.BlockSpec((1, B_HEADS * T, nk), lambda b, pt: (b, 0, 0)),
                pl.BlockSpec((1, T, nk), lambda b, pt: (b, 0, 0)),
                pl.BlockSpec((1, T, nk), lambda b, pt: (b, 0, 0)),
            ] + page_specs,
            out_specs=pl.BlockSpec((1, B_HEADS * T, HEAD_DIM), lambda b, pt: (b, 0, 0)),
        ),
        out_shape=jax.ShapeDtypeStruct((Bd, B_HEADS * T, HEAD_DIM), F32),
        compiler_params=_params("parallel"),
        name="b_sample",
    )(page_table, qbd, kn, vn, *([pool_t] * n_pages))
    return o.reshape(Bd, B_HEADS, T, HEAD_DIM).transpose(0, 2, 1, 3).reshape(Bd * T, nq)


def _mixer_b(xp, xs, gain, w_in, qk_gain, w_out, pool, page_table, tabs_p, tabs_s, Bn, S, Bd, T):
    tn = 512
    nq = B_HEADS * HEAD_DIM
    nk = B_KV_HEADS * HEAD_DIM
    flags = (1, 1, 2)
    ones = jnp.ones((nk,), F32)
    qg = jnp.tile(qk_gain[0], tn // HEAD_DIM)
    kg = jnp.concatenate([jnp.tile(qk_gain[1], B_KV_HEADS), ones])
    gain_rows = jnp.stack([qg, qg, kg]).reshape(3, 1, tn)
    mask_rows = jnp.stack([jnp.ones((tn,), F32), jnp.ones((tn,), F32),
                           jnp.concatenate([ones, 0.0 * ones])]).reshape(3, 1, tn)
    w = w_in.astype(BF16)
    yp, kvt = _proj_qk(xp, gain, w, flags, gain_rows, mask_rows, tabs_p, 1024, tn, t_chunks=[((2,), S)], seq=S)
    ys = _proj_qk(xs, gain, w, flags, gain_rows, mask_rows, tabs_s, xs.shape[0], tn)
    wo = w_out.astype(BF16)
    op = _b_prompt(yp, Bn, S)
    mix_p = (_mix_plain, [(op, nq, 0)], [wo])
    osm = _b_sample(ys, pool, page_table, Bd, T)
    mix_s = (_mix_plain, [(osm, nq, 0)], [wo])
    kv_p = kvt.reshape(Bn, 2, B_KV_HEADS, HEAD_DIM, S).transpose(0, 4, 1, 2, 3)
    kv_s = ys[:, nq:].reshape(Bd, T, 2, B_KV_HEADS, HEAD_DIM)
    return mix_p, mix_s, kv_p, kv_s


def _gla_kernel(nv, q_ref, k_ref, v_ref, glr_ref, wg_ref, bg_ref, tri_ref, blk_ref, s0_ref,
                o_ref, sT_ref, s_ref, qp_ref, kp_ref, vp_ref, gp_ref):
    t = pl.program_id(1)
    Tt = SEQ_TILE

    @pl.when(t == 0)
    def _():
        s_ref[...] = s0_ref[0]

    if nv < Tt:
        qp_ref[...] = jnp.zeros_like(qp_ref)
        kp_ref[...] = jnp.zeros_like(kp_ref)
        vp_ref[...] = jnp.zeros_like(vp_ref)
        gp_ref[...] = jnp.zeros_like(gp_ref)
    qp_ref[0:nv, :] = q_ref[0]
    kp_ref[0:nv, :] = k_ref[0]
    vp_ref[0:nv, :] = v_ref[0]
    gp_ref[0:nv, :] = glr_ref[0]
    q, k, v = qp_ref[...], kp_ref[...], vp_ref[...]

    x = _dot(gp_ref[...].astype(BF16), wg_ref[...]) + bg_ref[...]
    la = (jnp.minimum(x, 0.0) - jnp.log(1.0 + jnp.exp(-jnp.abs(x)))) * (1.0 / C_TAU)
    row = lax.broadcasted_iota(jnp.int32, (Tt, C_KEY), 0)
    la = jnp.where(row < nv, la, 0.0)
    b = _sel_left(tri_ref[...], la, terms=3)
    bend = _sel_left(blk_ref[...], la, terms=3)
    qe = q * (C_DK ** -0.5) * jnp.exp(b)
    ke = k * jnp.exp(-b)
    kd = k * jnp.exp(bend - b)
    kdt = kd.T
    bendt = bend.T
    qeb, keb, vb = qe.astype(BF16), ke.astype(BF16), v.astype(BF16)
    causal = tri_ref[...] > 0
    lane_t = lax.broadcasted_iota(jnp.int32, (C_DK, Tt), 1)
    n_chunks = -(-nv // C_CHUNK)
    ksl = [slice(h * C_DK, (h + 1) * C_DK) for h in range(C_HEADS)]
    vsl = [slice(h * C_DV, (h + 1) * C_DV) for h in range(C_HEADS)]
    att = [jnp.where(causal, _dot_nt(qeb[:, ks], keb[:, ks]), 0.0).astype(BF16) for ks in ksl]
    chunk_kv, chunk_dec = [], []
    for c in range(n_chunks):
        in_chunk = (lane_t >= c * C_CHUNK) & (lane_t < (c + 1) * C_CHUNK)
        last = lane_t == (c + 1) * C_CHUNK - 1
        chunk_kv.append([_dot(jnp.where(in_chunk, kdt[ks, :], 0.0).astype(BF16), vb[:, vs])
                         for ks, vs in zip(ksl, vsl)])
        chunk_dec.append([jnp.exp(jnp.sum(jnp.where(last, bendt[ks, :], 0.0), axis=1, keepdims=True))
                          for ks in ksl])
    o_intra = [_dot(att[h], vb[:, vsl[h]]) for h in range(C_HEADS)]
    for h in range(C_HEADS):
        st = s_ref[h]
        parts = []
        for c in range(n_chunks):
            rows = slice(c * C_CHUNK, (c + 1) * C_CHUNK)
            parts.append(o_intra[h][rows] + _dot(qeb[rows, ksl[h]], st.astype(BF16)))
            st = chunk_dec[c][h] * st + chunk_kv[c][h]
        s_ref[h] = st
        oh = parts[0] if n_chunks == 1 else jnp.concatenate(parts, axis=0)
        o_ref[0, :, vsl[h]] = oh[0:nv]

    @pl.when(t == pl.num_programs(1) - 1)
    def _():
        sT_ref[0] = s_ref[...]


def _gla(y, w_gate2, b_gate, s0, nb, nt, nv):
    Tt = SEQ_TILE
    y3 = y.reshape(nb * nt, nv, C_NPAD)
    wg = jnp.zeros((LANES, C_KEY), F32).at[:C_RANK].set(w_gate2).astype(BF16)
    idx = np.arange(Tt)
    same = (idx[:, None] // C_CHUNK) == (idx[None, :] // C_CHUNK)
    tri = jnp.asarray(same & (idx[None, :] <= idx[:, None]), BF16)
    blk = jnp.asarray(same, BF16)

    def yspec(width, cb):
        return pl.BlockSpec((1, nv, width), lambda b, t: (b * nt + t, 0, cb))

    def cspec(a):
        return pl.BlockSpec(a.shape, lambda b, t, nd=a.ndim: (0,) * nd)

    bg = b_gate.reshape(1, C_KEY)
    o, sT = pl.pallas_call(
        functools.partial(_gla_kernel, nv),
        grid=(nb, nt),
        in_specs=[
            yspec(C_KEY, 0), yspec(C_KEY, 1), yspec(C_VAL, 1), yspec(LANES, (2 * C_KEY + 2 * C_VAL) // LANES),
            cspec(wg), cspec(bg), cspec(tri), cspec(blk),
            pl.BlockSpec((1, C_HEADS, C_DK, C_DV), lambda b, t: (b, 0, 0, 0)),
        ],
        out_specs=[
            pl.BlockSpec((1, nv, C_VAL), lambda b, t: (b * nt + t, 0, 0)),
            pl.BlockSpec((1, C_HEADS, C_DK, C_DV), lambda b, t: (b, 0, 0, 0)),
        ],
        out_shape=[
            jax.ShapeDtypeStruct((nb * nt, nv, C_VAL), F32),
            jax.ShapeDtypeStruct((nb, C_HEADS, C_DK, C_DV), F32),
        ],
        scratch_shapes=[
            pltpu.VMEM((C_HEADS, C_DK, C_DV), F32),
            pltpu.VMEM((Tt, C_KEY), F32), pltpu.VMEM((Tt, C_KEY), F32),
            pltpu.VMEM((Tt, C_VAL), F32), pltpu.VMEM((Tt, LANES), F32),
        ],
        compiler_params=_params("parallel", "arbitrary"),
        name="gla",
    )(y3, y3, y3, y3, wg, bg, tri, blk, s0)
    return o.reshape(nb * nt * nv, C_VAL), sT


def _mixer_c(xp, xs, gain, w_in, w_gate2, b_gate, norm_g, w_out, state, Bn, S, Bd, T):
    n_in = w_in.shape[1]
    w = jnp.zeros((D_MODEL, C_NPAD), BF16).at[:, :n_in].set(w_in.astype(BF16))
    yp = _proj(xp, gain, w, 1024)
    ys = _proj(xs, gain, w, xs.shape[0])
    zero = jnp.zeros((Bn, C_HEADS, C_DK, C_DV), F32)
    op, sp = _gla(yp, w_gate2, b_gate, zero, Bn, S // SEQ_TILE, SEQ_TILE)
    osm, ss = _gla(ys, w_gate2, b_gate, state, Bd, 1, T)
    wo = w_out.astype(BF16)
    ng = norm_g.reshape(1, C_DV)
    rcb = (2 * C_KEY + C_VAL) // C_VAL
    mix_p = (_mix_c, [(op, C_VAL, 0), (yp, C_VAL, rcb)], [ng, wo])
    mix_s = (_mix_c, [(osm, C_VAL, 0), (ys, C_VAL, rcb)], [ng, wo])
    return mix_p, mix_s, sp, ss


def _ssd_kernel(nv, xa_ref, xb_ref, bc_ref, dt_ref, cs_ref, cw_ref, cb_ref, dtb_ref, alog_ref, dsk_ref,
                tri_ref, exp_ref, expt_ref, h0_ref, y_ref, hT_ref, h_ref, xp_ref, dtp_ref):
    t = pl.program_id(1)
    Tt = SEQ_TILE
    pre = 8

    @pl.when(t == 0)
    def _():
        h_ref[...] = h0_ref[0]
        xp_ref[0:pre, :] = cs_ref[0]

    if nv < Tt:
        xp_ref[pre:, :] = jnp.zeros((Tt, D_XBC), F32)
        dtp_ref[...] = jnp.zeros_like(dtp_ref)
    xp_ref[pre:pre + nv, 0:1024] = xa_ref[0]
    xp_ref[pre:pre + nv, 1024:2048] = xb_ref[0]
    xp_ref[pre:pre + nv, 2048:3072] = bc_ref[0]
    dtp_ref[0:nv, :] = dt_ref[0]

    full = xp_ref[...]
    conv = cb_ref[...] + full[pre:] * cw_ref[D_CONV - 1:D_CONV, :]
    for k in range(1, D_CONV):
        conv = conv + pltpu.roll(full, k, 0)[pre:] * cw_ref[D_CONV - 1 - k:D_CONV - k, :]
    xp_ref[0:pre, :] = xp_ref[Tt:Tt + pre, :]
    xbc = _silu(conv)
    x = xbc[:, :D_INNER]
    nbc = D_GROUPS * D_STATE
    bm = xbc[:, D_INNER:D_INNER + nbc].astype(BF16)
    cm = xbc[:, D_INNER + nbc:].astype(BF16)

    row = lax.broadcasted_iota(jnp.int32, (Tt, LANES), 0)
    lane = lax.broadcasted_iota(jnp.int32, (Tt, LANES), 1)
    live = (row < nv) & (lane < D_HEADS)
    dt = jnp.where(live, _softplus(dtp_ref[...] + dtb_ref[...]), 0.0)
    cum = _sel_left(tri_ref[...], dt * (-jnp.exp(alog_ref[...])), terms=3)
    cumt = cum.T
    cend = cum[Tt - 1:Tt, :]
    ex = exp_ref[...]
    stack = jnp.concatenate([jnp.exp(cend - cum) * dt, dt, jnp.exp(cum)], axis=0)
    wide = _sel_right(stack, ex)
    x_state = (x * wide[0:Tt]).astype(BF16)
    x_dt = (x * wide[Tt:2 * Tt]).astype(BF16)
    off_scale = wide[2 * Tt:]

    causal = tri_ref[...] > 0
    gw = D_INNER // D_GROUPS
    cbs, offs = [], []
    for g in range(D_GROUPS):
        cg = cm[:, g * D_STATE:(g + 1) * D_STATE]
        cbs.append(_dot_nt(cg, bm[:, g * D_STATE:(g + 1) * D_STATE]))
        offs.append(_dot_nt(cg, h_ref[g * gw:(g + 1) * gw, :].astype(BF16)))
    mats = []
    for j in range(D_HEADS):
        seg = cum[:, j:j + 1] - cumt[j:j + 1, :]
        mats.append((cbs[j // D_HPG] * jnp.exp(jnp.where(causal, seg, -jnp.inf))).astype(BF16))
    diag = [_dot(mats[j], x_dt[:, j * D_HEADDIM:(j + 1) * D_HEADDIM]) for j in range(D_HEADS)]
    y = jnp.concatenate(diag, axis=1) + jnp.concatenate(offs, axis=1) * off_scale + dsk_ref[...] * x
    y_ref[0] = y[0:nv]

    dcol = jnp.broadcast_to(jnp.exp(cumt[:, Tt - 1:Tt]), (LANES, D_STATE))
    dfull = _sel_left(expt_ref[...], dcol)
    xst = x_state.astype(F32).T.astype(BF16)
    for g in range(D_GROUPS):
        rows = slice(g * gw, (g + 1) * gw)
        h_ref[rows, :] = dfull[rows] * h_ref[rows, :] + _dot(xst[rows], bm[:, g * D_STATE:(g + 1) * D_STATE])

    @pl.when(t == pl.num_programs(1) - 1)
    def _():
        hT_ref[0] = h_ref[...]


def _ssd(y, conv_w, conv_b, dt_bias, a_log, d_skip, h0, c0, nb, nt, nv):
    Tt = SEQ_TILE
    y3 = y.reshape(nb * nt, nv, D_NPAD)
    idx = np.arange(Tt)
    tri = jnp.asarray(idx[None, :] <= idx[:, None], BF16)
    ex_np = np.zeros((LANES, D_INNER), np.float32)
    ex_np[np.arange(D_INNER) // D_HEADDIM, np.arange(D_INNER)] = 1.0
    ex = jnp.asarray(ex_np, BF16)
    ext = jnp.asarray(ex_np.T, BF16)

    def pad_row(v):
        return jnp.zeros((1, LANES), F32).at[0, :D_HEADS].set(v)

    cs = jnp.zeros((nb, 8, D_XBC), F32).at[:, 8 - (D_CONV - 1):].set(c0)
    dsk = jnp.repeat(d_skip, D_HEADDIM).reshape(1, D_INNER)

    def yspec(width, cb):
        return pl.BlockSpec((1, nv, width), lambda b, t: (b * nt + t, 0, cb))

    def cspec(a):
        return pl.BlockSpec(a.shape, lambda b, t, nd=a.ndim: (0,) * nd)

    consts = [conv_w, conv_b.reshape(1, D_XBC), pad_row(dt_bias), pad_row(a_log), dsk, tri, ex, ext]
    yo, hT = pl.pallas_call(
        functools.partial(_ssd_kernel, nv),
        grid=(nb, nt),
        in_specs=[yspec(1024, 2), yspec(1024, 3), yspec(1024, 4), yspec(LANES, D_DT_COL),
                  pl.BlockSpec((1, 8, D_XBC), lambda b, t: (b, 0, 0))]
        + [cspec(a) for a in consts]
        + [pl.BlockSpec((1, D_INNER, D_STATE), lambda b, t: (b, 0, 0))],
        out_specs=[
            pl.BlockSpec((1, nv, D_INNER), lambda b, t: (b * nt + t, 0, 0)),
            pl.BlockSpec((1, D_INNER, D_STATE), lambda b, t: (b, 0, 0)),
        ],
        out_shape=[
            jax.ShapeDtypeStruct((nb * nt, nv, D_INNER), F32),
            jax.ShapeDtypeStruct((nb, D_INNER, D_STATE), F32),
        ],
        scratch_shapes=[
            pltpu.VMEM((D_INNER, D_STATE), F32),
            pltpu.VMEM((Tt + 8, D_XBC), F32),
            pltpu.VMEM((Tt, LANES), F32),
        ],
        compiler_params=_params("parallel", "arbitrary"),
        name="ssd",
    )(y3, y3, y3, y3, cs, *consts, h0)
    return yo.reshape(nb * nt * nv, D_INNER), hT


def _ssd_step_kernel(nv, xa_ref, xb_ref, bc_ref, dt_ref, cs_ref, cw_ref, cb_ref, dtb_ref, alog_ref, dsk_ref,
                     eye_ref, h0_ref, y_ref, hT_ref, xp_ref):
    R = 8
    pre = 8
    xp_ref[0:pre, :] = cs_ref[0]
    xp_ref[pre:, :] = jnp.zeros((R, D_XBC), F32)
    xp_ref[pre:pre + nv, 0:1024] = xa_ref[0]
    xp_ref[pre:pre + nv, 1024:2048] = xb_ref[0]
    xp_ref[pre:pre + nv, 2048:3072] = bc_ref[0]
    conv = cb_ref[...]
    for w in range(D_CONV):
        conv = conv + xp_ref[pl.ds(pre - (D_CONV - 1) + w, R), :] * cw_ref[w:w + 1, :]
    xbc = _silu(conv)
    x = xbc[:, :D_INNER]
    nbc = D_GROUPS * D_STATE
    bmf = xbc[:, D_INNER:D_INNER + nbc]
    cmf = xbc[:, D_INNER + nbc:]

    row = lax.broadcasted_iota(jnp.int32, (R, LANES), 0)
    lane = lax.broadcasted_iota(jnp.int32, (R, LANES), 1)
    dtp = jnp.concatenate([dt_ref[0], jnp.zeros((R - nv, LANES), F32)], axis=0)
    dt = jnp.where((row < nv) & (lane < D_HEADS), _softplus(dtp + dtb_ref[...]), 0.0)
    cum = dt * (-jnp.exp(alog_ref[...]))
    for k in (1, 2, 4):
        cum = cum + jnp.where(row >= k, pltpu.roll(cum, k, 0), 0.0)
    cend = cum[R - 1:R, :]
    group_of_lane = lane // D_HPG
    low_half = lane < D_HEADDIM

    def widen(a):
        pairs = [jnp.where(low_half, a[:, j:j + 1], a[:, j + 1:j + 2]) for j in range(0, D_HEADS, 2)]
        return jnp.concatenate(pairs, axis=1)

    yd = jnp.zeros((R, D_INNER), F32)
    for s in range(nv):
        w_s = jnp.where(row >= s, jnp.exp(cum - cum[s:s + 1, :]), 0.0) * dt[s:s + 1, :]
        prod = cmf * bmf[s:s + 1, :]
        cb_s = jnp.zeros((R, LANES), F32)
        for g in range(D_GROUPS):
            tot = jnp.sum(prod[:, g * D_STATE:(g + 1) * D_STATE], axis=1, keepdims=True)
            cb_s = jnp.where(group_of_lane == g, tot, cb_s)
        yd = yd + widen(cb_s * w_s) * x[s:s + 1, :]
    off_scale = widen(jnp.exp(cum))
    x_state = (x * widen(jnp.exp(cend - cum) * dt)).astype(BF16)

    gw = D_INNER // D_GROUPS
    cm = cmf.astype(BF16)
    parts = []
    for g in range(D_GROUPS):
        hg = h0_ref[0, g * gw:(g + 1) * gw, :]
        parts.append(_dot_nt(cm[:, g * D_STATE:(g + 1) * D_STATE], hg.astype(BF16)))
    y = yd + jnp.concatenate(parts, axis=1) * off_scale + dsk_ref[...] * x
    y_ref[0] = y[0:nv]

    erow = jnp.broadcast_to(jnp.exp(cend), (LANES, LANES))
    eye128 = (lax.broadcasted_iota(jnp.int32, (LANES, LANES), 0)
              == lax.broadcasted_iota(jnp.int32, (LANES, LANES), 1)).astype(BF16)
    dcol = _nt_sel(eye128, erow)[0:D_HEADS].reshape(D_HEADS, 1, D_STATE)
    zpad = jnp.zeros((LANES - R, D_INNER), BF16)
    xs_pad = jnp.concatenate([x_state, zpad], axis=0)
    bm_pad = jnp.concatenate([bmf.astype(BF16), jnp.zeros((LANES - R, nbc), BF16)], axis=0)
    for g in range(D_GROUPS):
        rows = slice(g * gw, (g + 1) * gw)
        xt = _dot_nt(eye_ref[...], xs_pad[:, rows]).astype(BF16)
        kept = (h0_ref[0, rows, :].reshape(D_HPG, D_HEADDIM, D_STATE) * dcol[g * D_HPG:(g + 1) * D_HPG])
        hT_ref[0, rows, :] = kept.reshape(gw, D_STATE) + _dot(xt, bm_pad[:, g * D_STATE:(g + 1) * D_STATE])


def _ssd_step(y, conv_w, conv_b, dt_bias, a_log, d_skip, h0, c0, nb, nv):
    assert nv <= 8
    y3 = y.reshape(nb, nv, D_NPAD)
    gw = D_INNER // D_GROUPS
    eye = jnp.asarray(np.eye(gw), BF16)

    def pad_row(v):
        return jnp.zeros((1, LANES), F32).at[0, :D_HEADS].set(v)

    cs = jnp.zeros((nb, 8, D_XBC), F32).at[:, 8 - (D_CONV - 1):].set(c0)
    dsk = jnp.repeat(d_skip, D_HEADDIM).reshape(1, D_INNER)

    def yspec(width, cb):
        return pl.BlockSpec((1, nv, width), lambda b: (b, 0, cb))

    def cspec(a):
        return pl.BlockSpec(a.shape, lambda b, nd=a.ndim: (0,) * nd)

    consts = [conv_w, conv_b.reshape(1, D_XBC), pad_row(dt_bias), pad_row(a_log), dsk, eye]
    yo, hT = pl.pallas_call(
        functools.partial(_ssd_step_kernel, nv),
        grid=(nb,),
        in_specs=[yspec(1024, 2), yspec(1024, 3), yspec(1024, 4), yspec(LANES, D_DT_COL),
                  pl.BlockSpec((1, 8, D_XBC), lambda b: (b, 0, 0))]
        + [cspec(a) for a in consts]
        + [pl.BlockSpec((1, D_INNER, D_STATE), lambda b: (b, 0, 0))],
        out_specs=[
            pl.BlockSpec((1, nv, D_INNER), lambda b: (b, 0, 0)),
            pl.BlockSpec((1, D_INNER, D_STATE), lambda b: (b, 0, 0)),
        ],
        out_shape=[
            jax.ShapeDtypeStruct((nb, nv, D_INNER), F32),
            jax.ShapeDtypeStruct((nb, D_INNER, D_STATE), F32),
        ],
        scratch_shapes=[pltpu.VMEM((16, D_XBC), F32)],
        compiler_params=_params("parallel"),
        name="ssd_step",
    )(y3, y3, y3, y3, cs, *consts, h0)
    return yo.reshape(nb * nv, D_INNER), hT


def _mixer_d(xp, xs, gain, w_in, conv_w, conv_b, dt_bias, a_log, d_skip, norm_g, w_out,
             ssm_state, conv_state, Bn, S, Bd, T):
    n_in = w_in.shape[1]
    w = jnp.zeros((D_MODEL, D_NPAD), BF16).at[:, :n_in].set(w_in.astype(BF16))
    yp = _proj(xp, gain, w, 1024)
    ys = _proj(xs, gain, w, xs.shape[0])
    h0p = jnp.zeros((Bn, D_INNER, D_STATE), F32)
    c0p = jnp.zeros((Bn, D_CONV - 1, D_XBC), F32)
    op, hp = _ssd(yp, conv_w, conv_b, dt_bias, a_log, d_skip, h0p, c0p, Bn, S // SEQ_TILE, SEQ_TILE)
    h0s = ssm_state.reshape(Bd, D_INNER, D_STATE)
    osm, hs = _ssd_step(ys, conv_w, conv_b, dt_bias, a_log, d_skip, h0s, conv_state, Bd, T)
    wo = w_out.astype(BF16)
    ng = norm_g.reshape(1, D_INNER)
    mix_p = (_mix_d, [(op, D_INNER, 0), (yp, D_INNER, 0)], [ng, wo])
    mix_s = (_mix_d, [(osm, D_INNER, 0), (ys, D_INNER, 0)], [ng, wo])
    keep = D_CONV - 1
    xbc_p = yp.reshape(Bn, S, D_NPAD)[:, :, D_INNER:D_INNER + D_XBC]
    xbc_s = ys.reshape(Bd, T, D_NPAD)[:, :, D_INNER:D_INNER + D_XBC]
    cp = jnp.concatenate([c0p, xbc_p], axis=1)[:, -keep:] if S < keep else xbc_p[:, S - keep:]
    cs = jnp.concatenate([conv_state, xbc_s], axis=1)[:, -keep:]
    return (mix_p, mix_s, hp.reshape(Bn, D_HEADS, D_HEADDIM, D_STATE), hs.reshape(Bd, D_HEADS, D_HEADDIM, D_STATE),
            cp, cs)


def kernel(x_prompt, x_sample, cache_a_w1, cache_a_w2, cache_a_w3, cache_b_kv, page_table, state_c, state_d_ssm, state_d_conv, norm_gain, w_ffn_up, w_ffn_down, w_a_in, a_qk_gain, w_a_out, w_b_in, b_qk_gain, w_b_out, w_c_in, w_c_gate2, b_c_gate, c_norm_gain, w_c_out, w_d_in, d_conv_w, d_conv_b, d_dt_bias, d_a_log, d_skip, d_norm_gain, w_d_out):
    Bn, S, _ = x_prompt.shape
    Bd, T, _ = x_sample.shape
    depth = norm_gain.shape[0]
    past_len = page_table.shape[1] * cache_b_kv.shape[2]
    tabs_p = _rope_tables(jnp.arange(S, dtype=jnp.int32))
    tabs_s = _rope_tables(jnp.tile(past_len + jnp.arange(T, dtype=jnp.int32), Bd))
    xp = x_prompt.reshape(Bn * S, D_MODEL)
    xs = x_sample.reshape(Bd * T, D_MODEL)
    ts = xs.shape[0]
    w_up = w_ffn_up.astype(BF16)
    w_down = w_ffn_down.astype(BF16)
    outs = {k: [] for k in ("a0p", "a0s", "a1p", "a1s", "a2p", "a2s", "bp", "bs", "cp", "cs",
                            "hp", "hs", "dp", "ds")}
    for i in range(depth):
        m, j = i % 4, i // 4
        g = norm_gain[i]
        xp = _ffn(xp, g[0], w_up, w_down, i, 0, 1024)
        xs = _ffn(xs, g[0], w_up, w_down, i, 0, ts)
        if m == 0:
            mix_p, mix_s, new_p, new_s = _mixer_a(xp, xs, g[1], w_a_in[j], a_qk_gain[j], w_a_out[j],
                                            (cache_a_w1[j], cache_a_w2[j], cache_a_w3[j]),
                                            tabs_p, tabs_s, Bn, S, Bd, T)
            for gi in range(A_GROUPS):
                outs["a%dp" % gi].append(new_p[gi])
                outs["a%ds" % gi].append(new_s[gi])
        elif m == 1:
            mix_p, mix_s, kvp, kvs = _mixer_b(xp, xs, g[1], w_b_in[j], b_qk_gain[j], w_b_out[j],
                                        cache_b_kv[j], page_table, tabs_p, tabs_s, Bn, S, Bd, T)
            outs["bp"].append(kvp)
            outs["bs"].append(kvs)
        elif m == 2:
            mix_p, mix_s, sp, ss = _mixer_c(xp, xs, g[1], w_c_in[j], w_c_gate2[j], b_c_gate[j], c_norm_gain[j],
                                      w_c_out[j], state_c[j], Bn, S, Bd, T)
            outs["cp"].append(sp)
            outs["cs"].append(ss)
        else:
            mix_p, mix_s, hp, hs, cp, cs = _mixer_d(xp, xs, g[1], w_d_in[j], d_conv_w[j], d_conv_b[j], d_dt_bias[j],
                                              d_a_log[j], d_skip[j], d_norm_gain[j], w_d_out[j],
                                              state_d_ssm[j], state_d_conv[j], Bn, S, Bd, T)
            outs["hp"].append(hp)
            outs["hs"].append(hs)
            outs["dp"].append(cp)
            outs["ds"].append(cs)
        xp = _ffn(xp, g[2], w_up, w_down, i, 1, 1024, mix_p)
        xs = _ffn(xs, g[2], w_up, w_down, i, 1, ts, mix_s)
    st = {k: jnp.stack(v) for k, v in outs.items()}
    return (xp.reshape(Bn, S, D_MODEL), xs.reshape(Bd, T, D_MODEL),
            st["a0p"], st["a0s"], st["a1p"], st["a1s"], st["a2p"], st["a2s"],
            st["bp"], st["bs"], st["cp"], st["cs"], st["hp"], st["hs"], st["dp"], st["ds"])
```

```python
import functools
import math

import numpy as np
import jax
import jax.numpy as jnp
from jax import lax
from jax.experimental import pallas as pl
from jax.experimental.pallas import tpu as pltpu

F32 = jnp.float32
BF16 = jnp.bfloat16

D_MODEL = 1024
HEAD_DIM = 64
ROPE_DIM = HEAD_DIM // 4
ROPE_THETA = 500000.0
EPS = 1e-6
D_FF = 2816
NEG = -1e30

A_GROUPS = 3
A_DILATIONS = (1, 4, 16)
A_HEADS = 8
A_WIDTH = A_HEADS * HEAD_DIM
A_SPAN = 128

B_HEADS = 16
B_KV_HEADS = 4
B_REP = B_HEADS // B_KV_HEADS
B_BLOCK = 256
B_TOPK = 3
B_VT_ROWS = HEAD_DIM + 8
LOG2E = 1.4426950408889634

C_HEADS = 4
C_KEY = 512
C_VAL = 1024
C_DK = 128
C_DV = 256
C_RANK = 16
C_TAU = 16.0
C_CHUNK = 32
C_NPAD = 3200

D_INNER = 2048
D_HEADDIM = 64
D_HEADS = 32
D_GROUPS = 4
D_HPG = 8
D_STATE = 128
D_CONV = 4
D_XBC = D_INNER + 2 * D_GROUPS * D_STATE
D_NPAD = 5376
D_DT_COL = (D_INNER + D_XBC) // 128

LANES = 128
FFN_ROWS = 512
PROJ_ROWS = 512
SEQ_TILE = 128


def _params(*sem):
    return pltpu.CompilerParams(dimension_semantics=sem)


def _split2(a):
    hi = a.astype(BF16)
    lo = (a - hi.astype(F32)).astype(BF16)
    return hi, lo


def _split3(a):
    hi = a.astype(BF16)
    r = a - hi.astype(F32)
    mid = r.astype(BF16)
    lo = (r - mid.astype(F32)).astype(BF16)
    return hi, mid, lo


def _dot(a, b):
    return jnp.dot(a, b, preferred_element_type=F32)


def _dot_nt(a, b):
    return lax.dot_general(a, b, (((1,), (1,)), ((), ())), preferred_element_type=F32)


def _sel_right(a, sel01, terms=2):
    parts = _split2(a) if terms == 2 else _split3(a)
    out = None
    for p in reversed(parts):
        d = _dot(p, sel01)
        out = d if out is None else out + d
    return out


def _sel_left(sel01, a, terms=2):
    parts = _split2(a) if terms == 2 else _split3(a)
    out = None
    for p in reversed(parts):
        d = _dot(sel01, p)
        out = d if out is None else out + d
    return out


def _silu(x):
    return x * jax.nn.sigmoid(x)


def _softplus(x):
    return jnp.maximum(x, 0.0) + jnp.log(1.0 + jnp.exp(-jnp.abs(x)))


def _rms_rows(x, g):
    ms = jnp.mean(x * x, axis=-1, keepdims=True)
    return x * lax.rsqrt(ms + EPS) * g


def _ffn_kernel(mix_fn, n_mix, x_ref, *refs):
    mix_refs = refs[:n_mix]
    g_ref, wg_ref, wu_ref, wd_ref, o_ref = refs[n_mix:]
    x = x_ref[...]
    if mix_fn is not None:
        x = x + mix_fn(*mix_refs)
    h = _rms_rows(x, g_ref[...]).astype(BF16)
    a = _dot(h, wg_ref[...])
    u = _dot(h, wu_ref[...])
    act = (_silu(a) * u).astype(BF16)
    o_ref[...] = x + 0.5 * _dot(act, wd_ref[...])


def _ffn(x, gain, w_up, w_down, layer, which, tm, mix=None):
    T = x.shape[0]
    tm = min(tm, FFN_ROWS)
    resident = dict(pipeline_mode=pl.Buffered(1))
    mix_fn, row_inputs, const_inputs = mix if mix is not None else (None, [], [])
    mix_specs, mix_args = [], []
    for arr, width, cb in row_inputs:
        mix_specs.append(pl.BlockSpec((tm, width), lambda i, cb=cb: (i, cb)))
        mix_args.append(arr)
    for arr in const_inputs:
        mix_specs.append(pl.BlockSpec(arr.shape, lambda i, nd=arr.ndim: (0,) * nd, **resident))
        mix_args.append(arr)
    return pl.pallas_call(
        functools.partial(_ffn_kernel, mix_fn, len(mix_args)),
        grid=(T // tm,),
        in_specs=[pl.BlockSpec((tm, D_MODEL), lambda i: (i, 0))] + mix_specs + [
            pl.BlockSpec((1, D_MODEL), lambda i: (0, 0), **resident),
            pl.BlockSpec((None, None, D_MODEL, D_FF), lambda i: (layer, which, 0, 0), **resident),
            pl.BlockSpec((None, None, D_MODEL, D_FF), lambda i: (layer, which, 0, 1), **resident),
            pl.BlockSpec((None, None, D_FF, D_MODEL), lambda i: (layer, which, 0, 0), **resident),
        ],
        out_specs=pl.BlockSpec((tm, D_MODEL), lambda i: (i, 0)),
        out_shape=jax.ShapeDtypeStruct((T, D_MODEL), F32),
        compiler_params=_params("parallel"),
        name="ffn",
    )(x, *mix_args, gain.reshape(1, D_MODEL), w_up, w_up, w_down)


def _proj_kernel(x_ref, g_ref, w_ref, o_ref):
    o_ref[...] = _dot(_rms_rows(x_ref[...], g_ref[...]).astype(BF16), w_ref[...])


def _proj(x, gain, w, tm):
    T = x.shape[0]
    N = w.shape[1]
    tm = min(tm, PROJ_ROWS)
    resident = dict(pipeline_mode=pl.Buffered(1))
    return pl.pallas_call(
        _proj_kernel,
        grid=(T // tm,),
        in_specs=[
            pl.BlockSpec((tm, D_MODEL), lambda i: (i, 0)),
            pl.BlockSpec((1, D_MODEL), lambda i: (0, 0), **resident),
            pl.BlockSpec((D_MODEL, N), lambda i: (0, 0), **resident),
        ],
        out_specs=pl.BlockSpec((tm, N), lambda i: (i, 0)),
        out_shape=jax.ShapeDtypeStruct((T, N), F32),
        compiler_params=_params("parallel"),
        name="proj",
    )(x, gain.reshape(1, D_MODEL), w)


def _proj_qk_kernel(flags, t_chunks, x_ref, g_ref, w_ref, bd_ref, gain_ref, mask_ref, cos_ref, sa_ref, sb_ref,
                    o_ref, *t_refs):
    h = _rms_rows(x_ref[...], g_ref[...]).astype(BF16)
    tn = bd_ref.shape[0]
    cos, sa, sb = cos_ref[...], sa_ref[...], sb_ref[...]
    for j, flag in enumerate(flags):
        y = _dot(h, w_ref[:, j * tn:(j + 1) * tn])
        if not flag:
            o_ref[:, j * tn:(j + 1) * tn] = y
            continue
        ss = _dot((y * y).astype(BF16), bd_ref[...])
        yn = y * lax.rsqrt(ss * (1.0 / HEAD_DIM) + EPS) * gain_ref[j]
        mask = mask_ref[j]
        for c in range(tn // LANES):
            sl = slice(c * LANES, (c + 1) * LANES)
            v = yn[:, sl]
            up = pltpu.roll(v, LANES - ROPE_DIM // 2, 1)
            dn = pltpu.roll(v, ROPE_DIM // 2, 1)
            rot = v * cos + up * sa + dn * sb
            if flag == 2:
                rot = jnp.where(mask[:, sl] > 0.0, rot, y[:, sl])
            o_ref[:, j * tn + c * LANES:j * tn + (c + 1) * LANES] = rot
    for t_ref, chunks in zip(t_refs, t_chunks):
        for k, j in enumerate(chunks):
            t_ref[0, k * tn:(k + 1) * tn, :] = o_ref[:, j * tn:(j + 1) * tn].T


def _proj_qk(x, gain, w, flags, gain_rows, mask_rows, tables, tm, tn, t_chunks=(), seq=None):
    T = x.shape[0]
    N = w.shape[1]
    nj = N // tn
    tm = min(tm, PROJ_ROWS)
    cos, sa, sb = tables
    n_pos_tiles = cos.shape[0] // tm
    bd = jnp.asarray(np.kron(np.eye(tn // HEAD_DIM), np.ones((HEAD_DIM, HEAD_DIM))), BF16)
    resident = dict(pipeline_mode=pl.Buffered(1))
    tab_spec = pl.BlockSpec((tm, LANES), lambda i: (i % n_pos_tiles, 0))
    out_specs = pl.BlockSpec((tm, N), lambda i: (i, 0))
    out_shape = jax.ShapeDtypeStruct((T, N), F32)
    per_seq = seq // tm if t_chunks else 1
    windows = []
    if t_chunks:
        out_specs, out_shape = [out_specs], [out_shape]
        for chunks, window in t_chunks:
            n_tiles = -(-window // tm)
            first = per_seq - n_tiles
            windows.append(tuple(chunks))
            out_specs.append(pl.BlockSpec((1, len(chunks) * tn, tm),
                                          lambda i, first=first: (i // per_seq, 0, jnp.maximum(i % per_seq - first, 0))))
            out_shape.append(jax.ShapeDtypeStruct((T // seq, len(chunks) * tn, n_tiles * tm), F32))
    return pl.pallas_call(
        functools.partial(_proj_qk_kernel, flags, tuple(windows)),
        grid=(T // tm,),
        in_specs=[
            pl.BlockSpec((tm, D_MODEL), lambda i: (i, 0)),
            pl.BlockSpec((1, D_MODEL), lambda i: (0, 0), **resident),
            pl.BlockSpec((D_MODEL, N), lambda i: (0, 0), **resident),
            pl.BlockSpec((tn, tn), lambda i: (0, 0), **resident),
            pl.BlockSpec((nj, 1, tn), lambda i: (0, 0, 0), **resident),
            pl.BlockSpec((nj, 1, tn), lambda i: (0, 0, 0), **resident),
            tab_spec, tab_spec, tab_spec,
        ],
        out_specs=out_specs,
        out_shape=out_shape,
        compiler_params=_params("arbitrary"),
        name="proj_qk",
    )(x, gain.reshape(1, D_MODEL), w, bd, gain_rows, mask_rows, cos, sa, sb)


def _rope_tables(pos):
    half = ROPE_DIM // 2
    inv = ROPE_THETA ** (-jnp.arange(half, dtype=F32) / half)
    ang = pos.astype(F32)[:, None] * inv[None, :]
    cos, sin = jnp.cos(ang), jnp.sin(ang)
    n = pos.shape[0]
    one = jnp.ones((n, HEAD_DIM - ROPE_DIM), F32)
    zero = jnp.zeros((n, HEAD_DIM - ROPE_DIM), F32)
    zh = jnp.zeros((n, half), F32)
    c64 = jnp.concatenate([cos, cos, one], axis=1)
    sa64 = jnp.concatenate([-sin, zh, zero], axis=1)
    sb64 = jnp.concatenate([zh, sin, zero], axis=1)
    rep = LANES // HEAD_DIM
    return tuple(jnp.tile(t, (1, rep)) for t in (c64, sa64, sb64))


def _mix_plain(o_ref, w_ref):
    return _dot(o_ref[...].astype(BF16), w_ref[...])


def _mix_a(o0, o1, o2, l0, l1, l2, w_ref):
    la, lb, lc = l0[...], l1[...], l2[...]
    m = jnp.maximum(jnp.maximum(la, lb), lc)
    ea, eb, ec = jnp.exp(la - m), jnp.exp(lb - m), jnp.exp(lc - m)
    o = (ea * o0[...] + eb * o1[...] + ec * o2[...]) / (ea + eb + ec)
    return _dot(o.astype(BF16), w_ref[...])


def _mix_c(o_ref, r_ref, g_ref, w_ref):
    o = o_ref[...]
    g = g_ref[...]
    parts = [_rms_rows(o[:, h * C_DV:(h + 1) * C_DV], g) for h in range(C_HEADS)]
    on = jnp.concatenate(parts, axis=1) * _silu(r_ref[...])
    return _dot(on.astype(BF16), w_ref[...])


def _mix_d(o_ref, z_ref, g_ref, w_ref):
    gated = o_ref[...] * _silu(z_ref[...])
    gw = D_INNER // D_GROUPS
    g = g_ref[...]
    parts = [_rms_rows(gated[:, k * gw:(k + 1) * gw], g[:, k * gw:(k + 1) * gw]) for k in range(D_GROUPS)]
    return _dot(jnp.concatenate(parts, axis=1).astype(BF16), w_ref[...])


def _a_prompt_kernel(d, q_ref, kp_ref, kc_ref, vp_ref, vc_ref, o_ref, l_ref):
    j = pl.program_id(1)
    heads = q_ref.shape[2] // HEAD_DIM
    per_pass = max(1, A_HEADS // heads)
    n_prob = per_pass * heads
    scale = HEAD_DIM ** -0.5
    qq = lax.broadcasted_iota(jnp.int32, (n_prob * A_SPAN, 2 * A_SPAN), 0) & (A_SPAN - 1)
    kk = lax.broadcasted_iota(jnp.int32, (n_prob * A_SPAN, 2 * A_SPAN), 1)
    rel = qq + A_SPAN - kk
    valid = (rel >= 0) & (rel <= A_SPAN) & ((kk >= A_SPAN) | (j > 0))
    hs = [slice(h * HEAD_DIM, (h + 1) * HEAD_DIM) for h in range(heads)]
    for r0 in range(0, d, per_pass):
        rows = [pl.ds(r0 + e, A_SPAN, stride=d) if d > 1 else pl.ds(0, A_SPAN) for e in range(per_pass)]
        s, vs = [], []
        for rw in rows:
            q = (q_ref[0, rw, :] * scale).astype(BF16)
            k = jnp.concatenate([kp_ref[0, rw, :], kc_ref[0, rw, :]], axis=0).astype(BF16)
            vs.append(jnp.concatenate([vp_ref[0, rw, :], vc_ref[0, rw, :]], axis=0).astype(BF16))
            s += [_dot_nt(q[:, sl], k[:, sl]) for sl in hs]
        s = jnp.where(valid, jnp.concatenate(s, axis=0), -jnp.inf)
        m = jnp.max(s, axis=-1, keepdims=True)
        p = jnp.exp(s - m)
        den = jnp.sum(p, axis=-1, keepdims=True)
        pb = p.astype(BF16)
        lse = m + jnp.log(den)
        inv = 1.0 / den
        for e, rw in enumerate(rows):
            o, l = [], []
            for h, sl in enumerate(hs):
                blk = slice((e * heads + h) * A_SPAN, (e * heads + h + 1) * A_SPAN)
                o.append(_dot(pb[blk], vs[e][:, sl]) * inv[blk])
                l.append(jnp.broadcast_to(lse[blk], (A_SPAN, HEAD_DIM)))
            o_ref[0, rw, :] = jnp.concatenate(o, axis=1)
            l_ref[0, rw, :] = jnp.concatenate(l, axis=1)


def _a_prompt(y, g, Bn, S):
    d = A_DILATIONS[g]
    slab = A_SPAN * d
    heads = A_HEADS if d == 1 else LANES // HEAD_DIM
    cols = heads * HEAD_DIM
    ncb = A_WIDTH // cols
    y3 = y.reshape(Bn, S, A_GROUPS * 3 * A_WIDTH)

    def spec(which, prev):
        c0 = (3 * g + which) * ncb
        if prev:
            return pl.BlockSpec((1, slab, cols), lambda b, j, c: (b, jnp.maximum(j - 1, 0), c0 + c))
        return pl.BlockSpec((1, slab, cols), lambda b, j, c: (b, j, c0 + c))

    out_spec = pl.BlockSpec((1, slab, cols), lambda b, j, c: (b, j, c))
    o, l = pl.pallas_call(
        functools.partial(_a_prompt_kernel, d),
        grid=(Bn, S // slab, ncb),
        in_specs=[spec(0, False), spec(1, True), spec(1, False), spec(2, True), spec(2, False)],
        out_specs=[out_spec, out_spec],
        out_shape=[jax.ShapeDtypeStruct((Bn, S, A_WIDTH), F32)] * 2,
        compiler_params=_params("parallel", "arbitrary", "arbitrary"),
        name="a_prompt_d%d" % d,
    )(y3, y3, y3, y3, y3)
    return o.reshape(Bn * S, A_WIDTH), l.reshape(Bn * S, A_WIDTH)


def _nt_sel(mask, a):
    out = None
    for p in reversed(_split3(a)):
        d = _dot_nt(mask, p)
        out = d if out is None else out + d
    return out


def _a_sample_kernel(T, y_ref, c1_ref, c2_ref, c3_ref, hind_ref, hindt_ref, *out_refs):
    caches = (c1_ref, c2_ref, c3_ref)
    width = 3 * A_WIDTH
    scale = HEAD_DIM ** -0.5
    hind = hind_ref[...]
    hindt = hindt_ref[...]
    lane = lax.broadcasted_iota(jnp.int32, (8, LANES), 1)
    sub = lax.broadcasted_iota(jnp.int32, (8, LANES), 0)
    zrows = jnp.zeros((LANES - T, A_WIDTH), F32)
    zsq = jnp.zeros((LANES - 8, LANES), F32)

    def head_rows(prod):
        return _nt_sel(hind, jnp.concatenate([prod, zrows], axis=0))

    def sublane_total(x):
        x = x + pltpu.roll(x, 4, 0)
        x = x + pltpu.roll(x, 2, 0)
        return x + pltpu.roll(x, 1, 0)

    def head_scores(k_of, pat_of):
        out = jnp.zeros((8, LANES), F32)
        for h in range(A_HEADS):
            prod = k_of(h) * pat_of(h)
            part = prod[0:8]
            for j in range(1, HEAD_DIM // 8):
                part = part + prod[8 * j:8 * j + 8]
            out = jnp.where(sub == h, sublane_total(part), out)
        return out

    def rows_of(packed):
        return jnp.concatenate([packed, zsq], axis=0).T

    def widen(rows8):
        return _sel_right(rows8, hindt, terms=3)

    def place(col, at):
        return jnp.where(lane == at, col, 0.0)

    def patterns(q_t, pick):
        return _sel_right(q_t, pick.astype(BF16), terms=3)

    row_sq = lax.broadcasted_iota(jnp.int32, (LANES, LANES), 0)
    lane_sq = lax.broadcasted_iota(jnp.int32, (LANES, LANES), 1)

    prep = []
    for g in range(A_GROUPS):
        d = A_DILATIONS[g]
        q_nat = y_ref[0, :, g * width:g * width + A_WIDTH] * scale
        kn_nat = y_ref[0, :, g * width + A_WIDTH:g * width + 2 * A_WIDTH]
        vn_nat = y_ref[0, :, g * width + 2 * A_WIDTH:(g + 1) * width]
        q_t = jnp.concatenate([q_nat, zrows], axis=0).T
        if d == 1:
            pats = [patterns(q_t, row_sq == i) for i in range(T)]
            news = [head_rows(q_nat[i:i + 1, :] * kn_nat) for i in range(T)]
        else:
            pats = [patterns(q_t, (row_sq == (lane_sq & (d - 1))) & (row_sq < T))]
            news = [head_rows(q_nat * kn_nat)]
        prep.append((vn_nat, pats, news))

    swept = []
    for g in range(A_GROUPS):
        d = A_DILATIONS[g]
        c_ref = caches[g]
        n_tiles = c_ref.shape[-1] // LANES
        _, pats, news = prep[g]
        if d == 1:
            packed = jnp.zeros((8, LANES), F32)
            accs = []
            for i in range(T):
                pat = pats[i]
                s = head_scores(lambda h: c_ref[0, 0, h], lambda h: pat[h * HEAD_DIM:(h + 1) * HEAD_DIM])
                s = jnp.where(lane >= i, s, -jnp.inf)
                sn = jnp.where(lane <= i, news[i], -jnp.inf)
                m = jnp.maximum(jnp.max(s, axis=1, keepdims=True), jnp.max(sn, axis=1, keepdims=True))
                p = jnp.exp(s - m)
                pn = jnp.exp(sn - m)
                den = jnp.sum(p, axis=1, keepdims=True) + jnp.sum(pn, axis=1, keepdims=True)
                accs.append(jnp.concatenate([c_ref[0, 1, h] * p[h:h + 1, :] for h in range(A_HEADS)], axis=0))
                packed = packed + place(den, i) + place(m + jnp.log(den), 8 + i)
                packed = packed + pltpu.roll(pn, 16 + 8 * i, 1)
            swept.append((accs, packed))
        else:
            cls = lane & (d - 1)
            self_s = news[0]
            pat = pats[0]
            s_tiles = [head_scores(lambda h: c_ref[0, 0, h, :, t * LANES:(t + 1) * LANES],
                                   lambda h: pat[h * HEAD_DIM:(h + 1) * HEAD_DIM])
                       for t in range(n_tiles)]
            smax = s_tiles[0]
            for t in range(1, n_tiles):
                smax = jnp.maximum(smax, s_tiles[t])
            mrow = jnp.full((8, LANES), jnp.inf, F32)
            m_cls = []
            for i in range(T):
                mi = jnp.maximum(jnp.max(jnp.where(cls == i, smax, -jnp.inf), axis=1, keepdims=True),
                                 self_s[:, i:i + 1])
                m_cls.append(mi)
                mrow = jnp.where(cls == i, mi, mrow)
            p_tiles = [jnp.exp(s_tiles[t] - mrow) for t in range(n_tiles)]
            psum = p_tiles[0]
            for t in range(1, n_tiles):
                psum = psum + p_tiles[t]
            accs = []
            for h in range(A_HEADS):
                acc = jnp.zeros((HEAD_DIM, LANES), F32)
                for t in range(n_tiles):
                    acc = acc + c_ref[0, 1, h, :, t * LANES:(t + 1) * LANES] * p_tiles[t][h:h + 1, :]
                accs.append(acc)
            packed = jnp.zeros((8, LANES), F32)
            for i in range(T):
                ps = jnp.exp(self_s[:, i:i + 1] - m_cls[i])
                den = jnp.sum(jnp.where(cls == i, psum, 0.0), axis=1, keepdims=True) + ps
                packed = packed + place(ps, i) + place(den, 8 + i) + place(m_cls[i] + jnp.log(den), 16 + i)
            swept.append(([jnp.concatenate(accs, axis=0)], packed))

    for g in range(A_GROUPS):
        d = A_DILATIONS[g]
        vn_nat = prep[g][0]
        accs, packed = swept[g]
        sq = rows_of(packed)
        if d == 1:
            first = (sub == 0).astype(BF16)
            denx = widen(sq[0:8])
            lsex = widen(sq[8:16])
            for i in range(T):
                pnx = widen(sq[16 + 8 * i:24 + 8 * i])
                o = _nt_sel(first, accs[i])[0:1, :] + jnp.sum(pnx[0:T] * vn_nat, axis=0, keepdims=True)
                out_refs[g][0, i:i + 1, :] = o / denx[i:i + 1, :]
            out_refs[A_GROUPS + g][0] = lsex[0:T]
        else:
            cls = lane & (d - 1)
            res = _nt_sel((cls == sub).astype(BF16), accs[0])
            psx, denx, lsex = widen(sq[0:8]), widen(sq[8:16]), widen(sq[16:24])
            out_refs[g][0] = (res[0:T] + psx[0:T] * vn_nat) / denx[0:T]
            out_refs[A_GROUPS + g][0] = lsex[0:T]


def _a_sample(y, caches, Bd, T):
    width = A_GROUPS * 3 * A_WIDTH
    y3 = y.reshape(Bd, T, width)
    views = []
    specs = [pl.BlockSpec((1, T, width), lambda b: (b, 0, 0))]
    for g, c in enumerate(caches):
        d = A_DILATIONS[g]
        assert c.shape[1] == A_SPAN * d and (d == 1 or T <= d) and T <= 8
        views.append(c.transpose(0, 2, 3, 4, 1))
        specs.append(pl.BlockSpec((1, 2, A_HEADS, HEAD_DIM, A_SPAN * d), lambda b: (b, 0, 0, 0, 0)))
    hind_np = np.zeros((LANES, A_WIDTH), np.float32)
    hind_np[np.arange(A_WIDTH) // HEAD_DIM, np.arange(A_WIDTH)] = 1.0
    hind = jnp.asarray(hind_np[:A_HEADS], BF16)
    hindt = jnp.asarray(hind_np, BF16)
    specs += [pl.BlockSpec(hind.shape, lambda b: (0, 0)), pl.BlockSpec(hindt.shape, lambda b: (0, 0))]
    out_spec = pl.BlockSpec((1, T, A_WIDTH), lambda b: (b, 0, 0))
    outs = pl.pallas_call(
        functools.partial(_a_sample_kernel, T),
        grid=(Bd,),
        in_specs=specs,
        out_specs=[out_spec] * (2 * A_GROUPS),
        out_shape=[jax.ShapeDtypeStruct((Bd, T, A_WIDTH), F32)] * (2 * A_GROUPS),
        compiler_params=_params("parallel"),
        name="a_sample",
    )(y3, *views, hind, hindt)
    return [o.reshape(Bd * T, A_WIDTH) for o in outs]


def _mixer_a(xp, xs, gain, w_in, qk_gain, w_out, caches, tabs_p, tabs_s, Bn, S, Bd, T):
    flags = (1, 1, 0) * A_GROUPS
    ones = jnp.ones((A_WIDTH,), F32)
    rows = [jnp.tile(qk_gain[0], A_HEADS), jnp.tile(qk_gain[1], A_HEADS), ones] * A_GROUPS
    gain_rows = jnp.stack(rows).reshape(3 * A_GROUPS, 1, A_WIDTH)
    mask_rows = jnp.stack([ones, ones, 0.0 * ones] * A_GROUPS).reshape(3 * A_GROUPS, 1, A_WIDTH)
    w = w_in.astype(BF16)
    wins = [min(A_SPAN * A_DILATIONS[g], S) for g in range(A_GROUPS)]
    yp, *kvts = _proj_qk(xp, gain, w, flags, gain_rows, mask_rows, tabs_p, 1024, A_WIDTH,
                         t_chunks=[((3 * g + 1, 3 * g + 2), wins[g]) for g in range(A_GROUPS)], seq=S)
    ys = _proj_qk(xs, gain, w, flags, gain_rows, mask_rows, tabs_s, xs.shape[0], A_WIDTH)
    wo = w_out.astype(BF16)
    pr = [_a_prompt(yp, g, Bn, S) for g in range(A_GROUPS)]
    mix_p = (_mix_a, [(o, A_WIDTH, 0) for o, _ in pr] + [(l, A_WIDTH, 0) for _, l in pr], [wo])
    sr = _a_sample(ys, caches, Bd, T)
    mix_s = (_mix_a, [(o, A_WIDTH, 0) for o in sr], [wo])
    ys3 = ys.reshape(Bd, T, A_GROUPS * 3 * A_WIDTH)
    new_p, new_s = [], []
    for g in range(A_GROUPS):
        c0 = (3 * g + 1) * A_WIDTH
        kvt = kvts[g][:, :, kvts[g].shape[2] - wins[g]:]
        new_p.append(kvt.reshape(Bn, 2, A_HEADS, HEAD_DIM, wins[g]).transpose(0, 4, 1, 2, 3))
        new_s.append(ys3[:, :, c0:c0 + 2 * A_WIDTH].reshape(Bd, T, 2, A_HEADS, HEAD_DIM))
    return mix_p, mix_s, new_p, new_s


def _b_kmean_kernel(k_ref, o_ref):
    nblk = k_ref.shape[1] // B_BLOCK
    rows = [jnp.mean(k_ref[0, n * B_BLOCK:(n + 1) * B_BLOCK, :], axis=0, keepdims=True) for n in range(nblk)]
    o_ref[0] = jnp.concatenate(rows, axis=0)


def _b_gate_kernel(q_ref, kmh_ref, kml_ref, o_ref):
    i = pl.program_id(1)
    qh, ql = _split2(q_ref[0])
    kmh, kml = kmh_ref[0], kml_ref[0]
    gate = _dot_nt(kmh, qh) + (_dot_nt(kmh, ql) + _dot_nt(kml, qh))
    tq = gate.shape[1]
    nblk = gate.shape[0] // B_HEADS
    gate = gate.reshape(B_HEADS, nblk, tq)
    blk = lax.broadcasted_iota(jnp.int32, (B_HEADS, nblk, tq), 1)
    blkf = blk.astype(F32)
    gate = jnp.where(blk < i, gate, -jnp.inf)
    sel = jnp.zeros((B_HEADS, nblk, tq), jnp.bool_)
    for _ in range(B_TOPK):
        mx = jnp.max(gate, axis=1, keepdims=True)
        first = jnp.min(jnp.where(gate == mx, blkf, float(nblk)), axis=1, keepdims=True)
        hit = blkf == first
        sel = sel | (hit & (mx > -jnp.inf))
        gate = jnp.where(hit, -jnp.inf, gate)
    bias = jnp.where(sel | (blk >= i), 0.0, NEG).reshape(B_HEADS * nblk, tq)
    o_ref[0] = bias.T


def _b_attn_kernel(qi_ref, kn_ref, q_ref, k_ref, vt_ref, o_ref, *state):
    t = pl.program_id(1)
    i, n = qi_ref[t], kn_ref[t]
    tq = B_BLOCK
    wide = B_REP * tq
    m_refs, acc_refs = state[:B_KV_HEADS], state[B_KV_HEADS:]

    @pl.when(n == 0)
    def _():
        for kvh in range(B_KV_HEADS):
            m_refs[kvh][...] = jnp.full_like(m_refs[kvh], -jnp.inf)
            acc_refs[kvh][...] = jnp.zeros_like(acc_refs[kvh])

    def sweep(causal):
        if causal:
            kk = lax.broadcasted_iota(jnp.int32, (B_BLOCK, wide), 0)
            qq = lax.broadcasted_iota(jnp.int32, (B_BLOCK, wide), 1) & (tq - 1)
            keep = kk <= qq
        scores = []
        for kvh in range(B_KV_HEADS):
            k = k_ref[0, :, kvh * LANES:(kvh + 1) * LANES]
            s = _dot_nt(k, q_ref[0, 0, kvh])
            scores.append(jnp.where(keep, s, NEG) if causal else s)
        probs, scales = [], []
        for kvh in range(B_KV_HEADS):
            m_old = m_refs[kvh][...]
            m_new = jnp.maximum(m_old, jnp.max(scores[kvh], axis=0, keepdims=True))
            probs.append(jnp.exp2(scores[kvh] - m_new).astype(BF16))
            scales.append(jnp.exp2(m_old - m_new))
            m_refs[kvh][...] = m_new
        for kvh in range(B_KV_HEADS):
            acc_refs[kvh][...] = scales[kvh] * acc_refs[kvh][...] + _dot(vt_ref[0, kvh], probs[kvh])

    @pl.when(n < i)
    def _():
        sweep(False)

    @pl.when(n == i)
    def _():
        sweep(True)
        for kvh in range(B_KV_HEADS):
            acc = acc_refs[kvh][...]
            ot = acc[0:HEAD_DIM] / acc[HEAD_DIM:HEAD_DIM + 1]
            for r in range(0, B_REP, 2):
                pair = [ot[:, (r + e) * tq:(r + e + 1) * tq].T for e in range(2)]
                c0 = (kvh * B_REP + r) * HEAD_DIM
                o_ref[0, :, c0:c0 + 2 * HEAD_DIM] = jnp.concatenate(pair, axis=1)


def _b_prompt(y, Bn, S):
    nblk = S // B_BLOCK
    assert nblk % 8 == 0 and LANES % nblk == 0
    nq = B_HEADS * HEAD_DIM
    nk = B_KV_HEADS * HEAD_DIM
    y3 = y.reshape(Bn, S, nq + 2 * nk)
    kmean = pl.pallas_call(
        _b_kmean_kernel,
        grid=(Bn,),
        in_specs=[pl.BlockSpec((1, S, nk), lambda b: (b, 0, nq // nk))],
        out_specs=pl.BlockSpec((1, nblk, nk), lambda b: (b, 0, 0)),
        out_shape=jax.ShapeDtypeStruct((Bn, nblk, nk), F32),
        compiler_params=_params("parallel"),
        name="b_kmean",
    )(y3)
    km = kmean.reshape(Bn, nblk, B_KV_HEADS, HEAD_DIM)
    km = jnp.repeat(km, B_REP, axis=2)
    eye = jnp.eye(B_HEADS, dtype=F32)
    kmbd = jnp.einsum('bnhe,hg->bgnhe', km, eye).reshape(Bn, B_HEADS * nblk, nq)
    kmh = kmbd.astype(BF16)
    kml = (kmbd - kmh.astype(F32)).astype(BF16)
    bias = pl.pallas_call(
        _b_gate_kernel,
        grid=(Bn, nblk),
        in_specs=[
            pl.BlockSpec((1, B_BLOCK, nq), lambda b, i: (b, i, 0)),
            pl.BlockSpec((1, B_HEADS * nblk, nq), lambda b, i: (b, 0, 0)),
            pl.BlockSpec((1, B_HEADS * nblk, nq), lambda b, i: (b, 0, 0)),
        ],
        out_specs=pl.BlockSpec((1, B_BLOCK, B_HEADS * nblk), lambda b, i: (b, i, 0)),
        out_shape=jax.ShapeDtypeStruct((Bn, S, B_HEADS * nblk), F32),
        compiler_params=_params("parallel", "arbitrary"),
        name="b_gate",
    )(y3, kmh, kml)
    q = (y3[:, :, :nq] * (HEAD_DIM ** -0.5 * LOG2E)).astype(BF16).reshape(Bn, S, B_HEADS, HEAD_DIM)
    pad = LANES - HEAD_DIM - nblk
    q_aug = jnp.concatenate([q, bias.astype(BF16).reshape(Bn, S, B_HEADS, nblk),
                             jnp.zeros((Bn, S, B_HEADS, pad), BF16)], axis=-1)
    q_aug = q_aug.reshape(Bn, nblk, B_BLOCK, B_KV_HEADS, B_REP, LANES).transpose(0, 1, 3, 4, 2, 5)
    q_aug = q_aug.reshape(Bn, nblk, B_KV_HEADS, B_REP * B_BLOCK, LANES)
    k = y3[:, :, nq:nq + nk].astype(BF16).reshape(Bn, S, B_KV_HEADS, HEAD_DIM)
    onehot = jax.nn.one_hot(jnp.arange(S) // B_BLOCK, nblk, dtype=BF16)
    onehot = jnp.broadcast_to(onehot[None, :, None, :], (Bn, S, B_KV_HEADS, nblk))
    k_aug = jnp.concatenate([k, onehot, jnp.zeros((Bn, S, B_KV_HEADS, pad), BF16)], axis=-1)
    k_aug = k_aug.reshape(Bn, S, B_KV_HEADS * LANES)
    vt = y3[:, :, nq + nk:].astype(BF16).reshape(Bn, S, B_KV_HEADS, HEAD_DIM).transpose(0, 2, 3, 1)
    vt = jnp.concatenate([vt, jnp.ones((Bn, B_KV_HEADS, 1, S), BF16),
                          jnp.zeros((Bn, B_KV_HEADS, B_VT_ROWS - HEAD_DIM - 1, S), BF16)], axis=2)
    pairs = [(i, n) for i in range(nblk) for n in range(i + 1)]
    qi = jnp.asarray([p[0] for p in pairs], jnp.int32)
    kn = jnp.asarray([p[1] for p in pairs], jnp.int32)
    o = pl.pallas_call(
        _b_attn_kernel,
        grid_spec=pltpu.PrefetchScalarGridSpec(
            num_scalar_prefetch=2,
            grid=(Bn, len(pairs)),
            in_specs=[
                pl.BlockSpec((1, 1, B_KV_HEADS, B_REP * B_BLOCK, LANES),
                             lambda b, t, qi, kn: (b, qi[t], 0, 0, 0)),
                pl.BlockSpec((1, B_BLOCK, B_KV_HEADS * LANES), lambda b, t, qi, kn: (b, kn[t], 0)),
                pl.BlockSpec((1, B_KV_HEADS, B_VT_ROWS, B_BLOCK), lambda b, t, qi, kn: (b, 0, 0, kn[t])),
            ],
            out_specs=pl.BlockSpec((1, B_BLOCK, nq), lambda b, t, qi, kn: (b, qi[t], 0)),
            scratch_shapes=([pltpu.VMEM((1, B_REP * B_BLOCK), F32)] * B_KV_HEADS
                            + [pltpu.VMEM((B_VT_ROWS, B_REP * B_BLOCK), F32)] * B_KV_HEADS),
        ),
        out_shape=jax.ShapeDtypeStruct((Bn, S, nq), F32),
        compiler_params=_params("parallel", "arbitrary"),
        name="b_attn",
    )(qi, kn, q_aug, k_aug, vt)
    return o.reshape(Bn * S, nq)


def _b_sample_kernel(n_pages, T, pt_ref, q_ref, kn_ref, vn_ref, *refs):
    del pt_ref
    page_refs = refs[:n_pages]
    o_ref = refs[n_pages]
    nk = B_KV_HEADS * HEAD_DIM
    q = q_ref[0]
    qh, ql = _split2(q)
    rows = q.shape[0]
    page_rows = page_refs[0].shape[2]
    pages_per_block = B_BLOCK // page_rows
    n_blocks = n_pages // pages_per_block
    scale = HEAD_DIM ** -0.5
    lane = lax.broadcasted_iota(jnp.int32, (1, LANES), 1)
    lanef = lax.broadcasted_iota(jnp.int32, (rows, LANES), 1).astype(F32)
    scores = []
    block_sum = [None] * n_blocks
    for p in range(n_pages):
        kt = page_refs[p][0, :nk, :]
        scores.append(_dot(qh, kt.astype(BF16)) * scale)
        b = p // pages_per_block
        block_sum[b] = kt if block_sum[b] is None else block_sum[b] + kt
    km = jnp.zeros((nk, LANES), F32)
    for b in range(n_blocks):
        ksum = jnp.sum(block_sum[b], axis=1, keepdims=True) * (1.0 / B_BLOCK)
        km = km + ksum * (lane == b).astype(F32)
    kmh, kml = _split2(km)
    gate = _dot(qh, kmh) + (_dot(qh, kml) + _dot(ql, kmh))
    gate = jnp.where(lanef < float(n_blocks), gate, -jnp.inf)
    sel = jnp.zeros((rows, LANES), jnp.bool_)
    for _ in range(min(B_TOPK, n_blocks)):
        mx = jnp.max(gate, axis=1, keepdims=True)
        first = jnp.min(jnp.where(gate == mx, lanef, float(LANES)), axis=1, keepdims=True)
        hit = lanef == first
        sel = sel | (hit & (mx > -jnp.inf))
        gate = jnp.where(hit, -jnp.inf, gate)
    self = sel.astype(F32)
    qidx = lax.broadcasted_iota(jnp.int32, (rows, 1), 0) % T
    kn, vn = kn_ref[0], vn_ref[0]
    own = []
    for j in range(T):
        sj = jnp.sum(q * kn[j:j + 1, :], axis=1, keepdims=True) * scale
        own.append(jnp.where(qidx >= j, sj, NEG))
    m = own[0]
    for j in range(1, T):
        m = jnp.maximum(m, own[j])
    for p in range(n_pages):
        b = p // pages_per_block
        scores[p] = jnp.where(self[:, b:b + 1] > 0.0, scores[p], NEG)
        m = jnp.maximum(m, jnp.max(scores[p], axis=-1, keepdims=True))
    den = jnp.zeros((rows, 1), F32)
    acc = jnp.zeros((rows, nk), F32)
    for j in range(T):
        pj = jnp.exp(own[j] - m)
        den = den + pj
        acc = acc + pj * vn[j:j + 1, :]
    for p in range(n_pages):
        pp = jnp.exp(scores[p] - m)
        den = den + jnp.sum(pp, axis=-1, keepdims=True)
        acc = acc + _dot_nt(pp.astype(BF16), page_refs[p][0, nk:, :].astype(BF16))
    acc = acc / den
    kvh = lax.broadcasted_iota(jnp.int32, (rows, 1), 0) // (T * B_REP)
    out = jnp.zeros((rows, HEAD_DIM), F32)
    for h in range(B_KV_HEADS):
        out = out + jnp.where(kvh == h, acc[:, h * HEAD_DIM:(h + 1) * HEAD_DIM], 0.0)
    o_ref[0] = out


def _b_sample(y, pool, page_table, Bd, T):
    nq = B_HEADS * HEAD_DIM
    nk = B_KV_HEADS * HEAD_DIM
    n_pages = page_table.shape[1]
    page_rows = pool.shape[1]
    assert B_BLOCK % page_rows == 0 and (n_pages * page_rows) % B_BLOCK == 0 and T <= B_BLOCK
    assert n_pages * page_rows // B_BLOCK <= LANES
    y3 = y.reshape(Bd, T, nq + 2 * nk)
    q = y3[:, :, :nq].reshape(Bd, T, B_HEADS, HEAD_DIM).transpose(0, 2, 1, 3)
    kvsel = jnp.asarray(np.kron(np.eye(B_KV_HEADS), np.ones((B_REP, 1))), F32)
    qbd = (q[:, :, :, None, :] * kvsel[None, :, None, :, None]).reshape(Bd, B_HEADS * T, nk)
    kn = y3[:, :, nq:nq + nk]
    vn = y3[:, :, nq + nk:]
    pool_t = pool.transpose(0, 2, 3, 4, 1).reshape(pool.shape[0], 2 * nk, page_rows)
    page_specs = [pl.BlockSpec((1, 2 * nk, page_rows), lambda b, pt, p=p: (pt[b, p], 0, 0))
                  for p in range(n_pages)]
    o = pl.pallas_call(
        functools.partial(_b_sample_kernel, n_pages, T),
        grid_spec=pltpu.PrefetchScalarGridSpec(
            num_scalar_prefetch=1,
            grid=(Bd,),
            in_specs=[
                pl.BlockSpec((1, B_HEADS * T, nk), lambda b, pt: (b, 0, 0)),
                pl.BlockSpec((1, T, nk), lambda b, pt: (b, 0, 0)),
                pl.BlockSpec((1, T, nk), lambda b, pt: (b, 0, 0)),
            ] + page_specs,
            out_specs=pl.BlockSpec((1, B_HEADS * T, HEAD_DIM), lambda b, pt: (b, 0, 0)),
        ),
        out_shape=jax.ShapeDtypeStruct((Bd, B_HEADS * T, HEAD_DIM), F32),
        compiler_params=_params("parallel"),
        name="b_sample",
    )(page_table, qbd, kn, vn, *([pool_t] * n_pages))
    return o.reshape(Bd, B_HEADS, T, HEAD_DIM).transpose(0, 2, 1, 3).reshape(Bd * T, nq)


def _mixer_b(xp, xs, gain, w_in, qk_gain, w_out, pool, page_table, tabs_p, tabs_s, Bn, S, Bd, T):
    tn = 512
    nq = B_HEADS * HEAD_DIM
    nk = B_KV_HEADS * HEAD_DIM
    flags = (1, 1, 2)
    ones = jnp.ones((nk,), F32)
    qg = jnp.tile(qk_gain[0], tn // HEAD_DIM)
    kg = jnp.concatenate([jnp.tile(qk_gain[1], B_KV_HEADS), ones])
    gain_rows = jnp.stack([qg, qg, kg]).reshape(3, 1, tn)
    mask_rows = jnp.stack([jnp.ones((tn,), F32), jnp.ones((tn,), F32),
                           jnp.concatenate([ones, 0.0 * ones])]).reshape(3, 1, tn)
    w = w_in.astype(BF16)
    yp, kvt = _proj_qk(xp, gain, w, flags, gain_rows, mask_rows, tabs_p, 1024, tn, t_chunks=[((2,), S)], seq=S)
    ys = _proj_qk(xs, gain, w, flags, gain_rows, mask_rows, tabs_s, xs.shape[0], tn)
    wo = w_out.astype(BF16)
    op = _b_prompt(yp, Bn, S)
    mix_p = (_mix_plain, [(op, nq, 0)], [wo])
    osm = _b_sample(ys, pool, page_table, Bd, T)
    mix_s = (_mix_plain, [(osm, nq, 0)], [wo])
    kv_p = kvt.reshape(Bn, 2, B_KV_HEADS, HEAD_DIM, S).transpose(0, 4, 1, 2, 3)
    kv_s = ys[:, nq:].reshape(Bd, T, 2, B_KV_HEADS, HEAD_DIM)
    return mix_p, mix_s, kv_p, kv_s


def _gla_kernel(nv, q_ref, k_ref, v_ref, glr_ref, wg_ref, bg_ref, tri_ref, blk_ref, s0_ref,
                o_ref, sT_ref, s_ref, qp_ref, kp_ref, vp_ref, gp_ref):
    t = pl.program_id(1)
    Tt = SEQ_TILE

    @pl.when(t == 0)
    def _():
        s_ref[...] = s0_ref[0]

    if nv < Tt:
        qp_ref[...] = jnp.zeros_like(qp_ref)
        kp_ref[...] = jnp.zeros_like(kp_ref)
        vp_ref[...] = jnp.zeros_like(vp_ref)
        gp_ref[...] = jnp.zeros_like(gp_ref)
    qp_ref[0:nv, :] = q_ref[0]
    kp_ref[0:nv, :] = k_ref[0]
    vp_ref[0:nv, :] = v_ref[0]
    gp_ref[0:nv, :] = glr_ref[0]
    q, k, v = qp_ref[...], kp_ref[...], vp_ref[...]

    x = _dot(gp_ref[...].astype(BF16), wg_ref[...]) + bg_ref[...]
    la = (jnp.minimum(x, 0.0) - jnp.log(1.0 + jnp.exp(-jnp.abs(x)))) * (1.0 / C_TAU)
    row = lax.broadcasted_iota(jnp.int32, (Tt, C_KEY), 0)
    la = jnp.where(row < nv, la, 0.0)
    b = _sel_left(tri_ref[...], la, terms=3)
    bend = _sel_left(blk_ref[...], la, terms=3)
    qe = q * (C_DK ** -0.5) * jnp.exp(b)
    ke = k * jnp.exp(-b)
    kd = k * jnp.exp(bend - b)
    kdt = kd.T
    bendt = bend.T
    qeb, keb, vb = qe.astype(BF16), ke.astype(BF16), v.astype(BF16)
    causal = tri_ref[...] > 0
    lane_t = lax.broadcasted_iota(jnp.int32, (C_DK, Tt), 1)
    n_chunks = -(-nv // C_CHUNK)
    ksl = [slice(h * C_DK, (h + 1) * C_DK) for h in range(C_HEADS)]
    vsl = [slice(h * C_DV, (h + 1) * C_DV) for h in range(C_HEADS)]
    att = [jnp.where(causal, _dot_nt(qeb[:, ks], keb[:, ks]), 0.0).astype(BF16) for ks in ksl]
    chunk_kv, chunk_dec = [], []
    for c in range(n_chunks):
        in_chunk = (lane_t >= c * C_CHUNK) & (lane_t < (c + 1) * C_CHUNK)
        last = lane_t == (c + 1) * C_CHUNK - 1
        chunk_kv.append([_dot(jnp.where(in_chunk, kdt[ks, :], 0.0).astype(BF16), vb[:, vs])
                         for ks, vs in zip(ksl, vsl)])
        chunk_dec.append([jnp.exp(jnp.sum(jnp.where(last, bendt[ks, :], 0.0), axis=1, keepdims=True))
                          for ks in ksl])
    o_intra = [_dot(att[h], vb[:, vsl[h]]) for h in range(C_HEADS)]
    for h in range(C_HEADS):
        st = s_ref[h]
        parts = []
        for c in range(n_chunks):
            rows = slice(c * C_CHUNK, (c + 1) * C_CHUNK)
            parts.append(o_intra[h][rows] + _dot(qeb[rows, ksl[h]], st.astype(BF16)))
            st = chunk_dec[c][h] * st + chunk_kv[c][h]
        s_ref[h] = st
        oh = parts[0] if n_chunks == 1 else jnp.concatenate(parts, axis=0)
        o_ref[0, :, vsl[h]] = oh[0:nv]

    @pl.when(t == pl.num_programs(1) - 1)
    def _():
        sT_ref[0] = s_ref[...]


def _gla(y, w_gate2, b_gate, s0, nb, nt, nv):
    Tt = SEQ_TILE
    y3 = y.reshape(nb * nt, nv, C_NPAD)
    wg = jnp.zeros((LANES, C_KEY), F32).at[:C_RANK].set(w_gate2).astype(BF16)
    idx = np.arange(Tt)
    same = (idx[:, None] // C_CHUNK) == (idx[None, :] // C_CHUNK)
    tri = jnp.asarray(same & (idx[None, :] <= idx[:, None]), BF16)
    blk = jnp.asarray(same, BF16)

    def yspec(width, cb):
        return pl.BlockSpec((1, nv, width), lambda b, t: (b * nt + t, 0, cb))

    def cspec(a):
        return pl.BlockSpec(a.shape, lambda b, t, nd=a.ndim: (0,) * nd)

    bg = b_gate.reshape(1, C_KEY)
    o, sT = pl.pallas_call(
        functools.partial(_gla_kernel, nv),
        grid=(nb, nt),
        in_specs=[
            yspec(C_KEY, 0), yspec(C_KEY, 1), yspec(C_VAL, 1), yspec(LANES, (2 * C_KEY + 2 * C_VAL) // LANES),
            cspec(wg), cspec(bg), cspec(tri), cspec(blk),
            pl.BlockSpec((1, C_HEADS, C_DK, C_DV), lambda b, t: (b, 0, 0, 0)),
        ],
        out_specs=[
            pl.BlockSpec((1, nv, C_VAL), lambda b, t: (b * nt + t, 0, 0)),
            pl.BlockSpec((1, C_HEADS, C_DK, C_DV), lambda b, t: (b, 0, 0, 0)),
        ],
        out_shape=[
            jax.ShapeDtypeStruct((nb * nt, nv, C_VAL), F32),
            jax.ShapeDtypeStruct((nb, C_HEADS, C_DK, C_DV), F32),
        ],
        scratch_shapes=[
            pltpu.VMEM((C_HEADS, C_DK, C_DV), F32),
            pltpu.VMEM((Tt, C_KEY), F32), pltpu.VMEM((Tt, C_KEY), F32),
            pltpu.VMEM((Tt, C_VAL), F32), pltpu.VMEM((Tt, LANES), F32),
        ],
        compiler_params=_params("parallel", "arbitrary"),
        name="gla",
    )(y3, y3, y3, y3, wg, bg, tri, blk, s0)
    return o.reshape(nb * nt * nv, C_VAL), sT


def _mixer_c(xp, xs, gain, w_in, w_gate2, b_gate, norm_g, w_out, state, Bn, S, Bd, T):
    n_in = w_in.shape[1]
    w = jnp.zeros((D_MODEL, C_NPAD), BF16).at[:, :n_in].set(w_in.astype(BF16))
    yp = _proj(xp, gain, w, 1024)
    ys = _proj(xs, gain, w, xs.shape[0])
    zero = jnp.zeros((Bn, C_HEADS, C_DK, C_DV), F32)
    op, sp = _gla(yp, w_gate2, b_gate, zero, Bn, S // SEQ_TILE, SEQ_TILE)
    osm, ss = _gla(ys, w_gate2, b_gate, state, Bd, 1, T)
    wo = w_out.astype(BF16)
    ng = norm_g.reshape(1, C_DV)
    rcb = (2 * C_KEY + C_VAL) // C_VAL
    mix_p = (_mix_c, [(op, C_VAL, 0), (yp, C_VAL, rcb)], [ng, wo])
    mix_s = (_mix_c, [(osm, C_VAL, 0), (ys, C_VAL, rcb)], [ng, wo])
    return mix_p, mix_s, sp, ss


def _ssd_kernel(nv, xa_ref, xb_ref, bc_ref, dt_ref, cs_ref, cw_ref, cb_ref, dtb_ref, alog_ref, dsk_ref,
                tri_ref, exp_ref, expt_ref, h0_ref, y_ref, hT_ref, h_ref, xp_ref, dtp_ref):
    t = pl.program_id(1)
    Tt = SEQ_TILE
    pre = 8

    @pl.when(t == 0)
    def _():
        h_ref[...] = h0_ref[0]
        xp_ref[0:pre, :] = cs_ref[0]

    if nv < Tt:
        xp_ref[pre:, :] = jnp.zeros((Tt, D_XBC), F32)
        dtp_ref[...] = jnp.zeros_like(dtp_ref)
    xp_ref[pre:pre + nv, 0:1024] = xa_ref[0]
    xp_ref[pre:pre + nv, 1024:2048] = xb_ref[0]
    xp_ref[pre:pre + nv, 2048:3072] = bc_ref[0]
    dtp_ref[0:nv, :] = dt_ref[0]

    full = xp_ref[...]
    conv = cb_ref[...] + full[pre:] * cw_ref[D_CONV - 1:D_CONV, :]
    for k in range(1, D_CONV):
        conv = conv + pltpu.roll(full, k, 0)[pre:] * cw_ref[D_CONV - 1 - k:D_CONV - k, :]
    xp_ref[0:pre, :] = xp_ref[Tt:Tt + pre, :]
    xbc = _silu(conv)
    x = xbc[:, :D_INNER]
    nbc = D_GROUPS * D_STATE
    bm = xbc[:, D_INNER:D_INNER + nbc].astype(BF16)
    cm = xbc[:, D_INNER + nbc:].astype(BF16)

    row = lax.broadcasted_iota(jnp.int32, (Tt, LANES), 0)
    lane = lax.broadcasted_iota(jnp.int32, (Tt, LANES), 1)
    live = (row < nv) & (lane < D_HEADS)
    dt = jnp.where(live, _softplus(dtp_ref[...] + dtb_ref[...]), 0.0)
    cum = _sel_left(tri_ref[...], dt * (-jnp.exp(alog_ref[...])), terms=3)
    cumt = cum.T
    cend = cum[Tt - 1:Tt, :]
    ex = exp_ref[...]
    stack = jnp.concatenate([jnp.exp(cend - cum) * dt, dt, jnp.exp(cum)], axis=0)
    wide = _sel_right(stack, ex)
    x_state = (x * wide[0:Tt]).astype(BF16)
    x_dt = (x * wide[Tt:2 * Tt]).astype(BF16)
    off_scale = wide[2 * Tt:]

    causal = tri_ref[...] > 0
    gw = D_INNER // D_GROUPS
    cbs, offs = [], []
    for g in range(D_GROUPS):
        cg = cm[:, g * D_STATE:(g + 1) * D_STATE]
        cbs.append(_dot_nt(cg, bm[:, g * D_STATE:(g + 1) * D_STATE]))
        offs.append(_dot_nt(cg, h_ref[g * gw:(g + 1) * gw, :].astype(BF16)))
    mats = []
    for j in range(D_HEADS):
        seg = cum[:, j:j + 1] - cumt[j:j + 1, :]
        mats.append((cbs[j // D_HPG] * jnp.exp(jnp.where(causal, seg, -jnp.inf))).astype(BF16))
    diag = [_dot(mats[j], x_dt[:, j * D_HEADDIM:(j + 1) * D_HEADDIM]) for j in range(D_HEADS)]
    y = jnp.concatenate(diag, axis=1) + jnp.concatenate(offs, axis=1) * off_scale + dsk_ref[...] * x
    y_ref[0] = y[0:nv]

    dcol = jnp.broadcast_to(jnp.exp(cumt[:, Tt - 1:Tt]), (LANES, D_STATE))
    dfull = _sel_left(expt_ref[...], dcol)
    xst = x_state.astype(F32).T.astype(BF16)
    for g in range(D_GROUPS):
        rows = slice(g * gw, (g + 1) * gw)
        h_ref[rows, :] = dfull[rows] * h_ref[rows, :] + _dot(xst[rows], bm[:, g * D_STATE:(g + 1) * D_STATE])

    @pl.when(t == pl.num_programs(1) - 1)
    def _():
        hT_ref[0] = h_ref[...]


def _ssd(y, conv_w, conv_b, dt_bias, a_log, d_skip, h0, c0, nb, nt, nv):
    Tt = SEQ_TILE
    y3 = y.reshape(nb * nt, nv, D_NPAD)
    idx = np.arange(Tt)
    tri = jnp.asarray(idx[None, :] <= idx[:, None], BF16)
    ex_np = np.zeros((LANES, D_INNER), np.float32)
    ex_np[np.arange(D_INNER) // D_HEADDIM, np.arange(D_INNER)] = 1.0
    ex = jnp.asarray(ex_np, BF16)
    ext = jnp.asarray(ex_np.T, BF16)

    def pad_row(v):
        return jnp.zeros((1, LANES), F32).at[0, :D_HEADS].set(v)

    cs = jnp.zeros((nb, 8, D_XBC), F32).at[:, 8 - (D_CONV - 1):].set(c0)
    dsk = jnp.repeat(d_skip, D_HEADDIM).reshape(1, D_INNER)

    def yspec(width, cb):
        return pl.BlockSpec((1, nv, width), lambda b, t: (b * nt + t, 0, cb))

    def cspec(a):
        return pl.BlockSpec(a.shape, lambda b, t, nd=a.ndim: (0,) * nd)

    consts = [conv_w, conv_b.reshape(1, D_XBC), pad_row(dt_bias), pad_row(a_log), dsk, tri, ex, ext]
    yo, hT = pl.pallas_call(
        functools.partial(_ssd_kernel, nv),
        grid=(nb, nt),
        in_specs=[yspec(1024, 2), yspec(1024, 3), yspec(1024, 4), yspec(LANES, D_DT_COL),
                  pl.BlockSpec((1, 8, D_XBC), lambda b, t: (b, 0, 0))]
        + [cspec(a) for a in consts]
        + [pl.BlockSpec((1, D_INNER, D_STATE), lambda b, t: (b, 0, 0))],
        out_specs=[
            pl.BlockSpec((1, nv, D_INNER), lambda b, t: (b * nt + t, 0, 0)),
            pl.BlockSpec((1, D_INNER, D_STATE), lambda b, t: (b, 0, 0)),
        ],
        out_shape=[
            jax.ShapeDtypeStruct((nb * nt, nv, D_INNER), F32),
            jax.ShapeDtypeStruct((nb, D_INNER, D_STATE), F32),
        ],
        scratch_shapes=[
            pltpu.VMEM((D_INNER, D_STATE), F32),
            pltpu.VMEM((Tt + 8, D_XBC), F32),
            pltpu.VMEM((Tt, LANES), F32),
        ],
        compiler_params=_params("parallel", "arbitrary"),
        name="ssd",
    )(y3, y3, y3, y3, cs, *consts, h0)
    return yo.reshape(nb * nt * nv, D_INNER), hT


def _ssd_step_kernel(nv, xa_ref, xb_ref, bc_ref, dt_ref, cs_ref, cw_ref, cb_ref, dtb_ref, alog_ref, dsk_ref,
                     eye_ref, h0_ref, y_ref, hT_ref, xp_ref):
    R = 8
    pre = 8
    xp_ref[0:pre, :] = cs_ref[0]
    xp_ref[pre:, :] = jnp.zeros((R, D_XBC), F32)
    xp_ref[pre:pre + nv, 0:1024] = xa_ref[0]
    xp_ref[pre:pre + nv, 1024:2048] = xb_ref[0]
    xp_ref[pre:pre + nv, 2048:3072] = bc_ref[0]
    conv = cb_ref[...]
    for w in range(D_CONV):
        conv = conv + xp_ref[pl.ds(pre - (D_CONV - 1) + w, R), :] * cw_ref[w:w + 1, :]
    xbc = _silu(conv)
    x = xbc[:, :D_INNER]
    nbc = D_GROUPS * D_STATE
    bmf = xbc[:, D_INNER:D_INNER + nbc]
    cmf = xbc[:, D_INNER + nbc:]

    row = lax.broadcasted_iota(jnp.int32, (R, LANES), 0)
    lane = lax.broadcasted_iota(jnp.int32, (R, LANES), 1)
    dtp = jnp.concatenate([dt_ref[0], jnp.zeros((R - nv, LANES), F32)], axis=0)
    dt = jnp.where((row < nv) & (lane < D_HEADS), _softplus(dtp + dtb_ref[...]), 0.0)
    cum = dt * (-jnp.exp(alog_ref[...]))
    for k in (1, 2, 4):
        cum = cum + jnp.where(row >= k, pltpu.roll(cum, k, 0), 0.0)
    cend = cum[R - 1:R, :]
    group_of_lane = lane // D_HPG
    low_half = lane < D_HEADDIM

    def widen(a):
        pairs = [jnp.where(low_half, a[:, j:j + 1], a[:, j + 1:j + 2]) for j in range(0, D_HEADS, 2)]
        return jnp.concatenate(pairs, axis=1)

    yd = jnp.zeros((R, D_INNER), F32)
    for s in range(nv):
        w_s = jnp.where(row >= s, jnp.exp(cum - cum[s:s + 1, :]), 0.0) * dt[s:s + 1, :]
        prod = cmf * bmf[s:s + 1, :]
        cb_s = jnp.zeros((R, LANES), F32)
        for g in range(D_GROUPS):
            tot = jnp.sum(prod[:, g * D_STATE:(g + 1) * D_STATE], axis=1, keepdims=True)
            cb_s = jnp.where(group_of_lane == g, tot, cb_s)
        yd = yd + widen(cb_s * w_s) * x[s:s + 1, :]
    off_scale = widen(jnp.exp(cum))
    x_state = (x * widen(jnp.exp(cend - cum) * dt)).astype(BF16)

    gw = D_INNER // D_GROUPS
    cm = cmf.astype(BF16)
    parts = []
    for g in range(D_GROUPS):
        hg = h0_ref[0, g * gw:(g + 1) * gw, :]
        parts.append(_dot_nt(cm[:, g * D_STATE:(g + 1) * D_STATE], hg.astype(BF16)))
    y = yd + jnp.concatenate(parts, axis=1) * off_scale + dsk_ref[...] * x
    y_ref[0] = y[0:nv]

    erow = jnp.broadcast_to(jnp.exp(cend), (LANES, LANES))
    eye128 = (lax.broadcasted_iota(jnp.int32, (LANES, LANES), 0)
              == lax.broadcasted_iota(jnp.int32, (LANES, LANES), 1)).astype(BF16)
    dcol = _nt_sel(eye128, erow)[0:D_HEADS].reshape(D_HEADS, 1, D_STATE)
    zpad = jnp.zeros((LANES - R, D_INNER), BF16)
    xs_pad = jnp.concatenate([x_state, zpad], axis=0)
    bm_pad = jnp.concatenate([bmf.astype(BF16), jnp.zeros((LANES - R, nbc), BF16)], axis=0)
    for g in range(D_GROUPS):
        rows = slice(g * gw, (g + 1) * gw)
        xt = _dot_nt(eye_ref[...], xs_pad[:, rows]).astype(BF16)
        kept = (h0_ref[0, rows, :].reshape(D_HPG, D_HEADDIM, D_STATE) * dcol[g * D_HPG:(g + 1) * D_HPG])
        hT_ref[0, rows, :] = kept.reshape(gw, D_STATE) + _dot(xt, bm_pad[:, g * D_STATE:(g + 1) * D_STATE])


def _ssd_step(y, conv_w, conv_b, dt_bias, a_log, d_skip, h0, c0, nb, nv):
    assert nv <= 8
    y3 = y.reshape(nb, nv, D_NPAD)
    gw = D_INNER // D_GROUPS
    eye = jnp.asarray(np.eye(gw), BF16)

    def pad_row(v):
        return jnp.zeros((1, LANES), F32).at[0, :D_HEADS].set(v)

    cs = jnp.zeros((nb, 8, D_XBC), F32).at[:, 8 - (D_CONV - 1):].set(c0)
    dsk = jnp.repeat(d_skip, D_HEADDIM).reshape(1, D_INNER)

    def yspec(width, cb):
        return pl.BlockSpec((1, nv, width), lambda b: (b, 0, cb))

    def cspec(a):
        return pl.BlockSpec(a.shape, lambda b, nd=a.ndim: (0,) * nd)

    consts = [conv_w, conv_b.reshape(1, D_XBC), pad_row(dt_bias), pad_row(a_log), dsk, eye]
    yo, hT = pl.pallas_call(
        functools.partial(_ssd_step_kernel, nv),
        grid=(nb,),
        in_specs=[yspec(1024, 2), yspec(1024, 3), yspec(1024, 4), yspec(LANES, D_DT_COL),
                  pl.BlockSpec((1, 8, D_XBC), lambda b: (b, 0, 0))]
        + [cspec(a) for a in consts]
        + [pl.BlockSpec((1, D_INNER, D_STATE), lambda b: (b, 0, 0))],
        out_specs=[
            pl.BlockSpec((1, nv, D_INNER), lambda b: (b, 0, 0)),
            pl.BlockSpec((1, D_INNER, D_STATE), lambda b: (b, 0, 0)),
        ],
        out_shape=[
            jax.ShapeDtypeStruct((nb, nv, D_INNER), F32),
            jax.ShapeDtypeStruct((nb, D_INNER, D_STATE), F32),
        ],
        scratch_shapes=[pltpu.VMEM((16, D_XBC), F32)],
        compiler_params=_params("parallel"),
        name="ssd_step",
    )(y3, y3, y3, y3, cs, *consts, h0)
    return yo.reshape(nb * nv, D_INNER), hT


def _mixer_d(xp, xs, gain, w_in, conv_w, conv_b, dt_bias, a_log, d_skip, norm_g, w_out,
             ssm_state, conv_state, Bn, S, Bd, T):
    n_in = w_in.shape[1]
    w = jnp.zeros((D_MODEL, D_NPAD), BF16).at[:, :n_in].set(w_in.astype(BF16))
    yp = _proj(xp, gain, w, 1024)
    ys = _proj(xs, gain, w, xs.shape[0])
    h0p = jnp.zeros((Bn, D_INNER, D_STATE), F32)
    c0p = jnp.zeros((Bn, D_CONV - 1, D_XBC), F32)
    op, hp = _ssd(yp, conv_w, conv_b, dt_bias, a_log, d_skip, h0p, c0p, Bn, S // SEQ_TILE, SEQ_TILE)
    h0s = ssm_state.reshape(Bd, D_INNER, D_STATE)
    osm, hs = _ssd_step(ys, conv_w, conv_b, dt_bias, a_log, d_skip, h0s, conv_state, Bd, T)
    wo = w_out.astype(BF16)
    ng = norm_g.reshape(1, D_INNER)
    mix_p = (_mix_d, [(op, D_INNER, 0), (yp, D_INNER, 0)], [ng, wo])
    mix_s = (_mix_d, [(osm, D_INNER, 0), (ys, D_INNER, 0)], [ng, wo])
    keep = D_CONV - 1
    xbc_p = yp.reshape(Bn, S, D_NPAD)[:, :, D_INNER:D_INNER + D_XBC]
    xbc_s = ys.reshape(Bd, T, D_NPAD)[:, :, D_INNER:D_INNER + D_XBC]
    cp = jnp.concatenate([c0p, xbc_p], axis=1)[:, -keep:] if S < keep else xbc_p[:, S - keep:]
    cs = jnp.concatenate([conv_state, xbc_s], axis=1)[:, -keep:]
    return (mix_p, mix_s, hp.reshape(Bn, D_HEADS, D_HEADDIM, D_STATE), hs.reshape(Bd, D_HEADS, D_HEADDIM, D_STATE),
            cp, cs)


def kernel(x_prompt, x_sample, cache_a_w1, cache_a_w2, cache_a_w3, cache_b_kv, page_table, state_c, state_d_ssm, state_d_conv, norm_gain, w_ffn_up, w_ffn_down, w_a_in, a_qk_gain, w_a_out, w_b_in, b_qk_gain, w_b_out, w_c_in, w_c_gate2, b_c_gate, c_norm_gain, w_c_out, w_d_in, d_conv_w, d_conv_b, d_dt_bias, d_a_log, d_skip, d_norm_gain, w_d_out):
    Bn, S, _ = x_prompt.shape
    Bd, T, _ = x_sample.shape
    depth = norm_gain.shape[0]
    past_len = page_table.shape[1] * cache_b_kv.shape[2]
    tabs_p = _rope_tables(jnp.arange(S, dtype=jnp.int32))
    tabs_s = _rope_tables(jnp.tile(past_len + jnp.arange(T, dtype=jnp.int32), Bd))
    xp = x_prompt.reshape(Bn * S, D_MODEL)
    xs = x_sample.reshape(Bd * T, D_MODEL)
    ts = xs.shape[0]
    w_up = w_ffn_up.astype(BF16)
    w_down = w_ffn_down.astype(BF16)
    outs = {k: [] for k in ("a0p", "a0s", "a1p", "a1s", "a2p", "a2s", "bp", "bs", "cp", "cs",
                            "hp", "hs", "dp", "ds")}
    for i in range(depth):
        m, j = i % 4, i // 4
        g = norm_gain[i]
        xp = _ffn(xp, g[0], w_up, w_down, i, 0, 1024)
        xs = _ffn(xs, g[0], w_up, w_down, i, 0, ts)
        if m == 0:
            mix_p, mix_s, new_p, new_s = _mixer_a(xp, xs, g[1], w_a_in[j], a_qk_gain[j], w_a_out[j],
                                            (cache_a_w1[j], cache_a_w2[j], cache_a_w3[j]),
                                            tabs_p, tabs_s, Bn, S, Bd, T)
            for gi in range(A_GROUPS):
                outs["a%dp" % gi].append(new_p[gi])
                outs["a%ds" % gi].append(new_s[gi])
        elif m == 1:
            mix_p, mix_s, kvp, kvs = _mixer_b(xp, xs, g[1], w_b_in[j], b_qk_gain[j], w_b_out[j],
                                        cache_b_kv[j], page_table, tabs_p, tabs_s, Bn, S, Bd, T)
            outs["bp"].append(kvp)
            outs["bs"].append(kvs)
        elif m == 2:
            mix_p, mix_s, sp, ss = _mixer_c(xp, xs, g[1], w_c_in[j], w_c_gate2[j], b_c_gate[j], c_norm_gain[j],
                                      w_c_out[j], state_c[j], Bn, S, Bd, T)
            outs["cp"].append(sp)
            outs["cs"].append(ss)
        else:
            mix_p, mix_s, hp, hs, cp, cs = _mixer_d(xp, xs, g[1], w_d_in[j], d_conv_w[j], d_conv_b[j], d_dt_bias[j],
                                              d_a_log[j], d_skip[j], d_norm_gain[j], w_d_out[j],
                                              state_d_ssm[j], state_d_conv[j], Bn, S, Bd, T)
            outs["hp"].append(hp)
            outs["hs"].append(hs)
            outs["dp"].append(cp)
            outs["ds"].append(cs)
        xp = _ffn(xp, g[2], w_up, w_down, i, 1, 1024, mix_p)
        xs = _ffn(xs, g[2], w_up, w_down, i, 1, ts, mix_s)
    st = {k: jnp.stack(v) for k, v in outs.items()}
    return (xp.reshape(Bn, S, D_MODEL), xs.reshape(Bd, T, D_MODEL),
            st["a0p"], st["a0s"], st["a1p"], st["a1s"], st["a2p"], st["a2s"],
            st["bp"], st["bs"], st["cp"], st["cs"], st["hp"], st["hs"], st["dp"], st["ds"])
```
